```python
import math
import jax, jax.numpy as jnp
from jax import lax
import numpy as np

D_MODEL = 1024
BATCH = 8
SEQ = 2048
DEPTH = 1
DEC_BATCH = 32
DEC_SEQ = 8
PAST_LEN = 8192
PAGE_SIZE = 128

HEAD_DIM = 64
N_FOX_HEADS = 8
N_MOBA_HEADS = 8
FOX_WIDTH = N_FOX_HEADS * HEAD_DIM
MOBA_WIDTH = N_MOBA_HEADS * HEAD_DIM
IN_WIDTH = 3 * FOX_WIDTH + N_FOX_HEADS + 3 * MOBA_WIDTH
FOX_QBLOCK = 128
MOBA_BLOCK = 256
MOBA_TOPK = 3
MOBA_QCHUNK = 16
T5_BUCKETS = 32
T5_MAX_DIST = 128
N_GROUPS = 4
EXPERTS_PER_GROUP = 8
N_EXPERTS = N_GROUPS * EXPERTS_PER_GROUP
TOPK_IN_GROUP = 2
D_EXPERT = D_MODEL // 4
MOE_ROW_BLOCK = 128
RMS_EPS = 1e-6

kernel_name = 'hymba_fox_moba_hmoe_decode_step'


def rmsnorm(x, g):
    xf = x.astype(jnp.float32)
    y = xf * lax.rsqrt(jnp.mean(xf * xf, axis=-1, keepdims=True) + RMS_EPS)
    return (y * g.astype(jnp.float32)).astype(x.dtype)


def ada_modulation(c, w_ada, b_ada):
    m = jnp.einsum('bd,de->be', jax.nn.silu(c), w_ada) + b_ada
    return jnp.split(m[:, None, :], 6, axis=-1)


def modulate(h, shift, scale):
    return h * (1.0 + scale) + shift


def masked_softmax(s, mask):
    return jax.nn.softmax(jnp.where(mask, s.astype(jnp.float32), -jnp.inf), axis=-1)


def t5_bucket(rel):
    max_exact = T5_BUCKETS // 2
    rel = jnp.maximum(rel, 0)
    relf = jnp.maximum(rel, 1).astype(jnp.float32)
    large = max_exact + (jnp.log(relf / max_exact) / math.log(T5_MAX_DIST / max_exact)
                         * (T5_BUCKETS - max_exact)).astype(jnp.int32)
    large = jnp.minimum(large, T5_BUCKETS - 1)
    return jnp.where(rel < max_exact, rel, large)


def project(h, w_in, b_forget):
    B, S, _ = h.shape
    z = jnp.einsum('bsd,de->bse', h, w_in)
    widths = [FOX_WIDTH] * 3 + [N_FOX_HEADS] + [MOBA_WIDTH] * 3
    cuts = [int(c) for c in np.cumsum(widths)[:-1]]
    fq, fk, fv, fg, mq, mk, mv = jnp.split(z, cuts, axis=-1)
    logf = jax.nn.log_sigmoid(fg.astype(jnp.float32) + b_forget.astype(jnp.float32))
    hf = lambda t: t.reshape(B, S, N_FOX_HEADS, HEAD_DIM)
    hm = lambda t: t.reshape(B, S, N_MOBA_HEADS, HEAD_DIM)
    return hf(fq), hf(fk), hf(fv), logf, hm(mq), hm(mk), hm(mv)


def fox_attend(q, k, v, fq, fk, q_pos, k_pos):
    s = jnp.einsum('bqhd,bthd->bhqt', q, k).astype(jnp.float32) * HEAD_DIM ** -0.5
    s = s + fq[..., :, None] - fk[..., None, :]
    p = masked_softmax(s, k_pos[None, :] <= q_pos[:, None])
    return jnp.einsum('bhqt,bthd->bqhd', p.astype(v.dtype), v)


def fox_prompt(q, k, v, logf):
    B, S, H, Dh = q.shape
    F = jnp.cumsum(logf, axis=1).transpose(0, 2, 1)
    pos = jnp.arange(S, dtype=jnp.int32)
    nqb = S // FOX_QBLOCK
    q_blk = q.reshape(B, nqb, FOX_QBLOCK, H, Dh).transpose(1, 0, 2, 3, 4)
    f_blk = F.reshape(B, H, nqb, FOX_QBLOCK).transpose(2, 0, 1, 3)
    p_blk = pos.reshape(nqb, FOX_QBLOCK)
    out = lax.map(lambda a: fox_attend(a[0], k, v, a[1], F, a[2], pos), (q_blk, f_blk, p_blk))
    return out.transpose(1, 0, 2, 3, 4).reshape(B, S, H * Dh)


def fox_sample(q, k, v, logf, cache_k, cache_v, cache_logf, page_table):
    DB, DS, H, Dh = q.shape
    past_k = cache_k[page_table].reshape(DB, -1, H, Dh)
    past_v = cache_v[page_table].reshape(DB, -1, H, Dh)
    past_l = cache_logf[page_table].reshape(DB, -1, H)
    past_len = past_k.shape[1]
    k_all = jnp.concatenate([past_k, k], axis=1)
    v_all = jnp.concatenate([past_v, v], axis=1)
    F = jnp.cumsum(jnp.concatenate([past_l.astype(jnp.float32), logf], axis=1), axis=1).transpose(0, 2, 1)
    k_pos = jnp.arange(past_len + DS, dtype=jnp.int32)
    q_pos = k_pos[past_len:]
    out = fox_attend(q, k_all, v_all, F[..., past_len:], F, q_pos, k_pos)
    return out.reshape(DB, DS, H * Dh)


def pad_to_blocks(t, n_blocks):
    pad = n_blocks * MOBA_BLOCK - t.shape[1]
    return jnp.pad(t, ((0, 0), (0, pad), (0, 0), (0, 0)))


def block_means(k_pad):
    B, T, H, Dh = k_pad.shape
    kb = k_pad.reshape(B, T // MOBA_BLOCK, MOBA_BLOCK, H, Dh).astype(jnp.float32)
    return kb.mean(axis=2).astype(k_pad.dtype)


def moba_attend(q, q_pos, k_pad, v_pad, k_mean, t5_bias):
    B, Q, H, Dh = q.shape
    NB = k_mean.shape[1]
    n_sel = min(MOBA_TOPK, NB)
    own = (q_pos // MOBA_BLOCK).astype(jnp.int32)
    gate = jnp.einsum('bqhd,bnhd->bhqn', q, k_mean).astype(jnp.float32)
    gate = jnp.where(jnp.arange(NB)[None, :] < own[:, None], gate, -jnp.inf)
    _, sel = lax.top_k(gate, n_sel)
    sel_ok = jnp.arange(n_sel)[None, :] < own[:, None]
    blk = jnp.concatenate([sel.astype(jnp.int32),
                           jnp.broadcast_to(own[:, None], (B, H, Q, 1))], axis=-1)
    blk_ok = jnp.concatenate([jnp.broadcast_to(sel_ok, (B, H, Q, n_sel)),
                              jnp.ones((B, H, Q, 1), dtype=bool)], axis=-1)
    ns = blk.shape[-1]
    kb = k_pad.reshape(B, NB, MOBA_BLOCK, H, Dh)
    vb = v_pad.reshape(B, NB, MOBA_BLOCK, H, Dh)
    bi = jnp.arange(B)[:, None, None, None]
    hi = jnp.arange(H)[None, :, None, None]
    kg = kb[bi, blk, :, hi]
    vg = vb[bi, blk, :, hi]
    key_pos = blk[..., None] * MOBA_BLOCK + jnp.arange(MOBA_BLOCK, dtype=jnp.int32)
    rel = q_pos[:, None, None] - key_pos
    mask = blk_ok[..., None] & (rel >= 0)
    bias = t5_bias.T[hi[..., None], t5_bucket(rel)].astype(jnp.float32)
    s = jnp.einsum('bqhd,bhqnkd->bhqnk', q, kg).astype(jnp.float32) * HEAD_DIM ** -0.5 + bias
    p = masked_softmax(s.reshape(B, H, Q, ns * MOBA_BLOCK), mask.reshape(B, H, Q, ns * MOBA_BLOCK))
    return jnp.einsum('bhqm,bhqmd->bqhd', p.astype(vg.dtype),
                      vg.reshape(B, H, Q, ns * MOBA_BLOCK, Dh))


def moba_prompt(q, k, v, t5_bias):
    B, S, H, Dh = q.shape
    nb = -(-S // MOBA_BLOCK)
    k_pad = pad_to_blocks(k, nb)
    v_pad = pad_to_blocks(v, nb)
    k_mean = block_means(k_pad)
    nqc = S // MOBA_QCHUNK
    q_c = q.reshape(B, nqc, MOBA_QCHUNK, H, Dh).transpose(1, 0, 2, 3, 4)
    p_c = jnp.arange(S, dtype=jnp.int32).reshape(nqc, MOBA_QCHUNK)
    out = lax.map(lambda a: moba_attend(a[0], a[1], k_pad, v_pad, k_mean, t5_bias), (q_c, p_c))
    return out.transpose(1, 0, 2, 3, 4).reshape(B, S, H * Dh)


def moba_sample(q, k, v, cache_k, cache_v, page_table, t5_bias):
    DB, DS, H, Dh = q.shape
    past_k = cache_k[page_table].reshape(DB, -1, H, Dh)
    past_v = cache_v[page_table].reshape(DB, -1, H, Dh)
    past_len = past_k.shape[1]
    nb = -(-(past_len + DS) // MOBA_BLOCK)
    k_pad = pad_to_blocks(jnp.concatenate([past_k, k], axis=1), nb)
    v_pad = pad_to_blocks(jnp.concatenate([past_v, v], axis=1), nb)
    q_pos = past_len + jnp.arange(DS, dtype=jnp.int32)
    out = moba_attend(q, q_pos, k_pad, v_pad, block_means(k_pad), t5_bias)
    return out.reshape(DB, DS, H * Dh)


def merge_groups(o_fox, o_moba, g_fox, g_moba, w_out):
    o = jnp.concatenate([rmsnorm(o_fox, g_fox), rmsnorm(o_moba, g_moba)], axis=-1)
    return jnp.einsum('bse,ed->bsd', o, w_out)


def grouped_expert_mlp(h, expert_id, weight, w1, w3, w2):
    N, D = h.shape
    A = N * TOPK_IN_GROUP
    flat_e = expert_id.reshape(A).astype(jnp.int32)
    flat_tok = jnp.arange(A, dtype=jnp.int32) // TOPK_IN_GROUP
    flat_w = weight.reshape(A)
    order = jnp.argsort(flat_e)
    e_sorted = flat_e[order]
    tok_sorted = flat_tok[order]
    counts = jnp.bincount(flat_e, length=N_EXPERTS)
    start = jnp.cumsum(counts) - counts
    padded = (counts + MOE_ROW_BLOCK - 1) // MOE_ROW_BLOCK * MOE_ROW_BLOCK
    pend = jnp.cumsum(padded)
    pstart = pend - padded
    dest = pstart[e_sorted] + (jnp.arange(A, dtype=jnp.int32) - start[e_sorted])
    n_blocks = -(-(A + N_EXPERTS * (MOE_ROW_BLOCK - 1)) // MOE_ROW_BLOCK)
    rows = jnp.zeros((n_blocks * MOE_ROW_BLOCK, D), h.dtype).at[dest].set(h[tok_sorted])
    block_e = jnp.clip(jnp.searchsorted(pend, jnp.arange(n_blocks) * MOE_ROW_BLOCK, side='right'),
                       0, N_EXPERTS - 1)

    def block_mlp(a):
        xb, e = a
        return (jax.nn.silu(xb @ w1[e]) * (xb @ w3[e])) @ w2[e]

    y = lax.map(block_mlp, (rows.reshape(n_blocks, MOE_ROW_BLOCK, D), block_e)).reshape(-1, D)
    contrib = y[dest] * flat_w[order][:, None].astype(y.dtype)
    return jnp.zeros((N, D), y.dtype).at[tok_sorted].add(contrib)


def hier_moe(h, w_rg, b_rg, w_re, b_re, w1, w3, w2):
    N = h.shape[0]
    hf = h.astype(jnp.float32)
    g_logits = jnp.einsum('nd,dg->ng', hf, w_rg.astype(jnp.float32)) + b_rg.astype(jnp.float32)
    g_prob = jax.nn.softmax(g_logits, axis=-1)
    g_idx = jnp.argmax(g_logits, axis=-1)
    g_w = jnp.take_along_axis(g_prob, g_idx[:, None], axis=-1)
    e_logits = (jnp.einsum('nd,de->ne', hf, w_re.astype(jnp.float32)) + b_re.astype(jnp.float32))
    e_logits = jnp.take_along_axis(e_logits.reshape(N, N_GROUPS, EXPERTS_PER_GROUP),
                                   g_idx[:, None, None], axis=1)[:, 0]
    top_p, top_i = lax.top_k(jax.nn.softmax(e_logits, axis=-1), TOPK_IN_GROUP)
    top_p = top_p / jnp.sum(top_p, axis=-1, keepdims=True)
    expert_id = g_idx[:, None].astype(jnp.int32) * EXPERTS_PER_GROUP + top_i
    return grouped_expert_mlp(h, expert_id, g_w * top_p, w1, w3, w2)


def ffn_sublayer(x, shift, scale, gate, g, w_rg, b_rg, w_re, b_re, w1, w3, w2):
    B, S, D = x.shape
    h = modulate(rmsnorm(x, g), shift, scale)
    y = hier_moe(h.reshape(B * S, D), w_rg, b_rg, w_re, b_re, w1, w3, w2).reshape(B, S, D)
    return x + gate * y


def setup_inputs(seed: int = 0) -> dict:
    key = jax.random.key(seed)
    ks = jax.random.split(key, 32)
    D = D_MODEL
    n_pages = PAST_LEN // PAGE_SIZE
    n_used = DEC_BATCH * n_pages
    n_phys = (5 * n_used + 3) // 4
    nrm = lambda k, shape, s=1.0: s * jax.random.normal(k, shape, jnp.float32)
    page_table = jax.random.permutation(ks[7], n_phys)[:n_used].reshape(DEC_BATCH, n_pages).astype(jnp.int32)
    b_forget = jnp.broadcast_to(jnp.linspace(1.0, 5.0, N_FOX_HEADS), (DEPTH, N_FOX_HEADS)) \
        + nrm(ks[14], (DEPTH, N_FOX_HEADS), 0.1)
    return {
        'x_prompt': nrm(ks[0], (BATCH, SEQ, D)),
        'x_sample': nrm(ks[1], (DEC_BATCH, DEC_SEQ, D)),
        'cache_fox_k': nrm(ks[2], (DEPTH, n_phys, PAGE_SIZE, N_FOX_HEADS, HEAD_DIM)),
        'cache_fox_v': nrm(ks[3], (DEPTH, n_phys, PAGE_SIZE, N_FOX_HEADS, HEAD_DIM)),
        'cache_fox_logf': jax.nn.log_sigmoid(3.0 + nrm(ks[4], (DEPTH, n_phys, PAGE_SIZE, N_FOX_HEADS))),
        'cache_moba_k': nrm(ks[5], (DEPTH, n_phys, PAGE_SIZE, N_MOBA_HEADS, HEAD_DIM)),
        'cache_moba_v': nrm(ks[6], (DEPTH, n_phys, PAGE_SIZE, N_MOBA_HEADS, HEAD_DIM)),
        'page_table': page_table,
        'c_prompt': nrm(ks[8], (BATCH, D)),
        'c_sample': nrm(ks[9], (DEC_BATCH, D)),
        'w_ada': nrm(ks[10], (DEPTH, D, 6 * D), 0.5 * D ** -0.5),
        'b_ada': nrm(ks[11], (DEPTH, 6 * D), 0.02),
        'g_attn': 1.0 + nrm(ks[12], (DEPTH, D), 0.02),
        'w_in': nrm(ks[13], (DEPTH, D, IN_WIDTH), D ** -0.5),
        'b_forget': b_forget,
        'g_out_fox': 1.0 + nrm(ks[15], (DEPTH, FOX_WIDTH), 0.02),
        'g_out_moba': 1.0 + nrm(ks[16], (DEPTH, MOBA_WIDTH), 0.02),
        't5_bias': nrm(ks[17], (T5_BUCKETS, N_MOBA_HEADS), 0.3),
        'w_out': nrm(ks[18], (DEPTH, D, D), D ** -0.5),
        'g_ffn': 1.0 + nrm(ks[19], (DEPTH, D), 0.02),
        'w_router_group': nrm(ks[20], (DEPTH, D, N_GROUPS), D ** -0.5),
        'b_router_group': nrm(ks[21], (DEPTH, N_GROUPS), 0.01),
        'w_router_expert': nrm(ks[22], (DEPTH, D, N_EXPERTS), D ** -0.5),
        'b_router_expert': nrm(ks[23], (DEPTH, N_EXPERTS), 0.01),
        'w1': nrm(ks[24], (DEPTH, N_EXPERTS, D, D_EXPERT), D ** -0.5),
        'w3': nrm(ks[25], (DEPTH, N_EXPERTS, D, D_EXPERT), D ** -0.5),
        'w2': nrm(ks[26], (DEPTH, N_EXPERTS, D_EXPERT, D), D_EXPERT ** -0.5),
        'g_final': 1.0 + nrm(ks[27], (D,), 0.02),
    }


def reference(x_prompt, x_sample, cache_fox_k, cache_fox_v, cache_fox_logf, cache_moba_k, cache_moba_v,
              page_table, c_prompt, c_sample, w_ada, b_ada, g_attn, w_in, b_forget, g_out_fox, g_out_moba,
              t5_bias, w_out, g_ffn, w_router_group, b_router_group, w_router_expert, b_router_expert,
              w1, w3, w2, g_final):
    xp, xs = x_prompt, x_sample
    p_fk, p_fv, p_fl, p_mk, p_mv = [], [], [], [], []
    s_fk, s_fv, s_fl, s_mk, s_mv = [], [], [], [], []
    for l in range(DEPTH):
        mp = ada_modulation(c_prompt, w_ada[l], b_ada[l])
        ms = ada_modulation(c_sample, w_ada[l], b_ada[l])
        hp = modulate(rmsnorm(xp, g_attn[l]), mp[0], mp[1])
        fq, fk, fv, fl, mq, mk, mv = project(hp, w_in[l], b_forget[l])
        o = merge_groups(fox_prompt(fq, fk, fv, fl), moba_prompt(mq, mk, mv, t5_bias),
                         g_out_fox[l], g_out_moba[l], w_out[l])
        xp = xp + mp[2] * o
        p_fk.append(fk); p_fv.append(fv); p_fl.append(fl); p_mk.append(mk); p_mv.append(mv)
        hs = modulate(rmsnorm(xs, g_attn[l]), ms[0], ms[1])
        fq, fk, fv, fl, mq, mk, mv = project(hs, w_in[l], b_forget[l])
        o_fox = fox_sample(fq, fk, fv, fl, cache_fox_k[l], cache_fox_v[l], cache_fox_logf[l], page_table)
        o_moba = moba_sample(mq, mk, mv, cache_moba_k[l], cache_moba_v[l], page_table, t5_bias)
        xs = xs + ms[2] * merge_groups(o_fox, o_moba, g_out_fox[l], g_out_moba[l], w_out[l])
        s_fk.append(fk); s_fv.append(fv); s_fl.append(fl); s_mk.append(mk); s_mv.append(mv)
        xp = ffn_sublayer(xp, mp[3], mp[4], mp[5], g_ffn[l], w_router_group[l], b_router_group[l],
                          w_router_expert[l], b_router_expert[l], w1[l], w3[l], w2[l])
        xs = ffn_sublayer(xs, ms[3], ms[4], ms[5], g_ffn[l], w_router_group[l], b_router_group[l],
                          w_router_expert[l], b_router_expert[l], w1[l], w3[l], w2[l])
    y_prompt = rmsnorm(xp, g_final)
    y_sample = rmsnorm(xs, g_final)
    new_fox_k_prompt = jnp.stack(p_fk)
    new_fox_v_prompt = jnp.stack(p_fv)
    new_fox_logf_prompt = jnp.stack(p_fl)
    new_moba_k_prompt = jnp.stack(p_mk)
    new_moba_v_prompt = jnp.stack(p_mv)
    new_fox_k_sample = jnp.stack(s_fk)
    new_fox_v_sample = jnp.stack(s_fv)
    new_fox_logf_sample = jnp.stack(s_fl)
    new_moba_k_sample = jnp.stack(s_mk)
    new_moba_v_sample = jnp.stack(s_mv)
    return (y_prompt, y_sample, new_fox_k_prompt, new_fox_v_prompt, new_fox_logf_prompt,
            new_moba_k_prompt, new_moba_v_prompt, new_fox_k_sample, new_fox_v_sample,
            new_fox_logf_sample, new_moba_k_sample, new_moba_v_sample)
```

```python
import functools
import math

import jax
import jax.numpy as jnp
from jax import lax
from jax.experimental import pallas as pl
from jax.experimental.pallas import tpu as pltpu

F32 = jnp.float32
BF16 = jnp.bfloat16
I32 = jnp.int32

HEAD_DIM = 64
N_HEADS = 8
GROUP_W = N_HEADS * HEAD_DIM
LANES = 128
HEADS_PER_LANE_TILE = LANES // HEAD_DIM
N_PAIRS = N_HEADS // HEADS_PER_LANE_TILE
SM_SCALE = HEAD_DIM ** -0.5
MOBA_BLOCK = 256
MOBA_TOPK = 3
T5_BUCKETS = 32
T5_MAX_DIST = 128
N_GROUPS = 4
EXPERTS_PER_GROUP = 8
N_EXPERTS = N_GROUPS * EXPERTS_PER_GROUP
RMS_EPS = 1e-6
ROW_TILE = 256
MOE_BLOCK = 256
PAGES_PER_STEP = 4
NEG = -1e30
VMEM_LIMIT = 48 * 1024 * 1024
HIGHEST = lax.Precision.HIGHEST
NT = (((1,), (1,)), ((), ()))


def _cparams(*sem):
    return pltpu.CompilerParams(dimension_semantics=sem, vmem_limit_bytes=VMEM_LIMIT)


def _rms(x):
    return x * lax.rsqrt(jnp.mean(x * x, axis=-1, keepdims=True) + RMS_EPS)


def _split3_bf16(x):
    hi = x.astype(BF16)
    r1 = x - hi.astype(F32)
    mid = r1.astype(BF16)
    lo = (r1 - mid.astype(F32)).astype(BF16)
    return hi, mid, lo


def _ada_kernel(c_ref, w_ref, b_ref, o_ref):
    c = c_ref[...]
    a = c * jax.nn.sigmoid(c)
    o_ref[...] = jnp.dot(a, w_ref[...], preferred_element_type=F32, precision=HIGHEST) + b_ref[...]


def _ada(c, w, b):
    n, d = c.shape
    e = w.shape[1]
    tn = 1024
    return pl.pallas_call(
        _ada_kernel,
        grid=(e // tn,),
        in_specs=[pl.BlockSpec((n, d), lambda j: (0, 0)),
                  pl.BlockSpec((d, tn), lambda j: (0, j)),
                  pl.BlockSpec((1, tn), lambda j: (0, j))],
        out_specs=pl.BlockSpec((n, tn), lambda j: (0, j)),
        out_shape=jax.ShapeDtypeStruct((n, e), F32),
        compiler_params=_cparams("arbitrary"),
        name="ada",
    )(c, w, b.reshape(1, e))


def _inproj_kernel(x_ref, shift_ref, scale_ref, g_ref, w_ref, bf_ref,
                   fq_ref, fk_ref, fv_ref, lf_ref, mq_ref, mk_ref, mv_ref, *rest,
                   prompt_extras, tiles_per_seq):
    x = x_ref[...]
    h = _rms(x) * g_ref[...]
    h = h * (1.0 + scale_ref[...]) + shift_ref[...]
    z = jnp.dot(h.astype(BF16), w_ref[...], preferred_element_type=F32)
    w = GROUP_W
    fq_ref[...] = (z[:, 0:w] * SM_SCALE).astype(BF16)
    fk_ref[...] = z[:, w:2 * w]
    fv_ref[...] = z[:, 2 * w:3 * w]
    mq_ref[...] = (z[:, 3 * w:4 * w] * SM_SCALE).astype(BF16)
    mk = z[:, 4 * w:5 * w]
    mk_ref[...] = mk
    mv_ref[...] = z[:, 5 * w:6 * w]
    fg = z[:, 6 * w:6 * w + LANES] + bf_ref[...]
    lf = jnp.minimum(fg, 0.0) - jnp.log1p(jnp.exp(-jnp.abs(fg)))
    lf_ref[...] = lf[:, :N_HEADS]
    if prompt_extras:
        fcum_ref, kmean_ref, carry_ref = rest
        tm = x.shape[0]
        kmean_ref[0] = jnp.mean(mk, axis=0, keepdims=True)

        @pl.when(pl.program_id(0) % tiles_per_seq == 0)
        def _():
            carry_ref[...] = jnp.zeros_like(carry_ref)

        r = lax.broadcasted_iota(I32, (tm, tm), 0)
        c = lax.broadcasted_iota(I32, (tm, tm), 1)
        tri = jnp.where(c <= r, 1.0, 0.0).astype(BF16)
        hi, mid, lo = _split3_bf16(lf)
        cs = (jnp.dot(tri, hi, preferred_element_type=F32)
              + jnp.dot(tri, mid, preferred_element_type=F32)
              + jnp.dot(tri, lo, preferred_element_type=F32)) + carry_ref[...]
        fcum_ref[...] = cs[:, :N_HEADS]
        carry_ref[...] = cs[tm - 1:tm, :]


def _inproj(x2, shift3, scale3, g_attn, wcat, bf_pad, *, rows_per_mod, prompt_extras, seq_len):
    n, d = x2.shape
    tm = ROW_TILE
    assert n % tm == 0
    nt = n // tm
    mod_rows = shift3.shape[1]
    mod_map = lambda t: ((t * tm) // rows_per_mod, 0, 0)
    row_map = lambda t: (t, 0)
    const2 = lambda t: (0, 0)
    ecols = wcat.shape[1]
    out_shape = [jax.ShapeDtypeStruct((n, GROUP_W), BF16), jax.ShapeDtypeStruct((n, GROUP_W), F32),
                 jax.ShapeDtypeStruct((n, GROUP_W), F32), jax.ShapeDtypeStruct((n, N_HEADS), F32),
                 jax.ShapeDtypeStruct((n, GROUP_W), BF16), jax.ShapeDtypeStruct((n, GROUP_W), F32),
                 jax.ShapeDtypeStruct((n, GROUP_W), F32)]
    out_specs = [pl.BlockSpec((tm, GROUP_W), row_map), pl.BlockSpec((tm, GROUP_W), row_map),
                 pl.BlockSpec((tm, GROUP_W), row_map), pl.BlockSpec((tm, N_HEADS), row_map),
                 pl.BlockSpec((tm, GROUP_W), row_map), pl.BlockSpec((tm, GROUP_W), row_map),
                 pl.BlockSpec((tm, GROUP_W), row_map)]
    scratch = []
    tiles_per_seq = 1
    if prompt_extras:
        assert tm == MOBA_BLOCK and seq_len % tm == 0
        tiles_per_seq = seq_len // tm
        out_shape += [jax.ShapeDtypeStruct((n, N_HEADS), F32), jax.ShapeDtypeStruct((nt, 1, GROUP_W), F32)]
        out_specs += [pl.BlockSpec((tm, N_HEADS), row_map), pl.BlockSpec((1, 1, GROUP_W), lambda t: (t, 0, 0))]
        scratch = [pltpu.VMEM((1, LANES), F32)]
    return pl.pallas_call(
        functools.partial(_inproj_kernel, prompt_extras=prompt_extras, tiles_per_seq=tiles_per_seq),
        grid=(nt,),
        in_specs=[pl.BlockSpec((tm, d), row_map),
                  pl.BlockSpec((None, mod_rows, d), mod_map),
                  pl.BlockSpec((None, mod_rows, d), mod_map),
                  pl.BlockSpec((1, d), const2),
                  pl.BlockSpec((d, ecols), const2),
                  pl.BlockSpec((1, LANES), const2)],
        out_specs=out_specs,
        out_shape=out_shape,
        scratch_shapes=scratch,
        compiler_params=_cparams("arbitrary"),
        name="inproj_prompt" if prompt_extras else "inproj_sample",
    )(x2, shift3, scale3, g_attn, wcat, bf_pad)


def _t5_kernel(t5_ref, o_ref, *, offs):
    h = pl.program_id(0)
    rows, cols = o_ref.shape[2], o_ref.shape[3]
    r = lax.broadcasted_iota(I32, (rows, cols), 0)
    c = lax.broadcasted_iota(I32, (rows, cols), 1)
    max_exact = T5_BUCKETS // 2
    for k, off in enumerate(offs):
        rel = jnp.maximum(off + r - c, 0)
        relf = jnp.maximum(rel, 1).astype(F32)
        large = max_exact + (jnp.log(relf / max_exact) / math.log(T5_MAX_DIST / max_exact)
                             * (T5_BUCKETS - max_exact)).astype(I32)
        large = jnp.minimum(large, T5_BUCKETS - 1)
        bucket = jnp.where(rel < max_exact, rel, large)
        acc = jnp.zeros((rows, cols), F32)
        for b in range(T5_BUCKETS):
            acc = jnp.where(bucket == b, t5_ref[b, h], acc)
        o_ref[0, k] = acc


def _t5_tiles(t5_bias, offs, rows, cols):
    nh = t5_bias.shape[1]
    return pl.pallas_call(
        functools.partial(_t5_kernel, offs=tuple(offs)),
        grid=(nh,),
        in_specs=[pl.BlockSpec(memory_space=pltpu.SMEM)],
        out_specs=pl.BlockSpec((1, len(offs), rows, cols), lambda h: (h, 0, 0, 0)),
        out_shape=jax.ShapeDtypeStruct((nh, len(offs), rows, cols), F32),
        compiler_params=_cparams("arbitrary"),
        name="t5_tiles",
    )(t5_bias)


def _pair_masks(rows):
    lane = lax.broadcasted_iota(I32, (rows, LANES), 1)
    lo = lane < HEAD_DIM
    return lo, jnp.logical_not(lo)


def _online_softmax_step(s, vp, m_ref, l_ref, h):
    m_prev = m_ref[h]
    m_new = jnp.maximum(m_prev, jnp.max(s, axis=1, keepdims=True))
    alpha = jnp.exp(m_prev - m_new)
    p = jnp.exp(s - m_new[:, 0:1])
    l_ref[h] = alpha * l_ref[h] + jnp.sum(p, axis=1, keepdims=True)
    m_ref[h] = m_new
    return alpha, jnp.dot(p.astype(BF16), vp, preferred_element_type=F32)


def _fox_prompt_kernel(qi_ref, kj_ref, q_ref, k_ref, v_ref, fq_ref, fk_ref, o_ref, m_ref, l_ref, acc_ref):
    t = pl.program_id(1)
    i = qi_ref[t]
    j = kj_ref[t]
    tq, tk = q_ref.shape[0], k_ref.shape[0]

    @pl.when(j == 0)
    def _():
        m_ref[...] = jnp.full_like(m_ref, NEG)
        l_ref[...] = jnp.zeros_like(l_ref)
        acc_ref[...] = jnp.zeros_like(acc_ref)

    lo, hi = _pair_masks(tq)
    row = lax.broadcasted_iota(I32, (tq, tk), 0)
    col = lax.broadcasted_iota(I32, (tq, tk), 1)
    visible = jnp.logical_or(col <= row, j < i)
    for p in range(N_PAIRS):
        sl = slice(p * LANES, (p + 1) * LANES)
        qp = q_ref[:, sl]
        kp = k_ref[:, sl].astype(BF16)
        vp = v_ref[:, sl].astype(BF16)
        alphas, pvs = [], []
        for e, keep in enumerate((lo, hi)):
            h = HEADS_PER_LANE_TILE * p + e
            qm = jnp.where(keep, qp, jnp.zeros_like(qp))
            s = lax.dot_general(qm, kp, NT, preferred_element_type=F32)
            s = s + fq_ref[:, h:h + 1] - fk_ref[0, h:h + 1, :]
            s = jnp.where(visible, s, NEG)
            alpha, pv = _online_softmax_step(s, vp, m_ref, l_ref, h)
            alphas.append(alpha)
            pvs.append(pv)
        acc_ref[:, sl] = jnp.where(lo, alphas[0], alphas[1]) * acc_ref[:, sl] + jnp.where(lo, pvs[0], pvs[1])

    @pl.when(j == i)
    def _():
        for p in range(N_PAIRS):
            sl = slice(p * LANES, (p + 1) * LANES)
            l_pair = jnp.where(lo, l_ref[HEADS_PER_LANE_TILE * p], l_ref[HEADS_PER_LANE_TILE * p + 1])
            o_ref[:, sl] = acc_ref[:, sl] / l_pair


def _tri_tables(nq, own_first):
    qi, kj = [], []
    for i in range(nq):
        order = ([i] + list(range(i))) if own_first else list(range(i + 1))
        for j in order:
            qi.append(i)
            kj.append(j)
    return jnp.asarray(qi, I32), jnp.asarray(kj, I32)


def _fox_prompt(q, k, v, fcol, frow, batch, seq):
    tq = tk = 256
    nq = seq // tq
    qi, kj = _tri_tables(nq, own_first=False)
    qmap = lambda b, t, qi, kj: (b * nq + qi[t], 0)
    kmap = lambda b, t, qi, kj: (b * nq + kj[t], 0)
    return pl.pallas_call(
        _fox_prompt_kernel,
        grid_spec=pltpu.PrefetchScalarGridSpec(
            num_scalar_prefetch=2,
            grid=(batch, qi.shape[0]),
            in_specs=[pl.BlockSpec((tq, GROUP_W), qmap),
                      pl.BlockSpec((tk, GROUP_W), kmap),
                      pl.BlockSpec((tk, GROUP_W), kmap),
                      pl.BlockSpec((tq, N_HEADS), qmap),
                      pl.BlockSpec((1, N_HEADS, tk), lambda b, t, qi, kj: (b, 0, kj[t]))],
            out_specs=pl.BlockSpec((tq, GROUP_W), qmap),
            scratch_shapes=[pltpu.VMEM((N_HEADS, tq, LANES), F32),
                            pltpu.VMEM((N_HEADS, tq, LANES), F32),
                            pltpu.VMEM((tq, GROUP_W), F32)]),
        out_shape=jax.ShapeDtypeStruct((batch * seq, GROUP_W), F32),
        compiler_params=_cparams("arbitrary", "arbitrary"),
        name="fox_prompt",
    )(qi, kj, q, k, v, fcol, frow)


def _moba_prompt_kernel(qi_ref, kj_ref, q_ref, k_ref, v_ref, kmean_ref, bias_ref, o_ref,
                        m_ref, l_ref, acc_ref, sel_ref):
    t = pl.program_id(1)
    i = qi_ref[t]
    j = kj_ref[t]
    tq, tk = q_ref.shape[0], k_ref.shape[0]
    nb = kmean_ref.shape[1]
    first = j == i
    lo, hi = _pair_masks(tq)
    lane = lax.broadcasted_iota(I32, (tq, LANES), 1)

    @pl.when(first)
    def _():
        m_ref[...] = jnp.full_like(m_ref, NEG)
        l_ref[...] = jnp.zeros_like(l_ref)
        acc_ref[...] = jnp.zeros_like(acc_ref)
        past = lane < i
        for p in range(N_PAIRS):
            sl = slice(p * LANES, (p + 1) * LANES)
            qp = q_ref[:, sl]
            km = jnp.concatenate([kmean_ref[0, :, sl], jnp.zeros((LANES - nb, LANES), F32)], axis=0).astype(BF16)
            for e, keep in enumerate((lo, hi)):
                h = HEADS_PER_LANE_TILE * p + e
                qm = jnp.where(keep, qp, jnp.zeros_like(qp))
                g = lax.dot_general(qm, km, NT, preferred_element_type=F32)
                sel = jnp.zeros((tq, LANES), F32)
                for jb in range(nb):
                    gj = g[:, jb:jb + 1]
                    beats = jnp.logical_and(past, jnp.logical_or(g > gj, jnp.logical_and(g == gj, lane < jb)))
                    rank = jnp.sum(jnp.where(beats, 1.0, 0.0), axis=1, keepdims=True)
                    chosen = jnp.logical_and(rank < MOBA_TOPK, jb < i)
                    sel = jnp.where(jnp.logical_and(lane == jb, chosen), 1.0, sel)
                sel_ref[h] = sel

    row = lax.broadcasted_iota(I32, (tq, tk), 0)
    col = lax.broadcasted_iota(I32, (tq, tk), 1)
    causal = jnp.where(col <= row, 1.0, 0.0)
    bias_k = jnp.clip(i - j, 0, bias_ref.shape[1] - 1)
    for p in range(N_PAIRS):
        sl = slice(p * LANES, (p + 1) * LANES)
        qp = q_ref[:, sl]
        kp = k_ref[:, sl].astype(BF16)
        vp = v_ref[:, sl].astype(BF16)
        alphas, pvs = [], []
        for e, keep in enumerate((lo, hi)):
            h = HEADS_PER_LANE_TILE * p + e
            qm = jnp.where(keep, qp, jnp.zeros_like(qp))
            s = lax.dot_general(qm, kp, NT, preferred_element_type=F32) + bias_ref[h, bias_k]
            picked = jnp.sum(jnp.where(lane == j, sel_ref[h], 0.0), axis=1, keepdims=True)
            ok = jnp.where(first, causal, picked)
            s = jnp.where(ok > 0.5, s, NEG)
            alpha, pv = _online_softmax_step(s, vp, m_ref, l_ref, h)
            alphas.append(alpha)
            pvs.append(pv)
        acc_ref[:, sl] = jnp.where(lo, alphas[0], alphas[1]) * acc_ref[:, sl] + jnp.where(lo, pvs[0], pvs[1])

    last = jnp.logical_or(j == i - 1, i == 0)

    @pl.when(last)
    def _():
        for p in range(N_PAIRS):
            sl = slice(p * LANES, (p + 1) * LANES)
            l_pair = jnp.where(lo, l_ref[HEADS_PER_LANE_TILE * p], l_ref[HEADS_PER_LANE_TILE * p + 1])
            o_ref[:, sl] = acc_ref[:, sl] / l_pair


def _moba_prompt(q, k, v, kmean, bias, batch, seq):
    tq = tk = MOBA_BLOCK
    nq = seq // tq
    qi, kj = _tri_tables(nq, own_first=True)
    qmap = lambda b, t, qi, kj: (b * nq + qi[t], 0)
    kmap = lambda b, t, qi, kj: (b * nq + kj[t], 0)
    return pl.pallas_call(
        _moba_prompt_kernel,
        grid_spec=pltpu.PrefetchScalarGridSpec(
            num_scalar_prefetch=2,
            grid=(batch, qi.shape[0]),
            in_specs=[pl.BlockSpec((tq, GROUP_W), qmap),
                      pl.BlockSpec((tk, GROUP_W), kmap),
                      pl.BlockSpec((tk, GROUP_W), kmap),
                      pl.BlockSpec((1, nq, GROUP_W), lambda b, t, qi, kj: (b, 0, 0)),
                      pl.BlockSpec(bias.shape, lambda b, t, qi, kj: (0, 0, 0, 0))],
            out_specs=pl.BlockSpec((tq, GROUP_W), qmap),
            scratch_shapes=[pltpu.VMEM((N_HEADS, tq, LANES), F32),
                            pltpu.VMEM((N_HEADS, tq, LANES), F32),
                            pltpu.VMEM((tq, GROUP_W), F32),
                            pltpu.VMEM((N_HEADS, tq, LANES), F32)]),
        out_shape=jax.ShapeDtypeStruct((batch * seq, GROUP_W), F32),
        compiler_params=_cparams("arbitrary", "arbitrary"),
        name="moba_prompt",
    )(qi, kj, q, k, v, kmean, bias)


def _block_diag_q(q):
    ds = q.shape[0]
    lane_head = lax.broadcasted_iota(I32, (N_HEADS, ds, GROUP_W), 2) // HEAD_DIM
    head = lax.broadcasted_iota(I32, (N_HEADS, ds, GROUP_W), 0)
    q3 = jnp.where(lane_head == head, q.astype(F32)[None, :, :], 0.0)
    return q3.reshape(N_HEADS * ds, GROUP_W).astype(BF16)


def _head_diag_out(acc, ds):
    acc3 = acc.reshape(N_HEADS, ds, GROUP_W)
    lane_head = lax.broadcasted_iota(I32, (N_HEADS, ds, GROUP_W), 2) // HEAD_DIM
    head = lax.broadcasted_iota(I32, (N_HEADS, ds, GROUP_W), 0)
    return jnp.sum(jnp.where(lane_head == head, acc3, 0.0), axis=0)


def _pad_rows(x, rows):
    return jnp.concatenate([x, jnp.zeros((rows - x.shape[0], x.shape[1]), x.dtype)], axis=0)


def _fox_sample_kernel(pt_ref, q_ref, kn_ref, vn_ref, lfn_ref, *refs, pages):
    k_refs = refs[0:pages]
    v_refs = refs[pages:2 * pages]
    lf_refs = refs[2 * pages:3 * pages]
    o_ref, qbd_ref, m_ref, l_ref, acc_ref, carry_ref, rq_ref = refs[3 * pages:]
    g = pl.program_id(1)
    ds = q_ref.shape[1]
    page = k_refs[0].shape[1]
    lane3 = lax.broadcasted_iota(I32, (N_HEADS, ds, page), 2)
    qidx3 = lax.broadcasted_iota(I32, (N_HEADS, ds, page), 1)
    r = lax.broadcasted_iota(I32, (page, page), 0)
    c = lax.broadcasted_iota(I32, (page, page), 1)
    after = jnp.where(r > c, 1.0, 0.0).astype(BF16)

    def suffix_sum(x):
        hi, mid, lo = _split3_bf16(x)
        return (jnp.dot(hi, after, preferred_element_type=F32) + jnp.dot(mid, after, preferred_element_type=F32)
                + jnp.dot(lo, after, preferred_element_type=F32))

    def process(kpage, vpage, later, mask_new):
        s = lax.dot_general(qbd_ref[...], kpage.astype(BF16), NT, preferred_element_type=F32)
        s3 = s.reshape(N_HEADS, ds, page) + later[:, None, :] - rq_ref[...].reshape(N_HEADS, ds, 1)
        if mask_new:
            s3 = jnp.where(lane3 <= qidx3, s3, NEG)
        s = s3.reshape(N_HEADS * ds, page)
        m_prev = m_ref[...]
        m_new = jnp.maximum(m_prev, jnp.max(s, axis=1, keepdims=True))
        alpha = jnp.exp(m_prev - m_new)
        p = jnp.exp(s - m_new)
        l_ref[...] = alpha * l_ref[...] + jnp.sum(p, axis=1, keepdims=True)
        m_ref[...] = m_new
        acc_ref[...] = alpha * acc_ref[...] + jnp.dot(p.astype(BF16), vpage.astype(BF16),
                                                      preferred_element_type=F32)

    @pl.when(g == 0)
    def _():
        qbd_ref[...] = _block_diag_q(q_ref[0])
        m_ref[...] = jnp.full_like(m_ref, NEG)
        l_ref[...] = jnp.zeros_like(l_ref)
        acc_ref[...] = jnp.zeros_like(acc_ref)
        x = lfn_ref[0]
        later_new = suffix_sum(x)
        rq3 = jnp.sum(jnp.where(lane3 == qidx3, later_new[:, None, :], 0.0), axis=2, keepdims=True)
        rq_ref[...] = rq3.reshape(N_HEADS * ds, 1)
        process(_pad_rows(kn_ref[0], page), _pad_rows(vn_ref[0], page), later_new, True)
        carry_ref[...] = jnp.sum(x, axis=1, keepdims=True)

    for rr in range(pages):
        x = lf_refs[rr][0]
        later = suffix_sum(x) + carry_ref[...]
        process(k_refs[rr][0], v_refs[rr][0], later, False)
        carry_ref[...] = carry_ref[...] + jnp.sum(x, axis=1, keepdims=True)

    @pl.when(g == pl.num_programs(1) - 1)
    def _():
        o_ref[0] = _head_diag_out(acc_ref[...] / l_ref[...], ds)


def _fox_sample(page_table, q3, kn3, vn3, lfn_t, cache_k, cache_v, cache_lf_t):
    db, n_pages = page_table.shape
    ds = q3.shape[1]
    pages = PAGES_PER_STEP
    assert n_pages % pages == 0
    ng = n_pages // pages
    page = cache_k.shape[1]

    def page_map(rr):
        return lambda b, g, pt: (pt[b * n_pages + (ng - 1 - g) * pages + (pages - 1 - rr)], 0, 0)

    bmap = lambda b, g, pt: (b, 0, 0)
    in_specs = [pl.BlockSpec((1, ds, GROUP_W), bmap), pl.BlockSpec((1, ds, GROUP_W), bmap),
                pl.BlockSpec((1, ds, GROUP_W), bmap), pl.BlockSpec((1, N_HEADS, page), bmap)]
    in_specs += [pl.BlockSpec((1, page, GROUP_W), page_map(rr)) for rr in range(pages)]
    in_specs += [pl.BlockSpec((1, page, GROUP_W), page_map(rr)) for rr in range(pages)]
    in_specs += [pl.BlockSpec((1, N_HEADS, page), page_map(rr)) for rr in range(pages)]
    rows = N_HEADS * ds
    return pl.pallas_call(
        functools.partial(_fox_sample_kernel, pages=pages),
        grid_spec=pltpu.PrefetchScalarGridSpec(
            num_scalar_prefetch=1,
            grid=(db, ng),
            in_specs=in_specs,
            out_specs=pl.BlockSpec((1, ds, GROUP_W), bmap),
            scratch_shapes=[pltpu.VMEM((rows, GROUP_W), BF16), pltpu.VMEM((rows, 1), F32),
                            pltpu.VMEM((rows, 1), F32), pltpu.VMEM((rows, GROUP_W), F32),
                            pltpu.VMEM((N_HEADS, 1), F32), pltpu.VMEM((rows, 1), F32)]),
        out_shape=jax.ShapeDtypeStruct((db, ds, GROUP_W), F32),
        compiler_params=_cparams("arbitrary", "arbitrary"),
        name="fox_sample",
    )(page_table.reshape(-1), q3, kn3, vn3, lfn_t, *([cache_k] * pages), *([cache_v] * pages),
      *([cache_lf_t] * pages))


def _moba_sample_kernel(pt_ref, q_ref, kn_ref, vn_ref, bias_ref, *refs, pages):
    k_refs = refs[0:pages]
    v_refs = refs[pages:2 * pages]
    o_ref, qbd_ref, s_ref, p_ref, pnew_ref, linv_ref, acc_ref = refs[2 * pages:]
    ph = pl.program_id(1)
    g = pl.program_id(2)
    ng = pl.num_programs(2)
    ds = q_ref.shape[1]
    page = k_refs[0].shape[1]
    n_pages = s_ref.shape[0]
    rows = N_HEADS * ds
    pages_per_block = MOBA_BLOCK // page
    n_blocks = n_pages // pages_per_block

    @pl.when(jnp.logical_and(ph == 0, g == 0))
    def _():
        qbd_ref[...] = _block_diag_q(q_ref[0])

    @pl.when(ph == 0)
    def _():
        for rr in range(pages):
            s_ref[g * pages + rr] = lax.dot_general(qbd_ref[...], k_refs[rr][0].astype(BF16), NT,
                                                    preferred_element_type=F32)

    @pl.when(jnp.logical_and(ph == 0, g == ng - 1))
    def _():
        lane = lax.broadcasted_iota(I32, (rows, LANES), 1)
        gate = jnp.full((rows, LANES), -jnp.inf, F32)
        for b in range(n_blocks):
            tot = s_ref[b * pages_per_block]
            for u in range(1, pages_per_block):
                tot = tot + s_ref[b * pages_per_block + u]
            gate = jnp.where(lane == b, jnp.sum(tot, axis=1, keepdims=True) * (1.0 / MOBA_BLOCK), gate)
        sel = jnp.zeros((rows, LANES), F32)
        for _ in range(min(MOBA_TOPK, n_blocks)):
            mx = jnp.max(gate, axis=1, keepdims=True)
            idx = jnp.min(jnp.where(gate == mx, lane.astype(F32), float(LANES)), axis=1, keepdims=True)
            pick = lane.astype(F32) == idx
            sel = jnp.where(pick, 1.0, sel)
            gate = jnp.where(pick, -jnp.inf, gate)
        lane_n = lax.broadcasted_iota(I32, (N_HEADS, ds, page), 2)
        qidx_n = lax.broadcasted_iota(I32, (N_HEADS, ds, page), 1)
        s_new = lax.dot_general(qbd_ref[...], _pad_rows(kn_ref[0], page).astype(BF16), NT,
                                preferred_element_type=F32) + bias_ref[1]
        s_new = jnp.where((lane_n <= qidx_n).reshape(rows, page), s_new, NEG)
        m = jnp.max(s_new, axis=1, keepdims=True)
        for pg in range(n_pages):
            b = pg // pages_per_block
            picked = jnp.sum(jnp.where(lane == b, sel, 0.0), axis=1, keepdims=True)
            bias = bias_ref[0] if pg == n_pages - 1 else bias_ref[2]
            s = jnp.where(picked > 0.5, s_ref[pg] + bias, NEG)
            s_ref[pg] = s
            m = jnp.maximum(m, jnp.max(s, axis=1, keepdims=True))
        p_new = jnp.exp(s_new - m)
        l = jnp.sum(p_new, axis=1, keepdims=True)
        pnew_ref[...] = p_new.astype(BF16)
        for pg in range(n_pages):
            p = jnp.exp(s_ref[pg] - m)
            l = l + jnp.sum(p, axis=1, keepdims=True)
            p_ref[pg] = p.astype(BF16)
        linv_ref[...] = 1.0 / l

    @pl.when(jnp.logical_and(ph == 1, g == 0))
    def _():
        acc_ref[...] = jnp.dot(pnew_ref[...], _pad_rows(vn_ref[0], page).astype(BF16),
                               preferred_element_type=F32)

    @pl.when(ph == 1)
    def _():
        acc = acc_ref[...]
        for rr in range(pages):
            acc = acc + jnp.dot(p_ref[g * pages + rr], v_refs[rr][0].astype(BF16), preferred_element_type=F32)
        acc_ref[...] = acc

    @pl.when(jnp.logical_and(ph == 1, g == ng - 1))
    def _():
        o_ref[0] = _head_diag_out(acc_ref[...] * linv_ref[...], ds)


def _moba_sample(page_table, q3, kn3, vn3, bias3, cache_k, cache_v):
    db, n_pages = page_table.shape
    ds = q3.shape[1]
    pages = PAGES_PER_STEP
    page = cache_k.shape[1]
    assert n_pages % pages == 0 and MOBA_BLOCK % page == 0 and (n_pages * page) % MOBA_BLOCK == 0
    assert T5_MAX_DIST <= page and ds <= page
    ng = n_pages // pages
    rows = N_HEADS * ds

    def k_map(rr):
        return lambda b, ph, g, pt: (pt[b * n_pages + jnp.where(ph == 0, g, ng - 1) * pages + rr], 0, 0)

    def v_map(rr):
        return lambda b, ph, g, pt: (pt[b * n_pages + jnp.where(ph == 0, 0, g) * pages + rr], 0, 0)

    bmap = lambda b, ph, g, pt: (b, 0, 0)
    in_specs = [pl.BlockSpec((1, ds, GROUP_W), bmap), pl.BlockSpec((1, ds, GROUP_W), bmap),
                pl.BlockSpec((1, ds, GROUP_W), bmap),
                pl.BlockSpec(bias3.shape, lambda b, ph, g, pt: (0, 0, 0))]
    in_specs += [pl.BlockSpec((1, page, GROUP_W), k_map(rr)) for rr in range(pages)]
    in_specs += [pl.BlockSpec((1, page, GROUP_W), v_map(rr)) for rr in range(pages)]
    return pl.pallas_call(
        functools.partial(_moba_sample_kernel, pages=pages),
        grid_spec=pltpu.PrefetchScalarGridSpec(
            num_scalar_prefetch=1,
            grid=(db, 2, ng),
            in_specs=in_specs,
            out_specs=pl.BlockSpec((1, ds, GROUP_W), bmap),
            scratch_shapes=[pltpu.VMEM((rows, GROUP_W), BF16),
                            pltpu.VMEM((n_pages, rows, page), F32),
                            pltpu.VMEM((n_pages, rows, page), BF16),
                            pltpu.VMEM((rows, page), BF16),
                            pltpu.VMEM((rows, 1), F32),
                            pltpu.VMEM((rows, GROUP_W), F32)]),
        out_shape=jax.ShapeDtypeStruct((db, ds, GROUP_W), F32),
        compiler_params=_cparams("arbitrary", "arbitrary", "arbitrary"),
        name="moba_sample",
    )(page_table.reshape(-1), q3, kn3, vn3, bias3, *([cache_k] * pages), *([cache_v] * pages))


def _outproj_kernel(of_ref, om_ref, x_ref, gate_ref, shift_ref, scale_ref, gf_ref, gm_ref, wo_ref, gffn_ref,
                    wr_ref, br_ref, cnt0_ref, x1_ref, h2_ref, ri_ref, rw_ref, cnt_ref, carry_ref):
    t = pl.program_id(0)

    @pl.when(t == 0)
    def _():
        carry_ref[...] = cnt0_ref[...]

    tm = x_ref.shape[0]
    nf = (_rms(of_ref[...]) * gf_ref[...]).astype(BF16)
    nm = (_rms(om_ref[...]) * gm_ref[...]).astype(BF16)
    o = (jnp.dot(nf, wo_ref[0:GROUP_W, :], preferred_element_type=F32)
         + jnp.dot(nm, wo_ref[GROUP_W:2 * GROUP_W, :], preferred_element_type=F32))
    x1 = x_ref[...] + gate_ref[...] * o
    x1_ref[...] = x1
    h2 = _rms(x1) * gffn_ref[...]
    h2 = h2 * (1.0 + scale_ref[...]) + shift_ref[...]
    h2_ref[...] = h2

    hi = h2.astype(BF16)
    lo = (h2 - hi.astype(F32)).astype(BF16)
    wr = wr_ref[...]
    whi = wr.astype(BF16)
    wlo = (wr - whi.astype(F32)).astype(BF16)
    lg = (jnp.dot(hi, whi, preferred_element_type=F32) + jnp.dot(lo, whi, preferred_element_type=F32)
          + jnp.dot(hi, wlo, preferred_element_type=F32)) + br_ref[...]
    lane = lax.broadcasted_iota(I32, (tm, LANES), 1)
    lane_f = lane.astype(F32)
    ninf = -jnp.inf
    is_g = jnp.logical_and(lane >= N_EXPERTS, lane < N_EXPERTS + N_GROUPS)
    glog = jnp.where(is_g, lg, ninf)
    gmax = jnp.max(glog, axis=1, keepdims=True)
    gidx = jnp.min(jnp.where(glog == gmax, lane_f, 2.0 * LANES), axis=1, keepdims=True).astype(I32) - N_EXPERTS
    g_w = 1.0 / jnp.sum(jnp.exp(glog - gmax), axis=1, keepdims=True)
    in_grp = jnp.logical_and(lane >= gidx * EXPERTS_PER_GROUP, lane < (gidx + 1) * EXPERTS_PER_GROUP)
    elog = jnp.where(in_grp, lg, ninf)
    e1 = jnp.max(elog, axis=1, keepdims=True)
    i1 = jnp.min(jnp.where(elog == e1, lane_f, 2.0 * LANES), axis=1, keepdims=True).astype(I32)
    z = jnp.sum(jnp.exp(elog - e1), axis=1, keepdims=True)
    elog2 = jnp.where(lane == i1, ninf, elog)
    e2 = jnp.max(elog2, axis=1, keepdims=True)
    i2 = jnp.min(jnp.where(elog2 == e2, lane_f, 2.0 * LANES), axis=1, keepdims=True).astype(I32)
    p1 = 1.0 / z
    p2 = jnp.exp(e2 - e1) / z
    w1 = g_w * (p1 / (p1 + p2))
    w2 = g_w * (p2 / (p1 + p2))

    a = jnp.where(jnp.logical_or(lane == i1, lane == i2), 1.0, 0.0)
    r = lax.broadcasted_iota(I32, (tm, tm), 0)
    c = lax.broadcasted_iota(I32, (tm, tm), 1)
    before = jnp.where(c < r, 1.0, 0.0).astype(BF16)
    pos = jnp.dot(before, a.astype(BF16), preferred_element_type=F32) + carry_ref[...]
    r1 = jnp.sum(jnp.where(lane == i1, pos, 0.0), axis=1, keepdims=True)
    r2 = jnp.sum(jnp.where(lane == i2, pos, 0.0), axis=1, keepdims=True)
    carry_ref[...] = carry_ref[...] + jnp.sum(a, axis=0, keepdims=True)
    cnt_ref[...] = carry_ref[...]

    ri = jnp.where(lane == 0, i1, 0) + jnp.where(lane == 1, i2, 0)
    ri = ri + jnp.where(lane == 2, r1.astype(I32), 0) + jnp.where(lane == 3, r2.astype(I32), 0)
    ri_ref[...] = ri
    rw_ref[...] = jnp.where(lane == 0, w1, 0.0) + jnp.where(lane == 1, w2, 0.0)


def _outproj(of, om, x2, gate3, shift3, scale3, gf, gm, wo, gffn, wr, br, cnt0, *, rows_per_mod, name):
    n, d = x2.shape
    tm = ROW_TILE
    nt = n // tm
    mod_rows = gate3.shape[1]
    mod_map = lambda t: ((t * tm) // rows_per_mod, 0, 0)
    row_map = lambda t: (t, 0)
    const2 = lambda t: (0, 0)
    return pl.pallas_call(
        _outproj_kernel,
        grid=(nt,),
        in_specs=[pl.BlockSpec((tm, GROUP_W), row_map), pl.BlockSpec((tm, GROUP_W), row_map),
                  pl.BlockSpec((tm, d), row_map),
                  pl.BlockSpec((None, mod_rows, d), mod_map), pl.BlockSpec((None, mod_rows, d), mod_map),
                  pl.BlockSpec((None, mod_rows, d), mod_map),
                  pl.BlockSpec((1, GROUP_W), const2), pl.BlockSpec((1, GROUP_W), const2),
                  pl.BlockSpec((d, d), const2), pl.BlockSpec((1, d), const2),
                  pl.BlockSpec((d, LANES), const2), pl.BlockSpec((1, LANES), const2),
                  pl.BlockSpec((1, LANES), const2)],
        out_specs=[pl.BlockSpec((tm, d), row_map), pl.BlockSpec((tm, d), row_map),
                   pl.BlockSpec((tm, LANES), row_map), pl.BlockSpec((tm, LANES), row_map),
                   pl.BlockSpec((1, LANES), const2)],
        out_shape=[jax.ShapeDtypeStruct((n, d), F32), jax.ShapeDtypeStruct((n, d), F32),
                   jax.ShapeDtypeStruct((n, LANES), I32), jax.ShapeDtypeStruct((n, LANES), F32),
                   jax.ShapeDtypeStruct((1, LANES), F32)],
        scratch_shapes=[pltpu.VMEM((1, LANES), F32)],
        compiler_params=_cparams("arbitrary"),
        name=name,
    )(of, om, x2, gate3, shift3, scale3, gf, gm, wo, gffn, wr, br, cnt0)


def _scatter_kernel(dest_ref, h_ref, rows_in_ref, rows_ref, sem):
    del rows_in_ref
    tm = h_ref.shape[0]

    def row_copy(r, k):
        return pltpu.make_async_copy(h_ref.at[pl.ds(r, 1)], rows_ref.at[pl.ds(dest_ref[k, r], 1)], sem)

    def issue(r, carry):
        row_copy(r, 0).start()
        row_copy(r, 1).start()
        return carry

    def drain(r, carry):
        row_copy(r, 0).wait()
        row_copy(r, 1).wait()
        return carry

    lax.fori_loop(0, tm, issue, 0)
    lax.fori_loop(0, tm, drain, 0)


def _scatter_rows(dest2, h2, rows):
    n, d = h2.shape
    tm = ROW_TILE
    return pl.pallas_call(
        _scatter_kernel,
        grid=(n // tm,),
        in_specs=[pl.BlockSpec((2, tm), lambda t: (0, t), memory_space=pltpu.SMEM),
                  pl.BlockSpec((tm, d), lambda t: (t, 0)),
                  pl.BlockSpec(memory_space=pl.ANY)],
        out_specs=pl.BlockSpec(memory_space=pl.ANY),
        out_shape=jax.ShapeDtypeStruct(rows.shape, rows.dtype),
        scratch_shapes=[pltpu.SemaphoreType.DMA(())],
        input_output_aliases={2: 0},
        compiler_params=_cparams("arbitrary"),
        name="moe_scatter",
    )(dest2, h2, rows)


def _moe_kernel(be_ref, nu_ref, rows_ref, w1_ref, w3_ref, w2_ref, y_ref, w1b_ref, w3b_ref, w2b_ref):
    blk = pl.program_id(0)

    @pl.when(blk < nu_ref[0])
    def _():
        prev = be_ref[jnp.maximum(blk - 1, 0)]

        @pl.when(jnp.logical_or(blk == 0, be_ref[blk] != prev))
        def _():
            w1b_ref[...] = w1_ref[0].astype(BF16)
            w3b_ref[...] = w3_ref[0].astype(BF16)
            w2b_ref[...] = w2_ref[0].astype(BF16)

        x = rows_ref[...].astype(BF16)
        a = jnp.dot(x, w1b_ref[...], preferred_element_type=F32)
        b = jnp.dot(x, w3b_ref[...], preferred_element_type=F32)
        hm = (a * jax.nn.sigmoid(a)) * b
        y_ref[...] = jnp.dot(hm.astype(BF16), w2b_ref[...], preferred_element_type=F32)

    @pl.when(blk >= nu_ref[0])
    def _():
        y_ref[...] = jnp.zeros_like(y_ref)


def _moe_experts(block_e, n_used, rows, w1, w3, w2):
    p, d = rows.shape
    de = w1.shape[2]
    nblk = p // MOE_BLOCK
    row_map = lambda b, be, nu: (jnp.minimum(b, nu[0] - 1), 0)
    return pl.pallas_call(
        _moe_kernel,
        grid_spec=pltpu.PrefetchScalarGridSpec(
            num_scalar_prefetch=2,
            grid=(nblk,),
            in_specs=[pl.BlockSpec((MOE_BLOCK, d), row_map),
                      pl.BlockSpec((1, d, de), lambda b, be, nu: (be[b], 0, 0)),
                      pl.BlockSpec((1, d, de), lambda b, be, nu: (be[b], 0, 0)),
                      pl.BlockSpec((1, de, d), lambda b, be, nu: (be[b], 0, 0))],
            out_specs=pl.BlockSpec((MOE_BLOCK, d), lambda b, be, nu: (b, 0)),
            scratch_shapes=[pltpu.VMEM((d, de), BF16), pltpu.VMEM((d, de), BF16), pltpu.VMEM((de, d), BF16)]),
        out_shape=jax.ShapeDtypeStruct((p, d), F32),
        compiler_params=_cparams("arbitrary"),
        name="moe_experts",
    )(block_e, n_used, rows, w1, w3, w2)


def _combine_kernel(dest_ref, x1_ref, gate_ref, rw_ref, gfin_ref, y_ref, o_ref, ybuf_ref, sem):
    tm = x1_ref.shape[0]

    def row_copy(r, k):
        return pltpu.make_async_copy(y_ref.at[pl.ds(dest_ref[k, r], 1)], ybuf_ref.at[k, pl.ds(r, 1)], sem)

    def issue(r, carry):
        row_copy(r, 0).start()
        row_copy(r, 1).start()
        return carry

    def drain(r, carry):
        row_copy(r, 0).wait()
        row_copy(r, 1).wait()
        return carry

    lax.fori_loop(0, tm, issue, 0)
    lax.fori_loop(0, tm, drain, 0)
    rw = rw_ref[...]
    moe = rw[:, 0:1] * ybuf_ref[0] + rw[:, 1:2] * ybuf_ref[1]
    xo = x1_ref[...] + gate_ref[...] * moe
    o_ref[...] = _rms(xo) * gfin_ref[...]


def _combine(dest2, x1, gate3, rw, gfin, y, *, rows_per_mod, name):
    n, d = x1.shape
    tm = ROW_TILE
    mod_rows = gate3.shape[1]
    return pl.pallas_call(
        _combine_kernel,
        grid=(n // tm,),
        in_specs=[pl.BlockSpec((2, tm), lambda t: (0, t), memory_space=pltpu.SMEM),
                  pl.BlockSpec((tm, d), lambda t: (t, 0)),
                  pl.BlockSpec((None, mod_rows, d), lambda t: ((t * tm) // rows_per_mod, 0, 0)),
                  pl.BlockSpec((tm, LANES), lambda t: (t, 0)),
                  pl.BlockSpec((1, d), lambda t: (0, 0)),
                  pl.BlockSpec(memory_space=pl.ANY)],
        out_specs=pl.BlockSpec((tm, d), lambda t: (t, 0)),
        out_shape=jax.ShapeDtypeStruct((n, d), F32),
        scratch_shapes=[pltpu.VMEM((2, tm, d), F32), pltpu.SemaphoreType.DMA(())],
        compiler_params=_cparams("arbitrary"),
        name=name,
    )(dest2, x1, gate3, rw, gfin, y)


def kernel(x_prompt, x_sample, cache_fox_k, cache_fox_v, cache_fox_logf, cache_moba_k, cache_moba_v, page_table, c_prompt, c_sample, w_ada, b_ada, g_attn, w_in, b_forget, g_out_fox, g_out_moba, t5_bias, w_out, g_ffn, w_router_group, b_router_group, w_router_expert, b_router_expert, w1, w3, w2, g_final):
    bsz, seq, d = x_prompt.shape
    db, ds, _ = x_sample.shape
    depth = w_ada.shape[0]
    n_phys, page = cache_fox_k.shape[1], cache_fox_k.shape[2]
    assert depth == 1, "one trunk layer"
    assert seq % ROW_TILE == 0 and (db * ds) % ROW_TILE == 0 and ROW_TILE % ds == 0
    assert N_EXPERTS + N_GROUPS <= LANES and N_HEADS <= LANES
    l = 0
    n_p, n_s = bsz * seq, db * ds
    xp2 = x_prompt.reshape(n_p, d)
    xs2 = x_sample.reshape(n_s, d)

    mod = _ada(jnp.concatenate([c_prompt, c_sample], axis=0), w_ada[l], b_ada[l])
    mod_p = [mod[:bsz, i * d:(i + 1) * d].reshape(bsz, 1, d) for i in range(6)]
    mod_s = [jnp.repeat(mod[bsz:, i * d:(i + 1) * d], ds, axis=0).reshape(n_s // ROW_TILE, ROW_TILE, d)
             for i in range(6)]

    w = GROUP_W
    wl = w_in[l]
    wcat = jnp.concatenate([wl[:, :3 * w], wl[:, 3 * w + N_HEADS:],
                            wl[:, 3 * w:3 * w + N_HEADS], jnp.zeros((d, LANES - N_HEADS), F32)],
                           axis=1).astype(BF16)
    bf_pad = jnp.pad(b_forget[l], (0, LANES - N_HEADS)).reshape(1, LANES)
    g_attn2 = g_attn[l].reshape(1, d)

    fq, fk, fv, lf, mq, mk, mv, fcum, kmean = _inproj(
        xp2, mod_p[0], mod_p[1], g_attn2, wcat, bf_pad, rows_per_mod=seq, prompt_extras=True, seq_len=seq)
    sfq, sfk, sfv, slf, smq, smk, smv = _inproj(
        xs2, mod_s[0], mod_s[1], g_attn2, wcat, bf_pad, rows_per_mod=ROW_TILE, prompt_extras=False, seq_len=ds)

    frow = fcum.reshape(bsz, seq, N_HEADS).transpose(0, 2, 1)
    o_fox_p = _fox_prompt(fq, fk, fv, fcum, frow, bsz, seq)
    bias_p = _t5_tiles(t5_bias, (0, MOBA_BLOCK, 2 * MOBA_BLOCK), MOBA_BLOCK, MOBA_BLOCK)
    o_moba_p = _moba_prompt(mq, mk, mv, kmean.reshape(bsz, seq // MOBA_BLOCK, w), bias_p, bsz, seq)

    to3 = lambda a: a.reshape(db, ds, w)
    cfk = cache_fox_k[l].reshape(n_phys, page, w)
    cfv = cache_fox_v[l].reshape(n_phys, page, w)
    cmk = cache_moba_k[l].reshape(n_phys, page, w)
    cmv = cache_moba_v[l].reshape(n_phys, page, w)
    clf_t = cache_fox_logf[l].transpose(0, 2, 1)
    slf_t = jnp.pad(slf.reshape(db, ds, N_HEADS).transpose(0, 2, 1), ((0, 0), (0, 0), (0, page - ds)))
    o_fox_s = _fox_sample(page_table, to3(sfq), to3(sfk), to3(sfv), slf_t, cfk, cfv, clf_t)
    bias_s = _t5_tiles(t5_bias, (page, 0, 2 * T5_MAX_DIST + page), ds, page)
    bias_s = bias_s.transpose(1, 0, 2, 3).reshape(3, N_HEADS * ds, page)
    o_moba_s = _moba_sample(page_table, to3(smq), to3(smk), to3(smv), bias_s, cmk, cmv)

    wr = jnp.concatenate([w_router_expert[l], w_router_group[l],
                          jnp.zeros((d, LANES - N_EXPERTS - N_GROUPS), F32)], axis=1)
    br = jnp.concatenate([b_router_expert[l], b_router_group[l],
                          jnp.zeros((LANES - N_EXPERTS - N_GROUPS,), F32)]).reshape(1, LANES)
    wo = w_out[l].astype(BF16)
    gf, gm, gffn = g_out_fox[l].reshape(1, w), g_out_moba[l].reshape(1, w), g_ffn[l].reshape(1, d)
    x1_p, h2_p, ri_p, rw_p, cnt_p = _outproj(
        o_fox_p, o_moba_p, xp2, mod_p[2], mod_p[3], mod_p[4], gf, gm, wo, gffn, wr, br,
        jnp.zeros((1, LANES), F32), rows_per_mod=seq, name="outproj_prompt")
    x1_s, h2_s, ri_s, rw_s, cnt = _outproj(
        o_fox_s.reshape(n_s, w), o_moba_s.reshape(n_s, w), xs2, mod_s[2], mod_s[3], mod_s[4], gf, gm, wo, gffn,
        wr, br, cnt_p, rows_per_mod=ROW_TILE, name="outproj_sample")

    counts = cnt[0, :N_EXPERTS].astype(I32)
    padded = (counts + MOE_BLOCK - 1) // MOE_BLOCK * MOE_BLOCK
    pend = jnp.cumsum(padded)
    pstart = pend - padded
    n_asg = 2 * (n_p + n_s)
    n_blocks = -(-(n_asg + N_EXPERTS * (MOE_BLOCK - 1)) // MOE_BLOCK)
    block_e = jnp.clip(jnp.searchsorted(pend, jnp.arange(n_blocks, dtype=I32) * MOE_BLOCK, side='right'),
                       0, N_EXPERTS - 1).astype(I32)
    n_used = (pend[-1:] // MOE_BLOCK).astype(I32)

    def dest_of(ri):
        return (pstart[ri[:, 0:2]] + ri[:, 2:4]).T.astype(I32)

    dest_p, dest_s = dest_of(ri_p), dest_of(ri_s)
    rows = jnp.zeros((n_blocks * MOE_BLOCK, d), F32)
    rows = _scatter_rows(dest_p, h2_p, rows)
    rows = _scatter_rows(dest_s, h2_s, rows)
    y = _moe_experts(block_e, n_used, rows, w1[l], w3[l], w2[l])
    gfin = g_final.reshape(1, d)
    y_prompt = _combine(dest_p, x1_p, mod_p[5], rw_p, gfin, y, rows_per_mod=seq, name="combine_prompt")
    y_sample = _combine(dest_s, x1_s, mod_s[5], rw_s, gfin, y, rows_per_mod=ROW_TILE, name="combine_sample")

    hp = lambda a: a.reshape(depth, bsz, seq, N_HEADS, HEAD_DIM)
    hs = lambda a: a.reshape(depth, db, ds, N_HEADS, HEAD_DIM)
    return (y_prompt.reshape(bsz, seq, d), y_sample.reshape(db, ds, d),
            hp(fk), hp(fv), lf.reshape(depth, bsz, seq, N_HEADS), hp(mk), hp(mv),
            hs(sfk), hs(sfv), slf.reshape(depth, db, ds, N_HEADS), hs(smk), hs(smv))
```

```python
import functools
import math

import jax
import jax.numpy as jnp
from jax import lax
from jax.experimental import pallas as pl
from jax.experimental.pallas import tpu as pltpu

F32 = jnp.float32
BF16 = jnp.bfloat16
I32 = jnp.int32

HEAD_DIM = 64
N_HEADS = 8
GROUP_W = N_HEADS * HEAD_DIM
LANES = 128
HEADS_PER_LANE_TILE = LANES // HEAD_DIM
N_PAIRS = N_HEADS // HEADS_PER_LANE_TILE
SM_SCALE = HEAD_DIM ** -0.5
MOBA_BLOCK = 256
MOBA_TOPK = 3
T5_BUCKETS = 32
T5_MAX_DIST = 128
N_GROUPS = 4
EXPERTS_PER_GROUP = 8
N_EXPERTS = N_GROUPS * EXPERTS_PER_GROUP
RMS_EPS = 1e-6
ROW_TILE = 256
ATTN_TILE = 256
K_CHUNK = 64
MOE_BLOCK = 256
PAGES_PER_STEP = 8
DMA_ISSUE_UNROLL = 8
NEG = -1e30
VMEM_LIMIT = 48 * 1024 * 1024
HIGHEST = lax.Precision.HIGHEST
NT = (((1,), (1,)), ((), ()))


def _cparams(*sem):
    return pltpu.CompilerParams(dimension_semantics=sem, vmem_limit_bytes=VMEM_LIMIT)


def _rms(x):
    return x * lax.rsqrt(jnp.mean(x * x, axis=-1, keepdims=True) + RMS_EPS)


def _split3_bf16(x):
    hi = x.astype(BF16)
    r1 = x - hi.astype(F32)
    mid = r1.astype(BF16)
    lo = (r1 - mid.astype(F32)).astype(BF16)
    return hi, mid, lo


def _ada_kernel(c_ref, w_ref, b_ref, o_ref):
    c = c_ref[...]
    a = c * jax.nn.sigmoid(c)
    o_ref[...] = jnp.dot(a, w_ref[...], preferred_element_type=F32, precision=HIGHEST) + b_ref[...]


def _ada(c, w, b):
    n, d = c.shape
    e = w.shape[1]
    tn = 1024
    return pl.pallas_call(
        _ada_kernel,
        grid=(e // tn,),
        in_specs=[pl.BlockSpec((n, d), lambda j: (0, 0)),
                  pl.BlockSpec((d, tn), lambda j: (0, j)),
                  pl.BlockSpec((1, tn), lambda j: (0, j))],
        out_specs=pl.BlockSpec((n, tn), lambda j: (0, j)),
        out_shape=jax.ShapeDtypeStruct((n, e), F32),
        compiler_params=_cparams("arbitrary"),
        name="ada",
    )(c, w, b.reshape(1, e))


def _inproj_kernel(x_ref, shift_ref, scale_ref, g_ref, w_ref, bf_ref,
                   fq_ref, fk_ref, fv_ref, lf_ref, mq_ref, mk_ref, mv_ref, *rest,
                   prompt_extras, tiles_per_seq):
    x = x_ref[...]
    h = _rms(x) * g_ref[...]
    h = h * (1.0 + scale_ref[...]) + shift_ref[...]
    z = jnp.dot(h.astype(BF16), w_ref[...], preferred_element_type=F32)
    w = GROUP_W
    fq = z[:, 0:w] * SM_SCALE
    mq = z[:, 3 * w:4 * w] * SM_SCALE
    if prompt_extras:
        fq_ref[0] = fq.T.astype(BF16)
        mq_ref[0] = mq.T.astype(BF16)
    else:
        fq_ref[...] = fq.astype(BF16)
        mq_ref[...] = mq.astype(BF16)
    fk, fv = z[:, w:2 * w], z[:, 2 * w:3 * w]
    mk, mv = z[:, 4 * w:5 * w], z[:, 5 * w:6 * w]
    fg = z[:, 6 * w:6 * w + LANES] + bf_ref[...]
    lf = jnp.minimum(fg, 0.0) - jnp.log1p(jnp.exp(-jnp.abs(fg)))
    if not prompt_extras:
        fk_ref[...] = fk
        fv_ref[...] = fv
        mk_ref[...] = mk
        mv_ref[...] = mv
        lf_ref[...] = lf[:, :N_HEADS]
        return
    fcum_ref, frow_ref, kmean_ref, fkb_ref, fvtb_ref, mkb_ref, mvtb_ref, carry_ref = rest
    tm = x.shape[0]
    fk_ref[0] = fk.T
    fv_t = fv.T
    fv_ref[0] = fv_t
    mk_ref[0] = mk.T
    mv_t = mv.T
    mv_ref[0] = mv_t
    fkb_ref[...] = fk.astype(BF16)
    fvtb_ref[0] = fv_t.astype(BF16)
    mkb_ref[...] = mk.astype(BF16)
    mvtb_ref[0] = mv_t.astype(BF16)
    lf_ref[0] = lf.T[:N_HEADS, :]
    kmean_ref[0] = jnp.mean(mk, axis=0, keepdims=True)

    @pl.when(pl.program_id(0) % tiles_per_seq == 0)
    def _():
        carry_ref[...] = jnp.zeros_like(carry_ref)

    r = lax.broadcasted_iota(I32, (tm, tm), 0)
    c = lax.broadcasted_iota(I32, (tm, tm), 1)
    tri = jnp.where(c <= r, 1.0, 0.0).astype(BF16)
    hi, mid, lo = _split3_bf16(lf)
    cs = (jnp.dot(tri, hi, preferred_element_type=F32)
          + jnp.dot(tri, mid, preferred_element_type=F32)
          + jnp.dot(tri, lo, preferred_element_type=F32)) + carry_ref[...]
    fcum_ref[...] = cs[:, :N_HEADS]
    frow_ref[0] = cs.T[:N_HEADS, :]
    carry_ref[...] = cs[tm - 1:tm, :]


def _inproj(x2, shift3, scale3, g_attn, wcat, bf_pad, *, rows_per_mod, prompt_extras, seq_len):
    n, d = x2.shape
    tm = ROW_TILE
    assert n % tm == 0
    nt = n // tm
    mod_rows = shift3.shape[1]
    mod_map = lambda t: ((t * tm) // rows_per_mod, 0, 0)
    row_map = lambda t: (t, 0)
    const2 = lambda t: (0, 0)
    ecols = wcat.shape[1]
    row_bf16 = (jax.ShapeDtypeStruct((n, GROUP_W), BF16), pl.BlockSpec((tm, GROUP_W), row_map))
    scratch = []
    tiles_per_seq = 1
    if prompt_extras:
        assert tm == MOBA_BLOCK and seq_len % tm == 0
        tiles_per_seq = seq_len // tm
        nb = n // seq_len
        t_map = lambda t: (t // tiles_per_seq, 0, t % tiles_per_seq)
        kv = (jax.ShapeDtypeStruct((nb, GROUP_W, seq_len), F32), pl.BlockSpec((1, GROUP_W, tm), t_map))
        lfo = (jax.ShapeDtypeStruct((nb, N_HEADS, seq_len), F32), pl.BlockSpec((1, N_HEADS, tm), t_map))
        kvt_bf16 = (jax.ShapeDtypeStruct((nb, GROUP_W, seq_len), BF16), pl.BlockSpec((1, GROUP_W, tm), t_map))
        outs = [kvt_bf16, kv, kv, lfo, kvt_bf16, kv, kv,
                (jax.ShapeDtypeStruct((n, N_HEADS), F32), pl.BlockSpec((tm, N_HEADS), row_map)),
                lfo,
                (jax.ShapeDtypeStruct((nt, 1, GROUP_W), F32), pl.BlockSpec((1, 1, GROUP_W), lambda t: (t, 0, 0)))]
        outs += [row_bf16, kvt_bf16, row_bf16, kvt_bf16]
        scratch = [pltpu.VMEM((1, LANES), F32)]
    else:
        kv = (jax.ShapeDtypeStruct((n, GROUP_W), F32), pl.BlockSpec((tm, GROUP_W), row_map))
        lfo = (jax.ShapeDtypeStruct((n, N_HEADS), F32), pl.BlockSpec((tm, N_HEADS), row_map))
        outs = [row_bf16, kv, kv, lfo, row_bf16, kv, kv]
    return pl.pallas_call(
        functools.partial(_inproj_kernel, prompt_extras=prompt_extras, tiles_per_seq=tiles_per_seq),
        grid=(nt,),
        in_specs=[pl.BlockSpec((tm, d), row_map),
                  pl.BlockSpec((None, mod_rows, d), mod_map),
                  pl.BlockSpec((None, mod_rows, d), mod_map),
                  pl.BlockSpec((1, d), const2),
                  pl.BlockSpec((d, ecols), const2),
                  pl.BlockSpec((1, LANES), const2)],
        out_specs=[o[1] for o in outs],
        out_shape=[o[0] for o in outs],
        scratch_shapes=scratch,
        compiler_params=_cparams("arbitrary"),
        name="inproj_prompt" if prompt_extras else "inproj_sample",
    )(x2, shift3, scale3, g_attn, wcat, bf_pad)


def _t5_kernel(t5_ref, o_ref, *, offs, sign):
    h = pl.program_id(0)
    rows, cols = o_ref.shape[2], o_ref.shape[3]
    r = lax.broadcasted_iota(I32, (rows, cols), 0)
    c = lax.broadcasted_iota(I32, (rows, cols), 1)
    max_exact = T5_BUCKETS // 2
    for k, off in enumerate(offs):
        rel = jnp.maximum(off + sign * (r - c), 0)
        relf = jnp.maximum(rel, 1).astype(F32)
        large = max_exact + (jnp.log(relf / max_exact) / math.log(T5_MAX_DIST / max_exact)
                             * (T5_BUCKETS - max_exact)).astype(I32)
        large = jnp.minimum(large, T5_BUCKETS - 1)
        bucket = jnp.where(rel < max_exact, rel, large)
        acc = jnp.zeros((rows, cols), F32)
        for b in range(T5_BUCKETS):
            acc = jnp.where(bucket == b, t5_ref[b, h], acc)
        o_ref[0, k] = acc


def _t5_tiles(t5_bias, offs, rows, cols, sign):
    nh = t5_bias.shape[1]
    return pl.pallas_call(
        functools.partial(_t5_kernel, offs=tuple(offs), sign=sign),
        grid=(nh,),
        in_specs=[pl.BlockSpec(memory_space=pltpu.SMEM)],
        out_specs=pl.BlockSpec((1, len(offs), rows, cols), lambda h: (h, 0, 0, 0)),
        out_shape=jax.ShapeDtypeStruct((nh, len(offs), rows, cols), F32),
        compiler_params=_cparams("arbitrary"),
        name="t5_tiles",
    )(t5_bias)


def _pair_masks(rows):
    lane = lax.broadcasted_iota(I32, (rows, LANES), 1)
    lo = lane < HEAD_DIM
    return lo, jnp.logical_not(lo)


def _masked_qt(qt_ref, h):
    p, e = divmod(h, HEADS_PER_LANE_TILE)
    qt = qt_ref[0, p * LANES:(p + 1) * LANES, :]
    row = lax.broadcasted_iota(I32, qt.shape, 0)
    keep = (row >= HEAD_DIM) if e else (row < HEAD_DIM)
    return jnp.where(keep, qt, jnp.zeros_like(qt))


def _flash_step(qt_ref, k_ref, vt_ref, adjust, states):
    tk, tq = k_ref.shape[0], qt_ref.shape[2]
    chunks = [slice(c * K_CHUNK, (c + 1) * K_CHUNK) for c in range(tk // K_CHUNK)]
    for h, (m_ref, _, _, s_scr, _, a_ref) in enumerate(states):
        pair = slice((h // HEADS_PER_LANE_TILE) * LANES, (h // HEADS_PER_LANE_TILE + 1) * LANES)
        qmt = _masked_qt(qt_ref, h)
        m8 = jnp.full((8, tq), NEG, F32)
        for c, rows in enumerate(chunks):
            s = adjust(h, c, jnp.dot(k_ref[rows, pair], qmt, preferred_element_type=F32))
            s_scr[rows, :] = s
            m8 = jnp.maximum(m8, jnp.max(s.reshape(K_CHUNK // 8, 8, tq), axis=0))
        m_prev = m_ref[...]
        m_new = jnp.maximum(m_prev, jnp.max(m8, axis=0, keepdims=True))
        a_ref[...] = jnp.exp(m_prev - m_new)
        m_ref[...] = m_new
    for m_ref, l_ref, _, s_scr, p_scr, a_ref in states:
        m_new = m_ref[...]
        l8 = jnp.zeros((8, tq), F32)
        for rows in chunks:
            p = jnp.exp(s_scr[rows, :] - m_new)
            l8 = l8 + jnp.sum(p.reshape(K_CHUNK // 8, 8, tq), axis=0)
            p_scr[rows, :] = p.astype(BF16)
        l_ref[...] = a_ref[...] * l_ref[...] + jnp.sum(l8, axis=0, keepdims=True)
    for h, (_, _, acc_ref, _, p_scr, a_ref) in enumerate(states):
        feat = slice(h * HEAD_DIM, (h + 1) * HEAD_DIM)
        pv = jnp.dot(vt_ref[0, feat, :], p_scr[...], preferred_element_type=F32)
        acc_ref[...] = a_ref[...] * acc_ref[...] + pv


FLASH_BUFS_PER_HEAD = 6


def _flash_scratch(tq, tk):
    per_head = [pltpu.VMEM((1, tq), F32), pltpu.VMEM((1, tq), F32), pltpu.VMEM((HEAD_DIM, tq), F32),
                pltpu.VMEM((tk, tq), F32), pltpu.VMEM((tk, tq), BF16), pltpu.VMEM((1, tq), F32)]
    return per_head * N_HEADS


def _flash_states(scratch):
    n = FLASH_BUFS_PER_HEAD
    return [scratch[n * h:n * (h + 1)] for h in range(N_HEADS)]


def _flash_init(states):
    for m_ref, l_ref, acc_ref, _, _, _ in states:
        m_ref[...] = jnp.full_like(m_ref, NEG)
        l_ref[...] = jnp.zeros_like(l_ref)
        acc_ref[...] = jnp.zeros_like(acc_ref)


def _flash_finish(o_ref, states):
    o_t = jnp.concatenate([acc_ref[...] / l_ref[...] for _, l_ref, acc_ref, _, _, _ in states], axis=0)
    o_ref[...] = o_t.T


def _chunk_causal(c, tq):
    krow = c * K_CHUNK + lax.broadcasted_iota(I32, (K_CHUNK, tq), 0)
    qcol = lax.broadcasted_iota(I32, (K_CHUNK, tq), 1)
    return krow <= qcol


def _fox_prompt_kernel(qi_ref, kj_ref, q_ref, k_ref, vt_ref, fq_ref, fk_ref, o_ref, *scratch):
    states = _flash_states(scratch)
    t = pl.program_id(1)
    i = qi_ref[t]
    j = kj_ref[t]
    tq = q_ref.shape[2]

    @pl.when(j == 0)
    def _():
        _flash_init(states)

    def step(diagonal):
        def adjust(h, c, s):
            s = s + fq_ref[0, h:h + 1, :] - fk_ref[c * K_CHUNK:(c + 1) * K_CHUNK, h:h + 1]
            return jnp.where(_chunk_causal(c, tq), s, NEG) if diagonal else s

        _flash_step(q_ref, k_ref, vt_ref, adjust, states)

    @pl.when(j < i)
    def _():
        step(False)

    @pl.when(j == i)
    def _():
        step(True)
        _flash_finish(o_ref, states)


def _tri_tables(nq, own_first):
    qi, kj = [], []
    for i in range(nq):
        order = ([i] + list(range(i))) if own_first else list(range(i + 1))
        for j in order:
            qi.append(i)
            kj.append(j)
    return jnp.asarray(qi, I32), jnp.asarray(kj, I32)


def _fox_prompt(qt, k, vt, fcol, frow, batch, seq):
    tq = tk = ATTN_TILE
    nq = seq // tq
    qi, kj = _tri_tables(nq, own_first=False)
    qmap = lambda b, t, qi, kj: (b * nq + qi[t], 0)
    kmap = lambda b, t, qi, kj: (b * nq + kj[t], 0)
    qtmap = lambda b, t, qi, kj: (b, 0, qi[t])
    ktmap = lambda b, t, qi, kj: (b, 0, kj[t])
    return pl.pallas_call(
        _fox_prompt_kernel,
        grid_spec=pltpu.PrefetchScalarGridSpec(
            num_scalar_prefetch=2,
            grid=(batch, qi.shape[0]),
            in_specs=[pl.BlockSpec((1, GROUP_W, tq), qtmap),
                      pl.BlockSpec((tk, GROUP_W), kmap),
                      pl.BlockSpec((1, GROUP_W, tk), ktmap),
                      pl.BlockSpec((1, N_HEADS, tq), qtmap),
                      pl.BlockSpec((tk, N_HEADS), kmap)],
            out_specs=pl.BlockSpec((tq, GROUP_W), qmap),
            scratch_shapes=_flash_scratch(tq, tk)),
        out_shape=jax.ShapeDtypeStruct((batch * seq, GROUP_W), F32),
        compiler_params=_cparams("arbitrary", "arbitrary"),
        name="fox_prompt",
    )(qi, kj, qt, k, vt, frow, fcol)


def _moba_prompt_kernel(qi_ref, kj_ref, q_ref, k_ref, vt_ref, kmean_ref, bias_ref, o_ref, *scratch):
    states = _flash_states(scratch[:-1])
    sel_ref = scratch[-1]
    t = pl.program_id(1)
    i = qi_ref[t]
    j = kj_ref[t]
    tq = q_ref.shape[2]
    nb = kmean_ref.shape[1]
    nbp = sel_ref.shape[1]
    blk_row = lax.broadcasted_iota(I32, (nbp, tq), 0)

    def step(diagonal):
        if diagonal:
            def adjust(h, c, s):
                s = s + bias_ref[h, 0, c * K_CHUNK:(c + 1) * K_CHUNK, :]
                return jnp.where(_chunk_causal(c, tq), s, NEG)
        else:
            bias_k = jnp.minimum(i - j, bias_ref.shape[1] - 1)

            def adjust(h, c, s):
                picked = jnp.sum(jnp.where(blk_row == j, sel_ref[h], 0.0), axis=0, keepdims=True) > 0.5
                s = s + bias_ref[h, bias_k, c * K_CHUNK:(c + 1) * K_CHUNK, :]
                return jnp.where(picked, s, NEG)

        _flash_step(q_ref, k_ref, vt_ref, adjust, states)

    @pl.when(j == i)
    def _():
        _flash_init(states)
        past = blk_row < i
        for h in range(N_HEADS):
            pair = slice((h // HEADS_PER_LANE_TILE) * LANES, (h // HEADS_PER_LANE_TILE + 1) * LANES)
            km = _pad_rows(kmean_ref[0, :, pair], nbp).astype(BF16)
            g = jnp.dot(km, _masked_qt(q_ref, h), preferred_element_type=F32)
            sel = jnp.zeros((nbp, tq), F32)
            for jb in range(nb):
                gj = g[jb:jb + 1, :]
                beats = jnp.logical_and(past, jnp.logical_or(g > gj, jnp.logical_and(g == gj, blk_row < jb)))
                rank = jnp.sum(jnp.where(beats, 1.0, 0.0), axis=0, keepdims=True)
                chosen = jnp.logical_and(rank < MOBA_TOPK, jb < i)
                sel = jnp.where(jnp.logical_and(blk_row == jb, chosen), 1.0, sel)
            sel_ref[h] = sel
        step(True)

        @pl.when(i == 0)
        def _():
            _flash_finish(o_ref, states)

    @pl.when(j != i)
    def _():
        step(False)

        @pl.when(j == i - 1)
        def _():
            _flash_finish(o_ref, states)


def _moba_prompt(qt, k, vt, kmean, bias, batch, seq):
    tq = tk = MOBA_BLOCK
    nq = seq // tq
    nbp = -(-nq // 8) * 8
    qi, kj = _tri_tables(nq, own_first=True)
    qmap = lambda b, t, qi, kj: (b * nq + qi[t], 0)
    kmap = lambda b, t, qi, kj: (b * nq + kj[t], 0)
    return pl.pallas_call(
        _moba_prompt_kernel,
        grid_spec=pltpu.PrefetchScalarGridSpec(
            num_scalar_prefetch=2,
            grid=(batch, qi.shape[0]),
            in_specs=[pl.BlockSpec((1, GROUP_W, tq), lambda b, t, qi, kj: (b, 0, qi[t])),
                      pl.BlockSpec((tk, GROUP_W), kmap),
                      pl.BlockSpec((1, GROUP_W, tk), lambda b, t, qi, kj: (b, 0, kj[t])),
                      pl.BlockSpec((1, nq, GROUP_W), lambda b, t, qi, kj: (b, 0, 0)),
                      pl.BlockSpec(bias.shape, lambda b, t, qi, kj: (0, 0, 0, 0))],
            out_specs=pl.BlockSpec((tq, GROUP_W), qmap),
            scratch_shapes=_flash_scratch(tq, tk) + [pltpu.VMEM((N_HEADS, nbp, tq), F32)]),
        out_shape=jax.ShapeDtypeStruct((batch * seq, GROUP_W), F32),
        compiler_params=_cparams("arbitrary", "arbitrary"),
        name="moba_prompt",
    )(qi, kj, qt, k, vt, kmean, bias)


def _block_diag_q(q):
    ds = q.shape[0]
    lane_head = lax.broadcasted_iota(I32, (N_HEADS, ds, GROUP_W), 2) // HEAD_DIM
    head = lax.broadcasted_iota(I32, (N_HEADS, ds, GROUP_W), 0)
    q3 = jnp.where(lane_head == head, q.astype(F32)[None, :, :], 0.0)
    return q3.reshape(N_HEADS * ds, GROUP_W).astype(BF16)


def _head_diag_out(acc, ds):
    acc3 = acc.reshape(N_HEADS, ds, GROUP_W)
    lane_head = lax.broadcasted_iota(I32, (N_HEADS, ds, GROUP_W), 2) // HEAD_DIM
    head = lax.broadcasted_iota(I32, (N_HEADS, ds, GROUP_W), 0)
    return jnp.sum(jnp.where(lane_head == head, acc3, 0.0), axis=0)


def _pad_rows(x, rows):
    if x.shape[0] == rows:
        return x
    return jnp.concatenate([x, jnp.zeros((rows - x.shape[0], x.shape[1]), x.dtype)], axis=0)


def _fox_sample_kernel(pt_ref, q_ref, kn_ref, vn_ref, lfn_ref, *refs, pages):
    k_refs = refs[0:pages]
    v_refs = refs[pages:2 * pages]
    lf_refs = refs[2 * pages:3 * pages]
    o_ref, qbd_ref, m_ref, l_ref, acc_ref, carry_ref, rq_ref = refs[3 * pages:]
    g = pl.program_id(1)
    ds = q_ref.shape[1]
    page = k_refs[0].shape[2]
    lane3 = lax.broadcasted_iota(I32, (N_HEADS, ds, page), 2)
    qidx3 = lax.broadcasted_iota(I32, (N_HEADS, ds, page), 1)
    r = lax.broadcasted_iota(I32, (page, page), 0)
    c = lax.broadcasted_iota(I32, (page, page), 1)
    after = jnp.where(r > c, 1.0, 0.0).astype(BF16)

    def suffix_sum(x):
        hi, mid, lo = _split3_bf16(x)
        return (jnp.dot(hi, after, preferred_element_type=F32) + jnp.dot(mid, after, preferred_element_type=F32)
                + jnp.dot(lo, after, preferred_element_type=F32))

    def biased(s, later):
        s3 = s.reshape(N_HEADS, ds, page) + later[:, None, :] - rq_ref[...].reshape(N_HEADS, ds, 1)
        return s3

    def softmax_update(s, pv_of):
        m_prev = m_ref[...]
        m_new = jnp.maximum(m_prev, jnp.max(s, axis=1, keepdims=True))
        alpha = jnp.exp(m_prev - m_new)
        p = jnp.exp(s - m_new)
        l_ref[...] = alpha * l_ref[...] + jnp.sum(p, axis=1, keepdims=True)
        m_ref[...] = m_new
        acc_ref[...] = alpha * acc_ref[...] + pv_of(p.astype(BF16))

    @pl.when(g == 0)
    def _():
        qbd_ref[...] = _block_diag_q(q_ref[0])
        m_ref[...] = jnp.full_like(m_ref, NEG)
        l_ref[...] = jnp.zeros_like(l_ref)
        acc_ref[...] = jnp.zeros_like(acc_ref)
        x = lfn_ref[0]
        later_new = suffix_sum(x)
        rq3 = jnp.sum(jnp.where(lane3 == qidx3, later_new[:, None, :], 0.0), axis=2, keepdims=True)
        rq_ref[...] = rq3.reshape(N_HEADS * ds, 1)
        kn = _pad_rows(kn_ref[0], page).astype(BF16)
        vn = _pad_rows(vn_ref[0], page).astype(BF16)
        s3 = biased(lax.dot_general(qbd_ref[...], kn, NT, preferred_element_type=F32), later_new)
        s3 = jnp.where(lane3 <= qidx3, s3, NEG)
        softmax_update(s3.reshape(N_HEADS * ds, page), lambda p: jnp.dot(p, vn, preferred_element_type=F32))
        carry_ref[...] = jnp.sum(x, axis=1, keepdims=True)

    xs = [lf_refs[rr][0] for rr in range(pages)]
    within = suffix_sum(jnp.concatenate(xs, axis=0))
    run = carry_ref[...]
    parts = []
    for rr in range(pages):
        later = within[rr * N_HEADS:(rr + 1) * N_HEADS, :] + run
        s = jnp.dot(qbd_ref[...], k_refs[rr][0].astype(BF16), preferred_element_type=F32)
        parts.append(biased(s, later).reshape(N_HEADS * ds, page))
        run = run + jnp.sum(xs[rr], axis=1, keepdims=True)
    carry_ref[...] = run

    def pv_of(p):
        pv = None
        for rr in range(pages):
            term = lax.dot_general(p[:, rr * page:(rr + 1) * page], v_refs[rr][0].astype(BF16), NT,
                                   preferred_element_type=F32)
            pv = term if pv is None else pv + term
        return pv

    softmax_update(jnp.concatenate(parts, axis=1), pv_of)

    @pl.when(g == pl.num_programs(1) - 1)
    def _():
        o_ref[0] = _head_diag_out(acc_ref[...] / l_ref[...], ds)


def _fox_sample(page_table, q3, kn3, vn3, lfn_t, cache_k, cache_v, cache_lf_t):
    db, n_pages = page_table.shape
    ds = q3.shape[1]
    pages = PAGES_PER_STEP
    assert n_pages % pages == 0
    ng = n_pages // pages
    page = cache_k.shape[2]

    def page_map(rr):
        return lambda b, g, pt: (pt[b * n_pages + (ng - 1 - g) * pages + (pages - 1 - rr)], 0, 0)

    bmap = lambda b, g, pt: (b, 0, 0)
    in_specs = [pl.BlockSpec((1, ds, GROUP_W), bmap), pl.BlockSpec((1, ds, GROUP_W), bmap),
                pl.BlockSpec((1, ds, GROUP_W), bmap), pl.BlockSpec((1, N_HEADS, page), bmap)]
    in_specs += [pl.BlockSpec((1, GROUP_W, page), page_map(rr)) for rr in range(pages)]
    in_specs += [pl.BlockSpec((1, GROUP_W, page), page_map(rr)) for rr in range(pages)]
    in_specs += [pl.BlockSpec((1, N_HEADS, page), page_map(rr)) for rr in range(pages)]
    rows = N_HEADS * ds
    return pl.pallas_call(
        functools.partial(_fox_sample_kernel, pages=pages),
        grid_spec=pltpu.PrefetchScalarGridSpec(
            num_scalar_prefetch=1,
            grid=(db, ng),
            in_specs=in_specs,
            out_specs=pl.BlockSpec((1, ds, GROUP_W), bmap),
            scratch_shapes=[pltpu.VMEM((rows, GROUP_W), BF16), pltpu.VMEM((rows, 1), F32),
                            pltpu.VMEM((rows, 1), F32), pltpu.VMEM((rows, GROUP_W), F32),
                            pltpu.VMEM((N_HEADS, 1), F32), pltpu.VMEM((rows, 1), F32)]),
        out_shape=jax.ShapeDtypeStruct((db, ds, GROUP_W), F32),
        compiler_params=_cparams("arbitrary", "arbitrary"),
        name="fox_sample",
    )(page_table.reshape(-1), q3, kn3, vn3, lfn_t, *([cache_k] * pages), *([cache_v] * pages),
      *([cache_lf_t] * pages))


def _moba_sample_kernel(pt_ref, q_ref, kn_ref, vn_ref, bias_ref, *refs, pages):
    k_refs = refs[0:pages]
    v_refs = refs[pages:2 * pages]
    o_ref, qbd_ref, s_ref, p_ref, pnew_ref, linv_ref, acc_ref = refs[2 * pages:]
    ph = pl.program_id(1)
    g = pl.program_id(2)
    ng = pl.num_programs(2)
    ds = q_ref.shape[1]
    page = k_refs[0].shape[2]
    n_pages = s_ref.shape[0]
    rows = N_HEADS * ds
    pages_per_block = MOBA_BLOCK // page
    n_blocks = n_pages // pages_per_block

    @pl.when(jnp.logical_and(ph == 0, g == 0))
    def _():
        qbd_ref[...] = _block_diag_q(q_ref[0])

    @pl.when(ph == 0)
    def _():
        for rr in range(pages):
            s_ref[g * pages + rr] = jnp.dot(qbd_ref[...], k_refs[rr][0].astype(BF16), preferred_element_type=F32)

    @pl.when(jnp.logical_and(ph == 0, g == ng - 1))
    def _():
        lane = lax.broadcasted_iota(I32, (rows, LANES), 1)
        gate = jnp.full((rows, LANES), -jnp.inf, F32)
        for b in range(n_blocks):
            tot = s_ref[b * pages_per_block]
            for u in range(1, pages_per_block):
                tot = tot + s_ref[b * pages_per_block + u]
            gate = jnp.where(lane == b, jnp.sum(tot, axis=1, keepdims=True) * (1.0 / MOBA_BLOCK), gate)
        sel = jnp.zeros((rows, LANES), F32)
        for _ in range(min(MOBA_TOPK, n_blocks)):
            mx = jnp.max(gate, axis=1, keepdims=True)
            idx = jnp.min(jnp.where(gate == mx, lane.astype(F32), float(LANES)), axis=1, keepdims=True)
            pick = lane.astype(F32) == idx
            sel = jnp.where(pick, 1.0, sel)
            gate = jnp.where(pick, -jnp.inf, gate)
        lane_n = lax.broadcasted_iota(I32, (N_HEADS, ds, page), 2)
        qidx_n = lax.broadcasted_iota(I32, (N_HEADS, ds, page), 1)
        s_new = lax.dot_general(qbd_ref[...], _pad_rows(kn_ref[0], page).astype(BF16), NT,
                                preferred_element_type=F32) + bias_ref[1]
        s_new = jnp.where((lane_n <= qidx_n).reshape(rows, page), s_new, NEG)
        m = jnp.max(s_new, axis=1, keepdims=True)
        for pg in range(n_pages):
            b = pg // pages_per_block
            picked = jnp.sum(jnp.where(lane == b, sel, 0.0), axis=1, keepdims=True)
            bias = bias_ref[0] if pg == n_pages - 1 else bias_ref[2]
            s = jnp.where(picked > 0.5, s_ref[pg] + bias, NEG)
            s_ref[pg] = s
            m = jnp.maximum(m, jnp.max(s, axis=1, keepdims=True))
        p_new = jnp.exp(s_new - m)
        l = jnp.sum(p_new, axis=1, keepdims=True)
        pnew_ref[...] = p_new.astype(BF16)
        for pg in range(n_pages):
            p = jnp.exp(s_ref[pg] - m)
            l = l + jnp.sum(p, axis=1, keepdims=True)
            p_ref[pg] = p.astype(BF16)
        linv_ref[...] = 1.0 / l

    @pl.when(jnp.logical_and(ph == 1, g == 0))
    def _():
        acc_ref[...] = jnp.dot(pnew_ref[...], _pad_rows(vn_ref[0], page).astype(BF16),
                               preferred_element_type=F32)

    @pl.when(ph == 1)
    def _():
        acc = acc_ref[...]
        for rr in range(pages):
            acc = acc + lax.dot_general(p_ref[g * pages + rr], v_refs[rr][0].astype(BF16), NT,
                                        preferred_element_type=F32)
        acc_ref[...] = acc

    @pl.when(jnp.logical_and(ph == 1, g == ng - 1))
    def _():
        o_ref[0] = _head_diag_out(acc_ref[...] * linv_ref[...], ds)


def _moba_sample(page_table, q3, kn3, vn3, bias3, cache_k, cache_v):
    db, n_pages = page_table.shape
    ds = q3.shape[1]
    pages = PAGES_PER_STEP
    page = cache_k.shape[2]
    assert n_pages % pages == 0 and MOBA_BLOCK % page == 0 and (n_pages * page) % MOBA_BLOCK == 0
    assert T5_MAX_DIST <= page and ds <= page
    ng = n_pages // pages
    rows = N_HEADS * ds

    def k_map(rr):
        return lambda b, ph, g, pt: (pt[b * n_pages + jnp.where(ph == 0, g, ng - 1) * pages + rr], 0, 0)

    def v_map(rr):
        return lambda b, ph, g, pt: (pt[b * n_pages + jnp.where(ph == 0, 0, g) * pages + rr], 0, 0)

    bmap = lambda b, ph, g, pt: (b, 0, 0)
    in_specs = [pl.BlockSpec((1, ds, GROUP_W), bmap), pl.BlockSpec((1, ds, GROUP_W), bmap),
                pl.BlockSpec((1, ds, GROUP_W), bmap),
                pl.BlockSpec(bias3.shape, lambda b, ph, g, pt: (0, 0, 0))]
    in_specs += [pl.BlockSpec((1, GROUP_W, page), k_map(rr)) for rr in range(pages)]
    in_specs += [pl.BlockSpec((1, GROUP_W, page), v_map(rr)) for rr in range(pages)]
    return pl.pallas_call(
        functools.partial(_moba_sample_kernel, pages=pages),
        grid_spec=pltpu.PrefetchScalarGridSpec(
            num_scalar_prefetch=1,
            grid=(db, 2, ng),
            in_specs=in_specs,
            out_specs=pl.BlockSpec((1, ds, GROUP_W), bmap),
            scratch_shapes=[pltpu.VMEM((rows, GROUP_W), BF16),
                            pltpu.VMEM((n_pages, rows, page), F32),
                            pltpu.VMEM((n_pages, rows, page), BF16),
                            pltpu.VMEM((rows, page), BF16),
                            pltpu.VMEM((rows, 1), F32),
                            pltpu.VMEM((rows, GROUP_W), F32)]),
        out_shape=jax.ShapeDtypeStruct((db, ds, GROUP_W), F32),
        compiler_params=_cparams("arbitrary", "arbitrary", "arbitrary"),
        name="moba_sample",
    )(page_table.reshape(-1), q3, kn3, vn3, bias3, *([cache_k] * pages), *([cache_v] * pages))


def _outproj_kernel(of_ref, om_ref, x_ref, gate_ref, shift_ref, scale_ref, gf_ref, gm_ref, wo_ref, gffn_ref,
                    wr_ref, br_ref, cnt0_ref, x1_ref, h2_ref, ri_ref, rw_ref, cnt_ref, carry_ref):
    t = pl.program_id(0)

    @pl.when(t == 0)
    def _():
        carry_ref[...] = cnt0_ref[...]

    tm = x_ref.shape[0]
    nf = (_rms(of_ref[...]) * gf_ref[...]).astype(BF16)
    nm = (_rms(om_ref[...]) * gm_ref[...]).astype(BF16)
    o = (jnp.dot(nf, wo_ref[0:GROUP_W, :], preferred_element_type=F32)
         + jnp.dot(nm, wo_ref[GROUP_W:2 * GROUP_W, :], preferred_element_type=F32))
    x1 = x_ref[...] + gate_ref[...] * o
    x1_ref[...] = x1
    h2 = _rms(x1) * gffn_ref[...]
    h2 = h2 * (1.0 + scale_ref[...]) + shift_ref[...]
    h2_ref[...] = h2

    hi = h2.astype(BF16)
    lo = (h2 - hi.astype(F32)).astype(BF16)
    wr = wr_ref[...]
    whi = wr.astype(BF16)
    wlo = (wr - whi.astype(F32)).astype(BF16)
    lg = (jnp.dot(hi, whi, preferred_element_type=F32) + jnp.dot(lo, whi, preferred_element_type=F32)
          + jnp.dot(hi, wlo, preferred_element_type=F32)) + br_ref[...]
    lane = lax.broadcasted_iota(I32, (tm, LANES), 1)
    lane_f = lane.astype(F32)
    ninf = -jnp.inf
    is_g = jnp.logical_and(lane >= N_EXPERTS, lane < N_EXPERTS + N_GROUPS)
    glog = jnp.where(is_g, lg, ninf)
    gmax = jnp.max(glog, axis=1, keepdims=True)
    gidx = jnp.min(jnp.where(glog == gmax, lane_f, 2.0 * LANES), axis=1, keepdims=True).astype(I32) - N_EXPERTS
    g_w = 1.0 / jnp.sum(jnp.exp(glog - gmax), axis=1, keepdims=True)
    in_grp = jnp.logical_and(lane >= gidx * EXPERTS_PER_GROUP, lane < (gidx + 1) * EXPERTS_PER_GROUP)
    elog = jnp.where(in_grp, lg, ninf)
    e1 = jnp.max(elog, axis=1, keepdims=True)
    i1 = jnp.min(jnp.where(elog == e1, lane_f, 2.0 * LANES), axis=1, keepdims=True).astype(I32)
    z = jnp.sum(jnp.exp(elog - e1), axis=1, keepdims=True)
    elog2 = jnp.where(lane == i1, ninf, elog)
    e2 = jnp.max(elog2, axis=1, keepdims=True)
    i2 = jnp.min(jnp.where(elog2 == e2, lane_f, 2.0 * LANES), axis=1, keepdims=True).astype(I32)
    p1 = 1.0 / z
    p2 = jnp.exp(e2 - e1) / z
    w1 = g_w * (p1 / (p1 + p2))
    w2 = g_w * (p2 / (p1 + p2))

    a = jnp.where(jnp.logical_or(lane == i1, lane == i2), 1.0, 0.0)
    r = lax.broadcasted_iota(I32, (tm, tm), 0)
    c = lax.broadcasted_iota(I32, (tm, tm), 1)
    before = jnp.where(c < r, 1.0, 0.0).astype(BF16)
    pos = jnp.dot(before, a.astype(BF16), preferred_element_type=F32) + carry_ref[...]
    r1 = jnp.sum(jnp.where(lane == i1, pos, 0.0), axis=1, keepdims=True)
    r2 = jnp.sum(jnp.where(lane == i2, pos, 0.0), axis=1, keepdims=True)
    carry_ref[...] = carry_ref[...] + jnp.sum(a, axis=0, keepdims=True)
    cnt_ref[...] = carry_ref[...]

    ri = jnp.where(lane == 0, i1, 0) + jnp.where(lane == 1, i2, 0)
    ri = ri + jnp.where(lane == 2, r1.astype(I32), 0) + jnp.where(lane == 3, r2.astype(I32), 0)
    ri_ref[...] = ri
    rw_ref[...] = jnp.where(lane == 0, w1, 0.0) + jnp.where(lane == 1, w2, 0.0)


def _outproj(of, om, x2, gate3, shift3, scale3, gf, gm, wo, gffn, wr, br, cnt0, *, rows_per_mod, name):
    n, d = x2.shape
    tm = ROW_TILE
    nt = n // tm
    mod_rows = gate3.shape[1]
    mod_map = lambda t: ((t * tm) // rows_per_mod, 0, 0)
    row_map = lambda t: (t, 0)
    const2 = lambda t: (0, 0)
    return pl.pallas_call(
        _outproj_kernel,
        grid=(nt,),
        in_specs=[pl.BlockSpec((tm, GROUP_W), row_map), pl.BlockSpec((tm, GROUP_W), row_map),
                  pl.BlockSpec((tm, d), row_map),
                  pl.BlockSpec((None, mod_rows, d), mod_map), pl.BlockSpec((None, mod_rows, d), mod_map),
                  pl.BlockSpec((None, mod_rows, d), mod_map),
                  pl.BlockSpec((1, GROUP_W), const2), pl.BlockSpec((1, GROUP_W), const2),
                  pl.BlockSpec((d, d), const2), pl.BlockSpec((1, d), const2),
                  pl.BlockSpec((d, LANES), const2), pl.BlockSpec((1, LANES), const2),
                  pl.BlockSpec((1, LANES), const2)],
        out_specs=[pl.BlockSpec((tm, d), row_map), pl.BlockSpec((tm, d), row_map),
                   pl.BlockSpec((tm, LANES), row_map), pl.BlockSpec((tm, LANES), row_map),
                   pl.BlockSpec((1, LANES), const2)],
        out_shape=[jax.ShapeDtypeStruct((n, d), F32), jax.ShapeDtypeStruct((n, d), F32),
                   jax.ShapeDtypeStruct((n, LANES), I32), jax.ShapeDtypeStruct((n, LANES), F32),
                   jax.ShapeDtypeStruct((1, LANES), F32)],
        scratch_shapes=[pltpu.VMEM((1, LANES), F32)],
        compiler_params=_cparams("arbitrary"),
        name=name,
    )(of, om, x2, gate3, shift3, scale3, gf, gm, wo, gffn, wr, br, cnt0)


def _scatter_kernel(dest_ref, h_ref, rows_in_ref, rows_ref, sem):
    del rows_in_ref
    tm = h_ref.shape[0]

    def row_copy(r, k):
        return pltpu.make_async_copy(h_ref.at[pl.ds(r, 1)], rows_ref.at[pl.ds(dest_ref[k, r], 1)], sem)

    def issue(r, carry):
        row_copy(r, 0).start()
        row_copy(r, 1).start()
        return carry

    lax.fori_loop(0, tm, issue, 0, unroll=DMA_ISSUE_UNROLL)
    for _ in range(2):
        pltpu.make_async_copy(h_ref, rows_ref.at[pl.ds(0, tm)], sem).wait()


def _scatter_rows(dest2, h2, rows):
    n, d = h2.shape
    tm = ROW_TILE
    return pl.pallas_call(
        _scatter_kernel,
        grid=(n // tm,),
        in_specs=[pl.BlockSpec((2, tm), lambda t: (0, t), memory_space=pltpu.SMEM),
                  pl.BlockSpec((tm, d), lambda t: (t, 0)),
                  pl.BlockSpec(memory_space=pl.ANY)],
        out_specs=pl.BlockSpec(memory_space=pl.ANY),
        out_shape=jax.ShapeDtypeStruct(rows.shape, rows.dtype),
        scratch_shapes=[pltpu.SemaphoreType.DMA(())],
        input_output_aliases={2: 0},
        compiler_params=_cparams("arbitrary"),
        name="moe_scatter",
    )(dest2, h2, rows)


def _moe_kernel(be_ref, nu_ref, rows_ref, w1_ref, w3_ref, w2_ref, y_ref, w1b_ref, w3b_ref, w2b_ref):
    blk = pl.program_id(0)

    @pl.when(blk < nu_ref[0])
    def _():
        prev = be_ref[jnp.maximum(blk - 1, 0)]

        @pl.when(jnp.logical_or(blk == 0, be_ref[blk] != prev))
        def _():
            w1b_ref[...] = w1_ref[0].astype(BF16)
            w3b_ref[...] = w3_ref[0].astype(BF16)
            w2b_ref[...] = w2_ref[0].astype(BF16)

        x = rows_ref[...].astype(BF16)
        a = jnp.dot(x, w1b_ref[...], preferred_element_type=F32)
        b = jnp.dot(x, w3b_ref[...], preferred_element_type=F32)
        hm = (a * jax.nn.sigmoid(a)) * b
        y_ref[...] = jnp.dot(hm.astype(BF16), w2b_ref[...], preferred_element_type=F32)

    @pl.when(blk >= nu_ref[0])
    def _():
        y_ref[...] = jnp.zeros_like(y_ref)


def _moe_experts(block_e, n_used, rows, w1, w3, w2):
    p, d = rows.shape
    de = w1.shape[2]
    nblk = p // MOE_BLOCK
    row_map = lambda b, be, nu: (jnp.minimum(b, nu[0] - 1), 0)
    return pl.pallas_call(
        _moe_kernel,
        grid_spec=pltpu.PrefetchScalarGridSpec(
            num_scalar_prefetch=2,
            grid=(nblk,),
            in_specs=[pl.BlockSpec((MOE_BLOCK, d), row_map),
                      pl.BlockSpec((1, d, de), lambda b, be, nu: (be[b], 0, 0)),
                      pl.BlockSpec((1, d, de), lambda b, be, nu: (be[b], 0, 0)),
                      pl.BlockSpec((1, de, d), lambda b, be, nu: (be[b], 0, 0))],
            out_specs=pl.BlockSpec((MOE_BLOCK, d), lambda b, be, nu: (b, 0)),
            scratch_shapes=[pltpu.VMEM((d, de), BF16), pltpu.VMEM((d, de), BF16), pltpu.VMEM((de, d), BF16)]),
        out_shape=jax.ShapeDtypeStruct((p, d), F32),
        compiler_params=_cparams("arbitrary"),
        name="moe_experts",
    )(block_e, n_used, rows, w1, w3, w2)


def _combine_kernel(dest_ref, x1_ref, gate_ref, rw_ref, gfin_ref, y_ref, o_ref, ybuf_ref, sem):
    tm = x1_ref.shape[0]

    def row_copy(r, k):
        return pltpu.make_async_copy(y_ref.at[pl.ds(dest_ref[k, r], 1)], ybuf_ref.at[k, pl.ds(r, 1)], sem)

    def issue(r, carry):
        row_copy(r, 0).start()
        row_copy(r, 1).start()
        return carry

    lax.fori_loop(0, tm, issue, 0, unroll=DMA_ISSUE_UNROLL)
    for k in range(2):
        pltpu.make_async_copy(y_ref.at[pl.ds(0, tm)], ybuf_ref.at[k], sem).wait()
    rw = rw_ref[...]
    moe = rw[:, 0:1] * ybuf_ref[0] + rw[:, 1:2] * ybuf_ref[1]
    xo = x1_ref[...] + gate_ref[...] * moe
    o_ref[...] = _rms(xo) * gfin_ref[...]


def _combine(dest2, x1, gate3, rw, gfin, y, *, rows_per_mod, name):
    n, d = x1.shape
    tm = ROW_TILE
    mod_rows = gate3.shape[1]
    return pl.pallas_call(
        _combine_kernel,
        grid=(n // tm,),
        in_specs=[pl.BlockSpec((2, tm), lambda t: (0, t), memory_space=pltpu.SMEM),
                  pl.BlockSpec((tm, d), lambda t: (t, 0)),
                  pl.BlockSpec((None, mod_rows, d), lambda t: ((t * tm) // rows_per_mod, 0, 0)),
                  pl.BlockSpec((tm, LANES), lambda t: (t, 0)),
                  pl.BlockSpec((1, d), lambda t: (0, 0)),
                  pl.BlockSpec(memory_space=pl.ANY)],
        out_specs=pl.BlockSpec((tm, d), lambda t: (t, 0)),
        out_shape=jax.ShapeDtypeStruct((n, d), F32),
        scratch_shapes=[pltpu.VMEM((2, tm, d), F32), pltpu.SemaphoreType.DMA(())],
        compiler_params=_cparams("arbitrary"),
        name=name,
    )(dest2, x1, gate3, rw, gfin, y)


def kernel(x_prompt, x_sample, cache_fox_k, cache_fox_v, cache_fox_logf, cache_moba_k, cache_moba_v, page_table, c_prompt, c_sample, w_ada, b_ada, g_attn, w_in, b_forget, g_out_fox, g_out_moba, t5_bias, w_out, g_ffn, w_router_group, b_router_group, w_router_expert, b_router_expert, w1, w3, w2, g_final):
    bsz, seq, d = x_prompt.shape
    db, ds, _ = x_sample.shape
    depth = w_ada.shape[0]
    n_phys, page = cache_fox_k.shape[1], cache_fox_k.shape[2]
    assert depth == 1, "one trunk layer"
    assert seq % ROW_TILE == 0 and (db * ds) % ROW_TILE == 0 and ROW_TILE % ds == 0
    assert N_EXPERTS + N_GROUPS <= LANES and N_HEADS <= LANES
    l = 0
    n_p, n_s = bsz * seq, db * ds
    xp2 = x_prompt.reshape(n_p, d)
    xs2 = x_sample.reshape(n_s, d)

    mod = _ada(jnp.concatenate([c_prompt, c_sample], axis=0), w_ada[l], b_ada[l])
    mod_p = [mod[:bsz, i * d:(i + 1) * d].reshape(bsz, 1, d) for i in range(6)]
    mod_s = [jnp.repeat(mod[bsz:, i * d:(i + 1) * d], ds, axis=0).reshape(n_s // ROW_TILE, ROW_TILE, d)
             for i in range(6)]

    w = GROUP_W
    wl = w_in[l]
    wcat = jnp.concatenate([wl[:, :3 * w], wl[:, 3 * w + N_HEADS:],
                            wl[:, 3 * w:3 * w + N_HEADS], jnp.zeros((d, LANES - N_HEADS), F32)],
                           axis=1).astype(BF16)
    bf_pad = jnp.pad(b_forget[l], (0, LANES - N_HEADS)).reshape(1, LANES)
    g_attn2 = g_attn[l].reshape(1, d)

    fqt, fk_t, fv_t, lf_t, mqt, mk_t, mv_t, fcum, frow, kmean, fkb, fvtb, mkb, mvtb = _inproj(
        xp2, mod_p[0], mod_p[1], g_attn2, wcat, bf_pad, rows_per_mod=seq, prompt_extras=True, seq_len=seq)
    sfq, sfk, sfv, slf, smq, smk, smv = _inproj(
        xs2, mod_s[0], mod_s[1], g_attn2, wcat, bf_pad, rows_per_mod=ROW_TILE, prompt_extras=False, seq_len=ds)

    o_fox_p = _fox_prompt(fqt, fkb, fvtb, fcum, frow, bsz, seq)
    bias_p = _t5_tiles(t5_bias, (0, MOBA_BLOCK, 2 * MOBA_BLOCK), MOBA_BLOCK, MOBA_BLOCK, sign=-1)
    o_moba_p = _moba_prompt(mqt, mkb, mvtb, kmean.reshape(bsz, seq // MOBA_BLOCK, w), bias_p, bsz, seq)

    to3 = lambda a: a.reshape(db, ds, w)
    page_t = lambda c: c[l].transpose(0, 2, 3, 1).reshape(n_phys, w, page)
    cfk, cfv, cmk, cmv = page_t(cache_fox_k), page_t(cache_fox_v), page_t(cache_moba_k), page_t(cache_moba_v)
    clf_t = cache_fox_logf[l].transpose(0, 2, 1)
    slf_t = jnp.pad(slf.reshape(db, ds, N_HEADS).transpose(0, 2, 1), ((0, 0), (0, 0), (0, page - ds)))
    o_fox_s = _fox_sample(page_table, to3(sfq), to3(sfk), to3(sfv), slf_t, cfk, cfv, clf_t)
    bias_s = _t5_tiles(t5_bias, (page, 0, 2 * T5_MAX_DIST + page), ds, page, sign=1)
    bias_s = bias_s.transpose(1, 0, 2, 3).reshape(3, N_HEADS * ds, page)
    o_moba_s = _moba_sample(page_table, to3(smq), to3(smk), to3(smv), bias_s, cmk, cmv)

    wr = jnp.concatenate([w_router_expert[l], w_router_group[l],
                          jnp.zeros((d, LANES - N_EXPERTS - N_GROUPS), F32)], axis=1)
    br = jnp.concatenate([b_router_expert[l], b_router_group[l],
                          jnp.zeros((LANES - N_EXPERTS - N_GROUPS,), F32)]).reshape(1, LANES)
    wo = w_out[l].astype(BF16)
    gf, gm, gffn = g_out_fox[l].reshape(1, w), g_out_moba[l].reshape(1, w), g_ffn[l].reshape(1, d)
    x1_p, h2_p, ri_p, rw_p, cnt_p = _outproj(
        o_fox_p, o_moba_p, xp2, mod_p[2], mod_p[3], mod_p[4], gf, gm, wo, gffn, wr, br,
        jnp.zeros((1, LANES), F32), rows_per_mod=seq, name="outproj_prompt")
    x1_s, h2_s, ri_s, rw_s, cnt = _outproj(
        o_fox_s.reshape(n_s, w), o_moba_s.reshape(n_s, w), xs2, mod_s[2], mod_s[3], mod_s[4], gf, gm, wo, gffn,
        wr, br, cnt_p, rows_per_mod=ROW_TILE, name="outproj_sample")

    counts = cnt[0, :N_EXPERTS].astype(I32)
    padded = (counts + MOE_BLOCK - 1) // MOE_BLOCK * MOE_BLOCK
    pend = jnp.cumsum(padded)
    pstart = pend - padded
    n_asg = 2 * (n_p + n_s)
    n_blocks = -(-(n_asg + N_EXPERTS * (MOE_BLOCK - 1)) // MOE_BLOCK)
    blk_start = jnp.arange(n_blocks, dtype=I32) * MOE_BLOCK
    block_e = jnp.clip(jnp.sum((pend[None, :] <= blk_start[:, None]).astype(I32), axis=1), 0, N_EXPERTS - 1)
    n_used = (pend[-1:] // MOE_BLOCK).astype(I32)

    def dest_of(ri):
        return (pstart[ri[:, 0:2]] + ri[:, 2:4]).T.astype(I32)

    dest_p, dest_s = dest_of(ri_p), dest_of(ri_s)
    rows = jnp.zeros((n_blocks * MOE_BLOCK, d), F32)
    rows = _scatter_rows(dest_p, h2_p, rows)
    rows = _scatter_rows(dest_s, h2_s, rows)
    y = _moe_experts(block_e, n_used, rows, w1[l], w3[l], w2[l])
    gfin = g_final.reshape(1, d)
    y_prompt = _combine(dest_p, x1_p, mod_p[5], rw_p, gfin, y, rows_per_mod=seq, name="combine_prompt")
    y_sample = _combine(dest_s, x1_s, mod_s[5], rw_s, gfin, y, rows_per_mod=ROW_TILE, name="combine_sample")

    hp = lambda a: a.reshape(depth, bsz, N_HEADS, HEAD_DIM, seq).transpose(0, 1, 4, 2, 3)
    hs = lambda a: a.reshape(depth, db, ds, N_HEADS, HEAD_DIM)
    return (y_prompt.reshape(bsz, seq, d), y_sample.reshape(db, ds, d),
            hp(fk_t), hp(fv_t), lf_t.reshape(depth, bsz, N_HEADS, seq).transpose(0, 1, 3, 2), hp(mk_t), hp(mv_t),
            hs(sfk), hs(sfv), slf.reshape(depth, db, ds, N_HEADS), hs(smk), hs(smv))
```

```python
import functools
import math

import jax
import jax.numpy as jnp
from jax import lax
from jax.experimental import pallas as pl
from jax.experimental.pallas import tpu as pltpu

F32 = jnp.float32
BF16 = jnp.bfloat16
I32 = jnp.int32

HEAD_DIM = 64
N_HEADS = 8
GROUP_W = N_HEADS * HEAD_DIM
LANES = 128
HEADS_PER_LANE_TILE = LANES // HEAD_DIM
N_PAIRS = N_HEADS // HEADS_PER_LANE_TILE
SM_SCALE = HEAD_DIM ** -0.5
MOBA_BLOCK = 256
MOBA_TOPK = 3
T5_BUCKETS = 32
T5_MAX_DIST = 128
N_GROUPS = 4
EXPERTS_PER_GROUP = 8
N_EXPERTS = N_GROUPS * EXPERTS_PER_GROUP
RMS_EPS = 1e-6
ROW_TILE = 256
ATTN_TILE = 256
K_CHUNK = 64
MOE_BLOCK = 256
PAGES_PER_STEP = 8
PAGE_RING = 3
DMA_ISSUE_UNROLL = 8
NEG = -1e30
VMEM_LIMIT = 48 * 1024 * 1024
HIGHEST = lax.Precision.HIGHEST
NT = (((1,), (1,)), ((), ()))


def _cparams(*sem):
    return pltpu.CompilerParams(dimension_semantics=sem, vmem_limit_bytes=VMEM_LIMIT)


def _rms(x):
    return x * lax.rsqrt(jnp.mean(x * x, axis=-1, keepdims=True) + RMS_EPS)


def _split3_bf16(x):
    hi = x.astype(BF16)
    r1 = x - hi.astype(F32)
    mid = r1.astype(BF16)
    lo = (r1 - mid.astype(F32)).astype(BF16)
    return hi, mid, lo


def _ada_kernel(c_ref, w_ref, b_ref, o_ref):
    c = c_ref[...]
    a = c * jax.nn.sigmoid(c)
    o_ref[...] = jnp.dot(a, w_ref[...], preferred_element_type=F32, precision=HIGHEST) + b_ref[...]


def _ada(c, w, b):
    n, d = c.shape
    e = w.shape[1]
    tn = 1024
    return pl.pallas_call(
        _ada_kernel,
        grid=(e // tn,),
        in_specs=[pl.BlockSpec((n, d), lambda j: (0, 0)),
                  pl.BlockSpec((d, tn), lambda j: (0, j)),
                  pl.BlockSpec((1, tn), lambda j: (0, j))],
        out_specs=pl.BlockSpec((n, tn), lambda j: (0, j)),
        out_shape=jax.ShapeDtypeStruct((n, e), F32),
        compiler_params=_cparams("arbitrary"),
        name="ada",
    )(c, w, b.reshape(1, e))


def _inproj_kernel(x_ref, shift_ref, scale_ref, g_ref, w_ref, bf_ref,
                   fq_ref, fk_ref, fv_ref, lf_ref, mq_ref, mk_ref, mv_ref, *rest,
                   prompt_extras, tiles_per_seq):
    x = x_ref[...]
    h = _rms(x) * g_ref[...]
    h = h * (1.0 + scale_ref[...]) + shift_ref[...]
    z = jnp.dot(h.astype(BF16), w_ref[...], preferred_element_type=F32)
    w = GROUP_W
    fq = z[:, 0:w] * SM_SCALE
    mq = z[:, 3 * w:4 * w] * SM_SCALE
    if prompt_extras:
        fq_ref[0] = fq.T.astype(BF16)
        mq_ref[0] = mq.T.astype(BF16)
    else:
        fq_ref[...] = fq.astype(BF16)
        mq_ref[...] = mq.astype(BF16)
    fk, fv = z[:, w:2 * w], z[:, 2 * w:3 * w]
    mk, mv = z[:, 4 * w:5 * w], z[:, 5 * w:6 * w]
    fg = z[:, 6 * w:6 * w + LANES] + bf_ref[...]
    lf = jnp.minimum(fg, 0.0) - jnp.log1p(jnp.exp(-jnp.abs(fg)))
    if not prompt_extras:
        fk_ref[...] = fk
        fv_ref[...] = fv
        mk_ref[...] = mk
        mv_ref[...] = mv
        lf_ref[...] = lf[:, :N_HEADS]
        return
    fcum_ref, frow_ref, kmean_ref, fkb_ref, fvtb_ref, mkb_ref, mvtb_ref, carry_ref = rest
    tm = x.shape[0]
    fk_ref[0] = fk.T
    fv_t = fv.T
    fv_ref[0] = fv_t
    mk_ref[0] = mk.T
    mv_t = mv.T
    mv_ref[0] = mv_t
    fkb_ref[...] = fk.astype(BF16)
    fvtb_ref[0] = fv_t.astype(BF16)
    mkb_ref[...] = mk.astype(BF16)
    mvtb_ref[0] = mv_t.astype(BF16)
    lf_ref[0] = lf.T[:N_HEADS, :]
    kmean_ref[0] = jnp.mean(mk, axis=0, keepdims=True)

    @pl.when(pl.program_id(0) % tiles_per_seq == 0)
    def _():
        carry_ref[...] = jnp.zeros_like(carry_ref)

    r = lax.broadcasted_iota(I32, (tm, tm), 0)
    c = lax.broadcasted_iota(I32, (tm, tm), 1)
    tri = jnp.where(c <= r, 1.0, 0.0).astype(BF16)
    hi, mid, lo = _split3_bf16(lf)
    cs = (jnp.dot(tri, hi, preferred_element_type=F32)
          + jnp.dot(tri, mid, preferred_element_type=F32)
          + jnp.dot(tri, lo, preferred_element_type=F32)) + carry_ref[...]
    fcum_ref[...] = cs[:, :N_HEADS]
    frow_ref[0] = cs.T[:N_HEADS, :]
    carry_ref[...] = cs[tm - 1:tm, :]


def _inproj(x2, shift3, scale3, g_attn, wcat, bf_pad, *, rows_per_mod, prompt_extras, seq_len):
    n, d = x2.shape
    tm = ROW_TILE
    assert n % tm == 0
    nt = n // tm
    mod_rows = shift3.shape[1]
    mod_map = lambda t: ((t * tm) // rows_per_mod, 0, 0)
    row_map = lambda t: (t, 0)
    const2 = lambda t: (0, 0)
    ecols = wcat.shape[1]
    row_bf16 = (jax.ShapeDtypeStruct((n, GROUP_W), BF16), pl.BlockSpec((tm, GROUP_W), row_map))
    scratch = []
    tiles_per_seq = 1
    if prompt_extras:
        assert tm == MOBA_BLOCK and seq_len % tm == 0
        tiles_per_seq = seq_len // tm
        nb = n // seq_len
        t_map = lambda t: (t // tiles_per_seq, 0, t % tiles_per_seq)
        kv = (jax.ShapeDtypeStruct((nb, GROUP_W, seq_len), F32), pl.BlockSpec((1, GROUP_W, tm), t_map))
        lfo = (jax.ShapeDtypeStruct((nb, N_HEADS, seq_len), F32), pl.BlockSpec((1, N_HEADS, tm), t_map))
        kvt_bf16 = (jax.ShapeDtypeStruct((nb, GROUP_W, seq_len), BF16), pl.BlockSpec((1, GROUP_W, tm), t_map))
        outs = [kvt_bf16, kv, kv, lfo, kvt_bf16, kv, kv,
                (jax.ShapeDtypeStruct((n, N_HEADS), F32), pl.BlockSpec((tm, N_HEADS), row_map)),
                lfo,
                (jax.ShapeDtypeStruct((nt, 1, GROUP_W), F32), pl.BlockSpec((1, 1, GROUP_W), lambda t: (t, 0, 0)))]
        outs += [row_bf16, kvt_bf16, row_bf16, kvt_bf16]
        scratch = [pltpu.VMEM((1, LANES), F32)]
    else:
        kv = (jax.ShapeDtypeStruct((n, GROUP_W), F32), pl.BlockSpec((tm, GROUP_W), row_map))
        lfo = (jax.ShapeDtypeStruct((n, N_HEADS), F32), pl.BlockSpec((tm, N_HEADS), row_map))
        outs = [row_bf16, kv, kv, lfo, row_bf16, kv, kv]
    return pl.pallas_call(
        functools.partial(_inproj_kernel, prompt_extras=prompt_extras, tiles_per_seq=tiles_per_seq),
        grid=(nt,),
        in_specs=[pl.BlockSpec((tm, d), row_map),
                  pl.BlockSpec((None, mod_rows, d), mod_map),
                  pl.BlockSpec((None, mod_rows, d), mod_map),
                  pl.BlockSpec((1, d), const2),
                  pl.BlockSpec((d, ecols), const2),
                  pl.BlockSpec((1, LANES), const2)],
        out_specs=[o[1] for o in outs],
        out_shape=[o[0] for o in outs],
        scratch_shapes=scratch,
        compiler_params=_cparams("arbitrary"),
        name="inproj_prompt" if prompt_extras else "inproj_sample",
    )(x2, shift3, scale3, g_attn, wcat, bf_pad)


def _t5_kernel(t5_ref, o_ref, *, offs, sign):
    h = pl.program_id(0)
    rows, cols = o_ref.shape[2], o_ref.shape[3]
    r = lax.broadcasted_iota(I32, (rows, cols), 0)
    c = lax.broadcasted_iota(I32, (rows, cols), 1)
    max_exact = T5_BUCKETS // 2
    for k, off in enumerate(offs):
        rel = jnp.maximum(off + sign * (r - c), 0)
        relf = jnp.maximum(rel, 1).astype(F32)
        large = max_exact + (jnp.log(relf / max_exact) / math.log(T5_MAX_DIST / max_exact)
                             * (T5_BUCKETS - max_exact)).astype(I32)
        large = jnp.minimum(large, T5_BUCKETS - 1)
        bucket = jnp.where(rel < max_exact, rel, large)
        acc = jnp.zeros((rows, cols), F32)
        for b in range(T5_BUCKETS):
            acc = jnp.where(bucket == b, t5_ref[b, h], acc)
        o_ref[0, k] = acc


def _t5_tiles(t5_bias, offs, rows, cols, sign):
    nh = t5_bias.shape[1]
    return pl.pallas_call(
        functools.partial(_t5_kernel, offs=tuple(offs), sign=sign),
        grid=(nh,),
        in_specs=[pl.BlockSpec(memory_space=pltpu.SMEM)],
        out_specs=pl.BlockSpec((1, len(offs), rows, cols), lambda h: (h, 0, 0, 0)),
        out_shape=jax.ShapeDtypeStruct((nh, len(offs), rows, cols), F32),
        compiler_params=_cparams("arbitrary"),
        name="t5_tiles",
    )(t5_bias)


def _pair_masks(rows):
    lane = lax.broadcasted_iota(I32, (rows, LANES), 1)
    lo = lane < HEAD_DIM
    return lo, jnp.logical_not(lo)


def _masked_qt(qt_ref, h):
    p, e = divmod(h, HEADS_PER_LANE_TILE)
    qt = qt_ref[0, p * LANES:(p + 1) * LANES, :]
    row = lax.broadcasted_iota(I32, qt.shape, 0)
    keep = (row >= HEAD_DIM) if e else (row < HEAD_DIM)
    return jnp.where(keep, qt, jnp.zeros_like(qt))


def _flash_step(qt_ref, k_ref, vt_ref, adjust, states):
    tk, tq = k_ref.shape[0], qt_ref.shape[2]
    chunks = [slice(c * K_CHUNK, (c + 1) * K_CHUNK) for c in range(tk // K_CHUNK)]
    for h, (m_ref, _, _, s_scr, _, a_ref) in enumerate(states):
        pair = slice((h // HEADS_PER_LANE_TILE) * LANES, (h // HEADS_PER_LANE_TILE + 1) * LANES)
        qmt = _masked_qt(qt_ref, h)
        m8 = jnp.full((8, tq), NEG, F32)
        for c, rows in enumerate(chunks):
            s = adjust(h, c, jnp.dot(k_ref[rows, pair], qmt, preferred_element_type=F32))
            s_scr[rows, :] = s
            m8 = jnp.maximum(m8, jnp.max(s.reshape(K_CHUNK // 8, 8, tq), axis=0))
        m_prev = m_ref[...]
        m_new = jnp.maximum(m_prev, jnp.max(m8, axis=0, keepdims=True))
        a_ref[...] = jnp.exp(m_prev - m_new)
        m_ref[...] = m_new
    for m_ref, l_ref, _, s_scr, p_scr, a_ref in states:
        m_new = m_ref[...]
        l8 = jnp.zeros((8, tq), F32)
        for rows in chunks:
            p = jnp.exp(s_scr[rows, :] - m_new)
            l8 = l8 + jnp.sum(p.reshape(K_CHUNK // 8, 8, tq), axis=0)
            p_scr[rows, :] = p.astype(BF16)
        l_ref[...] = a_ref[...] * l_ref[...] + jnp.sum(l8, axis=0, keepdims=True)
    for h, (_, _, acc_ref, _, p_scr, a_ref) in enumerate(states):
        feat = slice(h * HEAD_DIM, (h + 1) * HEAD_DIM)
        pv = jnp.dot(vt_ref[0, feat, :], p_scr[...], preferred_element_type=F32)
        acc_ref[...] = a_ref[...] * acc_ref[...] + pv


FLASH_BUFS_PER_HEAD = 6


def _flash_scratch(tq, tk):
    per_head = [pltpu.VMEM((1, tq), F32), pltpu.VMEM((1, tq), F32), pltpu.VMEM((HEAD_DIM, tq), F32),
                pltpu.VMEM((tk, tq), F32), pltpu.VMEM((tk, tq), BF16), pltpu.VMEM((1, tq), F32)]
    return per_head * N_HEADS


def _flash_states(scratch):
    n = FLASH_BUFS_PER_HEAD
    return [scratch[n * h:n * (h + 1)] for h in range(N_HEADS)]


def _flash_init(states):
    for m_ref, l_ref, acc_ref, _, _, _ in states:
        m_ref[...] = jnp.full_like(m_ref, NEG)
        l_ref[...] = jnp.zeros_like(l_ref)
        acc_ref[...] = jnp.zeros_like(acc_ref)


def _flash_finish(o_ref, states):
    o_t = jnp.concatenate([acc_ref[...] / l_ref[...] for _, l_ref, acc_ref, _, _, _ in states], axis=0)
    o_ref[...] = o_t.T


def _chunk_causal(c, tq):
    krow = c * K_CHUNK + lax.broadcasted_iota(I32, (K_CHUNK, tq), 0)
    qcol = lax.broadcasted_iota(I32, (K_CHUNK, tq), 1)
    return krow <= qcol


def _fox_prompt_kernel(qi_ref, kj_ref, q_ref, k_ref, vt_ref, fq_ref, fk_ref, o_ref, *scratch):
    states = _flash_states(scratch)
    t = pl.program_id(1)
    i = qi_ref[t]
    j = kj_ref[t]
    tq = q_ref.shape[2]

    @pl.when(j == 0)
    def _():
        _flash_init(states)

    def step(diagonal):
        def adjust(h, c, s):
            s = s + fq_ref[0, h:h + 1, :] - fk_ref[c * K_CHUNK:(c + 1) * K_CHUNK, h:h + 1]
            return jnp.where(_chunk_causal(c, tq), s, NEG) if diagonal else s

        _flash_step(q_ref, k_ref, vt_ref, adjust, states)

    @pl.when(j < i)
    def _():
        step(False)

    @pl.when(j == i)
    def _():
        step(True)
        _flash_finish(o_ref, states)


def _tri_tables(nq, own_first):
    qi, kj = [], []
    for i in range(nq):
        order = ([i] + list(range(i))) if own_first else list(range(i + 1))
        for j in order:
            qi.append(i)
            kj.append(j)
    return jnp.asarray(qi, I32), jnp.asarray(kj, I32)


def _fox_prompt(qt, k, vt, fcol, frow, batch, seq):
    tq = tk = ATTN_TILE
    nq = seq // tq
    qi, kj = _tri_tables(nq, own_first=False)
    qmap = lambda b, t, qi, kj: (b * nq + qi[t], 0)
    kmap = lambda b, t, qi, kj: (b * nq + kj[t], 0)
    qtmap = lambda b, t, qi, kj: (b, 0, qi[t])
    ktmap = lambda b, t, qi, kj: (b, 0, kj[t])
    return pl.pallas_call(
        _fox_prompt_kernel,
        grid_spec=pltpu.PrefetchScalarGridSpec(
            num_scalar_prefetch=2,
            grid=(batch, qi.shape[0]),
            in_specs=[pl.BlockSpec((1, GROUP_W, tq), qtmap),
                      pl.BlockSpec((tk, GROUP_W), kmap),
                      pl.BlockSpec((1, GROUP_W, tk), ktmap),
                      pl.BlockSpec((1, N_HEADS, tq), qtmap),
                      pl.BlockSpec((tk, N_HEADS), kmap)],
            out_specs=pl.BlockSpec((tq, GROUP_W), qmap),
            scratch_shapes=_flash_scratch(tq, tk)),
        out_shape=jax.ShapeDtypeStruct((batch * seq, GROUP_W), F32),
        compiler_params=_cparams("arbitrary", "arbitrary"),
        name="fox_prompt",
    )(qi, kj, qt, k, vt, frow, fcol)


def _moba_prompt_kernel(qi_ref, kj_ref, q_ref, k_ref, vt_ref, kmean_ref, bias_ref, o_ref, *scratch):
    states = _flash_states(scratch[:-1])
    sel_ref = scratch[-1]
    t = pl.program_id(1)
    i = qi_ref[t]
    j = kj_ref[t]
    tq = q_ref.shape[2]
    nb = kmean_ref.shape[1]
    nbp = sel_ref.shape[1]
    blk_row = lax.broadcasted_iota(I32, (nbp, tq), 0)

    def step(diagonal):
        if diagonal:
            def adjust(h, c, s):
                s = s + bias_ref[h, 0, c * K_CHUNK:(c + 1) * K_CHUNK, :]
                return jnp.where(_chunk_causal(c, tq), s, NEG)
        else:
            bias_k = jnp.minimum(i - j, bias_ref.shape[1] - 1)

            def adjust(h, c, s):
                picked = jnp.sum(jnp.where(blk_row == j, sel_ref[h], 0.0), axis=0, keepdims=True) > 0.5
                s = s + bias_ref[h, bias_k, c * K_CHUNK:(c + 1) * K_CHUNK, :]
                return jnp.where(picked, s, NEG)

        _flash_step(q_ref, k_ref, vt_ref, adjust, states)

    @pl.when(j == i)
    def _():
        _flash_init(states)
        past = blk_row < i
        for h in range(N_HEADS):
            pair = slice((h // HEADS_PER_LANE_TILE) * LANES, (h // HEADS_PER_LANE_TILE + 1) * LANES)
            km = _pad_rows(kmean_ref[0, :, pair], nbp).astype(BF16)
            g = jnp.dot(km, _masked_qt(q_ref, h), preferred_element_type=F32)
            sel = jnp.zeros((nbp, tq), F32)
            for jb in range(nb):
                gj = g[jb:jb + 1, :]
                beats = jnp.logical_and(past, jnp.logical_or(g > gj, jnp.logical_and(g == gj, blk_row < jb)))
                rank = jnp.sum(jnp.where(beats, 1.0, 0.0), axis=0, keepdims=True)
                chosen = jnp.logical_and(rank < MOBA_TOPK, jb < i)
                sel = jnp.where(jnp.logical_and(blk_row == jb, chosen), 1.0, sel)
            sel_ref[h] = sel
        step(True)

        @pl.when(i == 0)
        def _():
            _flash_finish(o_ref, states)

    @pl.when(j != i)
    def _():
        step(False)

        @pl.when(j == i - 1)
        def _():
            _flash_finish(o_ref, states)


def _moba_prompt(qt, k, vt, kmean, bias, batch, seq):
    tq = tk = MOBA_BLOCK
    nq = seq // tq
    nbp = -(-nq // 8) * 8
    qi, kj = _tri_tables(nq, own_first=True)
    qmap = lambda b, t, qi, kj: (b * nq + qi[t], 0)
    kmap = lambda b, t, qi, kj: (b * nq + kj[t], 0)
    return pl.pallas_call(
        _moba_prompt_kernel,
        grid_spec=pltpu.PrefetchScalarGridSpec(
            num_scalar_prefetch=2,
            grid=(batch, qi.shape[0]),
            in_specs=[pl.BlockSpec((1, GROUP_W, tq), lambda b, t, qi, kj: (b, 0, qi[t])),
                      pl.BlockSpec((tk, GROUP_W), kmap),
                      pl.BlockSpec((1, GROUP_W, tk), lambda b, t, qi, kj: (b, 0, kj[t])),
                      pl.BlockSpec((1, nq, GROUP_W), lambda b, t, qi, kj: (b, 0, 0)),
                      pl.BlockSpec(bias.shape, lambda b, t, qi, kj: (0, 0, 0, 0))],
            out_specs=pl.BlockSpec((tq, GROUP_W), qmap),
            scratch_shapes=_flash_scratch(tq, tk) + [pltpu.VMEM((N_HEADS, nbp, tq), F32)]),
        out_shape=jax.ShapeDtypeStruct((batch * seq, GROUP_W), F32),
        compiler_params=_cparams("arbitrary", "arbitrary"),
        name="moba_prompt",
    )(qi, kj, qt, k, vt, kmean, bias)


def _block_diag_q(q):
    ds = q.shape[0]
    lane_head = lax.broadcasted_iota(I32, (N_HEADS, ds, GROUP_W), 2) // HEAD_DIM
    head = lax.broadcasted_iota(I32, (N_HEADS, ds, GROUP_W), 0)
    q3 = jnp.where(lane_head == head, q.astype(F32)[None, :, :], 0.0)
    return q3.reshape(N_HEADS * ds, GROUP_W).astype(BF16)


def _head_diag_out(acc, ds):
    acc3 = acc.reshape(N_HEADS, ds, GROUP_W)
    lane_head = lax.broadcasted_iota(I32, (N_HEADS, ds, GROUP_W), 2) // HEAD_DIM
    head = lax.broadcasted_iota(I32, (N_HEADS, ds, GROUP_W), 0)
    return jnp.sum(jnp.where(lane_head == head, acc3, 0.0), axis=0)


def _pad_rows(x, rows):
    if x.shape[0] == rows:
        return x
    return jnp.concatenate([x, jnp.zeros((rows - x.shape[0], x.shape[1]), x.dtype)], axis=0)


def _ring_copy(pt_ref, page_index, cache_ref, buf_ref, sem_ref, slot, rr):
    return pltpu.make_async_copy(cache_ref.at[pt_ref[page_index]], buf_ref.at[slot, rr], sem_ref.at[slot])


def _ring_fetch(pt_ref, page_index_of, u, cache_ref, buf_ref, sem_ref):
    slot = u % PAGE_RING
    for rr in range(buf_ref.shape[1]):
        _ring_copy(pt_ref, page_index_of(u, rr), cache_ref, buf_ref, sem_ref, slot, rr).start(priority=rr % 2)


def _ring_wait(pt_ref, page_index_of, u, cache_ref, buf_ref, sem_ref):
    slot = u % PAGE_RING
    for rr in range(buf_ref.shape[1]):
        _ring_copy(pt_ref, page_index_of(u, rr), cache_ref, buf_ref, sem_ref, slot, rr).wait()


def _ring_advance(pt_ref, page_index_of, u, n_fetches, cache_ref, buf_ref, sem_ref):
    @pl.when(u == 0)
    def _():
        for ahead in range(min(PAGE_RING - 1, n_fetches)):
            _ring_fetch(pt_ref, page_index_of, ahead, cache_ref, buf_ref, sem_ref)

    @pl.when(u + (PAGE_RING - 1) < n_fetches)
    def _():
        _ring_fetch(pt_ref, page_index_of, u + (PAGE_RING - 1), cache_ref, buf_ref, sem_ref)

    _ring_wait(pt_ref, page_index_of, u, cache_ref, buf_ref, sem_ref)
    return u % PAGE_RING


def _fox_sample_kernel(pt_ref, q_ref, kn_ref, vn_ref, lfn_ref, *refs, pages, n_pages, n_batch):
    lf_refs = refs[0:pages]
    (k_hbm, v_hbm, o_ref, qbd_ref, m_ref, l_ref, acc_ref, carry_ref, rq_ref,
     kbuf, vbuf, ksem, vsem) = refs[pages:]
    g = pl.program_id(1)
    ng = n_pages // pages
    ds = q_ref.shape[1]
    page = kbuf.shape[3]

    def page_index_of(u, rr):
        return (u // ng) * n_pages + (ng - 1 - u % ng) * pages + (pages - 1 - rr)

    u = pl.program_id(0) * ng + g
    slot = _ring_advance(pt_ref, page_index_of, u, n_batch * ng, k_hbm, kbuf, ksem)
    _ring_advance(pt_ref, page_index_of, u, n_batch * ng, v_hbm, vbuf, vsem)
    lane3 = lax.broadcasted_iota(I32, (N_HEADS, ds, page), 2)
    qidx3 = lax.broadcasted_iota(I32, (N_HEADS, ds, page), 1)
    r = lax.broadcasted_iota(I32, (page, page), 0)
    c = lax.broadcasted_iota(I32, (page, page), 1)
    after = jnp.where(r > c, 1.0, 0.0).astype(BF16)

    def suffix_sum(x):
        hi, mid, lo = _split3_bf16(x)
        return (jnp.dot(hi, after, preferred_element_type=F32) + jnp.dot(mid, after, preferred_element_type=F32)
                + jnp.dot(lo, after, preferred_element_type=F32))

    def biased(s, later):
        s3 = s.reshape(N_HEADS, ds, page) + later[:, None, :] - rq_ref[...].reshape(N_HEADS, ds, 1)
        return s3

    def softmax_update(s, pv_of):
        m_prev = m_ref[...]
        m_new = jnp.maximum(m_prev, jnp.max(s, axis=1, keepdims=True))
        alpha = jnp.exp(m_prev - m_new)
        p = jnp.exp(s - m_new)
        l_ref[...] = alpha * l_ref[...] + jnp.sum(p, axis=1, keepdims=True)
        m_ref[...] = m_new
        acc_ref[...] = alpha * acc_ref[...] + pv_of(p.astype(BF16))

    @pl.when(g == 0)
    def _():
        qbd_ref[...] = _block_diag_q(q_ref[0])
        m_ref[...] = jnp.full_like(m_ref, NEG)
        l_ref[...] = jnp.zeros_like(l_ref)
        acc_ref[...] = jnp.zeros_like(acc_ref)
        x = lfn_ref[0]
        later_new = suffix_sum(x)
        rq3 = jnp.sum(jnp.where(lane3 == qidx3, later_new[:, None, :], 0.0), axis=2, keepdims=True)
        rq_ref[...] = rq3.reshape(N_HEADS * ds, 1)
        kn = _pad_rows(kn_ref[0], page).astype(BF16)
        vn = _pad_rows(vn_ref[0], page).astype(BF16)
        s3 = biased(lax.dot_general(qbd_ref[...], kn, NT, preferred_element_type=F32), later_new)
        s3 = jnp.where(lane3 <= qidx3, s3, NEG)
        softmax_update(s3.reshape(N_HEADS * ds, page), lambda p: jnp.dot(p, vn, preferred_element_type=F32))
        carry_ref[...] = jnp.sum(x, axis=1, keepdims=True)

    xs = [lf_refs[rr][0] for rr in range(pages)]
    within = suffix_sum(jnp.concatenate(xs, axis=0))
    run = carry_ref[...]
    parts = []
    for rr in range(pages):
        later = within[rr * N_HEADS:(rr + 1) * N_HEADS, :] + run
        s = jnp.dot(qbd_ref[...], kbuf[slot, rr].astype(BF16), preferred_element_type=F32)
        parts.append(biased(s, later).reshape(N_HEADS * ds, page))
        run = run + jnp.sum(xs[rr], axis=1, keepdims=True)
    carry_ref[...] = run

    def pv_of(p):
        pv = None
        for rr in range(pages):
            term = lax.dot_general(p[:, rr * page:(rr + 1) * page], vbuf[slot, rr].astype(BF16), NT,
                                   preferred_element_type=F32)
            pv = term if pv is None else pv + term
        return pv

    softmax_update(jnp.concatenate(parts, axis=1), pv_of)

    @pl.when(g == ng - 1)
    def _():
        o_ref[0] = _head_diag_out(acc_ref[...] / l_ref[...], ds)


def _page_ring_scratch(pages, page):
    return [pltpu.VMEM((PAGE_RING, pages, GROUP_W, page), F32), pltpu.SemaphoreType.DMA((PAGE_RING,))]


def _fox_sample(page_table, q3, kn3, vn3, lfn_t, cache_k, cache_v, cache_lf_t):
    db, n_pages = page_table.shape
    ds = q3.shape[1]
    pages = PAGES_PER_STEP
    assert n_pages % pages == 0
    ng = n_pages // pages
    page = cache_k.shape[2]

    def page_map(rr):
        return lambda b, g, pt: (pt[b * n_pages + (ng - 1 - g) * pages + (pages - 1 - rr)], 0, 0)

    bmap = lambda b, g, pt: (b, 0, 0)
    in_specs = [pl.BlockSpec((1, ds, GROUP_W), bmap), pl.BlockSpec((1, ds, GROUP_W), bmap),
                pl.BlockSpec((1, ds, GROUP_W), bmap), pl.BlockSpec((1, N_HEADS, page), bmap)]
    in_specs += [pl.BlockSpec((1, N_HEADS, page), page_map(rr)) for rr in range(pages)]
    in_specs += [pl.BlockSpec(memory_space=pl.ANY), pl.BlockSpec(memory_space=pl.ANY)]
    rows = N_HEADS * ds
    k_ring, k_sem = _page_ring_scratch(pages, page)
    v_ring, v_sem = _page_ring_scratch(pages, page)
    return pl.pallas_call(
        functools.partial(_fox_sample_kernel, pages=pages, n_pages=n_pages, n_batch=db),
        grid_spec=pltpu.PrefetchScalarGridSpec(
            num_scalar_prefetch=1,
            grid=(db, ng),
            in_specs=in_specs,
            out_specs=pl.BlockSpec((1, ds, GROUP_W), bmap),
            scratch_shapes=[pltpu.VMEM((rows, GROUP_W), BF16), pltpu.VMEM((rows, 1), F32),
                            pltpu.VMEM((rows, 1), F32), pltpu.VMEM((rows, GROUP_W), F32),
                            pltpu.VMEM((N_HEADS, 1), F32), pltpu.VMEM((rows, 1), F32),
                            k_ring, v_ring, k_sem, v_sem]),
        out_shape=jax.ShapeDtypeStruct((db, ds, GROUP_W), F32),
        compiler_params=_cparams("arbitrary", "arbitrary"),
        name="fox_sample",
    )(page_table.reshape(-1), q3, kn3, vn3, lfn_t, *([cache_lf_t] * pages), cache_k, cache_v)


def _moba_sample_kernel(pt_ref, q_ref, kn_ref, vn_ref, bias_ref, k_hbm, v_hbm, o_ref,
                        qbd_ref, s_ref, p_ref, pnew_ref, linv_ref, acc_ref, kbuf, vbuf, ksem, vsem,
                        *, pages, n_batch):
    ph = pl.program_id(1)
    g = pl.program_id(2)
    ds = q_ref.shape[1]
    page = kbuf.shape[3]
    n_pages = s_ref.shape[0]
    ng = n_pages // pages
    rows = N_HEADS * ds
    pages_per_block = MOBA_BLOCK // page
    n_blocks = n_pages // pages_per_block
    u = pl.program_id(0) * ng + g

    def page_index_of(u, rr):
        return (u // ng) * n_pages + (u % ng) * pages + rr

    @pl.when(jnp.logical_and(ph == 0, g == 0))
    def _():
        qbd_ref[...] = _block_diag_q(q_ref[0])

    @pl.when(ph == 0)
    def _():
        slot = _ring_advance(pt_ref, page_index_of, u, n_batch * ng, k_hbm, kbuf, ksem)
        for rr in range(pages):
            s_ref[g * pages + rr] = jnp.dot(qbd_ref[...], kbuf[slot, rr].astype(BF16), preferred_element_type=F32)

    @pl.when(jnp.logical_and(ph == 0, g == ng - 1))
    def _():
        lane = lax.broadcasted_iota(I32, (rows, LANES), 1)
        gate = jnp.full((rows, LANES), -jnp.inf, F32)
        for b in range(n_blocks):
            tot = s_ref[b * pages_per_block]
            for u in range(1, pages_per_block):
                tot = tot + s_ref[b * pages_per_block + u]
            gate = jnp.where(lane == b, jnp.sum(tot, axis=1, keepdims=True) * (1.0 / MOBA_BLOCK), gate)
        sel = jnp.zeros((rows, LANES), F32)
        for _ in range(min(MOBA_TOPK, n_blocks)):
            mx = jnp.max(gate, axis=1, keepdims=True)
            idx = jnp.min(jnp.where(gate == mx, lane.astype(F32), float(LANES)), axis=1, keepdims=True)
            pick = lane.astype(F32) == idx
            sel = jnp.where(pick, 1.0, sel)
            gate = jnp.where(pick, -jnp.inf, gate)
        lane_n = lax.broadcasted_iota(I32, (N_HEADS, ds, page), 2)
        qidx_n = lax.broadcasted_iota(I32, (N_HEADS, ds, page), 1)
        s_new = lax.dot_general(qbd_ref[...], _pad_rows(kn_ref[0], page).astype(BF16), NT,
                                preferred_element_type=F32) + bias_ref[1]
        s_new = jnp.where((lane_n <= qidx_n).reshape(rows, page), s_new, NEG)
        m = jnp.max(s_new, axis=1, keepdims=True)
        for pg in range(n_pages):
            b = pg // pages_per_block
            picked = jnp.sum(jnp.where(lane == b, sel, 0.0), axis=1, keepdims=True)
            bias = bias_ref[0] if pg == n_pages - 1 else bias_ref[2]
            s = jnp.where(picked > 0.5, s_ref[pg] + bias, NEG)
            s_ref[pg] = s
            m = jnp.maximum(m, jnp.max(s, axis=1, keepdims=True))
        p_new = jnp.exp(s_new - m)
        l = jnp.sum(p_new, axis=1, keepdims=True)
        pnew_ref[...] = p_new.astype(BF16)
        for pg in range(n_pages):
            p = jnp.exp(s_ref[pg] - m)
            l = l + jnp.sum(p, axis=1, keepdims=True)
            p_ref[pg] = p.astype(BF16)
        linv_ref[...] = 1.0 / l

    @pl.when(jnp.logical_and(ph == 1, g == 0))
    def _():
        acc_ref[...] = jnp.dot(pnew_ref[...], _pad_rows(vn_ref[0], page).astype(BF16),
                               preferred_element_type=F32)

    @pl.when(ph == 1)
    def _():
        slot = _ring_advance(pt_ref, page_index_of, u, n_batch * ng, v_hbm, vbuf, vsem)
        acc = acc_ref[...]
        for rr in range(pages):
            acc = acc + lax.dot_general(p_ref[g * pages + rr], vbuf[slot, rr].astype(BF16), NT,
                                        preferred_element_type=F32)
        acc_ref[...] = acc

    @pl.when(jnp.logical_and(ph == 1, g == ng - 1))
    def _():
        o_ref[0] = _head_diag_out(acc_ref[...] * linv_ref[...], ds)


def _moba_sample(page_table, q3, kn3, vn3, bias3, cache_k, cache_v):
    db, n_pages = page_table.shape
    ds = q3.shape[1]
    pages = PAGES_PER_STEP
    page = cache_k.shape[2]
    assert n_pages % pages == 0 and MOBA_BLOCK % page == 0 and (n_pages * page) % MOBA_BLOCK == 0
    assert T5_MAX_DIST <= page and ds <= page
    ng = n_pages // pages
    rows = N_HEADS * ds
    bmap = lambda b, ph, g, pt: (b, 0, 0)
    in_specs = [pl.BlockSpec((1, ds, GROUP_W), bmap), pl.BlockSpec((1, ds, GROUP_W), bmap),
                pl.BlockSpec((1, ds, GROUP_W), bmap),
                pl.BlockSpec(bias3.shape, lambda b, ph, g, pt: (0, 0, 0)),
                pl.BlockSpec(memory_space=pl.ANY), pl.BlockSpec(memory_space=pl.ANY)]
    k_ring, k_sem = _page_ring_scratch(pages, page)
    v_ring, v_sem = _page_ring_scratch(pages, page)
    return pl.pallas_call(
        functools.partial(_moba_sample_kernel, pages=pages, n_batch=db),
        grid_spec=pltpu.PrefetchScalarGridSpec(
            num_scalar_prefetch=1,
            grid=(db, 2, ng),
            in_specs=in_specs,
            out_specs=pl.BlockSpec((1, ds, GROUP_W), bmap),
            scratch_shapes=[pltpu.VMEM((rows, GROUP_W), BF16),
                            pltpu.VMEM((n_pages, rows, page), F32),
                            pltpu.VMEM((n_pages, rows, page), BF16),
                            pltpu.VMEM((rows, page), BF16),
                            pltpu.VMEM((rows, 1), F32),
                            pltpu.VMEM((rows, GROUP_W), F32),
                            k_ring, v_ring, k_sem, v_sem]),
        out_shape=jax.ShapeDtypeStruct((db, ds, GROUP_W), F32),
        compiler_params=_cparams("arbitrary", "arbitrary", "arbitrary"),
        name="moba_sample",
    )(page_table.reshape(-1), q3, kn3, vn3, bias3, cache_k, cache_v)


def _outproj_kernel(of_ref, om_ref, x_ref, gate_ref, shift_ref, scale_ref, gf_ref, gm_ref, wo_ref, gffn_ref,
                    wr_ref, br_ref, cnt0_ref, x1_ref, h2_ref, ri_ref, rw_ref, cnt_ref, carry_ref):
    t = pl.program_id(0)

    @pl.when(t == 0)
    def _():
        carry_ref[...] = cnt0_ref[...]

    tm = x_ref.shape[0]
    nf = (_rms(of_ref[...]) * gf_ref[...]).astype(BF16)
    nm = (_rms(om_ref[...]) * gm_ref[...]).astype(BF16)
    o = (jnp.dot(nf, wo_ref[0:GROUP_W, :], preferred_element_type=F32)
         + jnp.dot(nm, wo_ref[GROUP_W:2 * GROUP_W, :], preferred_element_type=F32))
    x1 = x_ref[...] + gate_ref[...] * o
    x1_ref[...] = x1
    h2 = _rms(x1) * gffn_ref[...]
    h2 = h2 * (1.0 + scale_ref[...]) + shift_ref[...]
    hi = h2.astype(BF16)
    half = h2.shape[1] // 2
    hi32 = pltpu.bitcast(hi.astype(F32), jnp.uint32)
    h2_ref[...] = jnp.bitwise_or(jnp.right_shift(hi32[:, :half], jnp.uint32(16)),
                                 jnp.bitwise_and(hi32[:, half:], jnp.uint32(0xFFFF0000)))

    lo = (h2 - hi.astype(F32)).astype(BF16)
    wr = wr_ref[...]
    whi = wr.astype(BF16)
    wlo = (wr - whi.astype(F32)).astype(BF16)
    lg = (jnp.dot(hi, whi, preferred_element_type=F32) + jnp.dot(lo, whi, preferred_element_type=F32)
          + jnp.dot(hi, wlo, preferred_element_type=F32)) + br_ref[...]
    lane = lax.broadcasted_iota(I32, (tm, LANES), 1)
    lane_f = lane.astype(F32)
    ninf = -jnp.inf
    is_g = jnp.logical_and(lane >= N_EXPERTS, lane < N_EXPERTS + N_GROUPS)
    glog = jnp.where(is_g, lg, ninf)
    gmax = jnp.max(glog, axis=1, keepdims=True)
    gidx = jnp.min(jnp.where(glog == gmax, lane_f, 2.0 * LANES), axis=1, keepdims=True).astype(I32) - N_EXPERTS
    g_w = 1.0 / jnp.sum(jnp.exp(glog - gmax), axis=1, keepdims=True)
    in_grp = jnp.logical_and(lane >= gidx * EXPERTS_PER_GROUP, lane < (gidx + 1) * EXPERTS_PER_GROUP)
    elog = jnp.where(in_grp, lg, ninf)
    e1 = jnp.max(elog, axis=1, keepdims=True)
    i1 = jnp.min(jnp.where(elog == e1, lane_f, 2.0 * LANES), axis=1, keepdims=True).astype(I32)
    z = jnp.sum(jnp.exp(elog - e1), axis=1, keepdims=True)
    elog2 = jnp.where(lane == i1, ninf, elog)
    e2 = jnp.max(elog2, axis=1, keepdims=True)
    i2 = jnp.min(jnp.where(elog2 == e2, lane_f, 2.0 * LANES), axis=1, keepdims=True).astype(I32)
    p1 = 1.0 / z
    p2 = jnp.exp(e2 - e1) / z
    w1 = g_w * (p1 / (p1 + p2))
    w2 = g_w * (p2 / (p1 + p2))

    a = jnp.where(jnp.logical_or(lane == i1, lane == i2), 1.0, 0.0)
    r = lax.broadcasted_iota(I32, (tm, tm), 0)
    c = lax.broadcasted_iota(I32, (tm, tm), 1)
    before = jnp.where(c < r, 1.0, 0.0).astype(BF16)
    pos = jnp.dot(before, a.astype(BF16), preferred_element_type=F32) + carry_ref[...]
    r1 = jnp.sum(jnp.where(lane == i1, pos, 0.0), axis=1, keepdims=True)
    r2 = jnp.sum(jnp.where(lane == i2, pos, 0.0), axis=1, keepdims=True)
    carry_ref[...] = carry_ref[...] + jnp.sum(a, axis=0, keepdims=True)
    cnt_ref[...] = carry_ref[...]

    ri = jnp.where(lane == 0, i1, 0) + jnp.where(lane == 1, i2, 0)
    ri = ri + jnp.where(lane == 2, r1.astype(I32), 0) + jnp.where(lane == 3, r2.astype(I32), 0)
    ri_ref[...] = ri
    rw_ref[...] = jnp.where(lane == 0, w1, 0.0) + jnp.where(lane == 1, w2, 0.0)


def _outproj(of, om, x2, gate3, shift3, scale3, gf, gm, wo, gffn, wr, br, cnt0, *, rows_per_mod, name):
    n, d = x2.shape
    tm = ROW_TILE
    nt = n // tm
    mod_rows = gate3.shape[1]
    mod_map = lambda t: ((t * tm) // rows_per_mod, 0, 0)
    row_map = lambda t: (t, 0)
    const2 = lambda t: (0, 0)
    return pl.pallas_call(
        _outproj_kernel,
        grid=(nt,),
        in_specs=[pl.BlockSpec((tm, GROUP_W), row_map), pl.BlockSpec((tm, GROUP_W), row_map),
                  pl.BlockSpec((tm, d), row_map),
                  pl.BlockSpec((None, mod_rows, d), mod_map), pl.BlockSpec((None, mod_rows, d), mod_map),
                  pl.BlockSpec((None, mod_rows, d), mod_map),
                  pl.BlockSpec((1, GROUP_W), const2), pl.BlockSpec((1, GROUP_W), const2),
                  pl.BlockSpec((d, d), const2), pl.BlockSpec((1, d), const2),
                  pl.BlockSpec((d, LANES), const2), pl.BlockSpec((1, LANES), const2),
                  pl.BlockSpec((1, LANES), const2)],
        out_specs=[pl.BlockSpec((tm, d), row_map), pl.BlockSpec((tm, d // 2), row_map),
                   pl.BlockSpec((tm, LANES), row_map), pl.BlockSpec((tm, LANES), row_map),
                   pl.BlockSpec((1, LANES), const2)],
        out_shape=[jax.ShapeDtypeStruct((n, d), F32), jax.ShapeDtypeStruct((n, d // 2), jnp.uint32),
                   jax.ShapeDtypeStruct((n, LANES), I32), jax.ShapeDtypeStruct((n, LANES), F32),
                   jax.ShapeDtypeStruct((1, LANES), F32)],
        scratch_shapes=[pltpu.VMEM((1, LANES), F32)],
        compiler_params=_cparams("arbitrary"),
        name=name,
    )(of, om, x2, gate3, shift3, scale3, gf, gm, wo, gffn, wr, br, cnt0)


def _scatter_kernel(dest_ref, h_ref, rows_in_ref, rows_ref, sem):
    del rows_in_ref
    tm = h_ref.shape[0]

    def row_copy(r, k):
        return pltpu.make_async_copy(h_ref.at[pl.ds(r, 1)], rows_ref.at[pl.ds(dest_ref[k, r], 1)], sem)

    def issue(r, carry):
        row_copy(r, 0).start(priority=0)
        row_copy(r, 1).start(priority=1)
        return carry

    lax.fori_loop(0, tm, issue, 0, unroll=DMA_ISSUE_UNROLL)
    for _ in range(2):
        pltpu.make_async_copy(h_ref, rows_ref.at[pl.ds(0, tm)], sem).wait()


def _scatter_rows(dest2, h2, rows):
    n, d = h2.shape
    tm = ROW_TILE
    return pl.pallas_call(
        _scatter_kernel,
        grid=(n // tm,),
        in_specs=[pl.BlockSpec((2, tm), lambda t: (0, t), memory_space=pltpu.SMEM),
                  pl.BlockSpec((tm, d), lambda t: (t, 0)),
                  pl.BlockSpec(memory_space=pl.ANY)],
        out_specs=pl.BlockSpec(memory_space=pl.ANY),
        out_shape=jax.ShapeDtypeStruct(rows.shape, rows.dtype),
        scratch_shapes=[pltpu.SemaphoreType.DMA(())],
        input_output_aliases={2: 0},
        compiler_params=_cparams("arbitrary"),
        name="moe_scatter",
    )(dest2, h2, rows)


def _moe_kernel(be_ref, nu_ref, rows_ref, w1_ref, w3_ref, w2_ref, y_ref, w1b_ref, w3b_ref, w2b_ref):
    blk = pl.program_id(0)

    @pl.when(blk < nu_ref[0])
    def _():
        prev = be_ref[jnp.maximum(blk - 1, 0)]

        @pl.when(jnp.logical_or(blk == 0, be_ref[blk] != prev))
        def _():
            w1b_ref[...] = w1_ref[0].astype(BF16)
            w3b_ref[...] = w3_ref[0].astype(BF16)
            w2b_ref[...] = w2_ref[0].astype(BF16)

        packed = rows_ref[...]
        x_lo = pltpu.bitcast(jnp.left_shift(packed, jnp.uint32(16)), F32)
        x_hi = pltpu.bitcast(jnp.bitwise_and(packed, jnp.uint32(0xFFFF0000)), F32)
        x = jnp.concatenate([x_lo, x_hi], axis=1).astype(BF16)
        a = jnp.dot(x, w1b_ref[...], preferred_element_type=F32)
        b = jnp.dot(x, w3b_ref[...], preferred_element_type=F32)
        hm = (a * jax.nn.sigmoid(a)) * b
        y_ref[...] = jnp.dot(hm.astype(BF16), w2b_ref[...], preferred_element_type=F32)

    @pl.when(blk >= nu_ref[0])
    def _():
        y_ref[...] = jnp.zeros_like(y_ref)


def _moe_experts(block_e, n_used, rows, w1, w3, w2):
    p = rows.shape[0]
    d, de = w1.shape[1], w1.shape[2]
    assert rows.shape[1] * 2 == d
    nblk = p // MOE_BLOCK
    row_map = lambda b, be, nu: (jnp.minimum(b, nu[0] - 1), 0)
    return pl.pallas_call(
        _moe_kernel,
        grid_spec=pltpu.PrefetchScalarGridSpec(
            num_scalar_prefetch=2,
            grid=(nblk,),
            in_specs=[pl.BlockSpec((MOE_BLOCK, d // 2), row_map),
                      pl.BlockSpec((1, d, de), lambda b, be, nu: (be[b], 0, 0)),
                      pl.BlockSpec((1, d, de), lambda b, be, nu: (be[b], 0, 0)),
                      pl.BlockSpec((1, de, d), lambda b, be, nu: (be[b], 0, 0))],
            out_specs=pl.BlockSpec((MOE_BLOCK, d), lambda b, be, nu: (b, 0)),
            scratch_shapes=[pltpu.VMEM((d, de), BF16), pltpu.VMEM((d, de), BF16), pltpu.VMEM((de, d), BF16)]),
        out_shape=jax.ShapeDtypeStruct((p, d), F32),
        compiler_params=_cparams("arbitrary"),
        name="moe_experts",
    )(block_e, n_used, rows, w1, w3, w2)


def _combine_kernel(dest_ref, x1_ref, gate_ref, rw_ref, gfin_ref, y_ref, o_ref, ybuf_ref, sem):
    tm = x1_ref.shape[0]

    def row_copy(r, k):
        return pltpu.make_async_copy(y_ref.at[pl.ds(dest_ref[k, r], 1)], ybuf_ref.at[k, pl.ds(r, 1)], sem)

    def issue(r, carry):
        row_copy(r, 0).start(priority=0)
        row_copy(r, 1).start(priority=1)
        return carry

    lax.fori_loop(0, tm, issue, 0, unroll=DMA_ISSUE_UNROLL)
    for k in range(2):
        pltpu.make_async_copy(y_ref.at[pl.ds(0, tm)], ybuf_ref.at[k], sem).wait()
    rw = rw_ref[...]
    moe = rw[:, 0:1] * ybuf_ref[0] + rw[:, 1:2] * ybuf_ref[1]
    xo = x1_ref[...] + gate_ref[...] * moe
    o_ref[...] = _rms(xo) * gfin_ref[...]


def _combine(dest2, x1, gate3, rw, gfin, y, *, rows_per_mod, name):
    n, d = x1.shape
    tm = ROW_TILE
    mod_rows = gate3.shape[1]
    return pl.pallas_call(
        _combine_kernel,
        grid=(n // tm,),
        in_specs=[pl.BlockSpec((2, tm), lambda t: (0, t), memory_space=pltpu.SMEM),
                  pl.BlockSpec((tm, d), lambda t: (t, 0)),
                  pl.BlockSpec((None, mod_rows, d), lambda t: ((t * tm) // rows_per_mod, 0, 0)),
                  pl.BlockSpec((tm, LANES), lambda t: (t, 0)),
                  pl.BlockSpec((1, d), lambda t: (0, 0)),
                  pl.BlockSpec(memory_space=pl.ANY)],
        out_specs=pl.BlockSpec((tm, d), lambda t: (t, 0)),
        out_shape=jax.ShapeDtypeStruct((n, d), F32),
        scratch_shapes=[pltpu.VMEM((2, tm, d), F32), pltpu.SemaphoreType.DMA(())],
        compiler_params=_cparams("arbitrary"),
        name=name,
    )(dest2, x1, gate3, rw, gfin, y)


def kernel(x_prompt, x_sample, cache_fox_k, cache_fox_v, cache_fox_logf, cache_moba_k, cache_moba_v, page_table, c_prompt, c_sample, w_ada, b_ada, g_attn, w_in, b_forget, g_out_fox, g_out_moba, t5_bias, w_out, g_ffn, w_router_group, b_router_group, w_router_expert, b_router_expert, w1, w3, w2, g_final):
    bsz, seq, d = x_prompt.shape
    db, ds, _ = x_sample.shape
    depth = w_ada.shape[0]
    n_phys, page = cache_fox_k.shape[1], cache_fox_k.shape[2]
    assert depth == 1, "one trunk layer"
    assert seq % ROW_TILE == 0 and (db * ds) % ROW_TILE == 0 and ROW_TILE % ds == 0
    assert N_EXPERTS + N_GROUPS <= LANES and N_HEADS <= LANES
    l = 0
    n_p, n_s = bsz * seq, db * ds
    xp2 = x_prompt.reshape(n_p, d)
    xs2 = x_sample.reshape(n_s, d)

    mod = _ada(jnp.concatenate([c_prompt, c_sample], axis=0), w_ada[l], b_ada[l])
    mod_p = [mod[:bsz, i * d:(i + 1) * d].reshape(bsz, 1, d) for i in range(6)]
    mod_s = [jnp.repeat(mod[bsz:, i * d:(i + 1) * d], ds, axis=0).reshape(n_s // ROW_TILE, ROW_TILE, d)
             for i in range(6)]

    w = GROUP_W
    wl = w_in[l]
    wcat = jnp.concatenate([wl[:, :3 * w], wl[:, 3 * w + N_HEADS:],
                            wl[:, 3 * w:3 * w + N_HEADS], jnp.zeros((d, LANES - N_HEADS), F32)],
                           axis=1).astype(BF16)
    bf_pad = jnp.pad(b_forget[l], (0, LANES - N_HEADS)).reshape(1, LANES)
    g_attn2 = g_attn[l].reshape(1, d)

    fqt, fk_t, fv_t, lf_t, mqt, mk_t, mv_t, fcum, frow, kmean, fkb, fvtb, mkb, mvtb = _inproj(
        xp2, mod_p[0], mod_p[1], g_attn2, wcat, bf_pad, rows_per_mod=seq, prompt_extras=True, seq_len=seq)
    sfq, sfk, sfv, slf, smq, smk, smv = _inproj(
        xs2, mod_s[0], mod_s[1], g_attn2, wcat, bf_pad, rows_per_mod=ROW_TILE, prompt_extras=False, seq_len=ds)

    o_fox_p = _fox_prompt(fqt, fkb, fvtb, fcum, frow, bsz, seq)
    bias_p = _t5_tiles(t5_bias, (0, MOBA_BLOCK, 2 * MOBA_BLOCK), MOBA_BLOCK, MOBA_BLOCK, sign=-1)
    o_moba_p = _moba_prompt(mqt, mkb, mvtb, kmean.reshape(bsz, seq // MOBA_BLOCK, w), bias_p, bsz, seq)

    to3 = lambda a: a.reshape(db, ds, w)
    page_t = lambda c: c[l].transpose(0, 2, 3, 1).reshape(n_phys, w, page)
    cfk, cfv, cmk, cmv = page_t(cache_fox_k), page_t(cache_fox_v), page_t(cache_moba_k), page_t(cache_moba_v)
    clf_t = cache_fox_logf[l].transpose(0, 2, 1)
    slf_t = jnp.pad(slf.reshape(db, ds, N_HEADS).transpose(0, 2, 1), ((0, 0), (0, 0), (0, page - ds)))
    o_fox_s = _fox_sample(page_table, to3(sfq), to3(sfk), to3(sfv), slf_t, cfk, cfv, clf_t)
    bias_s = _t5_tiles(t5_bias, (page, 0, 2 * T5_MAX_DIST + page), ds, page, sign=1)
    bias_s = bias_s.transpose(1, 0, 2, 3).reshape(3, N_HEADS * ds, page)
    o_moba_s = _moba_sample(page_table, to3(smq), to3(smk), to3(smv), bias_s, cmk, cmv)

    wr = jnp.concatenate([w_router_expert[l], w_router_group[l],
                          jnp.zeros((d, LANES - N_EXPERTS - N_GROUPS), F32)], axis=1)
    br = jnp.concatenate([b_router_expert[l], b_router_group[l],
                          jnp.zeros((LANES - N_EXPERTS - N_GROUPS,), F32)]).reshape(1, LANES)
    wo = w_out[l].astype(BF16)
    gf, gm, gffn = g_out_fox[l].reshape(1, w), g_out_moba[l].reshape(1, w), g_ffn[l].reshape(1, d)
    x1_p, h2_p, ri_p, rw_p, cnt_p = _outproj(
        o_fox_p, o_moba_p, xp2, mod_p[2], mod_p[3], mod_p[4], gf, gm, wo, gffn, wr, br,
        jnp.zeros((1, LANES), F32), rows_per_mod=seq, name="outproj_prompt")
    x1_s, h2_s, ri_s, rw_s, cnt = _outproj(
        o_fox_s.reshape(n_s, w), o_moba_s.reshape(n_s, w), xs2, mod_s[2], mod_s[3], mod_s[4], gf, gm, wo, gffn,
        wr, br, cnt_p, rows_per_mod=ROW_TILE, name="outproj_sample")

    counts = cnt[0, :N_EXPERTS].astype(I32)
    padded = (counts + MOE_BLOCK - 1) // MOE_BLOCK * MOE_BLOCK
    pend = jnp.cumsum(padded)
    pstart = pend - padded
    n_asg = 2 * (n_p + n_s)
    n_blocks = -(-(n_asg + N_EXPERTS * (MOE_BLOCK - 1)) // MOE_BLOCK)
    blk_start = jnp.arange(n_blocks, dtype=I32) * MOE_BLOCK
    block_e = jnp.clip(jnp.sum((pend[None, :] <= blk_start[:, None]).astype(I32), axis=1), 0, N_EXPERTS - 1)
    n_used = (pend[-1:] // MOE_BLOCK).astype(I32)

    def dest_of(ri):
        return (pstart[ri[:, 0:2]] + ri[:, 2:4]).T.astype(I32)

    dest_p, dest_s = dest_of(ri_p), dest_of(ri_s)
    rows = jnp.zeros((n_blocks * MOE_BLOCK, d // 2), jnp.uint32)
    rows = _scatter_rows(dest_p, h2_p, rows)
    rows = _scatter_rows(dest_s, h2_s, rows)
    y = _moe_experts(block_e, n_used, rows, w1[l], w3[l], w2[l])
    gfin = g_final.reshape(1, d)
    y_prompt = _combine(dest_p, x1_p, mod_p[5], rw_p, gfin, y, rows_per_mod=seq, name="combine_prompt")
    y_sample = _combine(dest_s, x1_s, mod_s[5], rw_s, gfin, y, rows_per_mod=ROW_TILE, name="combine_sample")

    hp = lambda a: a.reshape(depth, bsz, N_HEADS, HEAD_DIM, seq).transpose(0, 1, 4, 2, 3)
    hs = lambda a: a.reshape(depth, db, ds, N_HEADS, HEAD_DIM)
    return (y_prompt.reshape(bsz, seq, d), y_sample.reshape(db, ds, d),
            hp(fk_t), hp(fv_t), lf_t.reshape(depth, bsz, N_HEADS, seq).transpose(0, 1, 3, 2), hp(mk_t), hp(mv_t),
            hs(sfk), hs(sfv), slf.reshape(depth, db, ds, N_HEADS), hs(smk), hs(smv))
```

```python
import functools
import math

import jax
import jax.numpy as jnp
from jax import lax
from jax.experimental import pallas as pl
from jax.experimental.pallas import tpu as pltpu

F32 = jnp.float32
BF16 = jnp.bfloat16
I32 = jnp.int32

HEAD_DIM = 64
N_HEADS = 8
GROUP_W = N_HEADS * HEAD_DIM
LANES = 128
HEADS_PER_LANE_TILE = LANES // HEAD_DIM
N_PAIRS = N_HEADS // HEADS_PER_LANE_TILE
SM_SCALE = HEAD_DIM ** -0.5
LOG2E = math.log2(math.e)
MOBA_BLOCK = 256
MOBA_TOPK = 3
T5_BUCKETS = 32
T5_MAX_DIST = 128
N_GROUPS = 4
EXPERTS_PER_GROUP = 8
N_EXPERTS = N_GROUPS * EXPERTS_PER_GROUP
RMS_EPS = 1e-6
ROW_TILE = 256
ATTN_TILE = 256
K_CHUNK = 64
MOE_BLOCK = 256
PAGES_PER_STEP = 8
PAGE_RING = 3
NEG = -1e30
VMEM_LIMIT = 48 * 1024 * 1024
HIGHEST = lax.Precision.HIGHEST
NT = (((1,), (1,)), ((), ()))


def _cparams(*sem):
    return pltpu.CompilerParams(dimension_semantics=sem, vmem_limit_bytes=VMEM_LIMIT)


def _rms(x):
    return x * lax.rsqrt(jnp.mean(x * x, axis=-1, keepdims=True) + RMS_EPS)


def _split3_bf16(x):
    hi = x.astype(BF16)
    r1 = x - hi.astype(F32)
    mid = r1.astype(BF16)
    lo = (r1 - mid.astype(F32)).astype(BF16)
    return hi, mid, lo


def _ada_kernel(c_ref, w_ref, b_ref, o_ref):
    c = c_ref[...]
    a = c * jax.nn.sigmoid(c)
    o_ref[...] = jnp.dot(a, w_ref[...], preferred_element_type=F32, precision=HIGHEST) + b_ref[...]


def _ada(c, w, b):
    n, d = c.shape
    e = w.shape[1]
    tn = 1024
    return pl.pallas_call(
        _ada_kernel,
        grid=(e // tn,),
        in_specs=[pl.BlockSpec((n, d), lambda j: (0, 0)),
                  pl.BlockSpec((d, tn), lambda j: (0, j)),
                  pl.BlockSpec((1, tn), lambda j: (0, j))],
        out_specs=pl.BlockSpec((n, tn), lambda j: (0, j)),
        out_shape=jax.ShapeDtypeStruct((n, e), F32),
        compiler_params=_cparams("arbitrary"),
        name="ada",
    )(c, w, b.reshape(1, e))


def _inproj_kernel(x_ref, shift_ref, scale_ref, g_ref, w_ref, bf_ref,
                   fq_ref, fk_ref, fv_ref, lf_ref, mq_ref, mk_ref, mv_ref, *rest,
                   prompt_extras, tiles_per_seq):
    x = x_ref[...]
    h = _rms(x) * g_ref[...]
    h = h * (1.0 + scale_ref[...]) + shift_ref[...]
    z = jnp.dot(h.astype(BF16), w_ref[...], preferred_element_type=F32)
    w = GROUP_W
    q_scale = SM_SCALE * LOG2E if prompt_extras else SM_SCALE
    fq = z[:, 0:w] * q_scale
    mq = z[:, 3 * w:4 * w] * q_scale
    if prompt_extras:
        fq_ref[0] = fq.T.astype(BF16)
        mq_ref[0] = mq.T.astype(BF16)
    else:
        fq_ref[...] = fq.astype(BF16)
        mq_ref[...] = mq.astype(BF16)
    fk, fv = z[:, w:2 * w], z[:, 2 * w:3 * w]
    mk, mv = z[:, 4 * w:5 * w], z[:, 5 * w:6 * w]
    fg = z[:, 6 * w:6 * w + LANES] + bf_ref[...]
    lf = jnp.minimum(fg, 0.0) - jnp.log1p(jnp.exp(-jnp.abs(fg)))
    if not prompt_extras:
        fk_ref[...] = fk
        fv_ref[...] = fv
        mk_ref[...] = mk
        mv_ref[...] = mv
        lf_ref[...] = lf[:, :N_HEADS]
        return
    fcum_ref, frow_ref, kmean_ref, fkb_ref, fvtb_ref, mkb_ref, mvtb_ref, carry_ref = rest
    tm = x.shape[0]
    fk_ref[0] = fk.T
    fv_t = fv.T
    fv_ref[0] = fv_t
    mk_ref[0] = mk.T
    mv_t = mv.T
    mv_ref[0] = mv_t
    fkb_ref[...] = fk.astype(BF16)
    fvtb_ref[0] = fv_t.astype(BF16)
    mkb_ref[...] = mk.astype(BF16)
    mvtb_ref[0] = mv_t.astype(BF16)
    lf_ref[0] = lf.T[:N_HEADS, :]
    kmean_ref[0] = jnp.mean(mk, axis=0, keepdims=True)

    @pl.when(pl.program_id(0) % tiles_per_seq == 0)
    def _():
        carry_ref[...] = jnp.zeros_like(carry_ref)

    r = lax.broadcasted_iota(I32, (tm, tm), 0)
    c = lax.broadcasted_iota(I32, (tm, tm), 1)
    tri = jnp.where(c <= r, 1.0, 0.0).astype(BF16)
    hi, mid, lo = _split3_bf16(lf)
    cs = (jnp.dot(tri, hi, preferred_element_type=F32)
          + jnp.dot(tri, mid, preferred_element_type=F32)
          + jnp.dot(tri, lo, preferred_element_type=F32)) + carry_ref[...]
    carry_ref[...] = cs[tm - 1:tm, :]
    cs2 = cs * LOG2E
    fcum_ref[...] = cs2[:, :N_HEADS]
    frow_ref[0] = cs2.T[:N_HEADS, :]


def _inproj(x2, shift3, scale3, g_attn, wcat, bf_pad, *, rows_per_mod, prompt_extras, seq_len):
    n, d = x2.shape
    tm = ROW_TILE
    assert n % tm == 0
    nt = n // tm
    mod_rows = shift3.shape[1]
    mod_map = lambda t: ((t * tm) // rows_per_mod, 0, 0)
    row_map = lambda t: (t, 0)
    const2 = lambda t: (0, 0)
    ecols = wcat.shape[1]
    row_bf16 = (jax.ShapeDtypeStruct((n, GROUP_W), BF16), pl.BlockSpec((tm, GROUP_W), row_map))
    scratch = []
    tiles_per_seq = 1
    if prompt_extras:
        assert tm == MOBA_BLOCK and seq_len % tm == 0
        tiles_per_seq = seq_len // tm
        nb = n // seq_len
        t_map = lambda t: (t // tiles_per_seq, 0, t % tiles_per_seq)
        kv = (jax.ShapeDtypeStruct((nb, GROUP_W, seq_len), F32), pl.BlockSpec((1, GROUP_W, tm), t_map))
        lfo = (jax.ShapeDtypeStruct((nb, N_HEADS, seq_len), F32), pl.BlockSpec((1, N_HEADS, tm), t_map))
        kvt_bf16 = (jax.ShapeDtypeStruct((nb, GROUP_W, seq_len), BF16), pl.BlockSpec((1, GROUP_W, tm), t_map))
        outs = [kvt_bf16, kv, kv, lfo, kvt_bf16, kv, kv,
                (jax.ShapeDtypeStruct((n, N_HEADS), F32), pl.BlockSpec((tm, N_HEADS), row_map)),
                lfo,
                (jax.ShapeDtypeStruct((nt, 1, GROUP_W), F32), pl.BlockSpec((1, 1, GROUP_W), lambda t: (t, 0, 0)))]
        outs += [row_bf16, kvt_bf16, row_bf16, kvt_bf16]
        scratch = [pltpu.VMEM((1, LANES), F32)]
    else:
        kv = (jax.ShapeDtypeStruct((n, GROUP_W), F32), pl.BlockSpec((tm, GROUP_W), row_map))
        lfo = (jax.ShapeDtypeStruct((n, N_HEADS), F32), pl.BlockSpec((tm, N_HEADS), row_map))
        outs = [row_bf16, kv, kv, lfo, row_bf16, kv, kv]
    return pl.pallas_call(
        functools.partial(_inproj_kernel, prompt_extras=prompt_extras, tiles_per_seq=tiles_per_seq),
        grid=(nt,),
        in_specs=[pl.BlockSpec((tm, d), row_map),
                  pl.BlockSpec((None, mod_rows, d), mod_map),
                  pl.BlockSpec((None, mod_rows, d), mod_map),
                  pl.BlockSpec((1, d), const2),
                  pl.BlockSpec((d, ecols), const2),
                  pl.BlockSpec((1, LANES), const2)],
        out_specs=[o[1] for o in outs],
        out_shape=[o[0] for o in outs],
        scratch_shapes=scratch,
        compiler_params=_cparams("arbitrary"),
        name="inproj_prompt" if prompt_extras else "inproj_sample",
    )(x2, shift3, scale3, g_attn, wcat, bf_pad)


def _t5_kernel(t5_ref, o_ref, *, offs, sign, scale):
    h = pl.program_id(0)
    rows, cols = o_ref.shape[2], o_ref.shape[3]
    r = lax.broadcasted_iota(I32, (rows, cols), 0)
    c = lax.broadcasted_iota(I32, (rows, cols), 1)
    max_exact = T5_BUCKETS // 2
    for k, off in enumerate(offs):
        rel = jnp.maximum(off + sign * (r - c), 0)
        relf = jnp.maximum(rel, 1).astype(F32)
        large = max_exact + (jnp.log(relf / max_exact) / math.log(T5_MAX_DIST / max_exact)
                             * (T5_BUCKETS - max_exact)).astype(I32)
        large = jnp.minimum(large, T5_BUCKETS - 1)
        bucket = jnp.where(rel < max_exact, rel, large)
        acc = jnp.zeros((rows, cols), F32)
        for b in range(T5_BUCKETS):
            acc = jnp.where(bucket == b, t5_ref[b, h], acc)
        o_ref[0, k] = acc if scale == 1.0 else acc * scale


def _t5_tiles(t5_bias, offs, rows, cols, sign, scale=1.0):
    nh = t5_bias.shape[1]
    return pl.pallas_call(
        functools.partial(_t5_kernel, offs=tuple(offs), sign=sign, scale=scale),
        grid=(nh,),
        in_specs=[pl.BlockSpec(memory_space=pltpu.SMEM)],
        out_specs=pl.BlockSpec((1, len(offs), rows, cols), lambda h: (h, 0, 0, 0)),
        out_shape=jax.ShapeDtypeStruct((nh, len(offs), rows, cols), F32),
        compiler_params=_cparams("arbitrary"),
        name="t5_tiles",
    )(t5_bias)


def _pair_masks(rows):
    lane = lax.broadcasted_iota(I32, (rows, LANES), 1)
    lo = lane < HEAD_DIM
    return lo, jnp.logical_not(lo)


def _masked_qt(qt_ref, h):
    p, e = divmod(h, HEADS_PER_LANE_TILE)
    qt = qt_ref[0, p * LANES:(p + 1) * LANES, :]
    row = lax.broadcasted_iota(I32, qt.shape, 0)
    keep = (row >= HEAD_DIM) if e else (row < HEAD_DIM)
    return jnp.where(keep, qt, jnp.zeros_like(qt))


def _flash_step(qt_ref, k_ref, vt_ref, adjust, states):
    tk, tq = k_ref.shape[0], qt_ref.shape[2]
    chunks = [slice(c * K_CHUNK, (c + 1) * K_CHUNK) for c in range(tk // K_CHUNK)]
    for h, (m_ref, _, _, s_scr, _, a_ref) in enumerate(states):
        pair = slice((h // HEADS_PER_LANE_TILE) * LANES, (h // HEADS_PER_LANE_TILE + 1) * LANES)
        qmt = _masked_qt(qt_ref, h)
        m8 = jnp.full((8, tq), NEG, F32)
        for c, rows in enumerate(chunks):
            s = adjust(h, c, jnp.dot(k_ref[rows, pair], qmt, preferred_element_type=F32))
            s_scr[rows, :] = s
            m8 = jnp.maximum(m8, jnp.max(s.reshape(K_CHUNK // 8, 8, tq), axis=0))
        m_prev = m_ref[...]
        m_new = jnp.maximum(m_prev, jnp.max(m8, axis=0, keepdims=True))
        a_ref[...] = jnp.exp2(m_prev - m_new)
        m_ref[...] = m_new
    for m_ref, l_ref, _, s_scr, p_scr, a_ref in states:
        m_new = m_ref[...]
        l8 = jnp.zeros((8, tq), F32)
        for rows in chunks:
            p = jnp.exp2(s_scr[rows, :] - m_new)
            l8 = l8 + jnp.sum(p.reshape(K_CHUNK // 8, 8, tq), axis=0)
            p_scr[rows, :] = p.astype(BF16)
        l_ref[...] = a_ref[...] * l_ref[...] + jnp.sum(l8, axis=0, keepdims=True)
    for h, (_, _, acc_ref, _, p_scr, a_ref) in enumerate(states):
        feat = slice(h * HEAD_DIM, (h + 1) * HEAD_DIM)
        pv = jnp.dot(vt_ref[0, feat, :], p_scr[...], preferred_element_type=F32)
        acc_ref[...] = a_ref[...] * acc_ref[...] + pv


FLASH_BUFS_PER_HEAD = 6


def _flash_scratch(tq, tk):
    per_head = [pltpu.VMEM((1, tq), F32), pltpu.VMEM((1, tq), F32), pltpu.VMEM((HEAD_DIM, tq), F32),
                pltpu.VMEM((tk, tq), F32), pltpu.VMEM((tk, tq), BF16), pltpu.VMEM((1, tq), F32)]
    return per_head * N_HEADS


def _flash_states(scratch):
    n = FLASH_BUFS_PER_HEAD
    return [scratch[n * h:n * (h + 1)] for h in range(N_HEADS)]


def _flash_init(states):
    for m_ref, l_ref, acc_ref, _, _, _ in states:
        m_ref[...] = jnp.full_like(m_ref, NEG)
        l_ref[...] = jnp.zeros_like(l_ref)
        acc_ref[...] = jnp.zeros_like(acc_ref)


def _flash_finish(o_ref, states):
    o_t = jnp.concatenate([acc_ref[...] / l_ref[...] for _, l_ref, acc_ref, _, _, _ in states], axis=0)
    o_ref[...] = o_t.T


def _chunk_causal(c, tq):
    krow = c * K_CHUNK + lax.broadcasted_iota(I32, (K_CHUNK, tq), 0)
    qcol = lax.broadcasted_iota(I32, (K_CHUNK, tq), 1)
    return krow <= qcol


def _fox_prompt_kernel(qi_ref, kj_ref, q_ref, k_ref, vt_ref, fq_ref, fk_ref, o_ref, *scratch):
    states = _flash_states(scratch)
    t = pl.program_id(1)
    i = qi_ref[t]
    j = kj_ref[t]
    tq = q_ref.shape[2]

    @pl.when(j == 0)
    def _():
        _flash_init(states)

    def step(diagonal):
        def adjust(h, c, s):
            s = s + fq_ref[0, h:h + 1, :] - fk_ref[c * K_CHUNK:(c + 1) * K_CHUNK, h:h + 1]
            return jnp.where(_chunk_causal(c, tq), s, NEG) if diagonal else s

        _flash_step(q_ref, k_ref, vt_ref, adjust, states)

    @pl.when(j < i)
    def _():
        step(False)

    @pl.when(j == i)
    def _():
        step(True)
        _flash_finish(o_ref, states)


def _tri_tables(nq, own_first):
    qi, kj = [], []
    for i in range(nq):
        order = ([i] + list(range(i))) if own_first else list(range(i + 1))
        for j in order:
            qi.append(i)
            kj.append(j)
    return jnp.asarray(qi, I32), jnp.asarray(kj, I32)


def _fox_prompt(qt, k, vt, fcol, frow, batch, seq):
    tq = tk = ATTN_TILE
    nq = seq // tq
    qi, kj = _tri_tables(nq, own_first=False)
    qmap = lambda b, t, qi, kj: (b * nq + qi[t], 0)
    kmap = lambda b, t, qi, kj: (b * nq + kj[t], 0)
    qtmap = lambda b, t, qi, kj: (b, 0, qi[t])
    ktmap = lambda b, t, qi, kj: (b, 0, kj[t])
    return pl.pallas_call(
        _fox_prompt_kernel,
        grid_spec=pltpu.PrefetchScalarGridSpec(
            num_scalar_prefetch=2,
            grid=(batch, qi.shape[0]),
            in_specs=[pl.BlockSpec((1, GROUP_W, tq), qtmap),
                      pl.BlockSpec((tk, GROUP_W), kmap),
                      pl.BlockSpec((1, GROUP_W, tk), ktmap),
                      pl.BlockSpec((1, N_HEADS, tq), qtmap),
                      pl.BlockSpec((tk, N_HEADS), kmap)],
            out_specs=pl.BlockSpec((tq, GROUP_W), qmap),
            scratch_shapes=_flash_scratch(tq, tk)),
        out_shape=jax.ShapeDtypeStruct((batch * seq, GROUP_W), F32),
        compiler_params=_cparams("arbitrary", "arbitrary"),
        name="fox_prompt",
    )(qi, kj, qt, k, vt, frow, fcol)


def _moba_prompt_kernel(qi_ref, kj_ref, q_ref, k_ref, vt_ref, kmean_ref, bias_ref, o_ref, *scratch):
    states = _flash_states(scratch[:-1])
    sel_ref = scratch[-1]
    t = pl.program_id(1)
    i = qi_ref[t]
    j = kj_ref[t]
    tq = q_ref.shape[2]
    nb = kmean_ref.shape[1]
    nbp = sel_ref.shape[1]
    blk_row = lax.broadcasted_iota(I32, (nbp, tq), 0)

    def step(diagonal):
        if diagonal:
            def adjust(h, c, s):
                s = s + bias_ref[h, 0, c * K_CHUNK:(c + 1) * K_CHUNK, :]
                return jnp.where(_chunk_causal(c, tq), s, NEG)
        else:
            bias_k = jnp.minimum(i - j, bias_ref.shape[1] - 1)

            def adjust(h, c, s):
                picked = jnp.sum(jnp.where(blk_row == j, sel_ref[h], 0.0), axis=0, keepdims=True) > 0.5
                s = s + bias_ref[h, bias_k, c * K_CHUNK:(c + 1) * K_CHUNK, :]
                return jnp.where(picked, s, NEG)

        _flash_step(q_ref, k_ref, vt_ref, adjust, states)

    @pl.when(j == i)
    def _():
        _flash_init(states)
        past = blk_row < i
        for h in range(N_HEADS):
            pair = slice((h // HEADS_PER_LANE_TILE) * LANES, (h // HEADS_PER_LANE_TILE + 1) * LANES)
            km = _pad_rows(kmean_ref[0, :, pair], nbp).astype(BF16)
            g = jnp.dot(km, _masked_qt(q_ref, h), preferred_element_type=F32)
            sel = jnp.zeros((nbp, tq), F32)
            for jb in range(nb):
                gj = g[jb:jb + 1, :]
                beats = jnp.logical_and(past, jnp.logical_or(g > gj, jnp.logical_and(g == gj, blk_row < jb)))
                rank = jnp.sum(jnp.where(beats, 1.0, 0.0), axis=0, keepdims=True)
                chosen = jnp.logical_and(rank < MOBA_TOPK, jb < i)
                sel = jnp.where(jnp.logical_and(blk_row == jb, chosen), 1.0, sel)
            sel_ref[h] = sel
        step(True)

        @pl.when(i == 0)
        def _():
            _flash_finish(o_ref, states)

    @pl.when(j != i)
    def _():
        step(False)

        @pl.when(j == i - 1)
        def _():
            _flash_finish(o_ref, states)


def _moba_prompt(qt, k, vt, kmean, bias, batch, seq):
    tq = tk = MOBA_BLOCK
    nq = seq // tq
    nbp = -(-nq // 8) * 8
    qi, kj = _tri_tables(nq, own_first=True)
    qmap = lambda b, t, qi, kj: (b * nq + qi[t], 0)
    kmap = lambda b, t, qi, kj: (b * nq + kj[t], 0)
    return pl.pallas_call(
        _moba_prompt_kernel,
        grid_spec=pltpu.PrefetchScalarGridSpec(
            num_scalar_prefetch=2,
            grid=(batch, qi.shape[0]),
            in_specs=[pl.BlockSpec((1, GROUP_W, tq), lambda b, t, qi, kj: (b, 0, qi[t])),
                      pl.BlockSpec((tk, GROUP_W), kmap),
                      pl.BlockSpec((1, GROUP_W, tk), lambda b, t, qi, kj: (b, 0, kj[t])),
                      pl.BlockSpec((1, nq, GROUP_W), lambda b, t, qi, kj: (b, 0, 0)),
                      pl.BlockSpec(bias.shape, lambda b, t, qi, kj: (0, 0, 0, 0))],
            out_specs=pl.BlockSpec((tq, GROUP_W), qmap),
            scratch_shapes=_flash_scratch(tq, tk) + [pltpu.VMEM((N_HEADS, nbp, tq), F32)]),
        out_shape=jax.ShapeDtypeStruct((batch * seq, GROUP_W), F32),
        compiler_params=_cparams("arbitrary", "arbitrary"),
        name="moba_prompt",
    )(qi, kj, qt, k, vt, kmean, bias)


def _block_diag_q(q):
    ds = q.shape[0]
    lane_head = lax.broadcasted_iota(I32, (N_HEADS, ds, GROUP_W), 2) // HEAD_DIM
    head = lax.broadcasted_iota(I32, (N_HEADS, ds, GROUP_W), 0)
    q3 = jnp.where(lane_head == head, q.astype(F32)[None, :, :], 0.0)
    return q3.reshape(N_HEADS * ds, GROUP_W).astype(BF16)


def _head_diag_out(acc, ds):
    acc3 = acc.reshape(N_HEADS, ds, GROUP_W)
    lane_head = lax.broadcasted_iota(I32, (N_HEADS, ds, GROUP_W), 2) // HEAD_DIM
    head = lax.broadcasted_iota(I32, (N_HEADS, ds, GROUP_W), 0)
    return jnp.sum(jnp.where(lane_head == head, acc3, 0.0), axis=0)


def _pad_rows(x, rows):
    if x.shape[0] == rows:
        return x
    return jnp.concatenate([x, jnp.zeros((rows - x.shape[0], x.shape[1]), x.dtype)], axis=0)


def _ring_copy(pt_ref, page_index, cache_ref, buf_ref, sem_ref, slot, rr):
    return pltpu.make_async_copy(cache_ref.at[pt_ref[page_index]], buf_ref.at[slot, rr], sem_ref.at[slot])


def _ring_fetch(pt_ref, page_index_of, u, cache_ref, buf_ref, sem_ref):
    slot = u % PAGE_RING
    for rr in range(buf_ref.shape[1]):
        _ring_copy(pt_ref, page_index_of(u, rr), cache_ref, buf_ref, sem_ref, slot, rr).start(priority=rr % 2)


def _ring_wait(pt_ref, page_index_of, u, cache_ref, buf_ref, sem_ref):
    slot = u % PAGE_RING
    for rr in range(buf_ref.shape[1]):
        _ring_copy(pt_ref, page_index_of(u, rr), cache_ref, buf_ref, sem_ref, slot, rr).wait()


def _ring_advance(pt_ref, page_index_of, u, n_fetches, cache_ref, buf_ref, sem_ref):
    @pl.when(u == 0)
    def _():
        for ahead in range(min(PAGE_RING - 1, n_fetches)):
            _ring_fetch(pt_ref, page_index_of, ahead, cache_ref, buf_ref, sem_ref)

    @pl.when(u + (PAGE_RING - 1) < n_fetches)
    def _():
        _ring_fetch(pt_ref, page_index_of, u + (PAGE_RING - 1), cache_ref, buf_ref, sem_ref)

    _ring_wait(pt_ref, page_index_of, u, cache_ref, buf_ref, sem_ref)
    return u % PAGE_RING


def _fox_sample_kernel(pt_ref, q_ref, kn_ref, vn_ref, lfn_ref, *refs, pages, n_pages, n_batch):
    lf_refs = refs[0:pages]
    (k_hbm, v_hbm, o_ref, qbd_ref, m_ref, l_ref, acc_ref, carry_ref, rq_ref,
     kbuf, vbuf, ksem, vsem) = refs[pages:]
    g = pl.program_id(1)
    ng = n_pages // pages
    ds = q_ref.shape[1]
    page = kbuf.shape[3]

    def page_index_of(u, rr):
        return (u // ng) * n_pages + (ng - 1 - u % ng) * pages + (pages - 1 - rr)

    u = pl.program_id(0) * ng + g
    slot = _ring_advance(pt_ref, page_index_of, u, n_batch * ng, k_hbm, kbuf, ksem)
    _ring_advance(pt_ref, page_index_of, u, n_batch * ng, v_hbm, vbuf, vsem)
    lane3 = lax.broadcasted_iota(I32, (N_HEADS, ds, page), 2)
    qidx3 = lax.broadcasted_iota(I32, (N_HEADS, ds, page), 1)
    r = lax.broadcasted_iota(I32, (page, page), 0)
    c = lax.broadcasted_iota(I32, (page, page), 1)
    after = jnp.where(r > c, 1.0, 0.0).astype(BF16)

    def suffix_sum(x):
        hi, mid, lo = _split3_bf16(x)
        return (jnp.dot(hi, after, preferred_element_type=F32) + jnp.dot(mid, after, preferred_element_type=F32)
                + jnp.dot(lo, after, preferred_element_type=F32))

    def biased(s, later):
        s3 = s.reshape(N_HEADS, ds, page) + later[:, None, :] - rq_ref[...].reshape(N_HEADS, ds, 1)
        return s3

    def softmax_update(s, pv_of):
        m_prev = m_ref[...]
        m_new = jnp.maximum(m_prev, jnp.max(s, axis=1, keepdims=True))
        alpha = jnp.exp(m_prev - m_new)
        p = jnp.exp(s - m_new)
        l_ref[...] = alpha * l_ref[...] + jnp.sum(p, axis=1, keepdims=True)
        m_ref[...] = m_new
        acc_ref[...] = alpha * acc_ref[...] + pv_of(p.astype(BF16))

    @pl.when(g == 0)
    def _():
        qbd_ref[...] = _block_diag_q(q_ref[0])
        m_ref[...] = jnp.full_like(m_ref, NEG)
        l_ref[...] = jnp.zeros_like(l_ref)
        acc_ref[...] = jnp.zeros_like(acc_ref)
        x = lfn_ref[0]
        later_new = suffix_sum(x)
        rq3 = jnp.sum(jnp.where(lane3 == qidx3, later_new[:, None, :], 0.0), axis=2, keepdims=True)
        rq_ref[...] = rq3.reshape(N_HEADS * ds, 1)
        kn = _pad_rows(kn_ref[0], page).astype(BF16)
        vn = _pad_rows(vn_ref[0], page).astype(BF16)
        s3 = biased(lax.dot_general(qbd_ref[...], kn, NT, preferred_element_type=F32), later_new)
        s3 = jnp.where(lane3 <= qidx3, s3, NEG)
        softmax_update(s3.reshape(N_HEADS * ds, page), lambda p: jnp.dot(p, vn, preferred_element_type=F32))
        carry_ref[...] = jnp.sum(x, axis=1, keepdims=True)

    xs = [lf_refs[rr][0] for rr in range(pages)]
    within = suffix_sum(jnp.concatenate(xs, axis=0))
    run = carry_ref[...]
    parts = []
    for rr in range(pages):
        later = within[rr * N_HEADS:(rr + 1) * N_HEADS, :] + run
        s = jnp.dot(qbd_ref[...], kbuf[slot, rr].astype(BF16), preferred_element_type=F32)
        parts.append(biased(s, later).reshape(N_HEADS * ds, page))
        run = run + jnp.sum(xs[rr], axis=1, keepdims=True)
    carry_ref[...] = run

    def pv_of(p):
        pv = None
        for rr in range(pages):
            term = lax.dot_general(p[:, rr * page:(rr + 1) * page], vbuf[slot, rr].astype(BF16), NT,
                                   preferred_element_type=F32)
            pv = term if pv is None else pv + term
        return pv

    softmax_update(jnp.concatenate(parts, axis=1), pv_of)

    @pl.when(g == ng - 1)
    def _():
        o_ref[0] = _head_diag_out(acc_ref[...] / l_ref[...], ds)


def _page_ring_scratch(pages, page):
    return [pltpu.VMEM((PAGE_RING, pages, GROUP_W, page), F32), pltpu.SemaphoreType.DMA((PAGE_RING,))]


def _fox_sample(page_table, q3, kn3, vn3, lfn_t, cache_k, cache_v, cache_lf_t):
    db, n_pages = page_table.shape
    ds = q3.shape[1]
    pages = PAGES_PER_STEP
    assert n_pages % pages == 0
    ng = n_pages // pages
    page = cache_k.shape[2]

    def page_map(rr):
        return lambda b, g, pt: (pt[b * n_pages + (ng - 1 - g) * pages + (pages - 1 - rr)], 0, 0)

    bmap = lambda b, g, pt: (b, 0, 0)
    in_specs = [pl.BlockSpec((1, ds, GROUP_W), bmap), pl.BlockSpec((1, ds, GROUP_W), bmap),
                pl.BlockSpec((1, ds, GROUP_W), bmap), pl.BlockSpec((1, N_HEADS, page), bmap)]
    in_specs += [pl.BlockSpec((1, N_HEADS, page), page_map(rr)) for rr in range(pages)]
    in_specs += [pl.BlockSpec(memory_space=pl.ANY), pl.BlockSpec(memory_space=pl.ANY)]
    rows = N_HEADS * ds
    k_ring, k_sem = _page_ring_scratch(pages, page)
    v_ring, v_sem = _page_ring_scratch(pages, page)
    return pl.pallas_call(
        functools.partial(_fox_sample_kernel, pages=pages, n_pages=n_pages, n_batch=db),
        grid_spec=pltpu.PrefetchScalarGridSpec(
            num_scalar_prefetch=1,
            grid=(db, ng),
            in_specs=in_specs,
            out_specs=pl.BlockSpec((1, ds, GROUP_W), bmap),
            scratch_shapes=[pltpu.VMEM((rows, GROUP_W), BF16), pltpu.VMEM((rows, 1), F32),
                            pltpu.VMEM((rows, 1), F32), pltpu.VMEM((rows, GROUP_W), F32),
                            pltpu.VMEM((N_HEADS, 1), F32), pltpu.VMEM((rows, 1), F32),
                            k_ring, v_ring, k_sem, v_sem]),
        out_shape=jax.ShapeDtypeStruct((db, ds, GROUP_W), F32),
        compiler_params=_cparams("arbitrary", "arbitrary"),
        name="fox_sample",
    )(page_table.reshape(-1), q3, kn3, vn3, lfn_t, *([cache_lf_t] * pages), cache_k, cache_v)


def _moba_sample_kernel(pt_ref, q_ref, kn_ref, vn_ref, bias_ref, k_hbm, v_hbm, o_ref,
                        qbd_ref, s_ref, p_ref, pnew_ref, linv_ref, acc_ref, kbuf, vbuf, ksem, vsem,
                        *, pages, n_batch):
    ph = pl.program_id(1)
    g = pl.program_id(2)
    ds = q_ref.shape[1]
    page = kbuf.shape[3]
    n_pages = s_ref.shape[0]
    ng = n_pages // pages
    rows = N_HEADS * ds
    pages_per_block = MOBA_BLOCK // page
    n_blocks = n_pages // pages_per_block
    u = pl.program_id(0) * ng + g

    def page_index_of(u, rr):
        return (u // ng) * n_pages + (u % ng) * pages + rr

    @pl.when(jnp.logical_and(ph == 0, g == 0))
    def _():
        qbd_ref[...] = _block_diag_q(q_ref[0])

    @pl.when(ph == 0)
    def _():
        slot = _ring_advance(pt_ref, page_index_of, u, n_batch * ng, k_hbm, kbuf, ksem)
        for rr in range(pages):
            s_ref[g * pages + rr] = jnp.dot(qbd_ref[...], kbuf[slot, rr].astype(BF16), preferred_element_type=F32)

    @pl.when(jnp.logical_and(ph == 0, g == ng - 1))
    def _():
        lane = lax.broadcasted_iota(I32, (rows, LANES), 1)
        gate = jnp.full((rows, LANES), -jnp.inf, F32)
        for b in range(n_blocks):
            tot = s_ref[b * pages_per_block]
            for u in range(1, pages_per_block):
                tot = tot + s_ref[b * pages_per_block + u]
            gate = jnp.where(lane == b, jnp.sum(tot, axis=1, keepdims=True) * (1.0 / MOBA_BLOCK), gate)
        chosen = []
        for _ in range(min(MOBA_TOPK, n_blocks)):
            mx = jnp.max(gate, axis=1, keepdims=True)
            idx = jnp.min(jnp.where(gate == mx, lane.astype(F32), float(LANES)), axis=1, keepdims=True)
            chosen.append(idx)
            gate = jnp.where(lane.astype(F32) == idx, -jnp.inf, gate)
        lane_n = lax.broadcasted_iota(I32, (N_HEADS, ds, page), 2)
        qidx_n = lax.broadcasted_iota(I32, (N_HEADS, ds, page), 1)
        s_new = lax.dot_general(qbd_ref[...], _pad_rows(kn_ref[0], page).astype(BF16), NT,
                                preferred_element_type=F32) + bias_ref[1]
        s_new = jnp.where((lane_n <= qidx_n).reshape(rows, page), s_new, NEG)
        m_tile = s_new
        for pg in range(n_pages):
            b = float(pg // pages_per_block)
            picked = chosen[0] == b
            for idx in chosen[1:]:
                picked = jnp.logical_or(picked, idx == b)
            bias = bias_ref[0] if pg == n_pages - 1 else bias_ref[2]
            s = jnp.where(picked, s_ref[pg] + bias, NEG)
            s_ref[pg] = s
            m_tile = jnp.maximum(m_tile, s)
        m = jnp.max(m_tile, axis=1, keepdims=True)
        p_new = jnp.exp(s_new - m)
        l_tile = p_new
        pnew_ref[...] = p_new.astype(BF16)
        for pg in range(n_pages):
            p = jnp.exp(s_ref[pg] - m)
            l_tile = l_tile + p
            p_ref[pg] = p.astype(BF16)
        linv_ref[...] = 1.0 / jnp.sum(l_tile, axis=1, keepdims=True)

    @pl.when(jnp.logical_and(ph == 1, g == 0))
    def _():
        acc_ref[...] = jnp.dot(pnew_ref[...], _pad_rows(vn_ref[0], page).astype(BF16),
                               preferred_element_type=F32)

    @pl.when(ph == 1)
    def _():
        slot = _ring_advance(pt_ref, page_index_of, u, n_batch * ng, v_hbm, vbuf, vsem)
        acc = acc_ref[...]
        for rr in range(pages):
            acc = acc + lax.dot_general(p_ref[g * pages + rr], vbuf[slot, rr].astype(BF16), NT,
                                        preferred_element_type=F32)
        acc_ref[...] = acc

    @pl.when(jnp.logical_and(ph == 1, g == ng - 1))
    def _():
        o_ref[0] = _head_diag_out(acc_ref[...] * linv_ref[...], ds)


def _moba_sample(page_table, q3, kn3, vn3, bias3, cache_k, cache_v):
    db, n_pages = page_table.shape
    ds = q3.shape[1]
    pages = PAGES_PER_STEP
    page = cache_k.shape[2]
    assert n_pages % pages == 0 and MOBA_BLOCK % page == 0 and (n_pages * page) % MOBA_BLOCK == 0
    assert T5_MAX_DIST <= page and ds <= page
    ng = n_pages // pages
    rows = N_HEADS * ds
    bmap = lambda b, ph, g, pt: (b, 0, 0)
    in_specs = [pl.BlockSpec((1, ds, GROUP_W), bmap), pl.BlockSpec((1, ds, GROUP_W), bmap),
                pl.BlockSpec((1, ds, GROUP_W), bmap),
                pl.BlockSpec(bias3.shape, lambda b, ph, g, pt: (0, 0, 0)),
                pl.BlockSpec(memory_space=pl.ANY), pl.BlockSpec(memory_space=pl.ANY)]
    k_ring, k_sem = _page_ring_scratch(pages, page)
    v_ring, v_sem = _page_ring_scratch(pages, page)
    return pl.pallas_call(
        functools.partial(_moba_sample_kernel, pages=pages, n_batch=db),
        grid_spec=pltpu.PrefetchScalarGridSpec(
            num_scalar_prefetch=1,
            grid=(db, 2, ng),
            in_specs=in_specs,
            out_specs=pl.BlockSpec((1, ds, GROUP_W), bmap),
            scratch_shapes=[pltpu.VMEM((rows, GROUP_W), BF16),
                            pltpu.VMEM((n_pages, rows, page), F32),
                            pltpu.VMEM((n_pages, rows, page), BF16),
                            pltpu.VMEM((rows, page), BF16),
                            pltpu.VMEM((rows, 1), F32),
                            pltpu.VMEM((rows, GROUP_W), F32),
                            k_ring, v_ring, k_sem, v_sem]),
        out_shape=jax.ShapeDtypeStruct((db, ds, GROUP_W), F32),
        compiler_params=_cparams("arbitrary", "arbitrary", "arbitrary"),
        name="moba_sample",
    )(page_table.reshape(-1), q3, kn3, vn3, bias3, cache_k, cache_v)


def _outproj_kernel(of_ref, om_ref, x_ref, gate_ref, shift_ref, scale_ref, gf_ref, gm_ref, wo_ref, gffn_ref,
                    wr_ref, br_ref, cnt0_ref, x1_ref, h2_ref, ri_ref, rw_ref, cnt_ref, carry_ref):
    t = pl.program_id(0)

    @pl.when(t == 0)
    def _():
        carry_ref[...] = cnt0_ref[...]

    tm = x_ref.shape[0]
    nf = (_rms(of_ref[...]) * gf_ref[...]).astype(BF16)
    nm = (_rms(om_ref[...]) * gm_ref[...]).astype(BF16)
    o = (jnp.dot(nf, wo_ref[0:GROUP_W, :], preferred_element_type=F32)
         + jnp.dot(nm, wo_ref[GROUP_W:2 * GROUP_W, :], preferred_element_type=F32))
    x1 = x_ref[...] + gate_ref[...] * o
    x1_ref[...] = x1
    h2 = _rms(x1) * gffn_ref[...]
    h2 = h2 * (1.0 + scale_ref[...]) + shift_ref[...]
    hi = h2.astype(BF16)
    half = h2.shape[1] // 2
    hi32 = pltpu.bitcast(hi.astype(F32), jnp.uint32)
    h2_ref[...] = jnp.bitwise_or(jnp.right_shift(hi32[:, :half], jnp.uint32(16)),
                                 jnp.bitwise_and(hi32[:, half:], jnp.uint32(0xFFFF0000)))

    lo = (h2 - hi.astype(F32)).astype(BF16)
    wr = wr_ref[...]
    whi = wr.astype(BF16)
    wlo = (wr - whi.astype(F32)).astype(BF16)
    lg = (jnp.dot(hi, whi, preferred_element_type=F32) + jnp.dot(lo, whi, preferred_element_type=F32)
          + jnp.dot(hi, wlo, preferred_element_type=F32)) + br_ref[...]
    lane = lax.broadcasted_iota(I32, (tm, LANES), 1)
    lane_f = lane.astype(F32)
    ninf = -jnp.inf
    is_g = jnp.logical_and(lane >= N_EXPERTS, lane < N_EXPERTS + N_GROUPS)
    glog = jnp.where(is_g, lg, ninf)
    gmax = jnp.max(glog, axis=1, keepdims=True)
    gidx = jnp.min(jnp.where(glog == gmax, lane_f, 2.0 * LANES), axis=1, keepdims=True).astype(I32) - N_EXPERTS
    g_w = 1.0 / jnp.sum(jnp.exp(glog - gmax), axis=1, keepdims=True)
    in_grp = jnp.logical_and(lane >= gidx * EXPERTS_PER_GROUP, lane < (gidx + 1) * EXPERTS_PER_GROUP)
    elog = jnp.where(in_grp, lg, ninf)
    e1 = jnp.max(elog, axis=1, keepdims=True)
    i1 = jnp.min(jnp.where(elog == e1, lane_f, 2.0 * LANES), axis=1, keepdims=True).astype(I32)
    z = jnp.sum(jnp.exp(elog - e1), axis=1, keepdims=True)
    elog2 = jnp.where(lane == i1, ninf, elog)
    e2 = jnp.max(elog2, axis=1, keepdims=True)
    i2 = jnp.min(jnp.where(elog2 == e2, lane_f, 2.0 * LANES), axis=1, keepdims=True).astype(I32)
    p1 = 1.0 / z
    p2 = jnp.exp(e2 - e1) / z
    w1 = g_w * (p1 / (p1 + p2))
    w2 = g_w * (p2 / (p1 + p2))

    a = jnp.where(jnp.logical_or(lane == i1, lane == i2), 1.0, 0.0)
    r = lax.broadcasted_iota(I32, (tm, tm), 0)
    c = lax.broadcasted_iota(I32, (tm, tm), 1)
    before = jnp.where(c < r, 1.0, 0.0).astype(BF16)
    pos = jnp.dot(before, a.astype(BF16), preferred_element_type=F32) + carry_ref[...]
    r1 = jnp.sum(jnp.where(lane == i1, pos, 0.0), axis=1, keepdims=True)
    r2 = jnp.sum(jnp.where(lane == i2, pos, 0.0), axis=1, keepdims=True)
    carry_ref[...] = carry_ref[...] + jnp.sum(a, axis=0, keepdims=True)
    cnt_ref[...] = carry_ref[...]

    ri = jnp.where(lane == 0, i1, 0) + jnp.where(lane == 1, i2, 0)
    ri = ri + jnp.where(lane == 2, r1.astype(I32), 0) + jnp.where(lane == 3, r2.astype(I32), 0)
    ri_ref[...] = ri
    rw_ref[...] = jnp.where(lane == 0, w1, 0.0) + jnp.where(lane == 1, w2, 0.0)


def _outproj(of, om, x2, gate3, shift3, scale3, gf, gm, wo, gffn, wr, br, cnt0, *, rows_per_mod, name):
    n, d = x2.shape
    tm = ROW_TILE
    nt = n // tm
    mod_rows = gate3.shape[1]
    mod_map = lambda t: ((t * tm) // rows_per_mod, 0, 0)
    row_map = lambda t: (t, 0)
    const2 = lambda t: (0, 0)
    return pl.pallas_call(
        _outproj_kernel,
        grid=(nt,),
        in_specs=[pl.BlockSpec((tm, GROUP_W), row_map), pl.BlockSpec((tm, GROUP_W), row_map),
                  pl.BlockSpec((tm, d), row_map),
                  pl.BlockSpec((None, mod_rows, d), mod_map), pl.BlockSpec((None, mod_rows, d), mod_map),
                  pl.BlockSpec((None, mod_rows, d), mod_map),
                  pl.BlockSpec((1, GROUP_W), const2), pl.BlockSpec((1, GROUP_W), const2),
                  pl.BlockSpec((d, d), const2), pl.BlockSpec((1, d), const2),
                  pl.BlockSpec((d, LANES), const2), pl.BlockSpec((1, LANES), const2),
                  pl.BlockSpec((1, LANES), const2)],
        out_specs=[pl.BlockSpec((tm, d), row_map), pl.BlockSpec((tm, d // 2), row_map),
                   pl.BlockSpec((tm, LANES), row_map), pl.BlockSpec((tm, LANES), row_map),
                   pl.BlockSpec((1, LANES), const2)],
        out_shape=[jax.ShapeDtypeStruct((n, d), F32), jax.ShapeDtypeStruct((n, d // 2), jnp.uint32),
                   jax.ShapeDtypeStruct((n, LANES), I32), jax.ShapeDtypeStruct((n, LANES), F32),
                   jax.ShapeDtypeStruct((1, LANES), F32)],
        scratch_shapes=[pltpu.VMEM((1, LANES), F32)],
        compiler_params=_cparams("arbitrary"),
        name=name,
    )(of, om, x2, gate3, shift3, scale3, gf, gm, wo, gffn, wr, br, cnt0)


def _scatter_kernel(dest_ref, h_ref, rows_in_ref, rows_ref, sem):
    del rows_in_ref
    tm = h_ref.shape[0]

    def row_copy(r, k):
        return pltpu.make_async_copy(h_ref.at[pl.ds(r, 1)], rows_ref.at[pl.ds(dest_ref[k, r], 1)], sem)

    for r in range(tm):
        row_copy(r, 0).start(priority=0)
        row_copy(r, 1).start(priority=1)
    for _ in range(2):
        pltpu.make_async_copy(h_ref, rows_ref.at[pl.ds(0, tm)], sem).wait()


def _scatter_rows(dest2, h2, rows):
    n, d = h2.shape
    tm = ROW_TILE
    return pl.pallas_call(
        _scatter_kernel,
        grid=(n // tm,),
        in_specs=[pl.BlockSpec((2, tm), lambda t: (0, t), memory_space=pltpu.SMEM),
                  pl.BlockSpec((tm, d), lambda t: (t, 0)),
                  pl.BlockSpec(memory_space=pl.ANY)],
        out_specs=pl.BlockSpec(memory_space=pl.ANY),
        out_shape=jax.ShapeDtypeStruct(rows.shape, rows.dtype),
        scratch_shapes=[pltpu.SemaphoreType.DMA(())],
        input_output_aliases={2: 0},
        compiler_params=_cparams("arbitrary"),
        name="moe_scatter",
    )(dest2, h2, rows)


def _moe_kernel(be_ref, nu_ref, rows_ref, w1_ref, w3_ref, w2_ref, y_ref, w1b_ref, w3b_ref, w2b_ref):
    blk = pl.program_id(0)

    @pl.when(blk < nu_ref[0])
    def _():
        prev = be_ref[jnp.maximum(blk - 1, 0)]

        @pl.when(jnp.logical_or(blk == 0, be_ref[blk] != prev))
        def _():
            w1b_ref[...] = w1_ref[0].astype(BF16)
            w3b_ref[...] = w3_ref[0].astype(BF16)
            w2b_ref[...] = w2_ref[0].astype(BF16)

        packed = rows_ref[...]
        x_lo = pltpu.bitcast(jnp.left_shift(packed, jnp.uint32(16)), F32)
        x_hi = pltpu.bitcast(jnp.bitwise_and(packed, jnp.uint32(0xFFFF0000)), F32)
        x = jnp.concatenate([x_lo, x_hi], axis=1).astype(BF16)
        a = jnp.dot(x, w1b_ref[...], preferred_element_type=F32)
        b = jnp.dot(x, w3b_ref[...], preferred_element_type=F32)
        hm = (a * jax.nn.sigmoid(a)) * b
        y_ref[...] = jnp.dot(hm.astype(BF16), w2b_ref[...], preferred_element_type=F32)

    @pl.when(blk >= nu_ref[0])
    def _():
        y_ref[...] = jnp.zeros_like(y_ref)


def _moe_experts(block_e, n_used, rows, w1, w3, w2):
    p = rows.shape[0]
    d, de = w1.shape[1], w1.shape[2]
    assert rows.shape[1] * 2 == d
    nblk = p // MOE_BLOCK
    row_map = lambda b, be, nu: (jnp.minimum(b, nu[0] - 1), 0)
    return pl.pallas_call(
        _moe_kernel,
        grid_spec=pltpu.PrefetchScalarGridSpec(
            num_scalar_prefetch=2,
            grid=(nblk,),
            in_specs=[pl.BlockSpec((MOE_BLOCK, d // 2), row_map),
                      pl.BlockSpec((1, d, de), lambda b, be, nu: (be[b], 0, 0)),
                      pl.BlockSpec((1, d, de), lambda b, be, nu: (be[b], 0, 0)),
                      pl.BlockSpec((1, de, d), lambda b, be, nu: (be[b], 0, 0))],
            out_specs=pl.BlockSpec((MOE_BLOCK, d), lambda b, be, nu: (b, 0)),
            scratch_shapes=[pltpu.VMEM((d, de), BF16), pltpu.VMEM((d, de), BF16), pltpu.VMEM((de, d), BF16)]),
        out_shape=jax.ShapeDtypeStruct((p, d), F32),
        compiler_params=_cparams("arbitrary"),
        name="moe_experts",
    )(block_e, n_used, rows, w1, w3, w2)


def _combine_kernel(dest_ref, x1_ref, gate_ref, rw_ref, gfin_ref, y_ref, o_ref, ybuf_ref, sem):
    tm = x1_ref.shape[0]

    def row_copy(r, k):
        return pltpu.make_async_copy(y_ref.at[pl.ds(dest_ref[k, r], 1)], ybuf_ref.at[k, pl.ds(r, 1)], sem)

    for r in range(tm):
        row_copy(r, 0).start(priority=0)
        row_copy(r, 1).start(priority=1)
    for k in range(2):
        pltpu.make_async_copy(y_ref.at[pl.ds(0, tm)], ybuf_ref.at[k], sem).wait()
    rw = rw_ref[...]
    moe = rw[:, 0:1] * ybuf_ref[0] + rw[:, 1:2] * ybuf_ref[1]
    xo = x1_ref[...] + gate_ref[...] * moe
    o_ref[...] = _rms(xo) * gfin_ref[...]


def _combine(dest2, x1, gate3, rw, gfin, y, *, rows_per_mod, name):
    n, d = x1.shape
    tm = ROW_TILE
    mod_rows = gate3.shape[1]
    return pl.pallas_call(
        _combine_kernel,
        grid=(n // tm,),
        in_specs=[pl.BlockSpec((2, tm), lambda t: (0, t), memory_space=pltpu.SMEM),
                  pl.BlockSpec((tm, d), lambda t: (t, 0)),
                  pl.BlockSpec((None, mod_rows, d), lambda t: ((t * tm) // rows_per_mod, 0, 0)),
                  pl.BlockSpec((tm, LANES), lambda t: (t, 0)),
                  pl.BlockSpec((1, d), lambda t: (0, 0)),
                  pl.BlockSpec(memory_space=pl.ANY)],
        out_specs=pl.BlockSpec((tm, d), lambda t: (t, 0)),
        out_shape=jax.ShapeDtypeStruct((n, d), F32),
        scratch_shapes=[pltpu.VMEM((2, tm, d), F32), pltpu.SemaphoreType.DMA(())],
        compiler_params=_cparams("arbitrary"),
        name=name,
    )(dest2, x1, gate3, rw, gfin, y)


def kernel(x_prompt, x_sample, cache_fox_k, cache_fox_v, cache_fox_logf, cache_moba_k, cache_moba_v, page_table, c_prompt, c_sample, w_ada, b_ada, g_attn, w_in, b_forget, g_out_fox, g_out_moba, t5_bias, w_out, g_ffn, w_router_group, b_router_group, w_router_expert, b_router_expert, w1, w3, w2, g_final):
    bsz, seq, d = x_prompt.shape
    db, ds, _ = x_sample.shape
    depth = w_ada.shape[0]
    n_phys, page = cache_fox_k.shape[1], cache_fox_k.shape[2]
    assert depth == 1, "one trunk layer"
    assert seq % ROW_TILE == 0 and (db * ds) % ROW_TILE == 0 and ROW_TILE % ds == 0
    assert N_EXPERTS + N_GROUPS <= LANES and N_HEADS <= LANES
    l = 0
    n_p, n_s = bsz * seq, db * ds
    xp2 = x_prompt.reshape(n_p, d)
    xs2 = x_sample.reshape(n_s, d)

    mod = _ada(jnp.concatenate([c_prompt, c_sample], axis=0), w_ada[l], b_ada[l])
    mod_p = [mod[:bsz, i * d:(i + 1) * d].reshape(bsz, 1, d) for i in range(6)]
    mod_s = [jnp.repeat(mod[bsz:, i * d:(i + 1) * d], ds, axis=0).reshape(n_s // ROW_TILE, ROW_TILE, d)
             for i in range(6)]

    w = GROUP_W
    wl = w_in[l]
    wcat = jnp.concatenate([wl[:, :3 * w], wl[:, 3 * w + N_HEADS:],
                            wl[:, 3 * w:3 * w + N_HEADS], jnp.zeros((d, LANES - N_HEADS), F32)],
                           axis=1).astype(BF16)
    bf_pad = jnp.pad(b_forget[l], (0, LANES - N_HEADS)).reshape(1, LANES)
    g_attn2 = g_attn[l].reshape(1, d)

    fqt, fk_t, fv_t, lf_t, mqt, mk_t, mv_t, fcum, frow, kmean, fkb, fvtb, mkb, mvtb = _inproj(
        xp2, mod_p[0], mod_p[1], g_attn2, wcat, bf_pad, rows_per_mod=seq, prompt_extras=True, seq_len=seq)
    sfq, sfk, sfv, slf, smq, smk, smv = _inproj(
        xs2, mod_s[0], mod_s[1], g_attn2, wcat, bf_pad, rows_per_mod=ROW_TILE, prompt_extras=False, seq_len=ds)

    o_fox_p = _fox_prompt(fqt, fkb, fvtb, fcum, frow, bsz, seq)
    bias_p = _t5_tiles(t5_bias, (0, MOBA_BLOCK, 2 * MOBA_BLOCK), MOBA_BLOCK, MOBA_BLOCK, sign=-1, scale=LOG2E)
    o_moba_p = _moba_prompt(mqt, mkb, mvtb, kmean.reshape(bsz, seq // MOBA_BLOCK, w), bias_p, bsz, seq)

    to3 = lambda a: a.reshape(db, ds, w)
    page_t = lambda c: c[l].transpose(0, 2, 3, 1).reshape(n_phys, w, page)
    cfk, cfv, cmk, cmv = page_t(cache_fox_k), page_t(cache_fox_v), page_t(cache_moba_k), page_t(cache_moba_v)
    clf_t = cache_fox_logf[l].transpose(0, 2, 1)
    slf_t = jnp.pad(slf.reshape(db, ds, N_HEADS).transpose(0, 2, 1), ((0, 0), (0, 0), (0, page - ds)))
    o_fox_s = _fox_sample(page_table, to3(sfq), to3(sfk), to3(sfv), slf_t, cfk, cfv, clf_t)
    bias_s = _t5_tiles(t5_bias, (page, 0, 2 * T5_MAX_DIST + page), ds, page, sign=1)
    bias_s = bias_s.transpose(1, 0, 2, 3).reshape(3, N_HEADS * ds, page)
    o_moba_s = _moba_sample(page_table, to3(smq), to3(smk), to3(smv), bias_s, cmk, cmv)

    wr = jnp.concatenate([w_router_expert[l], w_router_group[l],
                          jnp.zeros((d, LANES - N_EXPERTS - N_GROUPS), F32)], axis=1)
    br = jnp.concatenate([b_router_expert[l], b_router_group[l],
                          jnp.zeros((LANES - N_EXPERTS - N_GROUPS,), F32)]).reshape(1, LANES)
    wo = w_out[l].astype(BF16)
    gf, gm, gffn = g_out_fox[l].reshape(1, w), g_out_moba[l].reshape(1, w), g_ffn[l].reshape(1, d)
    x1_p, h2_p, ri_p, rw_p, cnt_p = _outproj(
        o_fox_p, o_moba_p, xp2, mod_p[2], mod_p[3], mod_p[4], gf, gm, wo, gffn, wr, br,
        jnp.zeros((1, LANES), F32), rows_per_mod=seq, name="outproj_prompt")
    x1_s, h2_s, ri_s, rw_s, cnt = _outproj(
        o_fox_s.reshape(n_s, w), o_moba_s.reshape(n_s, w), xs2, mod_s[2], mod_s[3], mod_s[4], gf, gm, wo, gffn,
        wr, br, cnt_p, rows_per_mod=ROW_TILE, name="outproj_sample")

    counts = cnt[0, :N_EXPERTS].astype(I32)
    padded = (counts + MOE_BLOCK - 1) // MOE_BLOCK * MOE_BLOCK
    pend = jnp.cumsum(padded)
    pstart = pend - padded
    n_asg = 2 * (n_p + n_s)
    n_blocks = -(-(n_asg + N_EXPERTS * (MOE_BLOCK - 1)) // MOE_BLOCK)
    blk_start = jnp.arange(n_blocks, dtype=I32) * MOE_BLOCK
    block_e = jnp.clip(jnp.sum((pend[None, :] <= blk_start[:, None]).astype(I32), axis=1), 0, N_EXPERTS - 1)
    n_used = (pend[-1:] // MOE_BLOCK).astype(I32)

    def dest_of(ri):
        return (pstart[ri[:, 0:2]] + ri[:, 2:4]).T.astype(I32)

    dest_p, dest_s = dest_of(ri_p), dest_of(ri_s)
    rows = jnp.zeros((n_blocks * MOE_BLOCK, d // 2), jnp.uint32)
    rows = _scatter_rows(dest_p, h2_p, rows)
    rows = _scatter_rows(dest_s, h2_s, rows)
    y = _moe_experts(block_e, n_used, rows, w1[l], w3[l], w2[l])
    gfin = g_final.reshape(1, d)
    y_prompt = _combine(dest_p, x1_p, mod_p[5], rw_p, gfin, y, rows_per_mod=seq, name="combine_prompt")
    y_sample = _combine(dest_s, x1_s, mod_s[5], rw_s, gfin, y, rows_per_mod=ROW_TILE, name="combine_sample")

    hp = lambda a: a.reshape(depth, bsz, N_HEADS, HEAD_DIM, seq).transpose(0, 1, 4, 2, 3)
    hs = lambda a: a.reshape(depth, db, ds, N_HEADS, HEAD_DIM)
    return (y_prompt.reshape(bsz, seq, d), y_sample.reshape(db, ds, d),
            hp(fk_t), hp(fv_t), lf_t.reshape(depth, bsz, N_HEADS, seq).transpose(0, 1, 3, 2), hp(mk_t), hp(mv_t),
            hs(sfk), hs(sfv), slf.reshape(depth, db, ds, N_HEADS), hs(smk), hs(smv))
```

```python
import functools
import math

import jax
import jax.numpy as jnp
from jax import lax
from jax.experimental import pallas as pl
from jax.experimental.pallas import tpu as pltpu

F32 = jnp.float32
BF16 = jnp.bfloat16
I32 = jnp.int32

HEAD_DIM = 64
N_HEADS = 8
GROUP_W = N_HEADS * HEAD_DIM
LANES = 128
HEADS_PER_LANE_TILE = LANES // HEAD_DIM
N_PAIRS = N_HEADS // HEADS_PER_LANE_TILE
SM_SCALE = HEAD_DIM ** -0.5
LOG2E = math.log2(math.e)
MOBA_BLOCK = 256
MOBA_TOPK = 3
T5_BUCKETS = 32
T5_MAX_DIST = 128
N_GROUPS = 4
EXPERTS_PER_GROUP = 8
N_EXPERTS = N_GROUPS * EXPERTS_PER_GROUP
RMS_EPS = 1e-6
ROW_TILE = 256
ATTN_TILE = 256
K_CHUNK = 64
AUG = LANES // N_HEADS
SUM_ROWS = 16
MOE_BLOCK = 256
PAGES_PER_STEP = 16
PAGE_RING = 3
NEG = -1e30
VMEM_LIMIT = 48 * 1024 * 1024
HIGHEST = lax.Precision.HIGHEST
NT = (((1,), (1,)), ((), ()))


def _cparams(*sem):
    return pltpu.CompilerParams(dimension_semantics=sem, vmem_limit_bytes=VMEM_LIMIT)


def _rms(x):
    return x * lax.rsqrt(jnp.mean(x * x, axis=-1, keepdims=True) + RMS_EPS)


def _split3_bf16(x):
    hi = x.astype(BF16)
    r1 = x - hi.astype(F32)
    mid = r1.astype(BF16)
    lo = (r1 - mid.astype(F32)).astype(BF16)
    return hi, mid, lo


def _ada_kernel(c_ref, w_ref, b_ref, o_ref):
    c = c_ref[...]
    a = c * jax.nn.sigmoid(c)
    o_ref[...] = jnp.dot(a, w_ref[...], preferred_element_type=F32, precision=HIGHEST) + b_ref[...]


def _ada(c, w, b):
    n, d = c.shape
    e = w.shape[1]
    tn = 1024
    return pl.pallas_call(
        _ada_kernel,
        grid=(e // tn,),
        in_specs=[pl.BlockSpec((n, d), lambda j: (0, 0)),
                  pl.BlockSpec((d, tn), lambda j: (0, j)),
                  pl.BlockSpec((1, tn), lambda j: (0, j))],
        out_specs=pl.BlockSpec((n, tn), lambda j: (0, j)),
        out_shape=jax.ShapeDtypeStruct((n, e), F32),
        compiler_params=_cparams("arbitrary"),
        name="ada",
    )(c, w, b.reshape(1, e))


def _inproj_kernel(x_ref, shift_ref, scale_ref, g_ref, w_ref, bf_ref,
                   fq_ref, fk_ref, fv_ref, lf_ref, mq_ref, mk_ref, mv_ref, *rest,
                   prompt_extras, tiles_per_seq):
    x = x_ref[...]
    h = _rms(x) * g_ref[...]
    h = h * (1.0 + scale_ref[...]) + shift_ref[...]
    z = jnp.dot(h.astype(BF16), w_ref[...], preferred_element_type=F32)
    w = GROUP_W
    q_scale = SM_SCALE * LOG2E if prompt_extras else SM_SCALE
    fq = z[:, 0:w] * q_scale
    mq = z[:, 3 * w:4 * w] * q_scale
    if prompt_extras:
        fq_ref[0] = fq.T.astype(BF16)
        mq_ref[0] = mq.T.astype(BF16)
    else:
        fq_ref[...] = fq.astype(BF16)
        mq_ref[...] = mq.astype(BF16)
    fk, fv = z[:, w:2 * w], z[:, 2 * w:3 * w]
    mk, mv = z[:, 4 * w:5 * w], z[:, 5 * w:6 * w]
    fg = z[:, 6 * w:6 * w + LANES] + bf_ref[...]
    lf = jnp.minimum(fg, 0.0) - jnp.log1p(jnp.exp(-jnp.abs(fg)))
    if not prompt_extras:
        fk_ref[...] = fk
        fv_ref[...] = fv
        mk_ref[...] = mk
        mv_ref[...] = mv
        lf_ref[...] = lf[:, :N_HEADS]
        return
    fkaug_ref, fqtaug_ref, kmean_ref, fkb_ref, fvtb_ref, mkb_ref, mvtb_ref, carry_ref = rest
    tm = x.shape[0]
    fk_ref[0] = fk.T
    fv_t = fv.T
    fv_ref[0] = fv_t
    mk_ref[0] = mk.T
    mv_t = mv.T
    mv_ref[0] = mv_t
    fkb_ref[...] = fk.astype(BF16)
    fvtb_ref[0] = fv_t.astype(BF16)
    mkb_ref[...] = mk.astype(BF16)
    mvtb_ref[0] = mv_t.astype(BF16)
    lf_ref[0] = lf.T[:N_HEADS, :]
    kmean_ref[0] = jnp.mean(mk, axis=0, keepdims=True)

    @pl.when(pl.program_id(0) % tiles_per_seq == 0)
    def _():
        carry_ref[...] = jnp.zeros_like(carry_ref)

    r = lax.broadcasted_iota(I32, (tm, tm), 0)
    c = lax.broadcasted_iota(I32, (tm, tm), 1)
    tri = jnp.where(c <= r, 1.0, 0.0).astype(BF16)
    hi, mid, lo = _split3_bf16(lf)
    cs = (jnp.dot(tri, hi, preferred_element_type=F32)
          + jnp.dot(tri, mid, preferred_element_type=F32)
          + jnp.dot(tri, lo, preferred_element_type=F32)) + carry_ref[...]
    carry_ref[...] = cs[tm - 1:tm, :]
    hi, mid, lo = _split3_bf16(cs * LOG2E)
    er = lax.broadcasted_iota(I32, (LANES, LANES), 0)
    ec = lax.broadcasted_iota(I32, (LANES, LANES), 1)

    def spread(x, off):
        e = jnp.where(jnp.logical_and(er < N_HEADS, ec == AUG * er + off), 1.0, 0.0).astype(BF16)
        return jnp.dot(x, e, preferred_element_type=F32)

    slot = lax.broadcasted_iota(I32, (tm, LANES), 1) % AUG
    fq_aug = (spread(hi, 0) + spread(mid, 1) + spread(lo, 2)
              + jnp.where(jnp.logical_and(slot >= 3, slot < 6), 1.0, 0.0))
    fk_aug = jnp.where(slot < 3, 1.0, 0.0) - (spread(hi, 3) + spread(mid, 4) + spread(lo, 5))
    fkaug_ref[...] = fk_aug.astype(BF16)
    fqtaug_ref[0] = fq_aug.T.astype(BF16)


def _inproj(x2, shift3, scale3, g_attn, wcat, bf_pad, *, rows_per_mod, prompt_extras, seq_len):
    n, d = x2.shape
    tm = ROW_TILE
    assert n % tm == 0
    nt = n // tm
    mod_rows = shift3.shape[1]
    mod_map = lambda t: ((t * tm) // rows_per_mod, 0, 0)
    row_map = lambda t: (t, 0)
    const2 = lambda t: (0, 0)
    ecols = wcat.shape[1]
    row_bf16 = (jax.ShapeDtypeStruct((n, GROUP_W), BF16), pl.BlockSpec((tm, GROUP_W), row_map))
    scratch = []
    tiles_per_seq = 1
    if prompt_extras:
        assert tm == MOBA_BLOCK and seq_len % tm == 0
        tiles_per_seq = seq_len // tm
        nb = n // seq_len
        t_map = lambda t: (t // tiles_per_seq, 0, t % tiles_per_seq)
        kv = (jax.ShapeDtypeStruct((nb, GROUP_W, seq_len), F32), pl.BlockSpec((1, GROUP_W, tm), t_map))
        lfo = (jax.ShapeDtypeStruct((nb, N_HEADS, seq_len), F32), pl.BlockSpec((1, N_HEADS, tm), t_map))
        kvt_bf16 = (jax.ShapeDtypeStruct((nb, GROUP_W, seq_len), BF16), pl.BlockSpec((1, GROUP_W, tm), t_map))
        outs = [kvt_bf16, kv, kv, lfo, kvt_bf16, kv, kv,
                (jax.ShapeDtypeStruct((n, LANES), BF16), pl.BlockSpec((tm, LANES), row_map)),
                (jax.ShapeDtypeStruct((nb, LANES, seq_len), BF16), pl.BlockSpec((1, LANES, tm), t_map)),
                (jax.ShapeDtypeStruct((nt, 1, GROUP_W), F32), pl.BlockSpec((1, 1, GROUP_W), lambda t: (t, 0, 0)))]
        outs += [row_bf16, kvt_bf16, row_bf16, kvt_bf16]
        scratch = [pltpu.VMEM((1, LANES), F32)]
    else:
        kv = (jax.ShapeDtypeStruct((n, GROUP_W), F32), pl.BlockSpec((tm, GROUP_W), row_map))
        lfo = (jax.ShapeDtypeStruct((n, N_HEADS), F32), pl.BlockSpec((tm, N_HEADS), row_map))
        outs = [row_bf16, kv, kv, lfo, row_bf16, kv, kv]
    return pl.pallas_call(
        functools.partial(_inproj_kernel, prompt_extras=prompt_extras, tiles_per_seq=tiles_per_seq),
        grid=(nt,),
        in_specs=[pl.BlockSpec((tm, d), row_map),
                  pl.BlockSpec((None, mod_rows, d), mod_map),
                  pl.BlockSpec((None, mod_rows, d), mod_map),
                  pl.BlockSpec((1, d), const2),
                  pl.BlockSpec((d, ecols), const2),
                  pl.BlockSpec((1, LANES), const2)],
        out_specs=[o[1] for o in outs],
        out_shape=[o[0] for o in outs],
        scratch_shapes=scratch,
        compiler_params=_cparams("arbitrary"),
        name="inproj_prompt" if prompt_extras else "inproj_sample",
    )(x2, shift3, scale3, g_attn, wcat, bf_pad)


def _t5_kernel(t5_ref, o_ref, *, offs, sign, scale):
    h = pl.program_id(0)
    rows, cols = o_ref.shape[2], o_ref.shape[3]
    r = lax.broadcasted_iota(I32, (rows, cols), 0)
    c = lax.broadcasted_iota(I32, (rows, cols), 1)
    max_exact = T5_BUCKETS // 2
    for k, off in enumerate(offs):
        rel = jnp.maximum(off + sign * (r - c), 0)
        relf = jnp.maximum(rel, 1).astype(F32)
        large = max_exact + (jnp.log(relf / max_exact) / math.log(T5_MAX_DIST / max_exact)
                             * (T5_BUCKETS - max_exact)).astype(I32)
        large = jnp.minimum(large, T5_BUCKETS - 1)
        bucket = jnp.where(rel < max_exact, rel, large)
        acc = jnp.zeros((rows, cols), F32)
        for b in range(T5_BUCKETS):
            acc = jnp.where(bucket == b, t5_ref[b, h], acc)
        o_ref[0, k] = acc if scale == 1.0 else acc * scale


def _t5_tiles(t5_bias, offs, rows, cols, sign, scale=1.0):
    nh = t5_bias.shape[1]
    return pl.pallas_call(
        functools.partial(_t5_kernel, offs=tuple(offs), sign=sign, scale=scale),
        grid=(nh,),
        in_specs=[pl.BlockSpec(memory_space=pltpu.SMEM)],
        out_specs=pl.BlockSpec((1, len(offs), rows, cols), lambda h: (h, 0, 0, 0)),
        out_shape=jax.ShapeDtypeStruct((nh, len(offs), rows, cols), F32),
        compiler_params=_cparams("arbitrary"),
        name="t5_tiles",
    )(t5_bias)


def _pair_masks(rows):
    lane = lax.broadcasted_iota(I32, (rows, LANES), 1)
    lo = lane < HEAD_DIM
    return lo, jnp.logical_not(lo)


def _masked_qt(qt_ref, h):
    p, e = divmod(h, HEADS_PER_LANE_TILE)
    qt = qt_ref[0, p * LANES:(p + 1) * LANES, :]
    row = lax.broadcasted_iota(I32, qt.shape, 0)
    keep = (row >= HEAD_DIM) if e else (row < HEAD_DIM)
    return jnp.where(keep, qt, jnp.zeros_like(qt))


def _pair_of(h):
    return slice((h // HEADS_PER_LANE_TILE) * LANES, (h // HEADS_PER_LANE_TILE + 1) * LANES)


def _flash_step(lhs_of, rhs_of, vt_ref, adjust, states, tk, tq):
    chunks = [slice(c * K_CHUNK, (c + 1) * K_CHUNK) for c in range(tk // K_CHUNK)]
    for h, (m_ref, _, _, s_scr, _, a_ref) in enumerate(states):
        rhs = rhs_of(h)
        m8 = jnp.full((8, tq), NEG, F32)
        for c, rows in enumerate(chunks):
            s = jnp.dot(lhs_of(h, c), rhs, preferred_element_type=F32)
            if adjust is not None:
                s = adjust(h, c, s)
            s_scr[rows, :] = s
            m8 = jnp.maximum(m8, jnp.max(s.reshape(K_CHUNK // 8, 8, tq), axis=0))
        m_prev = m_ref[...]
        m_new = jnp.maximum(m_prev, jnp.max(m8, axis=0, keepdims=True))
        a_ref[...] = jnp.exp2(m_prev - m_new)
        m_ref[...] = m_new
    for m_ref, _, _, s_scr, p_scr, _ in states:
        m_new = m_ref[...]
        for rows in chunks:
            p_scr[rows, :] = jnp.exp2(s_scr[rows, :] - m_new).astype(BF16)
    ones = jnp.ones((SUM_ROWS, tk), BF16)
    for h, (_, l_ref, acc_ref, _, p_scr, a_ref) in enumerate(states):
        feat = slice(h * HEAD_DIM, (h + 1) * HEAD_DIM)
        vt_sum = jnp.concatenate([vt_ref[0, feat, :], ones], axis=0)
        pv = jnp.dot(vt_sum, p_scr[...], preferred_element_type=F32)
        acc_ref[...] = a_ref[...] * acc_ref[...] + pv[:HEAD_DIM]
        l_ref[...] = a_ref[...] * l_ref[...] + pv[HEAD_DIM:HEAD_DIM + 1]


FLASH_BUFS_PER_HEAD = 6


def _flash_scratch(tq, tk):
    per_head = [pltpu.VMEM((1, tq), F32), pltpu.VMEM((1, tq), F32), pltpu.VMEM((HEAD_DIM, tq), F32),
                pltpu.VMEM((tk, tq), F32), pltpu.VMEM((tk, tq), BF16), pltpu.VMEM((1, tq), F32)]
    return per_head * N_HEADS


def _flash_states(scratch):
    n = FLASH_BUFS_PER_HEAD
    return [scratch[n * h:n * (h + 1)] for h in range(N_HEADS)]


def _flash_init(states):
    for m_ref, l_ref, acc_ref, _, _, _ in states:
        m_ref[...] = jnp.full_like(m_ref, NEG)
        l_ref[...] = jnp.zeros_like(l_ref)
        acc_ref[...] = jnp.zeros_like(acc_ref)


def _flash_finish(o_ref, states):
    o_t = jnp.concatenate([acc_ref[...] / l_ref[...] for _, l_ref, acc_ref, _, _, _ in states], axis=0)
    o_ref[...] = o_t.T


def _chunk_causal(c, tq):
    krow = c * K_CHUNK + lax.broadcasted_iota(I32, (K_CHUNK, tq), 0)
    qcol = lax.broadcasted_iota(I32, (K_CHUNK, tq), 1)
    return krow <= qcol


def _head_rows(x, h, rows_per_head):
    row = lax.broadcasted_iota(I32, x.shape, 0)
    keep = jnp.logical_and(row >= h * rows_per_head, row < (h + 1) * rows_per_head)
    return jnp.where(keep, x, jnp.zeros_like(x))


def _fox_prompt_kernel(qi_ref, kj_ref, q_ref, k_ref, vt_ref, fqt_ref, fk_ref, o_ref, *scratch):
    states = _flash_states(scratch)
    t = pl.program_id(1)
    i = qi_ref[t]
    j = kj_ref[t]
    tq, tk = q_ref.shape[2], k_ref.shape[0]

    @pl.when(j == 0)
    def _():
        _flash_init(states)

    def lhs_of(h, c):
        rows = slice(c * K_CHUNK, (c + 1) * K_CHUNK)
        return jnp.concatenate([k_ref[rows, _pair_of(h)], fk_ref[rows, :]], axis=1)

    def rhs_of(h):
        return jnp.concatenate([_masked_qt(q_ref, h), _head_rows(fqt_ref[0], h, AUG)], axis=0)

    def step(diagonal):
        adjust = (lambda h, c, s: jnp.where(_chunk_causal(c, tq), s, NEG)) if diagonal else None
        _flash_step(lhs_of, rhs_of, vt_ref, adjust, states, tk, tq)

    @pl.when(j < i)
    def _():
        step(False)

    @pl.when(j == i)
    def _():
        step(True)
        _flash_finish(o_ref, states)


def _tri_tables(nq, own_first):
    qi, kj = [], []
    for i in range(nq):
        order = ([i] + list(range(i))) if own_first else list(range(i + 1))
        for j in order:
            qi.append(i)
            kj.append(j)
    return jnp.asarray(qi, I32), jnp.asarray(kj, I32)


def _fox_prompt(qt, k, vt, fk_aug, fqt_aug, batch, seq):
    tq = tk = ATTN_TILE
    nq = seq // tq
    qi, kj = _tri_tables(nq, own_first=False)
    qmap = lambda b, t, qi, kj: (b * nq + qi[t], 0)
    kmap = lambda b, t, qi, kj: (b * nq + kj[t], 0)
    qtmap = lambda b, t, qi, kj: (b, 0, qi[t])
    ktmap = lambda b, t, qi, kj: (b, 0, kj[t])
    return pl.pallas_call(
        _fox_prompt_kernel,
        grid_spec=pltpu.PrefetchScalarGridSpec(
            num_scalar_prefetch=2,
            grid=(batch, qi.shape[0]),
            in_specs=[pl.BlockSpec((1, GROUP_W, tq), qtmap),
                      pl.BlockSpec((tk, GROUP_W), kmap),
                      pl.BlockSpec((1, GROUP_W, tk), ktmap),
                      pl.BlockSpec((1, LANES, tq), qtmap),
                      pl.BlockSpec((tk, LANES), kmap)],
            out_specs=pl.BlockSpec((tq, GROUP_W), qmap),
            scratch_shapes=_flash_scratch(tq, tk)),
        out_shape=jax.ShapeDtypeStruct((batch * seq, GROUP_W), F32),
        compiler_params=_cparams("arbitrary", "arbitrary"),
        name="fox_prompt",
    )(qi, kj, qt, k, vt, fqt_aug, fk_aug)


def _moba_prompt_kernel(qi_ref, kj_ref, q_ref, k_ref, vt_ref, kmean_ref, bias_ref, o_ref, *scratch):
    states = _flash_states(scratch[:-1])
    qb_ref = scratch[-1]
    t = pl.program_id(1)
    i = qi_ref[t]
    j = kj_ref[t]
    tq, tk = q_ref.shape[2], k_ref.shape[0]
    nb = kmean_ref.shape[1]
    nbp = -(-nb // 8) * 8
    assert 3 + nb <= AUG
    far_kind = bias_ref.shape[1] - 1

    def k_chunk(h, c):
        return k_ref[c * K_CHUNK:(c + 1) * K_CHUNK, _pair_of(h)]

    def off_diagonal_step(far):
        slot = lax.broadcasted_iota(I32, (K_CHUNK, LANES), 1) % AUG
        ones_at = (slot == 3 + j)
        if far:
            ones_at = jnp.logical_or(ones_at, slot < 3)
        key_side = jnp.where(ones_at, 1.0, 0.0).astype(BF16)
        adjust = None if far else (lambda h, c, s: s + bias_ref[h, 1, c * K_CHUNK:(c + 1) * K_CHUNK, :])
        _flash_step(lambda h, c: jnp.concatenate([k_chunk(h, c), key_side], axis=1),
                    lambda h: qb_ref[h], vt_ref, adjust, states, tk, tq)

    @pl.when(j == i)
    def _():
        _flash_init(states)
        blk_row = lax.broadcasted_iota(I32, (nbp, tq), 0)
        aug_row = lax.broadcasted_iota(I32, (AUG, tq), 0)
        past = blk_row < i
        for h in range(N_HEADS):
            qmt = _masked_qt(q_ref, h)
            km = _pad_rows(kmean_ref[0, :, _pair_of(h)], nbp).astype(BF16)
            g = jnp.dot(km, qmt, preferred_element_type=F32)
            c_hi, c_mid, c_lo = _split3_bf16(bias_ref[h, far_kind, 0:1, 0:1])
            aug = jnp.where(aug_row == 0, c_hi.astype(F32),
                            jnp.where(aug_row == 1, c_mid.astype(F32),
                                      jnp.where(aug_row == 2, c_lo.astype(F32), 0.0)))
            for jb in range(nb):
                gj = g[jb:jb + 1, :]
                beats = jnp.logical_and(past, jnp.logical_or(g > gj, jnp.logical_and(g == gj, blk_row < jb)))
                rank = jnp.sum(jnp.where(beats, 1.0, 0.0), axis=0, keepdims=True)
                chosen = jnp.logical_and(rank < MOBA_TOPK, jb < i)
                aug = jnp.where(aug_row == 3 + jb, jnp.where(chosen, 0.0, NEG), aug)
            pieces = [aug]
            if h > 0:
                pieces.insert(0, jnp.zeros((AUG * h, tq), F32))
            if h < N_HEADS - 1:
                pieces.append(jnp.zeros((AUG * (N_HEADS - 1 - h), tq), F32))
            qb_ref[h, 0:LANES, :] = qmt
            qb_ref[h, LANES:2 * LANES, :] = jnp.concatenate(pieces, axis=0).astype(BF16)

        def adjust(h, c, s):
            s = s + bias_ref[h, 0, c * K_CHUNK:(c + 1) * K_CHUNK, :]
            return jnp.where(_chunk_causal(c, tq), s, NEG)

        _flash_step(k_chunk, lambda h: qb_ref[h, 0:LANES, :], vt_ref, adjust, states, tk, tq)

        @pl.when(i == 0)
        def _():
            _flash_finish(o_ref, states)

    @pl.when(j == i - 1)
    def _():
        off_diagonal_step(far=False)
        _flash_finish(o_ref, states)

    @pl.when(j < i - 1)
    def _():
        off_diagonal_step(far=True)


def _moba_prompt(qt, k, vt, kmean, bias, batch, seq):
    tq = tk = MOBA_BLOCK
    nq = seq // tq
    nbp = -(-nq // 8) * 8
    qi, kj = _tri_tables(nq, own_first=True)
    qmap = lambda b, t, qi, kj: (b * nq + qi[t], 0)
    kmap = lambda b, t, qi, kj: (b * nq + kj[t], 0)
    return pl.pallas_call(
        _moba_prompt_kernel,
        grid_spec=pltpu.PrefetchScalarGridSpec(
            num_scalar_prefetch=2,
            grid=(batch, qi.shape[0]),
            in_specs=[pl.BlockSpec((1, GROUP_W, tq), lambda b, t, qi, kj: (b, 0, qi[t])),
                      pl.BlockSpec((tk, GROUP_W), kmap),
                      pl.BlockSpec((1, GROUP_W, tk), lambda b, t, qi, kj: (b, 0, kj[t])),
                      pl.BlockSpec((1, nq, GROUP_W), lambda b, t, qi, kj: (b, 0, 0)),
                      pl.BlockSpec(bias.shape, lambda b, t, qi, kj: (0, 0, 0, 0))],
            out_specs=pl.BlockSpec((tq, GROUP_W), qmap),
            scratch_shapes=_flash_scratch(tq, tk) + [pltpu.VMEM((N_HEADS, 2 * LANES, tq), BF16)]),
        out_shape=jax.ShapeDtypeStruct((batch * seq, GROUP_W), F32),
        compiler_params=_cparams("arbitrary", "arbitrary"),
        name="moba_prompt",
    )(qi, kj, qt, k, vt, kmean, bias)


def _block_diag_q(q):
    ds = q.shape[0]
    lane_head = lax.broadcasted_iota(I32, (N_HEADS, ds, GROUP_W), 2) // HEAD_DIM
    head = lax.broadcasted_iota(I32, (N_HEADS, ds, GROUP_W), 0)
    q3 = jnp.where(lane_head == head, q.astype(F32)[None, :, :], 0.0)
    return q3.reshape(N_HEADS * ds, GROUP_W).astype(BF16)


def _head_diag_out(acc, ds):
    acc3 = acc.reshape(N_HEADS, ds, GROUP_W)
    lane_head = lax.broadcasted_iota(I32, (N_HEADS, ds, GROUP_W), 2) // HEAD_DIM
    head = lax.broadcasted_iota(I32, (N_HEADS, ds, GROUP_W), 0)
    return jnp.sum(jnp.where(lane_head == head, acc3, 0.0), axis=0)


def _pad_rows(x, rows):
    if x.shape[0] == rows:
        return x
    return jnp.concatenate([x, jnp.zeros((rows - x.shape[0], x.shape[1]), x.dtype)], axis=0)


def _ring_copy(pt_ref, page_index, cache_ref, buf_ref, sem_ref, slot, rr):
    return pltpu.make_async_copy(cache_ref.at[pt_ref[page_index]], buf_ref.at[slot, rr], sem_ref.at[slot])


def _ring_fetch(pt_ref, page_index_of, u, cache_ref, buf_ref, sem_ref):
    slot = u % PAGE_RING
    for rr in range(buf_ref.shape[1]):
        _ring_copy(pt_ref, page_index_of(u, rr), cache_ref, buf_ref, sem_ref, slot, rr).start(priority=rr % 2)


def _ring_wait(pt_ref, page_index_of, u, cache_ref, buf_ref, sem_ref):
    slot = u % PAGE_RING
    for rr in range(buf_ref.shape[1]):
        _ring_copy(pt_ref, page_index_of(u, rr), cache_ref, buf_ref, sem_ref, slot, rr).wait()


def _ring_advance(pt_ref, page_index_of, u, n_fetches, cache_ref, buf_ref, sem_ref):
    @pl.when(u == 0)
    def _():
        for ahead in range(min(PAGE_RING - 1, n_fetches)):
            _ring_fetch(pt_ref, page_index_of, ahead, cache_ref, buf_ref, sem_ref)

    @pl.when(u + (PAGE_RING - 1) < n_fetches)
    def _():
        _ring_fetch(pt_ref, page_index_of, u + (PAGE_RING - 1), cache_ref, buf_ref, sem_ref)

    _ring_wait(pt_ref, page_index_of, u, cache_ref, buf_ref, sem_ref)
    return u % PAGE_RING


def _fox_sample_kernel(pt_ref, q_ref, kn_ref, vn_ref, lfn_ref, *refs, pages, n_pages, n_batch):
    lf_refs = refs[0:pages]
    (k_hbm, v_hbm, o_ref, qbd_ref, m_ref, l_ref, acc_ref, carry_ref, rq_ref,
     kbuf, vbuf, ksem, vsem) = refs[pages:]
    g = pl.program_id(1)
    ng = n_pages // pages
    ds = q_ref.shape[1]
    page = kbuf.shape[3]

    def page_index_of(u, rr):
        return (u // ng) * n_pages + (ng - 1 - u % ng) * pages + (pages - 1 - rr)

    u = pl.program_id(0) * ng + g
    slot = _ring_advance(pt_ref, page_index_of, u, n_batch * ng, k_hbm, kbuf, ksem)
    _ring_advance(pt_ref, page_index_of, u, n_batch * ng, v_hbm, vbuf, vsem)
    lane3 = lax.broadcasted_iota(I32, (N_HEADS, ds, page), 2)
    qidx3 = lax.broadcasted_iota(I32, (N_HEADS, ds, page), 1)
    r = lax.broadcasted_iota(I32, (page, page), 0)
    c = lax.broadcasted_iota(I32, (page, page), 1)
    after = jnp.where(r > c, 1.0, 0.0).astype(BF16)

    def suffix_sum(x):
        hi, mid, lo = _split3_bf16(x)
        return (jnp.dot(hi, after, preferred_element_type=F32) + jnp.dot(mid, after, preferred_element_type=F32)
                + jnp.dot(lo, after, preferred_element_type=F32))

    def biased(s, later):
        s3 = s.reshape(N_HEADS, ds, page) + later[:, None, :] - rq_ref[...].reshape(N_HEADS, ds, 1)
        return s3

    def softmax_update(s, pv_of):
        m_prev = m_ref[...]
        m_new = jnp.maximum(m_prev, jnp.max(s, axis=1, keepdims=True))
        alpha = jnp.exp(m_prev - m_new)
        p = jnp.exp(s - m_new)
        l_ref[...] = alpha * l_ref[...] + jnp.sum(p, axis=1, keepdims=True)
        m_ref[...] = m_new
        acc_ref[...] = alpha * acc_ref[...] + pv_of(p.astype(BF16))

    @pl.when(g == 0)
    def _():
        qbd_ref[...] = _block_diag_q(q_ref[0])
        m_ref[...] = jnp.full_like(m_ref, NEG)
        l_ref[...] = jnp.zeros_like(l_ref)
        acc_ref[...] = jnp.zeros_like(acc_ref)
        x = lfn_ref[0]
        later_new = suffix_sum(x)
        rq3 = jnp.sum(jnp.where(lane3 == qidx3, later_new[:, None, :], 0.0), axis=2, keepdims=True)
        rq_ref[...] = rq3.reshape(N_HEADS * ds, 1)
        kn = _pad_rows(kn_ref[0], page).astype(BF16)
        vn = _pad_rows(vn_ref[0], page).astype(BF16)
        s3 = biased(lax.dot_general(qbd_ref[...], kn, NT, preferred_element_type=F32), later_new)
        s3 = jnp.where(lane3 <= qidx3, s3, NEG)
        softmax_update(s3.reshape(N_HEADS * ds, page), lambda p: jnp.dot(p, vn, preferred_element_type=F32))
        carry_ref[...] = jnp.sum(x, axis=1, keepdims=True)

    xs = [lf_refs[rr][0] for rr in range(pages)]
    within = suffix_sum(jnp.concatenate(xs, axis=0))
    run = carry_ref[...]
    parts = []
    for rr in range(pages):
        later = within[rr * N_HEADS:(rr + 1) * N_HEADS, :] + run
        s = jnp.dot(qbd_ref[...], kbuf[slot, rr].astype(BF16), preferred_element_type=F32)
        parts.append(biased(s, later).reshape(N_HEADS * ds, page))
        run = run + jnp.sum(xs[rr], axis=1, keepdims=True)
    carry_ref[...] = run

    def pv_of(p):
        pv = None
        for rr in range(pages):
            term = lax.dot_general(p[:, rr * page:(rr + 1) * page], vbuf[slot, rr].astype(BF16), NT,
                                   preferred_element_type=F32)
            pv = term if pv is None else pv + term
        return pv

    softmax_update(jnp.concatenate(parts, axis=1), pv_of)

    @pl.when(g == ng - 1)
    def _():
        o_ref[0] = _head_diag_out(acc_ref[...] / l_ref[...], ds)


def _page_ring_scratch(pages, page):
    return [pltpu.VMEM((PAGE_RING, pages, GROUP_W, page), F32), pltpu.SemaphoreType.DMA((PAGE_RING,))]


def _fox_sample(page_table, q3, kn3, vn3, lfn_t, cache_k, cache_v, cache_lf_t):
    db, n_pages = page_table.shape
    ds = q3.shape[1]
    pages = PAGES_PER_STEP
    assert n_pages % pages == 0
    ng = n_pages // pages
    page = cache_k.shape[2]

    def page_map(rr):
        return lambda b, g, pt: (pt[b * n_pages + (ng - 1 - g) * pages + (pages - 1 - rr)], 0, 0)

    bmap = lambda b, g, pt: (b, 0, 0)
    in_specs = [pl.BlockSpec((1, ds, GROUP_W), bmap), pl.BlockSpec((1, ds, GROUP_W), bmap),
                pl.BlockSpec((1, ds, GROUP_W), bmap), pl.BlockSpec((1, N_HEADS, page), bmap)]
    in_specs += [pl.BlockSpec((1, N_HEADS, page), page_map(rr)) for rr in range(pages)]
    in_specs += [pl.BlockSpec(memory_space=pl.ANY), pl.BlockSpec(memory_space=pl.ANY)]
    rows = N_HEADS * ds
    k_ring, k_sem = _page_ring_scratch(pages, page)
    v_ring, v_sem = _page_ring_scratch(pages, page)
    return pl.pallas_call(
        functools.partial(_fox_sample_kernel, pages=pages, n_pages=n_pages, n_batch=db),
        grid_spec=pltpu.PrefetchScalarGridSpec(
            num_scalar_prefetch=1,
            grid=(db, ng),
            in_specs=in_specs,
            out_specs=pl.BlockSpec((1, ds, GROUP_W), bmap),
            scratch_shapes=[pltpu.VMEM((rows, GROUP_W), BF16), pltpu.VMEM((rows, 1), F32),
                            pltpu.VMEM((rows, 1), F32), pltpu.VMEM((rows, GROUP_W), F32),
                            pltpu.VMEM((N_HEADS, 1), F32), pltpu.VMEM((rows, 1), F32),
                            k_ring, v_ring, k_sem, v_sem]),
        out_shape=jax.ShapeDtypeStruct((db, ds, GROUP_W), F32),
        compiler_params=_cparams("arbitrary", "arbitrary"),
        name="fox_sample",
    )(page_table.reshape(-1), q3, kn3, vn3, lfn_t, *([cache_lf_t] * pages), cache_k, cache_v)


def _moba_sample_kernel(pt_ref, q_ref, kn_ref, vn_ref, bias_ref, k_hbm, v_hbm, o_ref,
                        qbd_ref, s_ref, p_ref, pnew_ref, linv_ref, acc_ref, kbuf, vbuf, ksem, vsem,
                        *, pages, n_batch):
    ph = pl.program_id(1)
    g = pl.program_id(2)
    ds = q_ref.shape[1]
    page = kbuf.shape[3]
    n_pages = s_ref.shape[0]
    ng = n_pages // pages
    rows = N_HEADS * ds
    pages_per_block = MOBA_BLOCK // page
    n_blocks = n_pages // pages_per_block
    u = pl.program_id(0) * ng + g

    def page_index_of(u, rr):
        return (u // ng) * n_pages + (u % ng) * pages + rr

    @pl.when(jnp.logical_and(ph == 0, g == 0))
    def _():
        qbd_ref[...] = _block_diag_q(q_ref[0])

    @pl.when(ph == 0)
    def _():
        slot = _ring_advance(pt_ref, page_index_of, u, n_batch * ng, k_hbm, kbuf, ksem)
        for rr in range(pages):
            s_ref[g * pages + rr] = jnp.dot(qbd_ref[...], kbuf[slot, rr].astype(BF16), preferred_element_type=F32)

    @pl.when(jnp.logical_and(ph == 0, g == ng - 1))
    def _():
        lane = lax.broadcasted_iota(I32, (rows, LANES), 1)
        gate = jnp.full((rows, LANES), -jnp.inf, F32)
        for b in range(n_blocks):
            tot = s_ref[b * pages_per_block]
            for u in range(1, pages_per_block):
                tot = tot + s_ref[b * pages_per_block + u]
            gate = jnp.where(lane == b, jnp.sum(tot, axis=1, keepdims=True) * (1.0 / MOBA_BLOCK), gate)
        chosen = []
        for _ in range(min(MOBA_TOPK, n_blocks)):
            mx = jnp.max(gate, axis=1, keepdims=True)
            idx = jnp.min(jnp.where(gate == mx, lane.astype(F32), float(LANES)), axis=1, keepdims=True)
            chosen.append(idx)
            gate = jnp.where(lane.astype(F32) == idx, -jnp.inf, gate)
        lane_n = lax.broadcasted_iota(I32, (N_HEADS, ds, page), 2)
        qidx_n = lax.broadcasted_iota(I32, (N_HEADS, ds, page), 1)
        s_new = lax.dot_general(qbd_ref[...], _pad_rows(kn_ref[0], page).astype(BF16), NT,
                                preferred_element_type=F32) + bias_ref[1]
        s_new = jnp.where((lane_n <= qidx_n).reshape(rows, page), s_new, NEG)
        m_tile = s_new
        for pg in range(n_pages):
            b = float(pg // pages_per_block)
            picked = chosen[0] == b
            for idx in chosen[1:]:
                picked = jnp.logical_or(picked, idx == b)
            bias = bias_ref[0] if pg == n_pages - 1 else bias_ref[2]
            s = jnp.where(picked, s_ref[pg] + bias, NEG)
            s_ref[pg] = s
            m_tile = jnp.maximum(m_tile, s)
        m = jnp.max(m_tile, axis=1, keepdims=True)
        p_new = jnp.exp(s_new - m)
        l_tile = p_new
        pnew_ref[...] = p_new.astype(BF16)
        for pg in range(n_pages):
            p = jnp.exp(s_ref[pg] - m)
            l_tile = l_tile + p
            p_ref[pg] = p.astype(BF16)
        linv_ref[...] = 1.0 / jnp.sum(l_tile, axis=1, keepdims=True)

    @pl.when(jnp.logical_and(ph == 1, g == 0))
    def _():
        acc_ref[...] = jnp.dot(pnew_ref[...], _pad_rows(vn_ref[0], page).astype(BF16),
                               preferred_element_type=F32)

    @pl.when(ph == 1)
    def _():
        slot = _ring_advance(pt_ref, page_index_of, u, n_batch * ng, v_hbm, vbuf, vsem)
        acc = acc_ref[...]
        for rr in range(pages):
            acc = acc + lax.dot_general(p_ref[g * pages + rr], vbuf[slot, rr].astype(BF16), NT,
                                        preferred_element_type=F32)
        acc_ref[...] = acc

    @pl.when(jnp.logical_and(ph == 1, g == ng - 1))
    def _():
        o_ref[0] = _head_diag_out(acc_ref[...] * linv_ref[...], ds)


def _moba_sample(page_table, q3, kn3, vn3, bias3, cache_k, cache_v):
    db, n_pages = page_table.shape
    ds = q3.shape[1]
    pages = PAGES_PER_STEP
    page = cache_k.shape[2]
    assert n_pages % pages == 0 and MOBA_BLOCK % page == 0 and (n_pages * page) % MOBA_BLOCK == 0
    assert T5_MAX_DIST <= page and ds <= page
    ng = n_pages // pages
    rows = N_HEADS * ds
    bmap = lambda b, ph, g, pt: (b, 0, 0)
    in_specs = [pl.BlockSpec((1, ds, GROUP_W), bmap), pl.BlockSpec((1, ds, GROUP_W), bmap),
                pl.BlockSpec((1, ds, GROUP_W), bmap),
                pl.BlockSpec(bias3.shape, lambda b, ph, g, pt: (0, 0, 0)),
                pl.BlockSpec(memory_space=pl.ANY), pl.BlockSpec(memory_space=pl.ANY)]
    k_ring, k_sem = _page_ring_scratch(pages, page)
    v_ring, v_sem = _page_ring_scratch(pages, page)
    return pl.pallas_call(
        functools.partial(_moba_sample_kernel, pages=pages, n_batch=db),
        grid_spec=pltpu.PrefetchScalarGridSpec(
            num_scalar_prefetch=1,
            grid=(db, 2, ng),
            in_specs=in_specs,
            out_specs=pl.BlockSpec((1, ds, GROUP_W), bmap),
            scratch_shapes=[pltpu.VMEM((rows, GROUP_W), BF16),
                            pltpu.VMEM((n_pages, rows, page), F32),
                            pltpu.VMEM((n_pages, rows, page), BF16),
                            pltpu.VMEM((rows, page), BF16),
                            pltpu.VMEM((rows, 1), F32),
                            pltpu.VMEM((rows, GROUP_W), F32),
                            k_ring, v_ring, k_sem, v_sem]),
        out_shape=jax.ShapeDtypeStruct((db, ds, GROUP_W), F32),
        compiler_params=_cparams("arbitrary", "arbitrary", "arbitrary"),
        name="moba_sample",
    )(page_table.reshape(-1), q3, kn3, vn3, bias3, cache_k, cache_v)


def _outproj_kernel(of_ref, om_ref, x_ref, gate_ref, shift_ref, scale_ref, gf_ref, gm_ref, wo_ref, gffn_ref,
                    wr_ref, br_ref, cnt0_ref, x1_ref, h2_ref, ri_ref, rw_ref, cnt_ref, carry_ref):
    t = pl.program_id(0)

    @pl.when(t == 0)
    def _():
        carry_ref[...] = cnt0_ref[...]

    tm = x_ref.shape[0]
    nf = (_rms(of_ref[...]) * gf_ref[...]).astype(BF16)
    nm = (_rms(om_ref[...]) * gm_ref[...]).astype(BF16)
    o = (jnp.dot(nf, wo_ref[0:GROUP_W, :], preferred_element_type=F32)
         + jnp.dot(nm, wo_ref[GROUP_W:2 * GROUP_W, :], preferred_element_type=F32))
    x1 = x_ref[...] + gate_ref[...] * o
    x1_ref[...] = x1
    h2 = _rms(x1) * gffn_ref[...]
    h2 = h2 * (1.0 + scale_ref[...]) + shift_ref[...]
    hi = h2.astype(BF16)
    half = h2.shape[1] // 2
    hi32 = pltpu.bitcast(hi.astype(F32), jnp.uint32)
    h2_ref[...] = jnp.bitwise_or(jnp.right_shift(hi32[:, :half], jnp.uint32(16)),
                                 jnp.bitwise_and(hi32[:, half:], jnp.uint32(0xFFFF0000)))

    lo = (h2 - hi.astype(F32)).astype(BF16)
    wr = wr_ref[...]
    whi = wr.astype(BF16)
    wlo = (wr - whi.astype(F32)).astype(BF16)
    lg = (jnp.dot(hi, whi, preferred_element_type=F32) + jnp.dot(lo, whi, preferred_element_type=F32)
          + jnp.dot(hi, wlo, preferred_element_type=F32)) + br_ref[...]
    lane = lax.broadcasted_iota(I32, (tm, LANES), 1)
    lane_f = lane.astype(F32)
    ninf = -jnp.inf
    is_g = jnp.logical_and(lane >= N_EXPERTS, lane < N_EXPERTS + N_GROUPS)
    glog = jnp.where(is_g, lg, ninf)
    gmax = jnp.max(glog, axis=1, keepdims=True)
    gidx = jnp.min(jnp.where(glog == gmax, lane_f, 2.0 * LANES), axis=1, keepdims=True).astype(I32) - N_EXPERTS
    g_w = 1.0 / jnp.sum(jnp.exp(glog - gmax), axis=1, keepdims=True)
    in_grp = jnp.logical_and(lane >= gidx * EXPERTS_PER_GROUP, lane < (gidx + 1) * EXPERTS_PER_GROUP)
    elog = jnp.where(in_grp, lg, ninf)
    e1 = jnp.max(elog, axis=1, keepdims=True)
    i1 = jnp.min(jnp.where(elog == e1, lane_f, 2.0 * LANES), axis=1, keepdims=True).astype(I32)
    z = jnp.sum(jnp.exp(elog - e1), axis=1, keepdims=True)
    elog2 = jnp.where(lane == i1, ninf, elog)
    e2 = jnp.max(elog2, axis=1, keepdims=True)
    i2 = jnp.min(jnp.where(elog2 == e2, lane_f, 2.0 * LANES), axis=1, keepdims=True).astype(I32)
    p1 = 1.0 / z
    p2 = jnp.exp(e2 - e1) / z
    w1 = g_w * (p1 / (p1 + p2))
    w2 = g_w * (p2 / (p1 + p2))

    a = jnp.where(jnp.logical_or(lane == i1, lane == i2), 1.0, 0.0)
    r = lax.broadcasted_iota(I32, (tm, tm), 0)
    c = lax.broadcasted_iota(I32, (tm, tm), 1)
    before = jnp.where(c < r, 1.0, 0.0).astype(BF16)
    pos = jnp.dot(before, a.astype(BF16), preferred_element_type=F32) + carry_ref[...]
    r1 = jnp.sum(jnp.where(lane == i1, pos, 0.0), axis=1, keepdims=True)
    r2 = jnp.sum(jnp.where(lane == i2, pos, 0.0), axis=1, keepdims=True)
    carry_ref[...] = carry_ref[...] + jnp.sum(a, axis=0, keepdims=True)
    cnt_ref[...] = carry_ref[...]

    ri = jnp.where(lane == 0, i1, 0) + jnp.where(lane == 1, i2, 0)
    ri = ri + jnp.where(lane == 2, r1.astype(I32), 0) + jnp.where(lane == 3, r2.astype(I32), 0)
    ri_ref[...] = ri
    rw_ref[...] = jnp.where(lane == 0, w1, 0.0) + jnp.where(lane == 1, w2, 0.0)


def _outproj(of, om, x2, gate3, shift3, scale3, gf, gm, wo, gffn, wr, br, cnt0, *, rows_per_mod, name):
    n, d = x2.shape
    tm = ROW_TILE
    nt = n // tm
    mod_rows = gate3.shape[1]
    mod_map = lambda t: ((t * tm) // rows_per_mod, 0, 0)
    row_map = lambda t: (t, 0)
    const2 = lambda t: (0, 0)
    return pl.pallas_call(
        _outproj_kernel,
        grid=(nt,),
        in_specs=[pl.BlockSpec((tm, GROUP_W), row_map), pl.BlockSpec((tm, GROUP_W), row_map),
                  pl.BlockSpec((tm, d), row_map),
                  pl.BlockSpec((None, mod_rows, d), mod_map), pl.BlockSpec((None, mod_rows, d), mod_map),
                  pl.BlockSpec((None, mod_rows, d), mod_map),
                  pl.BlockSpec((1, GROUP_W), const2), pl.BlockSpec((1, GROUP_W), const2),
                  pl.BlockSpec((d, d), const2), pl.BlockSpec((1, d), const2),
                  pl.BlockSpec((d, LANES), const2), pl.BlockSpec((1, LANES), const2),
                  pl.BlockSpec((1, LANES), const2)],
        out_specs=[pl.BlockSpec((tm, d), row_map), pl.BlockSpec((tm, d // 2), row_map),
                   pl.BlockSpec((tm, LANES), row_map), pl.BlockSpec((tm, LANES), row_map),
                   pl.BlockSpec((1, LANES), const2)],
        out_shape=[jax.ShapeDtypeStruct((n, d), F32), jax.ShapeDtypeStruct((n, d // 2), jnp.uint32),
                   jax.ShapeDtypeStruct((n, LANES), I32), jax.ShapeDtypeStruct((n, LANES), F32),
                   jax.ShapeDtypeStruct((1, LANES), F32)],
        scratch_shapes=[pltpu.VMEM((1, LANES), F32)],
        compiler_params=_cparams("arbitrary"),
        name=name,
    )(of, om, x2, gate3, shift3, scale3, gf, gm, wo, gffn, wr, br, cnt0)


def _scatter_kernel(dest_ref, h_ref, rows_in_ref, rows_ref, sem):
    del rows_in_ref
    tm = h_ref.shape[0]

    def row_copy(r, k):
        return pltpu.make_async_copy(h_ref.at[pl.ds(r, 1)], rows_ref.at[pl.ds(dest_ref[k, r], 1)], sem)

    for r in range(tm):
        row_copy(r, 0).start(priority=0)
        row_copy(r, 1).start(priority=1)
    for _ in range(2):
        pltpu.make_async_copy(h_ref, rows_ref.at[pl.ds(0, tm)], sem).wait()


def _scatter_rows(dest2, h2, rows):
    n, d = h2.shape
    tm = ROW_TILE
    return pl.pallas_call(
        _scatter_kernel,
        grid=(n // tm,),
        in_specs=[pl.BlockSpec((2, tm), lambda t: (0, t), memory_space=pltpu.SMEM),
                  pl.BlockSpec((tm, d), lambda t: (t, 0)),
                  pl.BlockSpec(memory_space=pl.ANY)],
        out_specs=pl.BlockSpec(memory_space=pl.ANY),
        out_shape=jax.ShapeDtypeStruct(rows.shape, rows.dtype),
        scratch_shapes=[pltpu.SemaphoreType.DMA(())],
        input_output_aliases={2: 0},
        compiler_params=_cparams("arbitrary"),
        name="moe_scatter",
    )(dest2, h2, rows)


def _moe_kernel(be_ref, nu_ref, rows_ref, w1_ref, w3_ref, w2_ref, y_ref, w1b_ref, w3b_ref, w2b_ref):
    blk = pl.program_id(0)

    @pl.when(blk < nu_ref[0])
    def _():
        prev = be_ref[jnp.maximum(blk - 1, 0)]

        @pl.when(jnp.logical_or(blk == 0, be_ref[blk] != prev))
        def _():
            w1b_ref[...] = w1_ref[0].astype(BF16)
            w3b_ref[...] = w3_ref[0].astype(BF16)
            w2b_ref[...] = w2_ref[0].astype(BF16)

        packed = rows_ref[...]
        x_lo = pltpu.bitcast(jnp.left_shift(packed, jnp.uint32(16)), F32)
        x_hi = pltpu.bitcast(jnp.bitwise_and(packed, jnp.uint32(0xFFFF0000)), F32)
        x = jnp.concatenate([x_lo, x_hi], axis=1).astype(BF16)
        a = jnp.dot(x, w1b_ref[...], preferred_element_type=F32)
        b = jnp.dot(x, w3b_ref[...], preferred_element_type=F32)
        hm = (a * jax.nn.sigmoid(a)) * b
        y_ref[...] = jnp.dot(hm.astype(BF16), w2b_ref[...], preferred_element_type=F32)

    @pl.when(blk >= nu_ref[0])
    def _():
        y_ref[...] = jnp.zeros_like(y_ref)


def _moe_experts(block_e, n_used, rows, w1, w3, w2):
    p = rows.shape[0]
    d, de = w1.shape[1], w1.shape[2]
    assert rows.shape[1] * 2 == d
    nblk = p // MOE_BLOCK
    row_map = lambda b, be, nu: (jnp.minimum(b, nu[0] - 1), 0)
    return pl.pallas_call(
        _moe_kernel,
        grid_spec=pltpu.PrefetchScalarGridSpec(
            num_scalar_prefetch=2,
            grid=(nblk,),
            in_specs=[pl.BlockSpec((MOE_BLOCK, d // 2), row_map),
                      pl.BlockSpec((1, d, de), lambda b, be, nu: (be[b], 0, 0)),
                      pl.BlockSpec((1, d, de), lambda b, be, nu: (be[b], 0, 0)),
                      pl.BlockSpec((1, de, d), lambda b, be, nu: (be[b], 0, 0))],
            out_specs=pl.BlockSpec((MOE_BLOCK, d), lambda b, be, nu: (b, 0)),
            scratch_shapes=[pltpu.VMEM((d, de), BF16), pltpu.VMEM((d, de), BF16), pltpu.VMEM((de, d), BF16)]),
        out_shape=jax.ShapeDtypeStruct((p, d), F32),
        compiler_params=_cparams("arbitrary"),
        name="moe_experts",
    )(block_e, n_used, rows, w1, w3, w2)


def _combine_kernel(dest_ref, x1_ref, gate_ref, rw_ref, gfin_ref, y_ref, o_ref, ybuf_ref, sem):
    tm = x1_ref.shape[0]

    def row_copy(r, k):
        return pltpu.make_async_copy(y_ref.at[pl.ds(dest_ref[k, r], 1)], ybuf_ref.at[k, pl.ds(r, 1)], sem)

    for r in range(tm):
        row_copy(r, 0).start(priority=0)
        row_copy(r, 1).start(priority=1)
    for k in range(2):
        pltpu.make_async_copy(y_ref.at[pl.ds(0, tm)], ybuf_ref.at[k], sem).wait()
    rw = rw_ref[...]
    moe = rw[:, 0:1] * ybuf_ref[0] + rw[:, 1:2] * ybuf_ref[1]
    xo = x1_ref[...] + gate_ref[...] * moe
    o_ref[...] = _rms(xo) * gfin_ref[...]


def _combine(dest2, x1, gate3, rw, gfin, y, *, rows_per_mod, name):
    n, d = x1.shape
    tm = ROW_TILE
    mod_rows = gate3.shape[1]
    return pl.pallas_call(
        _combine_kernel,
        grid=(n // tm,),
        in_specs=[pl.BlockSpec((2, tm), lambda t: (0, t), memory_space=pltpu.SMEM),
                  pl.BlockSpec((tm, d), lambda t: (t, 0)),
                  pl.BlockSpec((None, mod_rows, d), lambda t: ((t * tm) // rows_per_mod, 0, 0)),
                  pl.BlockSpec((tm, LANES), lambda t: (t, 0)),
                  pl.BlockSpec((1, d), lambda t: (0, 0)),
                  pl.BlockSpec(memory_space=pl.ANY)],
        out_specs=pl.BlockSpec((tm, d), lambda t: (t, 0)),
        out_shape=jax.ShapeDtypeStruct((n, d), F32),
        scratch_shapes=[pltpu.VMEM((2, tm, d), F32), pltpu.SemaphoreType.DMA(())],
        compiler_params=_cparams("arbitrary"),
        name=name,
    )(dest2, x1, gate3, rw, gfin, y)


def kernel(x_prompt, x_sample, cache_fox_k, cache_fox_v, cache_fox_logf, cache_moba_k, cache_moba_v, page_table, c_prompt, c_sample, w_ada, b_ada, g_attn, w_in, b_forget, g_out_fox, g_out_moba, t5_bias, w_out, g_ffn, w_router_group, b_router_group, w_router_expert, b_router_expert, w1, w3, w2, g_final):
    bsz, seq, d = x_prompt.shape
    db, ds, _ = x_sample.shape
    depth = w_ada.shape[0]
    n_phys, page = cache_fox_k.shape[1], cache_fox_k.shape[2]
    assert depth == 1, "one trunk layer"
    assert seq % ROW_TILE == 0 and (db * ds) % ROW_TILE == 0 and ROW_TILE % ds == 0
    assert N_EXPERTS + N_GROUPS <= LANES and N_HEADS <= LANES
    l = 0
    n_p, n_s = bsz * seq, db * ds
    xp2 = x_prompt.reshape(n_p, d)
    xs2 = x_sample.reshape(n_s, d)

    mod = _ada(jnp.concatenate([c_prompt, c_sample], axis=0), w_ada[l], b_ada[l])
    mod_p = [mod[:bsz, i * d:(i + 1) * d].reshape(bsz, 1, d) for i in range(6)]
    mod_s = [jnp.repeat(mod[bsz:, i * d:(i + 1) * d], ds, axis=0).reshape(n_s // ROW_TILE, ROW_TILE, d)
             for i in range(6)]

    w = GROUP_W
    wl = w_in[l]
    wcat = jnp.concatenate([wl[:, :3 * w], wl[:, 3 * w + N_HEADS:],
                            wl[:, 3 * w:3 * w + N_HEADS], jnp.zeros((d, LANES - N_HEADS), F32)],
                           axis=1).astype(BF16)
    bf_pad = jnp.pad(b_forget[l], (0, LANES - N_HEADS)).reshape(1, LANES)
    g_attn2 = g_attn[l].reshape(1, d)

    fqt, fk_t, fv_t, lf_t, mqt, mk_t, mv_t, fk_aug, fqt_aug, kmean, fkb, fvtb, mkb, mvtb = _inproj(
        xp2, mod_p[0], mod_p[1], g_attn2, wcat, bf_pad, rows_per_mod=seq, prompt_extras=True, seq_len=seq)
    sfq, sfk, sfv, slf, smq, smk, smv = _inproj(
        xs2, mod_s[0], mod_s[1], g_attn2, wcat, bf_pad, rows_per_mod=ROW_TILE, prompt_extras=False, seq_len=ds)

    o_fox_p = _fox_prompt(fqt, fkb, fvtb, fk_aug, fqt_aug, bsz, seq)
    bias_p = _t5_tiles(t5_bias, (0, MOBA_BLOCK, 2 * MOBA_BLOCK), MOBA_BLOCK, MOBA_BLOCK, sign=-1, scale=LOG2E)
    o_moba_p = _moba_prompt(mqt, mkb, mvtb, kmean.reshape(bsz, seq // MOBA_BLOCK, w), bias_p, bsz, seq)

    to3 = lambda a: a.reshape(db, ds, w)
    page_t = lambda c: c[l].transpose(0, 2, 3, 1).reshape(n_phys, w, page)
    cfk, cfv, cmk, cmv = page_t(cache_fox_k), page_t(cache_fox_v), page_t(cache_moba_k), page_t(cache_moba_v)
    clf_t = cache_fox_logf[l].transpose(0, 2, 1)
    slf_t = jnp.pad(slf.reshape(db, ds, N_HEADS).transpose(0, 2, 1), ((0, 0), (0, 0), (0, page - ds)))
    o_fox_s = _fox_sample(page_table, to3(sfq), to3(sfk), to3(sfv), slf_t, cfk, cfv, clf_t)
    bias_s = _t5_tiles(t5_bias, (page, 0, 2 * T5_MAX_DIST + page), ds, page, sign=1)
    bias_s = bias_s.transpose(1, 0, 2, 3).reshape(3, N_HEADS * ds, page)
    o_moba_s = _moba_sample(page_table, to3(smq), to3(smk), to3(smv), bias_s, cmk, cmv)

    wr = jnp.concatenate([w_router_expert[l], w_router_group[l],
                          jnp.zeros((d, LANES - N_EXPERTS - N_GROUPS), F32)], axis=1)
    br = jnp.concatenate([b_router_expert[l], b_router_group[l],
                          jnp.zeros((LANES - N_EXPERTS - N_GROUPS,), F32)]).reshape(1, LANES)
    wo = w_out[l].astype(BF16)
    gf, gm, gffn = g_out_fox[l].reshape(1, w), g_out_moba[l].reshape(1, w), g_ffn[l].reshape(1, d)
    x1_p, h2_p, ri_p, rw_p, cnt_p = _outproj(
        o_fox_p, o_moba_p, xp2, mod_p[2], mod_p[3], mod_p[4], gf, gm, wo, gffn, wr, br,
        jnp.zeros((1, LANES), F32), rows_per_mod=seq, name="outproj_prompt")
    x1_s, h2_s, ri_s, rw_s, cnt = _outproj(
        o_fox_s.reshape(n_s, w), o_moba_s.reshape(n_s, w), xs2, mod_s[2], mod_s[3], mod_s[4], gf, gm, wo, gffn,
        wr, br, cnt_p, rows_per_mod=ROW_TILE, name="outproj_sample")

    counts = cnt[0, :N_EXPERTS].astype(I32)
    padded = (counts + MOE_BLOCK - 1) // MOE_BLOCK * MOE_BLOCK
    pend = jnp.cumsum(padded)
    pstart = pend - padded
    n_asg = 2 * (n_p + n_s)
    n_blocks = -(-(n_asg + N_EXPERTS * (MOE_BLOCK - 1)) // MOE_BLOCK)
    blk_start = jnp.arange(n_blocks, dtype=I32) * MOE_BLOCK
    block_e = jnp.clip(jnp.sum((pend[None, :] <= blk_start[:, None]).astype(I32), axis=1), 0, N_EXPERTS - 1)
    n_used = (pend[-1:] // MOE_BLOCK).astype(I32)

    def dest_of(ri):
        return (pstart[ri[:, 0:2]] + ri[:, 2:4]).T.astype(I32)

    dest_p, dest_s = dest_of(ri_p), dest_of(ri_s)
    rows = jnp.zeros((n_blocks * MOE_BLOCK, d // 2), jnp.uint32)
    rows = _scatter_rows(dest_p, h2_p, rows)
    rows = _scatter_rows(dest_s, h2_s, rows)
    y = _moe_experts(block_e, n_used, rows, w1[l], w3[l], w2[l])
    gfin = g_final.reshape(1, d)
    y_prompt = _combine(dest_p, x1_p, mod_p[5], rw_p, gfin, y, rows_per_mod=seq, name="combine_prompt")
    y_sample = _combine(dest_s, x1_s, mod_s[5], rw_s, gfin, y, rows_per_mod=ROW_TILE, name="combine_sample")

    hp = lambda a: a.reshape(depth, bsz, N_HEADS, HEAD_DIM, seq).transpose(0, 1, 4, 2, 3)
    hs = lambda a: a.reshape(depth, db, ds, N_HEADS, HEAD_DIM)
    return (y_prompt.reshape(bsz, seq, d), y_sample.reshape(db, ds, d),
            hp(fk_t), hp(fv_t), lf_t.reshape(depth, bsz, N_HEADS, seq).transpose(0, 1, 3, 2), hp(mk_t), hp(mv_t),
            hs(sfk), hs(sfv), slf.reshape(depth, db, ds, N_HEADS), hs(smk), hs(smv))
```

```python
import functools
import math

import jax
import jax.numpy as jnp
from jax import lax
from jax.experimental import pallas as pl
from jax.experimental.pallas import tpu as pltpu

F32 = jnp.float32
BF16 = jnp.bfloat16
I32 = jnp.int32

HEAD_DIM = 64
N_HEADS = 8
GROUP_W = N_HEADS * HEAD_DIM
LANES = 128
HEADS_PER_LANE_TILE = LANES // HEAD_DIM
N_PAIRS = N_HEADS // HEADS_PER_LANE_TILE
SM_SCALE = HEAD_DIM ** -0.5
LOG2E = math.log2(math.e)
MOBA_BLOCK = 256
MOBA_TOPK = 3
T5_BUCKETS = 32
T5_MAX_DIST = 128
N_GROUPS = 4
EXPERTS_PER_GROUP = 8
N_EXPERTS = N_GROUPS * EXPERTS_PER_GROUP
RMS_EPS = 1e-6
ROW_TILE = 256
ATTN_TILE = 256
K_CHUNK = 64
AUG = LANES // N_HEADS
SUM_ROWS = 16
MOE_BLOCK = 256
PAGES_PER_STEP = 16
PAGE_RING = 3
ROW_RING = 3
NEG = -1e30
VMEM_LIMIT = 48 * 1024 * 1024
HIGHEST = lax.Precision.HIGHEST
NT = (((1,), (1,)), ((), ()))


def _cparams(*sem):
    return pltpu.CompilerParams(dimension_semantics=sem, vmem_limit_bytes=VMEM_LIMIT)


def _rms(x):
    return x * lax.rsqrt(jnp.mean(x * x, axis=-1, keepdims=True) + RMS_EPS)


def _spare_half(h):
    return 0 if h % HEADS_PER_LANE_TILE else HEAD_DIM


def _split3_bf16(x):
    hi = x.astype(BF16)
    r1 = x - hi.astype(F32)
    mid = r1.astype(BF16)
    lo = (r1 - mid.astype(F32)).astype(BF16)
    return hi, mid, lo


def _ada_kernel(c_ref, w_ref, b_ref, o_ref):
    c = c_ref[...]
    a = c * jax.nn.sigmoid(c)
    o_ref[...] = jnp.dot(a, w_ref[...], preferred_element_type=F32, precision=HIGHEST) + b_ref[...]


def _ada(c, w, b):
    n, d = c.shape
    e = w.shape[1]
    tn = 1024
    return pl.pallas_call(
        _ada_kernel,
        grid=(e // tn,),
        in_specs=[pl.BlockSpec((n, d), lambda j: (0, 0)),
                  pl.BlockSpec((d, tn), lambda j: (0, j)),
                  pl.BlockSpec((1, tn), lambda j: (0, j))],
        out_specs=pl.BlockSpec((n, tn), lambda j: (0, j)),
        out_shape=jax.ShapeDtypeStruct((n, e), F32),
        compiler_params=_cparams("arbitrary"),
        name="ada",
    )(c, w, b.reshape(1, e))


def _inproj_kernel(x_ref, shift_ref, scale_ref, g_ref, w_ref, bf_ref,
                   fq_ref, fk_ref, fv_ref, lf_ref, mq_ref, mk_ref, mv_ref, *rest,
                   prompt_extras, tiles_per_seq):
    x = x_ref[...]
    h = _rms(x) * g_ref[...]
    h = h * (1.0 + scale_ref[...]) + shift_ref[...]
    z = jnp.dot(h.astype(BF16), w_ref[...], preferred_element_type=F32)
    w = GROUP_W
    q_scale = SM_SCALE * LOG2E if prompt_extras else SM_SCALE
    fq = z[:, 0:w] * q_scale
    mq = z[:, 3 * w:4 * w] * q_scale
    if prompt_extras:
        fq_ref[0] = fq.T.astype(BF16)
        mq_ref[0] = mq.T.astype(BF16)
    else:
        fq_ref[...] = fq.astype(BF16)
        mq_ref[...] = mq.astype(BF16)
    fk, fv = z[:, w:2 * w], z[:, 2 * w:3 * w]
    mk, mv = z[:, 4 * w:5 * w], z[:, 5 * w:6 * w]
    fg = z[:, 6 * w:6 * w + LANES] + bf_ref[...]
    lf = jnp.minimum(fg, 0.0) - jnp.log1p(jnp.exp(-jnp.abs(fg)))
    if not prompt_extras:
        fk_ref[...] = fk
        fv_ref[...] = fv
        mk_ref[...] = mk
        mv_ref[...] = mv
        lf_ref[...] = lf[:, :N_HEADS]
        return
    fkaug_ref, fqtaug_ref, kmean_ref, fkb_ref, fvtb_ref, mkb_ref, mvtb_ref, carry_ref = rest
    tm = x.shape[0]
    fk_ref[0] = fk.T
    fv_t = fv.T
    fv_ref[0] = fv_t
    mk_ref[0] = mk.T
    mv_t = mv.T
    mv_ref[0] = mv_t
    fkb_ref[...] = fk.astype(BF16)
    fvtb_ref[0] = fv_t.astype(BF16)
    mkb_ref[...] = mk.astype(BF16)
    mvtb_ref[0] = mv_t.astype(BF16)
    lf_ref[0] = lf.T[:N_HEADS, :]
    kmean_ref[0] = jnp.mean(mk, axis=0, keepdims=True)

    @pl.when(pl.program_id(0) % tiles_per_seq == 0)
    def _():
        carry_ref[...] = jnp.zeros_like(carry_ref)

    r = lax.broadcasted_iota(I32, (tm, tm), 0)
    c = lax.broadcasted_iota(I32, (tm, tm), 1)
    tri = jnp.where(c <= r, 1.0, 0.0).astype(BF16)
    hi, mid, lo = _split3_bf16(lf)
    cs = (jnp.dot(tri, hi, preferred_element_type=F32)
          + jnp.dot(tri, mid, preferred_element_type=F32)
          + jnp.dot(tri, lo, preferred_element_type=F32)) + carry_ref[...]
    carry_ref[...] = cs[tm - 1:tm, :]
    hi, mid, lo = _split3_bf16(cs * LOG2E)
    er = lax.broadcasted_iota(I32, (LANES, LANES), 0)
    ec = lax.broadcasted_iota(I32, (LANES, LANES), 1)

    def spread(x, off):
        e = jnp.where(jnp.logical_and(er < N_HEADS, ec == AUG * er + off), 1.0, 0.0).astype(BF16)
        return jnp.dot(x, e, preferred_element_type=F32)

    lane = lax.broadcasted_iota(I32, (tm, LANES), 1)
    slot = lane % AUG
    fq_aug = (spread(hi, 0) + spread(mid, 1) + spread(lo, 2)
              + jnp.where(jnp.logical_and(slot >= 3, slot < 6), 1.0, 0.0))
    fk_aug = jnp.where(slot < 3, 1.0, 0.0) - (spread(hi, 3) + spread(mid, 4) + spread(lo, 5))
    fq_aug_t = fq_aug.T
    k_tiles, q_tiles = [], []
    for hd in range(N_HEADS):
        base = _spare_half(hd)
        shift = (base - AUG * hd) % LANES
        moved = pltpu.roll(fk_aug, shift, 1) if shift else fk_aug
        k_tiles.append(jnp.where(jnp.logical_and(lane >= base, lane < base + AUG), moved, 0.0))
        q_rows = [fq_aug_t[AUG * hd:AUG * (hd + 1), :]]
        if base:
            q_rows.insert(0, jnp.zeros((base, tm), F32))
        if LANES - base - AUG:
            q_rows.append(jnp.zeros((LANES - base - AUG, tm), F32))
        q_tiles.append(jnp.concatenate(q_rows, axis=0))
    fkaug_ref[...] = jnp.concatenate(k_tiles, axis=1).astype(BF16)
    fqtaug_ref[0] = jnp.concatenate(q_tiles, axis=0).astype(BF16)


def _inproj(x2, shift3, scale3, g_attn, wcat, bf_pad, *, rows_per_mod, prompt_extras, seq_len):
    n, d = x2.shape
    tm = ROW_TILE
    assert n % tm == 0
    nt = n // tm
    mod_rows = shift3.shape[1]
    mod_map = lambda t: ((t * tm) // rows_per_mod, 0, 0)
    row_map = lambda t: (t, 0)
    const2 = lambda t: (0, 0)
    ecols = wcat.shape[1]
    row_bf16 = (jax.ShapeDtypeStruct((n, GROUP_W), BF16), pl.BlockSpec((tm, GROUP_W), row_map))
    scratch = []
    tiles_per_seq = 1
    if prompt_extras:
        assert tm == MOBA_BLOCK and seq_len % tm == 0
        tiles_per_seq = seq_len // tm
        nb = n // seq_len
        t_map = lambda t: (t // tiles_per_seq, 0, t % tiles_per_seq)
        kv = (jax.ShapeDtypeStruct((nb, GROUP_W, seq_len), F32), pl.BlockSpec((1, GROUP_W, tm), t_map))
        lfo = (jax.ShapeDtypeStruct((nb, N_HEADS, seq_len), F32), pl.BlockSpec((1, N_HEADS, tm), t_map))
        kvt_bf16 = (jax.ShapeDtypeStruct((nb, GROUP_W, seq_len), BF16), pl.BlockSpec((1, GROUP_W, tm), t_map))
        outs = [kvt_bf16, kv, kv, lfo, kvt_bf16, kv, kv,
                (jax.ShapeDtypeStruct((n, N_HEADS * LANES), BF16), pl.BlockSpec((tm, N_HEADS * LANES), row_map)),
                (jax.ShapeDtypeStruct((nb, N_HEADS * LANES, seq_len), BF16),
                 pl.BlockSpec((1, N_HEADS * LANES, tm), t_map)),
                (jax.ShapeDtypeStruct((nt, 1, GROUP_W), F32), pl.BlockSpec((1, 1, GROUP_W), lambda t: (t, 0, 0)))]
        outs += [row_bf16, kvt_bf16, row_bf16, kvt_bf16]
        scratch = [pltpu.VMEM((1, LANES), F32)]
    else:
        kv = (jax.ShapeDtypeStruct((n, GROUP_W), F32), pl.BlockSpec((tm, GROUP_W), row_map))
        lfo = (jax.ShapeDtypeStruct((n, N_HEADS), F32), pl.BlockSpec((tm, N_HEADS), row_map))
        outs = [row_bf16, kv, kv, lfo, row_bf16, kv, kv]
    return pl.pallas_call(
        functools.partial(_inproj_kernel, prompt_extras=prompt_extras, tiles_per_seq=tiles_per_seq),
        grid=(nt,),
        in_specs=[pl.BlockSpec((tm, d), row_map),
                  pl.BlockSpec((None, mod_rows, d), mod_map),
                  pl.BlockSpec((None, mod_rows, d), mod_map),
                  pl.BlockSpec((1, d), const2),
                  pl.BlockSpec((d, ecols), const2),
                  pl.BlockSpec((1, LANES), const2)],
        out_specs=[o[1] for o in outs],
        out_shape=[o[0] for o in outs],
        scratch_shapes=scratch,
        compiler_params=_cparams("arbitrary"),
        name="inproj_prompt" if prompt_extras else "inproj_sample",
    )(x2, shift3, scale3, g_attn, wcat, bf_pad)


def _t5_kernel(t5_ref, o_ref, *, offs, sign, scale):
    h = pl.program_id(0)
    rows, cols = o_ref.shape[2], o_ref.shape[3]
    r = lax.broadcasted_iota(I32, (rows, cols), 0)
    c = lax.broadcasted_iota(I32, (rows, cols), 1)
    max_exact = T5_BUCKETS // 2
    for k, off in enumerate(offs):
        rel = jnp.maximum(off + sign * (r - c), 0)
        relf = jnp.maximum(rel, 1).astype(F32)
        large = max_exact + (jnp.log(relf / max_exact) / math.log(T5_MAX_DIST / max_exact)
                             * (T5_BUCKETS - max_exact)).astype(I32)
        large = jnp.minimum(large, T5_BUCKETS - 1)
        bucket = jnp.where(rel < max_exact, rel, large)
        acc = jnp.zeros((rows, cols), F32)
        for b in range(T5_BUCKETS):
            acc = jnp.where(bucket == b, t5_ref[b, h], acc)
        o_ref[0, k] = acc if scale == 1.0 else acc * scale


def _t5_tiles(t5_bias, offs, rows, cols, sign, scale=1.0):
    nh = t5_bias.shape[1]
    return pl.pallas_call(
        functools.partial(_t5_kernel, offs=tuple(offs), sign=sign, scale=scale),
        grid=(nh,),
        in_specs=[pl.BlockSpec(memory_space=pltpu.SMEM)],
        out_specs=pl.BlockSpec((1, len(offs), rows, cols), lambda h: (h, 0, 0, 0)),
        out_shape=jax.ShapeDtypeStruct((nh, len(offs), rows, cols), F32),
        compiler_params=_cparams("arbitrary"),
        name="t5_tiles",
    )(t5_bias)


def _pair_masks(rows):
    lane = lax.broadcasted_iota(I32, (rows, LANES), 1)
    lo = lane < HEAD_DIM
    return lo, jnp.logical_not(lo)


def _masked_qt(qt_ref, h):
    p, e = divmod(h, HEADS_PER_LANE_TILE)
    qt = qt_ref[0, p * LANES:(p + 1) * LANES, :]
    row = lax.broadcasted_iota(I32, qt.shape, 0)
    keep = (row >= HEAD_DIM) if e else (row < HEAD_DIM)
    return jnp.where(keep, qt, jnp.zeros_like(qt))


def _pair_of(h):
    return slice((h // HEADS_PER_LANE_TILE) * LANES, (h // HEADS_PER_LANE_TILE + 1) * LANES)


def _flash_step(lhs_of, rhs_of, vt_ref, adjust, states, tk, tq):
    chunks = [slice(c * K_CHUNK, (c + 1) * K_CHUNK) for c in range(tk // K_CHUNK)]
    for h, (m_ref, _, _, s_scr, _, a_ref) in enumerate(states):
        rhs = rhs_of(h)
        m8 = jnp.full((8, tq), NEG, F32)
        for c, rows in enumerate(chunks):
            s = jnp.dot(lhs_of(h, c), rhs, preferred_element_type=F32)
            if adjust is not None:
                s = adjust(h, c, s)
            s_scr[rows, :] = s
            m8 = jnp.maximum(m8, jnp.max(s.reshape(K_CHUNK // 8, 8, tq), axis=0))
        m_prev = m_ref[...]
        m_new = jnp.maximum(m_prev, jnp.max(m8, axis=0, keepdims=True))
        a_ref[...] = jnp.exp2(m_prev - m_new)
        m_ref[...] = m_new
    for m_ref, _, _, s_scr, p_scr, _ in states:
        m_new = m_ref[...]
        for rows in chunks:
            p_scr[rows, :] = jnp.exp2(s_scr[rows, :] - m_new).astype(BF16)
    ones = jnp.ones((SUM_ROWS, tk), BF16)
    for h, (_, l_ref, acc_ref, _, p_scr, a_ref) in enumerate(states):
        feat = slice(h * HEAD_DIM, (h + 1) * HEAD_DIM)
        vt_sum = jnp.concatenate([vt_ref[0, feat, :], ones], axis=0)
        pv = jnp.dot(vt_sum, p_scr[...], preferred_element_type=F32)
        acc_ref[...] = a_ref[...] * acc_ref[...] + pv[:HEAD_DIM]
        l_ref[...] = a_ref[...] * l_ref[...] + pv[HEAD_DIM:HEAD_DIM + 1]


FLASH_BUFS_PER_HEAD = 6


def _flash_scratch(tq, tk):
    per_head = [pltpu.VMEM((1, tq), F32), pltpu.VMEM((1, tq), F32), pltpu.VMEM((HEAD_DIM, tq), F32),
                pltpu.VMEM((tk, tq), F32), pltpu.VMEM((tk, tq), BF16), pltpu.VMEM((1, tq), F32)]
    return per_head * N_HEADS


def _flash_states(scratch):
    n = FLASH_BUFS_PER_HEAD
    return [scratch[n * h:n * (h + 1)] for h in range(N_HEADS)]


def _flash_init(states):
    for m_ref, l_ref, acc_ref, _, _, _ in states:
        m_ref[...] = jnp.full_like(m_ref, NEG)
        l_ref[...] = jnp.zeros_like(l_ref)
        acc_ref[...] = jnp.zeros_like(acc_ref)


def _flash_finish(o_ref, states):
    o_t = jnp.concatenate([acc_ref[...] / l_ref[...] for _, l_ref, acc_ref, _, _, _ in states], axis=0)
    o_ref[...] = o_t.T


def _chunk_causal(c, tq):
    krow = c * K_CHUNK + lax.broadcasted_iota(I32, (K_CHUNK, tq), 0)
    qcol = lax.broadcasted_iota(I32, (K_CHUNK, tq), 1)
    return krow <= qcol


def _head_rows(x, h, rows_per_head):
    row = lax.broadcasted_iota(I32, x.shape, 0)
    keep = jnp.logical_and(row >= h * rows_per_head, row < (h + 1) * rows_per_head)
    return jnp.where(keep, x, jnp.zeros_like(x))


def _own_half(shape, h, axis):
    pos = lax.broadcasted_iota(I32, shape, axis)
    return (pos >= HEAD_DIM) if h % HEADS_PER_LANE_TILE else (pos < HEAD_DIM)


def _fox_prompt_kernel(qi_ref, kj_ref, q_ref, k_ref, vt_ref, fqt_ref, fk_ref, o_ref, *scratch):
    states = _flash_states(scratch)
    t = pl.program_id(1)
    i = qi_ref[t]
    j = kj_ref[t]
    tq, tk = q_ref.shape[2], k_ref.shape[0]

    @pl.when(j == 0)
    def _():
        _flash_init(states)

    def lhs_of(h, c):
        rows = slice(c * K_CHUNK, (c + 1) * K_CHUNK)
        k = k_ref[rows, _pair_of(h)]
        return jnp.where(_own_half(k.shape, h, 1), k, fk_ref[rows, h * LANES:(h + 1) * LANES])

    def rhs_of(h):
        qt = q_ref[0, _pair_of(h), :]
        return jnp.where(_own_half(qt.shape, h, 0), qt, fqt_ref[0, h * LANES:(h + 1) * LANES, :])

    def step(diagonal):
        adjust = (lambda h, c, s: jnp.where(_chunk_causal(c, tq), s, NEG)) if diagonal else None
        _flash_step(lhs_of, rhs_of, vt_ref, adjust, states, tk, tq)

    @pl.when(j < i)
    def _():
        step(False)

    @pl.when(j == i)
    def _():
        step(True)
        _flash_finish(o_ref, states)


def _tri_tables(nq, own_first):
    qi, kj = [], []
    for i in range(nq):
        order = ([i] + list(range(i))) if own_first else list(range(i + 1))
        for j in order:
            qi.append(i)
            kj.append(j)
    return jnp.asarray(qi, I32), jnp.asarray(kj, I32)


def _fox_prompt(qt, k, vt, fk_aug, fqt_aug, batch, seq):
    tq = tk = ATTN_TILE
    nq = seq // tq
    qi, kj = _tri_tables(nq, own_first=False)
    qmap = lambda b, t, qi, kj: (b * nq + qi[t], 0)
    kmap = lambda b, t, qi, kj: (b * nq + kj[t], 0)
    qtmap = lambda b, t, qi, kj: (b, 0, qi[t])
    ktmap = lambda b, t, qi, kj: (b, 0, kj[t])
    return pl.pallas_call(
        _fox_prompt_kernel,
        grid_spec=pltpu.PrefetchScalarGridSpec(
            num_scalar_prefetch=2,
            grid=(batch, qi.shape[0]),
            in_specs=[pl.BlockSpec((1, GROUP_W, tq), qtmap),
                      pl.BlockSpec((tk, GROUP_W), kmap),
                      pl.BlockSpec((1, GROUP_W, tk), ktmap),
                      pl.BlockSpec((1, N_HEADS * LANES, tq), qtmap),
                      pl.BlockSpec((tk, N_HEADS * LANES), kmap)],
            out_specs=pl.BlockSpec((tq, GROUP_W), qmap),
            scratch_shapes=_flash_scratch(tq, tk)),
        out_shape=jax.ShapeDtypeStruct((batch * seq, GROUP_W), F32),
        compiler_params=_cparams("arbitrary", "arbitrary"),
        name="fox_prompt",
    )(qi, kj, qt, k, vt, fqt_aug, fk_aug)


def _moba_prompt_kernel(qi_ref, kj_ref, q_ref, k_ref, vt_ref, kmean_ref, bias_ref, o_ref, *scratch):
    states = _flash_states(scratch[:-1])
    qb_ref = scratch[-1]
    t = pl.program_id(1)
    i = qi_ref[t]
    j = kj_ref[t]
    tq, tk = q_ref.shape[2], k_ref.shape[0]
    nb = kmean_ref.shape[1]
    nbp = -(-nb // 8) * 8
    assert 3 + nb <= AUG
    far_kind = bias_ref.shape[1] - 1

    def k_chunk(h, c, spare):
        k = k_ref[c * K_CHUNK:(c + 1) * K_CHUNK, _pair_of(h)]
        return jnp.where(_own_half(k.shape, h, 1), k, spare)

    def off_diagonal_step(far):
        lane = lax.broadcasted_iota(I32, (K_CHUNK, LANES), 1)
        key_side = []
        for base in (_spare_half(0), _spare_half(1)):
            slot = lane - base
            ones_at = (slot == 3 + j)
            if far:
                ones_at = jnp.logical_or(ones_at, jnp.logical_and(slot >= 0, slot < 3))
            key_side.append(jnp.where(ones_at, 1.0, 0.0).astype(BF16))
        adjust = None if far else (lambda h, c, s: s + bias_ref[h, 1, c * K_CHUNK:(c + 1) * K_CHUNK, :])
        _flash_step(lambda h, c: k_chunk(h, c, key_side[h % HEADS_PER_LANE_TILE]),
                    lambda h: qb_ref[h], vt_ref, adjust, states, tk, tq)

    @pl.when(j == i)
    def _():
        _flash_init(states)
        blk_row = lax.broadcasted_iota(I32, (nbp, tq), 0)
        aug_row = lax.broadcasted_iota(I32, (AUG, tq), 0)
        past = blk_row < i
        for h in range(N_HEADS):
            km = _pad_rows(kmean_ref[0, :, _pair_of(h)], nbp).astype(BF16)
            g = jnp.dot(km, _masked_qt(q_ref, h), preferred_element_type=F32)
            c_hi, c_mid, c_lo = _split3_bf16(bias_ref[h, far_kind, 0:1, 0:1])
            aug = jnp.where(aug_row == 0, c_hi.astype(F32),
                            jnp.where(aug_row == 1, c_mid.astype(F32),
                                      jnp.where(aug_row == 2, c_lo.astype(F32), 0.0)))
            for jb in range(nb):
                gj = g[jb:jb + 1, :]
                beats = jnp.logical_and(past, jnp.logical_or(g > gj, jnp.logical_and(g == gj, blk_row < jb)))
                rank = jnp.sum(jnp.where(beats, 1.0, 0.0), axis=0, keepdims=True)
                chosen = jnp.logical_and(rank < MOBA_TOPK, jb < i)
                aug = jnp.where(aug_row == 3 + jb, jnp.where(chosen, 0.0, NEG), aug)
            own = q_ref[0, h * HEAD_DIM:(h + 1) * HEAD_DIM, :]
            spare = jnp.concatenate([aug.astype(BF16), jnp.zeros((HEAD_DIM - AUG, tq), BF16)], axis=0)
            qb_ref[h] = jnp.concatenate([spare, own] if h % HEADS_PER_LANE_TILE else [own, spare], axis=0)

        def adjust(h, c, s):
            s = s + bias_ref[h, 0, c * K_CHUNK:(c + 1) * K_CHUNK, :]
            return jnp.where(_chunk_causal(c, tq), s, NEG)

        zero = jnp.zeros((K_CHUNK, LANES), BF16)
        _flash_step(lambda h, c: k_chunk(h, c, zero), lambda h: qb_ref[h], vt_ref, adjust, states, tk, tq)

        @pl.when(i == 0)
        def _():
            _flash_finish(o_ref, states)

    @pl.when(j == i - 1)
    def _():
        off_diagonal_step(far=False)
        _flash_finish(o_ref, states)

    @pl.when(j < i - 1)
    def _():
        off_diagonal_step(far=True)


def _moba_prompt(qt, k, vt, kmean, bias, batch, seq):
    tq = tk = MOBA_BLOCK
    nq = seq // tq
    nbp = -(-nq // 8) * 8
    qi, kj = _tri_tables(nq, own_first=True)
    qmap = lambda b, t, qi, kj: (b * nq + qi[t], 0)
    kmap = lambda b, t, qi, kj: (b * nq + kj[t], 0)
    return pl.pallas_call(
        _moba_prompt_kernel,
        grid_spec=pltpu.PrefetchScalarGridSpec(
            num_scalar_prefetch=2,
            grid=(batch, qi.shape[0]),
            in_specs=[pl.BlockSpec((1, GROUP_W, tq), lambda b, t, qi, kj: (b, 0, qi[t])),
                      pl.BlockSpec((tk, GROUP_W), kmap),
                      pl.BlockSpec((1, GROUP_W, tk), lambda b, t, qi, kj: (b, 0, kj[t])),
                      pl.BlockSpec((1, nq, GROUP_W), lambda b, t, qi, kj: (b, 0, 0)),
                      pl.BlockSpec(bias.shape, lambda b, t, qi, kj: (0, 0, 0, 0))],
            out_specs=pl.BlockSpec((tq, GROUP_W), qmap),
            scratch_shapes=_flash_scratch(tq, tk) + [pltpu.VMEM((N_HEADS, LANES, tq), BF16)]),
        out_shape=jax.ShapeDtypeStruct((batch * seq, GROUP_W), F32),
        compiler_params=_cparams("arbitrary", "arbitrary"),
        name="moba_prompt",
    )(qi, kj, qt, k, vt, kmean, bias)


def _block_diag_q(q):
    ds = q.shape[0]
    lane_head = lax.broadcasted_iota(I32, (N_HEADS, ds, GROUP_W), 2) // HEAD_DIM
    head = lax.broadcasted_iota(I32, (N_HEADS, ds, GROUP_W), 0)
    q3 = jnp.where(lane_head == head, q.astype(F32)[None, :, :], 0.0)
    return q3.reshape(N_HEADS * ds, GROUP_W).astype(BF16)


def _head_diag_out(acc, ds):
    acc3 = acc.reshape(N_HEADS, ds, GROUP_W)
    lane_head = lax.broadcasted_iota(I32, (N_HEADS, ds, GROUP_W), 2) // HEAD_DIM
    head = lax.broadcasted_iota(I32, (N_HEADS, ds, GROUP_W), 0)
    return jnp.sum(jnp.where(lane_head == head, acc3, 0.0), axis=0)


def _pad_rows(x, rows):
    if x.shape[0] == rows:
        return x
    return jnp.concatenate([x, jnp.zeros((rows - x.shape[0], x.shape[1]), x.dtype)], axis=0)


def _ring_copy(pt_ref, page_index, cache_ref, buf_ref, sem_ref, slot, rr):
    return pltpu.make_async_copy(cache_ref.at[pt_ref[page_index]], buf_ref.at[slot, rr], sem_ref.at[slot])


def _ring_fetch(pt_ref, page_index_of, u, cache_ref, buf_ref, sem_ref):
    slot = u % PAGE_RING
    for rr in range(buf_ref.shape[1]):
        _ring_copy(pt_ref, page_index_of(u, rr), cache_ref, buf_ref, sem_ref, slot, rr).start(priority=rr % 2)


def _ring_wait(pt_ref, page_index_of, u, cache_ref, buf_ref, sem_ref):
    slot = u % PAGE_RING
    for rr in range(buf_ref.shape[1]):
        _ring_copy(pt_ref, page_index_of(u, rr), cache_ref, buf_ref, sem_ref, slot, rr).wait()


def _ring_advance(pt_ref, page_index_of, u, n_fetches, cache_ref, buf_ref, sem_ref):
    @pl.when(u == 0)
    def _():
        for ahead in range(min(PAGE_RING - 1, n_fetches)):
            _ring_fetch(pt_ref, page_index_of, ahead, cache_ref, buf_ref, sem_ref)

    @pl.when(u + (PAGE_RING - 1) < n_fetches)
    def _():
        _ring_fetch(pt_ref, page_index_of, u + (PAGE_RING - 1), cache_ref, buf_ref, sem_ref)

    _ring_wait(pt_ref, page_index_of, u, cache_ref, buf_ref, sem_ref)
    return u % PAGE_RING


def _fox_sample_kernel(pt_ref, q_ref, kn_ref, vn_ref, lfn_ref, *refs, pages, n_pages, n_batch):
    lf_refs = refs[0:pages]
    (k_hbm, v_hbm, o_ref, qbd_ref, m_ref, l_ref, acc_ref, carry_ref, rq_ref,
     kbuf, vbuf, ksem, vsem) = refs[pages:]
    g = pl.program_id(1)
    ng = n_pages // pages
    ds = q_ref.shape[1]
    page = kbuf.shape[3]

    def page_index_of(u, rr):
        return (u // ng) * n_pages + (ng - 1 - u % ng) * pages + (pages - 1 - rr)

    u = pl.program_id(0) * ng + g
    slot = _ring_advance(pt_ref, page_index_of, u, n_batch * ng, k_hbm, kbuf, ksem)
    _ring_advance(pt_ref, page_index_of, u, n_batch * ng, v_hbm, vbuf, vsem)
    lane3 = lax.broadcasted_iota(I32, (N_HEADS, ds, page), 2)
    qidx3 = lax.broadcasted_iota(I32, (N_HEADS, ds, page), 1)
    r = lax.broadcasted_iota(I32, (page, page), 0)
    c = lax.broadcasted_iota(I32, (page, page), 1)
    after = jnp.where(r > c, 1.0, 0.0).astype(BF16)

    def suffix_sum(x):
        hi, mid, lo = _split3_bf16(x)
        return (jnp.dot(hi, after, preferred_element_type=F32) + jnp.dot(mid, after, preferred_element_type=F32)
                + jnp.dot(lo, after, preferred_element_type=F32))

    def biased(s, later):
        s3 = s.reshape(N_HEADS, ds, page) + later[:, None, :] - rq_ref[...].reshape(N_HEADS, ds, 1)
        return s3

    def softmax_update(s, pv_of):
        m_prev = m_ref[...]
        m_new = jnp.maximum(m_prev, jnp.max(s, axis=1, keepdims=True))
        alpha = jnp.exp(m_prev - m_new)
        p = jnp.exp(s - m_new)
        l_ref[...] = alpha * l_ref[...] + jnp.sum(p, axis=1, keepdims=True)
        m_ref[...] = m_new
        acc_ref[...] = alpha * acc_ref[...] + pv_of(p.astype(BF16))

    @pl.when(g == 0)
    def _():
        qbd_ref[...] = _block_diag_q(q_ref[0])
        m_ref[...] = jnp.full_like(m_ref, NEG)
        l_ref[...] = jnp.zeros_like(l_ref)
        acc_ref[...] = jnp.zeros_like(acc_ref)
        x = lfn_ref[0]
        later_new = suffix_sum(x)
        rq3 = jnp.sum(jnp.where(lane3 == qidx3, later_new[:, None, :], 0.0), axis=2, keepdims=True)
        rq_ref[...] = rq3.reshape(N_HEADS * ds, 1)
        kn = _pad_rows(kn_ref[0], page).astype(BF16)
        vn = _pad_rows(vn_ref[0], page).astype(BF16)
        s3 = biased(lax.dot_general(qbd_ref[...], kn, NT, preferred_element_type=F32), later_new)
        s3 = jnp.where(lane3 <= qidx3, s3, NEG)
        softmax_update(s3.reshape(N_HEADS * ds, page), lambda p: jnp.dot(p, vn, preferred_element_type=F32))
        carry_ref[...] = jnp.sum(x, axis=1, keepdims=True)

    xs = [lf_refs[rr][0] for rr in range(pages)]
    within = suffix_sum(jnp.concatenate(xs, axis=0))
    run = carry_ref[...]
    parts = []
    for rr in range(pages):
        later = within[rr * N_HEADS:(rr + 1) * N_HEADS, :] + run
        s = jnp.dot(qbd_ref[...], kbuf[slot, rr].astype(BF16), preferred_element_type=F32)
        parts.append(biased(s, later).reshape(N_HEADS * ds, page))
        run = run + jnp.sum(xs[rr], axis=1, keepdims=True)
    carry_ref[...] = run

    def pv_of(p):
        pv = None
        for rr in range(pages):
            term = lax.dot_general(p[:, rr * page:(rr + 1) * page], vbuf[slot, rr].astype(BF16), NT,
                                   preferred_element_type=F32)
            pv = term if pv is None else pv + term
        return pv

    softmax_update(jnp.concatenate(parts, axis=1), pv_of)

    @pl.when(g == ng - 1)
    def _():
        o_ref[0] = _head_diag_out(acc_ref[...] / l_ref[...], ds)


def _page_ring_scratch(pages, page):
    return [pltpu.VMEM((PAGE_RING, pages, GROUP_W, page), F32), pltpu.SemaphoreType.DMA((PAGE_RING,))]


def _fox_sample(page_table, q3, kn3, vn3, lfn_t, cache_k, cache_v, cache_lf_t):
    db, n_pages = page_table.shape
    ds = q3.shape[1]
    pages = PAGES_PER_STEP
    assert n_pages % pages == 0
    ng = n_pages // pages
    page = cache_k.shape[2]

    def page_map(rr):
        return lambda b, g, pt: (pt[b * n_pages + (ng - 1 - g) * pages + (pages - 1 - rr)], 0, 0)

    bmap = lambda b, g, pt: (b, 0, 0)
    in_specs = [pl.BlockSpec((1, ds, GROUP_W), bmap), pl.BlockSpec((1, ds, GROUP_W), bmap),
                pl.BlockSpec((1, ds, GROUP_W), bmap), pl.BlockSpec((1, N_HEADS, page), bmap)]
    in_specs += [pl.BlockSpec((1, N_HEADS, page), page_map(rr)) for rr in range(pages)]
    in_specs += [pl.BlockSpec(memory_space=pl.ANY), pl.BlockSpec(memory_space=pl.ANY)]
    rows = N_HEADS * ds
    k_ring, k_sem = _page_ring_scratch(pages, page)
    v_ring, v_sem = _page_ring_scratch(pages, page)
    return pl.pallas_call(
        functools.partial(_fox_sample_kernel, pages=pages, n_pages=n_pages, n_batch=db),
        grid_spec=pltpu.PrefetchScalarGridSpec(
            num_scalar_prefetch=1,
            grid=(db, ng),
            in_specs=in_specs,
            out_specs=pl.BlockSpec((1, ds, GROUP_W), bmap),
            scratch_shapes=[pltpu.VMEM((rows, GROUP_W), BF16), pltpu.VMEM((rows, 1), F32),
                            pltpu.VMEM((rows, 1), F32), pltpu.VMEM((rows, GROUP_W), F32),
                            pltpu.VMEM((N_HEADS, 1), F32), pltpu.VMEM((rows, 1), F32),
                            k_ring, v_ring, k_sem, v_sem]),
        out_shape=jax.ShapeDtypeStruct((db, ds, GROUP_W), F32),
        compiler_params=_cparams("arbitrary", "arbitrary"),
        name="fox_sample",
    )(page_table.reshape(-1), q3, kn3, vn3, lfn_t, *([cache_lf_t] * pages), cache_k, cache_v)


def _moba_sample_kernel(pt_ref, q_ref, kn_ref, vn_ref, bias_ref, k_hbm, v_hbm, o_ref,
                        qbd_ref, s_ref, p_ref, pnew_ref, linv_ref, acc_ref, kbuf, vbuf, ksem, vsem,
                        *, pages, n_batch):
    ph = pl.program_id(1)
    g = pl.program_id(2)
    ds = q_ref.shape[1]
    page = kbuf.shape[3]
    n_pages = s_ref.shape[0]
    ng = n_pages // pages
    rows = N_HEADS * ds
    pages_per_block = MOBA_BLOCK // page
    n_blocks = n_pages // pages_per_block
    u = pl.program_id(0) * ng + g

    def page_index_of(u, rr):
        return (u // ng) * n_pages + (u % ng) * pages + rr

    @pl.when(jnp.logical_and(ph == 0, g == 0))
    def _():
        qbd_ref[...] = _block_diag_q(q_ref[0])

    @pl.when(ph == 0)
    def _():
        slot = _ring_advance(pt_ref, page_index_of, u, n_batch * ng, k_hbm, kbuf, ksem)
        for rr in range(pages):
            s_ref[g * pages + rr] = jnp.dot(qbd_ref[...], kbuf[slot, rr].astype(BF16), preferred_element_type=F32)

    @pl.when(jnp.logical_and(ph == 0, g == ng - 1))
    def _():
        lane = lax.broadcasted_iota(I32, (rows, LANES), 1)
        gate = jnp.full((rows, LANES), -jnp.inf, F32)
        for b in range(n_blocks):
            tot = s_ref[b * pages_per_block]
            for u in range(1, pages_per_block):
                tot = tot + s_ref[b * pages_per_block + u]
            gate = jnp.where(lane == b, jnp.sum(tot, axis=1, keepdims=True) * (1.0 / MOBA_BLOCK), gate)
        chosen = []
        for _ in range(min(MOBA_TOPK, n_blocks)):
            mx = jnp.max(gate, axis=1, keepdims=True)
            idx = jnp.min(jnp.where(gate == mx, lane.astype(F32), float(LANES)), axis=1, keepdims=True)
            chosen.append(idx)
            gate = jnp.where(lane.astype(F32) == idx, -jnp.inf, gate)
        lane_n = lax.broadcasted_iota(I32, (N_HEADS, ds, page), 2)
        qidx_n = lax.broadcasted_iota(I32, (N_HEADS, ds, page), 1)
        s_new = lax.dot_general(qbd_ref[...], _pad_rows(kn_ref[0], page).astype(BF16), NT,
                                preferred_element_type=F32) + bias_ref[1]
        s_new = jnp.where((lane_n <= qidx_n).reshape(rows, page), s_new, NEG)
        m_tile = s_new
        for pg in range(n_pages):
            b = float(pg // pages_per_block)
            picked = chosen[0] == b
            for idx in chosen[1:]:
                picked = jnp.logical_or(picked, idx == b)
            bias = bias_ref[0] if pg == n_pages - 1 else bias_ref[2]
            s = jnp.where(picked, s_ref[pg] + bias, NEG)
            s_ref[pg] = s
            m_tile = jnp.maximum(m_tile, s)
        m = jnp.max(m_tile, axis=1, keepdims=True)
        p_new = jnp.exp(s_new - m)
        l_tile = p_new
        pnew_ref[...] = p_new.astype(BF16)
        for pg in range(n_pages):
            p = jnp.exp(s_ref[pg] - m)
            l_tile = l_tile + p
            p_ref[pg] = p.astype(BF16)
        linv_ref[...] = 1.0 / jnp.sum(l_tile, axis=1, keepdims=True)

    @pl.when(jnp.logical_and(ph == 1, g == 0))
    def _():
        acc_ref[...] = jnp.dot(pnew_ref[...], _pad_rows(vn_ref[0], page).astype(BF16),
                               preferred_element_type=F32)

    @pl.when(ph == 1)
    def _():
        slot = _ring_advance(pt_ref, page_index_of, u, n_batch * ng, v_hbm, vbuf, vsem)
        acc = acc_ref[...]
        for rr in range(pages):
            acc = acc + lax.dot_general(p_ref[g * pages + rr], vbuf[slot, rr].astype(BF16), NT,
                                        preferred_element_type=F32)
        acc_ref[...] = acc

    @pl.when(jnp.logical_and(ph == 1, g == ng - 1))
    def _():
        o_ref[0] = _head_diag_out(acc_ref[...] * linv_ref[...], ds)


def _moba_sample(page_table, q3, kn3, vn3, bias3, cache_k, cache_v):
    db, n_pages = page_table.shape
    ds = q3.shape[1]
    pages = PAGES_PER_STEP
    page = cache_k.shape[2]
    assert n_pages % pages == 0 and MOBA_BLOCK % page == 0 and (n_pages * page) % MOBA_BLOCK == 0
    assert T5_MAX_DIST <= page and ds <= page
    ng = n_pages // pages
    rows = N_HEADS * ds
    bmap = lambda b, ph, g, pt: (b, 0, 0)
    in_specs = [pl.BlockSpec((1, ds, GROUP_W), bmap), pl.BlockSpec((1, ds, GROUP_W), bmap),
                pl.BlockSpec((1, ds, GROUP_W), bmap),
                pl.BlockSpec(bias3.shape, lambda b, ph, g, pt: (0, 0, 0)),
                pl.BlockSpec(memory_space=pl.ANY), pl.BlockSpec(memory_space=pl.ANY)]
    k_ring, k_sem = _page_ring_scratch(pages, page)
    v_ring, v_sem = _page_ring_scratch(pages, page)
    return pl.pallas_call(
        functools.partial(_moba_sample_kernel, pages=pages, n_batch=db),
        grid_spec=pltpu.PrefetchScalarGridSpec(
            num_scalar_prefetch=1,
            grid=(db, 2, ng),
            in_specs=in_specs,
            out_specs=pl.BlockSpec((1, ds, GROUP_W), bmap),
            scratch_shapes=[pltpu.VMEM((rows, GROUP_W), BF16),
                            pltpu.VMEM((n_pages, rows, page), F32),
                            pltpu.VMEM((n_pages, rows, page), BF16),
                            pltpu.VMEM((rows, page), BF16),
                            pltpu.VMEM((rows, 1), F32),
                            pltpu.VMEM((rows, GROUP_W), F32),
                            k_ring, v_ring, k_sem, v_sem]),
        out_shape=jax.ShapeDtypeStruct((db, ds, GROUP_W), F32),
        compiler_params=_cparams("arbitrary", "arbitrary", "arbitrary"),
        name="moba_sample",
    )(page_table.reshape(-1), q3, kn3, vn3, bias3, cache_k, cache_v)


def _outproj_kernel(of_ref, om_ref, x_ref, gate_ref, shift_ref, scale_ref, gf_ref, gm_ref, wo_ref, gffn_ref,
                    wr_ref, br_ref, cnt0_ref, x1_ref, h2_ref, ri_ref, rw_ref, cnt_ref, carry_ref):
    t = pl.program_id(0)

    @pl.when(t == 0)
    def _():
        carry_ref[...] = cnt0_ref[...]

    tm = x_ref.shape[0]
    nf = (_rms(of_ref[...]) * gf_ref[...]).astype(BF16)
    nm = (_rms(om_ref[...]) * gm_ref[...]).astype(BF16)
    o = (jnp.dot(nf, wo_ref[0:GROUP_W, :], preferred_element_type=F32)
         + jnp.dot(nm, wo_ref[GROUP_W:2 * GROUP_W, :], preferred_element_type=F32))
    x1 = x_ref[...] + gate_ref[...] * o
    x1_ref[...] = x1
    h2 = _rms(x1) * gffn_ref[...]
    h2 = h2 * (1.0 + scale_ref[...]) + shift_ref[...]
    hi = h2.astype(BF16)
    half = h2.shape[1] // 2
    hi32 = pltpu.bitcast(hi.astype(F32), jnp.uint32)
    h2_ref[...] = jnp.bitwise_or(jnp.right_shift(hi32[:, :half], jnp.uint32(16)),
                                 jnp.bitwise_and(hi32[:, half:], jnp.uint32(0xFFFF0000)))

    lo = (h2 - hi.astype(F32)).astype(BF16)
    wr = wr_ref[...]
    whi = wr.astype(BF16)
    wlo = (wr - whi.astype(F32)).astype(BF16)
    lg = (jnp.dot(hi, whi, preferred_element_type=F32) + jnp.dot(lo, whi, preferred_element_type=F32)
          + jnp.dot(hi, wlo, preferred_element_type=F32)) + br_ref[...]
    lane = lax.broadcasted_iota(I32, (tm, LANES), 1)
    lane_f = lane.astype(F32)
    ninf = -jnp.inf
    is_g = jnp.logical_and(lane >= N_EXPERTS, lane < N_EXPERTS + N_GROUPS)
    glog = jnp.where(is_g, lg, ninf)
    gmax = jnp.max(glog, axis=1, keepdims=True)
    gidx = jnp.min(jnp.where(glog == gmax, lane_f, 2.0 * LANES), axis=1, keepdims=True).astype(I32) - N_EXPERTS
    g_w = 1.0 / jnp.sum(jnp.exp(glog - gmax), axis=1, keepdims=True)
    in_grp = jnp.logical_and(lane >= gidx * EXPERTS_PER_GROUP, lane < (gidx + 1) * EXPERTS_PER_GROUP)
    elog = jnp.where(in_grp, lg, ninf)
    e1 = jnp.max(elog, axis=1, keepdims=True)
    i1 = jnp.min(jnp.where(elog == e1, lane_f, 2.0 * LANES), axis=1, keepdims=True).astype(I32)
    z = jnp.sum(jnp.exp(elog - e1), axis=1, keepdims=True)
    elog2 = jnp.where(lane == i1, ninf, elog)
    e2 = jnp.max(elog2, axis=1, keepdims=True)
    i2 = jnp.min(jnp.where(elog2 == e2, lane_f, 2.0 * LANES), axis=1, keepdims=True).astype(I32)
    p1 = 1.0 / z
    p2 = jnp.exp(e2 - e1) / z
    w1 = g_w * (p1 / (p1 + p2))
    w2 = g_w * (p2 / (p1 + p2))

    a = jnp.where(jnp.logical_or(lane == i1, lane == i2), 1.0, 0.0)
    r = lax.broadcasted_iota(I32, (tm, tm), 0)
    c = lax.broadcasted_iota(I32, (tm, tm), 1)
    before = jnp.where(c < r, 1.0, 0.0).astype(BF16)
    pos = jnp.dot(before, a.astype(BF16), preferred_element_type=F32) + carry_ref[...]
    r1 = jnp.sum(jnp.where(lane == i1, pos, 0.0), axis=1, keepdims=True)
    r2 = jnp.sum(jnp.where(lane == i2, pos, 0.0), axis=1, keepdims=True)
    carry_ref[...] = carry_ref[...] + jnp.sum(a, axis=0, keepdims=True)
    cnt_ref[...] = carry_ref[...]

    ri = jnp.where(lane == 0, i1, 0) + jnp.where(lane == 1, i2, 0)
    ri = ri + jnp.where(lane == 2, r1.astype(I32), 0) + jnp.where(lane == 3, r2.astype(I32), 0)
    ri_ref[...] = ri
    rw_ref[...] = jnp.where(lane == 0, w1, 0.0) + jnp.where(lane == 1, w2, 0.0)


def _outproj(of, om, x2, gate3, shift3, scale3, gf, gm, wo, gffn, wr, br, cnt0, *, rows_per_mod, name):
    n, d = x2.shape
    tm = ROW_TILE
    nt = n // tm
    mod_rows = gate3.shape[1]
    mod_map = lambda t: ((t * tm) // rows_per_mod, 0, 0)
    row_map = lambda t: (t, 0)
    const2 = lambda t: (0, 0)
    return pl.pallas_call(
        _outproj_kernel,
        grid=(nt,),
        in_specs=[pl.BlockSpec((tm, GROUP_W), row_map), pl.BlockSpec((tm, GROUP_W), row_map),
                  pl.BlockSpec((tm, d), row_map),
                  pl.BlockSpec((None, mod_rows, d), mod_map), pl.BlockSpec((None, mod_rows, d), mod_map),
                  pl.BlockSpec((None, mod_rows, d), mod_map),
                  pl.BlockSpec((1, GROUP_W), const2), pl.BlockSpec((1, GROUP_W), const2),
                  pl.BlockSpec((d, d), const2), pl.BlockSpec((1, d), const2),
                  pl.BlockSpec((d, LANES), const2), pl.BlockSpec((1, LANES), const2),
                  pl.BlockSpec((1, LANES), const2)],
        out_specs=[pl.BlockSpec((tm, d), row_map), pl.BlockSpec((tm, d // 2), row_map),
                   pl.BlockSpec((tm, LANES), row_map), pl.BlockSpec((tm, LANES), row_map),
                   pl.BlockSpec((1, LANES), const2)],
        out_shape=[jax.ShapeDtypeStruct((n, d), F32), jax.ShapeDtypeStruct((n, d // 2), jnp.uint32),
                   jax.ShapeDtypeStruct((n, LANES), I32), jax.ShapeDtypeStruct((n, LANES), F32),
                   jax.ShapeDtypeStruct((1, LANES), F32)],
        scratch_shapes=[pltpu.VMEM((1, LANES), F32)],
        compiler_params=_cparams("arbitrary"),
        name=name,
    )(of, om, x2, gate3, shift3, scale3, gf, gm, wo, gffn, wr, br, cnt0)


def _scatter_kernel(dest_ref, h_ref, rows_in_ref, rows_ref, sem):
    del rows_in_ref
    tm = h_ref.shape[0]

    def row_copy(r, k):
        return pltpu.make_async_copy(h_ref.at[pl.ds(r, 1)], rows_ref.at[pl.ds(dest_ref[k, r], 1)], sem)

    for r in range(tm):
        row_copy(r, 0).start(priority=0)
        row_copy(r, 1).start(priority=1)
    for _ in range(2):
        pltpu.make_async_copy(h_ref, rows_ref.at[pl.ds(0, tm)], sem).wait()


def _scatter_rows(dest2, h2, rows):
    n, d = h2.shape
    tm = ROW_TILE
    return pl.pallas_call(
        _scatter_kernel,
        grid=(n // tm,),
        in_specs=[pl.BlockSpec((2, tm), lambda t: (0, t), memory_space=pltpu.SMEM),
                  pl.BlockSpec((tm, d), lambda t: (t, 0)),
                  pl.BlockSpec(memory_space=pl.ANY)],
        out_specs=pl.BlockSpec(memory_space=pl.ANY),
        out_shape=jax.ShapeDtypeStruct(rows.shape, rows.dtype),
        scratch_shapes=[pltpu.SemaphoreType.DMA(())],
        input_output_aliases={2: 0},
        compiler_params=_cparams("arbitrary"),
        name="moe_scatter",
    )(dest2, h2, rows)


def _moe_kernel(be_ref, nu_ref, rows_ref, w1_ref, w3_ref, w2_ref, y_ref, w1b_ref, w3b_ref, w2b_ref,
                rbuf_ref, rsem):
    blk = pl.program_id(0)
    n_used = nu_ref[0]

    def fetch(u):
        start = pl.multiple_of(u * MOE_BLOCK, MOE_BLOCK)
        return pltpu.make_async_copy(rows_ref.at[pl.ds(start, MOE_BLOCK)], rbuf_ref.at[u % ROW_RING],
                                     rsem.at[u % ROW_RING])

    @pl.when(blk == 0)
    def _():
        for ahead in range(ROW_RING - 1):
            @pl.when(ahead < n_used)
            def _():
                fetch(ahead).start()

    @pl.when(blk + (ROW_RING - 1) < n_used)
    def _():
        fetch(blk + (ROW_RING - 1)).start()

    @pl.when(blk < n_used)
    def _():
        prev = be_ref[jnp.maximum(blk - 1, 0)]

        @pl.when(jnp.logical_or(blk == 0, be_ref[blk] != prev))
        def _():
            w1b_ref[...] = w1_ref[0].astype(BF16)
            w3b_ref[...] = w3_ref[0].astype(BF16)
            w2b_ref[...] = w2_ref[0].astype(BF16)

        fetch(blk).wait()
        packed = rbuf_ref[blk % ROW_RING]
        x_lo = pltpu.bitcast(jnp.left_shift(packed, jnp.uint32(16)), F32)
        x_hi = pltpu.bitcast(jnp.bitwise_and(packed, jnp.uint32(0xFFFF0000)), F32)
        x = jnp.concatenate([x_lo, x_hi], axis=1).astype(BF16)
        a = jnp.dot(x, w1b_ref[...], preferred_element_type=F32)
        b = jnp.dot(x, w3b_ref[...], preferred_element_type=F32)
        hm = (a * jax.nn.sigmoid(a)) * b
        y_ref[...] = jnp.dot(hm.astype(BF16), w2b_ref[...], preferred_element_type=F32)

    @pl.when(blk >= nu_ref[0])
    def _():
        y_ref[...] = jnp.zeros_like(y_ref)


def _moe_experts(block_e, n_used, rows, w1, w3, w2):
    p = rows.shape[0]
    d, de = w1.shape[1], w1.shape[2]
    assert rows.shape[1] * 2 == d
    nblk = p // MOE_BLOCK
    return pl.pallas_call(
        _moe_kernel,
        grid_spec=pltpu.PrefetchScalarGridSpec(
            num_scalar_prefetch=2,
            grid=(nblk,),
            in_specs=[pl.BlockSpec(memory_space=pl.ANY),
                      pl.BlockSpec((1, d, de), lambda b, be, nu: (be[b], 0, 0)),
                      pl.BlockSpec((1, d, de), lambda b, be, nu: (be[b], 0, 0)),
                      pl.BlockSpec((1, de, d), lambda b, be, nu: (be[b], 0, 0))],
            out_specs=pl.BlockSpec((MOE_BLOCK, d), lambda b, be, nu: (b, 0)),
            scratch_shapes=[pltpu.VMEM((d, de), BF16), pltpu.VMEM((d, de), BF16), pltpu.VMEM((de, d), BF16),
                            pltpu.VMEM((ROW_RING, MOE_BLOCK, d // 2), jnp.uint32),
                            pltpu.SemaphoreType.DMA((ROW_RING,))]),
        out_shape=jax.ShapeDtypeStruct((p, d), F32),
        compiler_params=_cparams("arbitrary"),
        name="moe_experts",
    )(block_e, n_used, rows, w1, w3, w2)


def _combine_kernel(dest_ref, x1_ref, gate_ref, rw_ref, gfin_ref, y_ref, o_ref, ybuf_ref, sem):
    tm = x1_ref.shape[0]

    def row_copy(r, k):
        return pltpu.make_async_copy(y_ref.at[pl.ds(dest_ref[k, r], 1)], ybuf_ref.at[k, pl.ds(r, 1)], sem)

    for r in range(tm):
        row_copy(r, 0).start(priority=0)
        row_copy(r, 1).start(priority=1)
    for k in range(2):
        pltpu.make_async_copy(y_ref.at[pl.ds(0, tm)], ybuf_ref.at[k], sem).wait()
    rw = rw_ref[...]
    moe = rw[:, 0:1] * ybuf_ref[0] + rw[:, 1:2] * ybuf_ref[1]
    xo = x1_ref[...] + gate_ref[...] * moe
    o_ref[...] = _rms(xo) * gfin_ref[...]


def _combine(dest2, x1, gate3, rw, gfin, y, *, rows_per_mod, name):
    n, d = x1.shape
    tm = ROW_TILE
    mod_rows = gate3.shape[1]
    return pl.pallas_call(
        _combine_kernel,
        grid=(n // tm,),
        in_specs=[pl.BlockSpec((2, tm), lambda t: (0, t), memory_space=pltpu.SMEM),
                  pl.BlockSpec((tm, d), lambda t: (t, 0)),
                  pl.BlockSpec((None, mod_rows, d), lambda t: ((t * tm) // rows_per_mod, 0, 0)),
                  pl.BlockSpec((tm, LANES), lambda t: (t, 0)),
                  pl.BlockSpec((1, d), lambda t: (0, 0)),
                  pl.BlockSpec(memory_space=pl.ANY)],
        out_specs=pl.BlockSpec((tm, d), lambda t: (t, 0)),
        out_shape=jax.ShapeDtypeStruct((n, d), F32),
        scratch_shapes=[pltpu.VMEM((2, tm, d), F32), pltpu.SemaphoreType.DMA(())],
        compiler_params=_cparams("arbitrary"),
        name=name,
    )(dest2, x1, gate3, rw, gfin, y)


def kernel(x_prompt, x_sample, cache_fox_k, cache_fox_v, cache_fox_logf, cache_moba_k, cache_moba_v, page_table, c_prompt, c_sample, w_ada, b_ada, g_attn, w_in, b_forget, g_out_fox, g_out_moba, t5_bias, w_out, g_ffn, w_router_group, b_router_group, w_router_expert, b_router_expert, w1, w3, w2, g_final):
    bsz, seq, d = x_prompt.shape
    db, ds, _ = x_sample.shape
    depth = w_ada.shape[0]
    n_phys, page = cache_fox_k.shape[1], cache_fox_k.shape[2]
    assert depth == 1, "one trunk layer"
    assert seq % ROW_TILE == 0 and (db * ds) % ROW_TILE == 0 and ROW_TILE % ds == 0
    assert N_EXPERTS + N_GROUPS <= LANES and N_HEADS <= LANES
    l = 0
    n_p, n_s = bsz * seq, db * ds
    xp2 = x_prompt.reshape(n_p, d)
    xs2 = x_sample.reshape(n_s, d)

    mod = _ada(jnp.concatenate([c_prompt, c_sample], axis=0), w_ada[l], b_ada[l])
    mod_p = [mod[:bsz, i * d:(i + 1) * d].reshape(bsz, 1, d) for i in range(6)]
    mod_s = [jnp.repeat(mod[bsz:, i * d:(i + 1) * d], ds, axis=0).reshape(n_s // ROW_TILE, ROW_TILE, d)
             for i in range(6)]

    w = GROUP_W
    wl = w_in[l]
    wcat = jnp.concatenate([wl[:, :3 * w], wl[:, 3 * w + N_HEADS:],
                            wl[:, 3 * w:3 * w + N_HEADS], jnp.zeros((d, LANES - N_HEADS), F32)],
                           axis=1).astype(BF16)
    bf_pad = jnp.pad(b_forget[l], (0, LANES - N_HEADS)).reshape(1, LANES)
    g_attn2 = g_attn[l].reshape(1, d)

    fqt, fk_t, fv_t, lf_t, mqt, mk_t, mv_t, fk_aug, fqt_aug, kmean, fkb, fvtb, mkb, mvtb = _inproj(
        xp2, mod_p[0], mod_p[1], g_attn2, wcat, bf_pad, rows_per_mod=seq, prompt_extras=True, seq_len=seq)
    sfq, sfk, sfv, slf, smq, smk, smv = _inproj(
        xs2, mod_s[0], mod_s[1], g_attn2, wcat, bf_pad, rows_per_mod=ROW_TILE, prompt_extras=False, seq_len=ds)

    o_fox_p = _fox_prompt(fqt, fkb, fvtb, fk_aug, fqt_aug, bsz, seq)
    bias_p = _t5_tiles(t5_bias, (0, MOBA_BLOCK, 2 * MOBA_BLOCK), MOBA_BLOCK, MOBA_BLOCK, sign=-1, scale=LOG2E)
    o_moba_p = _moba_prompt(mqt, mkb, mvtb, kmean.reshape(bsz, seq // MOBA_BLOCK, w), bias_p, bsz, seq)

    to3 = lambda a: a.reshape(db, ds, w)
    page_t = lambda c: c[l].transpose(0, 2, 3, 1).reshape(n_phys, w, page)
    cfk, cfv, cmk, cmv = page_t(cache_fox_k), page_t(cache_fox_v), page_t(cache_moba_k), page_t(cache_moba_v)
    clf_t = cache_fox_logf[l].transpose(0, 2, 1)
    slf_t = jnp.pad(slf.reshape(db, ds, N_HEADS).transpose(0, 2, 1), ((0, 0), (0, 0), (0, page - ds)))
    o_fox_s = _fox_sample(page_table, to3(sfq), to3(sfk), to3(sfv), slf_t, cfk, cfv, clf_t)
    bias_s = _t5_tiles(t5_bias, (page, 0, 2 * T5_MAX_DIST + page), ds, page, sign=1)
    bias_s = bias_s.transpose(1, 0, 2, 3).reshape(3, N_HEADS * ds, page)
    o_moba_s = _moba_sample(page_table, to3(smq), to3(smk), to3(smv), bias_s, cmk, cmv)

    wr = jnp.concatenate([w_router_expert[l], w_router_group[l],
                          jnp.zeros((d, LANES - N_EXPERTS - N_GROUPS), F32)], axis=1)
    br = jnp.concatenate([b_router_expert[l], b_router_group[l],
                          jnp.zeros((LANES - N_EXPERTS - N_GROUPS,), F32)]).reshape(1, LANES)
    wo = w_out[l].astype(BF16)
    gf, gm, gffn = g_out_fox[l].reshape(1, w), g_out_moba[l].reshape(1, w), g_ffn[l].reshape(1, d)
    x1_p, h2_p, ri_p, rw_p, cnt_p = _outproj(
        o_fox_p, o_moba_p, xp2, mod_p[2], mod_p[3], mod_p[4], gf, gm, wo, gffn, wr, br,
        jnp.zeros((1, LANES), F32), rows_per_mod=seq, name="outproj_prompt")
    x1_s, h2_s, ri_s, rw_s, cnt = _outproj(
        o_fox_s.reshape(n_s, w), o_moba_s.reshape(n_s, w), xs2, mod_s[2], mod_s[3], mod_s[4], gf, gm, wo, gffn,
        wr, br, cnt_p, rows_per_mod=ROW_TILE, name="outproj_sample")

    counts = cnt[0, :N_EXPERTS].astype(I32)
    padded = (counts + MOE_BLOCK - 1) // MOE_BLOCK * MOE_BLOCK
    pend = jnp.cumsum(padded)
    pstart = pend - padded
    n_asg = 2 * (n_p + n_s)
    n_blocks = -(-(n_asg + N_EXPERTS * (MOE_BLOCK - 1)) // MOE_BLOCK)
    blk_start = jnp.arange(n_blocks, dtype=I32) * MOE_BLOCK
    block_e = jnp.clip(jnp.sum((pend[None, :] <= blk_start[:, None]).astype(I32), axis=1), 0, N_EXPERTS - 1)
    n_used = (pend[-1:] // MOE_BLOCK).astype(I32)

    def dest_of(ri):
        return (pstart[ri[:, 0:2]] + ri[:, 2:4]).T.astype(I32)

    dest_p, dest_s = dest_of(ri_p), dest_of(ri_s)
    rows = jnp.zeros((n_blocks * MOE_BLOCK, d // 2), jnp.uint32)
    rows = _scatter_rows(dest_p, h2_p, rows)
    rows = _scatter_rows(dest_s, h2_s, rows)
    y = _moe_experts(block_e, n_used, rows, w1[l], w3[l], w2[l])
    gfin = g_final.reshape(1, d)
    y_prompt = _combine(dest_p, x1_p, mod_p[5], rw_p, gfin, y, rows_per_mod=seq, name="combine_prompt")
    y_sample = _combine(dest_s, x1_s, mod_s[5], rw_s, gfin, y, rows_per_mod=ROW_TILE, name="combine_sample")

    hp = lambda a: a.reshape(depth, bsz, N_HEADS, HEAD_DIM, seq).transpose(0, 1, 4, 2, 3)
    hs = lambda a: a.reshape(depth, db, ds, N_HEADS, HEAD_DIM)
    return (y_prompt.reshape(bsz, seq, d), y_sample.reshape(db, ds, d),
            hp(fk_t), hp(fv_t), lf_t.reshape(depth, bsz, N_HEADS, seq).transpose(0, 1, 3, 2), hp(mk_t), hp(mv_t),
            hs(sfk), hs(sfv), slf.reshape(depth, db, ds, N_HEADS), hs(smk), hs(smv))
```

```python
import functools
import math

import jax
import jax.numpy as jnp
from jax import lax
from jax.experimental import pallas as pl
from jax.experimental.pallas import tpu as pltpu

F32 = jnp.float32
BF16 = jnp.bfloat16
I32 = jnp.int32

HEAD_DIM = 64
N_HEADS = 8
GROUP_W = N_HEADS * HEAD_DIM
LANES = 128
HEADS_PER_LANE_TILE = LANES // HEAD_DIM
N_PAIRS = N_HEADS // HEADS_PER_LANE_TILE
SM_SCALE = HEAD_DIM ** -0.5
LOG2E = math.log2(math.e)
MOBA_BLOCK = 256
MOBA_TOPK = 3
T5_BUCKETS = 32
T5_MAX_DIST = 128
N_GROUPS = 4
EXPERTS_PER_GROUP = 8
N_EXPERTS = N_GROUPS * EXPERTS_PER_GROUP
RMS_EPS = 1e-6
ROW_TILE = 256
ATTN_TILE = 256
K_CHUNK = 64
AUG = LANES // N_HEADS
SUM_ROWS = 16
MOE_BLOCK = 256
PAGES_PER_STEP = 16
PAGE_RING = 3
ROW_RING = 3
NEG = -1e30
VMEM_LIMIT = 48 * 1024 * 1024
HIGHEST = lax.Precision.HIGHEST
NT = (((1,), (1,)), ((), ()))


def _cparams(*sem):
    return pltpu.CompilerParams(dimension_semantics=sem, vmem_limit_bytes=VMEM_LIMIT)


def _rms(x):
    return x * lax.rsqrt(jnp.mean(x * x, axis=-1, keepdims=True) + RMS_EPS)


def _spare_half(h):
    return 0 if h % HEADS_PER_LANE_TILE else HEAD_DIM


def _split3_bf16(x):
    hi = x.astype(BF16)
    r1 = x - hi.astype(F32)
    mid = r1.astype(BF16)
    lo = (r1 - mid.astype(F32)).astype(BF16)
    return hi, mid, lo


def _ada_kernel(c_ref, w_ref, b_ref, o_ref):
    c = c_ref[...]
    a = c * jax.nn.sigmoid(c)
    o_ref[...] = jnp.dot(a, w_ref[...], preferred_element_type=F32, precision=HIGHEST) + b_ref[...]


def _ada(c, w, b):
    n, d = c.shape
    e = w.shape[1]
    tn = 1024
    return pl.pallas_call(
        _ada_kernel,
        grid=(e // tn,),
        in_specs=[pl.BlockSpec((n, d), lambda j: (0, 0)),
                  pl.BlockSpec((d, tn), lambda j: (0, j)),
                  pl.BlockSpec((1, tn), lambda j: (0, j))],
        out_specs=pl.BlockSpec((n, tn), lambda j: (0, j)),
        out_shape=jax.ShapeDtypeStruct((n, e), F32),
        compiler_params=_cparams("arbitrary"),
        name="ada",
    )(c, w, b.reshape(1, e))


def _inproj_kernel(x_ref, shift_ref, scale_ref, g_ref, w_ref, bf_ref,
                   fq_ref, fk_ref, fv_ref, lf_ref, mq_ref, mk_ref, mv_ref, *rest,
                   prompt_extras, tiles_per_seq):
    x = x_ref[...]
    h = _rms(x) * g_ref[...]
    h = h * (1.0 + scale_ref[...]) + shift_ref[...]
    z = jnp.dot(h.astype(BF16), w_ref[...], preferred_element_type=F32)
    w = GROUP_W
    q_scale = SM_SCALE * LOG2E if prompt_extras else SM_SCALE
    fq = z[:, 0:w] * q_scale
    mq = z[:, 3 * w:4 * w] * q_scale
    if prompt_extras:
        fq_ref[0] = fq.T.astype(BF16)
        mq_ref[0] = mq.T.astype(BF16)
    else:
        fq_ref[...] = fq.astype(BF16)
        mq_ref[...] = mq.astype(BF16)
    fk, fv = z[:, w:2 * w], z[:, 2 * w:3 * w]
    mk, mv = z[:, 4 * w:5 * w], z[:, 5 * w:6 * w]
    fg = z[:, 6 * w:6 * w + LANES] + bf_ref[...]
    lf = jnp.minimum(fg, 0.0) - jnp.log1p(jnp.exp(-jnp.abs(fg)))
    if not prompt_extras:
        fk_ref[...] = fk
        fv_ref[...] = fv
        mk_ref[...] = mk
        mv_ref[...] = mv
        lf_ref[...] = lf[:, :N_HEADS]
        return
    fkaug_ref, fqtaug_ref, kmean_ref, fkb_ref, fvtb_ref, mkb_ref, mvtb_ref, carry_ref = rest
    tm = x.shape[0]
    fk_ref[0] = fk.T
    fv_t = fv.T
    fv_ref[0] = fv_t
    mk_ref[0] = mk.T
    mv_t = mv.T
    mv_ref[0] = mv_t
    fkb_ref[...] = fk.astype(BF16)
    fvtb_ref[0] = fv_t.astype(BF16)
    mkb_ref[...] = mk.astype(BF16)
    mvtb_ref[0] = mv_t.astype(BF16)
    lf_ref[0] = lf.T[:N_HEADS, :]
    kmean_ref[0] = jnp.mean(mk, axis=0, keepdims=True)

    @pl.when(pl.program_id(0) % tiles_per_seq == 0)
    def _():
        carry_ref[...] = jnp.zeros_like(carry_ref)

    r = lax.broadcasted_iota(I32, (tm, tm), 0)
    c = lax.broadcasted_iota(I32, (tm, tm), 1)
    tri = jnp.where(c <= r, 1.0, 0.0).astype(BF16)
    hi, mid, lo = _split3_bf16(lf)
    cs = (jnp.dot(tri, hi, preferred_element_type=F32)
          + jnp.dot(tri, mid, preferred_element_type=F32)
          + jnp.dot(tri, lo, preferred_element_type=F32)) + carry_ref[...]
    carry_ref[...] = cs[tm - 1:tm, :]
    hi, mid, lo = _split3_bf16(cs * LOG2E)
    er = lax.broadcasted_iota(I32, (LANES, LANES), 0)
    ec = lax.broadcasted_iota(I32, (LANES, LANES), 1)

    def spread(x, off):
        e = jnp.where(jnp.logical_and(er < N_HEADS, ec == AUG * er + off), 1.0, 0.0).astype(BF16)
        return jnp.dot(x, e, preferred_element_type=F32)

    lane = lax.broadcasted_iota(I32, (tm, LANES), 1)
    slot = lane % AUG
    fq_aug = (spread(hi, 0) + spread(mid, 1) + spread(lo, 2)
              + jnp.where(jnp.logical_and(slot >= 3, slot < 6), 1.0, 0.0))
    fk_aug = jnp.where(slot < 3, 1.0, 0.0) - (spread(hi, 3) + spread(mid, 4) + spread(lo, 5))
    fq_aug_t = fq_aug.T
    k_tiles, q_tiles = [], []
    for hd in range(N_HEADS):
        base = _spare_half(hd)
        shift = (base - AUG * hd) % LANES
        moved = pltpu.roll(fk_aug, shift, 1) if shift else fk_aug
        k_tiles.append(jnp.where(jnp.logical_and(lane >= base, lane < base + AUG), moved, 0.0))
        q_rows = [fq_aug_t[AUG * hd:AUG * (hd + 1), :]]
        if base:
            q_rows.insert(0, jnp.zeros((base, tm), F32))
        if LANES - base - AUG:
            q_rows.append(jnp.zeros((LANES - base - AUG, tm), F32))
        q_tiles.append(jnp.concatenate(q_rows, axis=0))
    fkaug_ref[...] = jnp.concatenate(k_tiles, axis=1).astype(BF16)
    fqtaug_ref[0] = jnp.concatenate(q_tiles, axis=0).astype(BF16)


def _inproj(x2, shift3, scale3, g_attn, wcat, bf_pad, *, rows_per_mod, prompt_extras, seq_len):
    n, d = x2.shape
    tm = ROW_TILE
    assert n % tm == 0
    nt = n // tm
    mod_rows = shift3.shape[1]
    mod_map = lambda t: ((t * tm) // rows_per_mod, 0, 0)
    row_map = lambda t: (t, 0)
    const2 = lambda t: (0, 0)
    ecols = wcat.shape[1]
    row_bf16 = (jax.ShapeDtypeStruct((n, GROUP_W), BF16), pl.BlockSpec((tm, GROUP_W), row_map))
    scratch = []
    tiles_per_seq = 1
    if prompt_extras:
        assert tm == MOBA_BLOCK and seq_len % tm == 0
        tiles_per_seq = seq_len // tm
        nb = n // seq_len
        t_map = lambda t: (t // tiles_per_seq, 0, t % tiles_per_seq)
        kv = (jax.ShapeDtypeStruct((nb, GROUP_W, seq_len), F32), pl.BlockSpec((1, GROUP_W, tm), t_map))
        lfo = (jax.ShapeDtypeStruct((nb, N_HEADS, seq_len), F32), pl.BlockSpec((1, N_HEADS, tm), t_map))
        kvt_bf16 = (jax.ShapeDtypeStruct((nb, GROUP_W, seq_len), BF16), pl.BlockSpec((1, GROUP_W, tm), t_map))
        outs = [kvt_bf16, kv, kv, lfo, kvt_bf16, kv, kv,
                (jax.ShapeDtypeStruct((n, N_HEADS * LANES), BF16), pl.BlockSpec((tm, N_HEADS * LANES), row_map)),
                (jax.ShapeDtypeStruct((nb, N_HEADS * LANES, seq_len), BF16),
                 pl.BlockSpec((1, N_HEADS * LANES, tm), t_map)),
                (jax.ShapeDtypeStruct((nt, 1, GROUP_W), F32), pl.BlockSpec((1, 1, GROUP_W), lambda t: (t, 0, 0)))]
        outs += [row_bf16, kvt_bf16, row_bf16, kvt_bf16]
        scratch = [pltpu.VMEM((1, LANES), F32)]
    else:
        kv = (jax.ShapeDtypeStruct((n, GROUP_W), F32), pl.BlockSpec((tm, GROUP_W), row_map))
        lfo = (jax.ShapeDtypeStruct((n, N_HEADS), F32), pl.BlockSpec((tm, N_HEADS), row_map))
        outs = [row_bf16, kv, kv, lfo, row_bf16, kv, kv]
    return pl.pallas_call(
        functools.partial(_inproj_kernel, prompt_extras=prompt_extras, tiles_per_seq=tiles_per_seq),
        grid=(nt,),
        in_specs=[pl.BlockSpec((tm, d), row_map),
                  pl.BlockSpec((None, mod_rows, d), mod_map),
                  pl.BlockSpec((None, mod_rows, d), mod_map),
                  pl.BlockSpec((1, d), const2),
                  pl.BlockSpec((d, ecols), const2),
                  pl.BlockSpec((1, LANES), const2)],
        out_specs=[o[1] for o in outs],
        out_shape=[o[0] for o in outs],
        scratch_shapes=scratch,
        compiler_params=_cparams("arbitrary"),
        name="inproj_prompt" if prompt_extras else "inproj_sample",
    )(x2, shift3, scale3, g_attn, wcat, bf_pad)


def _t5_kernel(t5_ref, o_ref, *, offs, sign, scale):
    h = pl.program_id(0)
    rows, cols = o_ref.shape[2], o_ref.shape[3]
    r = lax.broadcasted_iota(I32, (rows, cols), 0)
    c = lax.broadcasted_iota(I32, (rows, cols), 1)
    max_exact = T5_BUCKETS // 2
    for k, off in enumerate(offs):
        rel = jnp.maximum(off + sign * (r - c), 0)
        relf = jnp.maximum(rel, 1).astype(F32)
        large = max_exact + (jnp.log(relf / max_exact) / math.log(T5_MAX_DIST / max_exact)
                             * (T5_BUCKETS - max_exact)).astype(I32)
        large = jnp.minimum(large, T5_BUCKETS - 1)
        bucket = jnp.where(rel < max_exact, rel, large)
        acc = jnp.zeros((rows, cols), F32)
        for b in range(T5_BUCKETS):
            acc = jnp.where(bucket == b, t5_ref[b, h], acc)
        o_ref[0, k] = acc if scale == 1.0 else acc * scale


def _t5_tiles(t5_bias, offs, rows, cols, sign, scale=1.0):
    nh = t5_bias.shape[1]
    return pl.pallas_call(
        functools.partial(_t5_kernel, offs=tuple(offs), sign=sign, scale=scale),
        grid=(nh,),
        in_specs=[pl.BlockSpec(memory_space=pltpu.SMEM)],
        out_specs=pl.BlockSpec((1, len(offs), rows, cols), lambda h: (h, 0, 0, 0)),
        out_shape=jax.ShapeDtypeStruct((nh, len(offs), rows, cols), F32),
        compiler_params=_cparams("arbitrary"),
        name="t5_tiles",
    )(t5_bias)


def _pair_masks(rows):
    lane = lax.broadcasted_iota(I32, (rows, LANES), 1)
    lo = lane < HEAD_DIM
    return lo, jnp.logical_not(lo)


def _masked_qt(qt_ref, h):
    p, e = divmod(h, HEADS_PER_LANE_TILE)
    qt = qt_ref[0, p * LANES:(p + 1) * LANES, :]
    row = lax.broadcasted_iota(I32, qt.shape, 0)
    keep = (row >= HEAD_DIM) if e else (row < HEAD_DIM)
    return jnp.where(keep, qt, jnp.zeros_like(qt))


def _pair_of(h):
    return slice((h // HEADS_PER_LANE_TILE) * LANES, (h // HEADS_PER_LANE_TILE + 1) * LANES)


def _flash_step(lhs_of, rhs_of, vt_ref, adjust, states, tk, tq):
    chunks = [slice(c * K_CHUNK, (c + 1) * K_CHUNK) for c in range(tk // K_CHUNK)]
    for h, (m_ref, _, _, s_scr, _, a_ref) in enumerate(states):
        rhs = rhs_of(h)
        m8 = jnp.full((8, tq), NEG, F32)
        for c, rows in enumerate(chunks):
            s = jnp.dot(lhs_of(h, c), rhs, preferred_element_type=F32)
            if adjust is not None:
                s = adjust(h, c, s)
            s_scr[rows, :] = s
            m8 = jnp.maximum(m8, jnp.max(s.reshape(K_CHUNK // 8, 8, tq), axis=0))
        m_prev = m_ref[...]
        m_new = jnp.maximum(m_prev, jnp.max(m8, axis=0, keepdims=True))
        a_ref[...] = jnp.exp2(m_prev - m_new)
        m_ref[...] = m_new
    for m_ref, _, _, s_scr, p_scr, _ in states:
        m_new = m_ref[...]
        for rows in chunks:
            p_scr[rows, :] = jnp.exp2(s_scr[rows, :] - m_new).astype(BF16)
    ones = jnp.ones((SUM_ROWS, tk), BF16)
    for h, (_, l_ref, acc_ref, _, p_scr, a_ref) in enumerate(states):
        feat = slice(h * HEAD_DIM, (h + 1) * HEAD_DIM)
        vt_sum = jnp.concatenate([vt_ref[0, feat, :], ones], axis=0)
        pv = jnp.dot(vt_sum, p_scr[...], preferred_element_type=F32)
        acc_ref[...] = a_ref[...] * acc_ref[...] + pv[:HEAD_DIM]
        l_ref[...] = a_ref[...] * l_ref[...] + pv[HEAD_DIM:HEAD_DIM + 1]


FLASH_BUFS_PER_HEAD = 6


def _flash_scratch(tq, tk):
    per_head = [pltpu.VMEM((1, tq), F32), pltpu.VMEM((1, tq), F32), pltpu.VMEM((HEAD_DIM, tq), F32),
                pltpu.VMEM((tk, tq), F32), pltpu.VMEM((tk, tq), BF16), pltpu.VMEM((1, tq), F32)]
    return per_head * N_HEADS


def _flash_states(scratch):
    n = FLASH_BUFS_PER_HEAD
    return [scratch[n * h:n * (h + 1)] for h in range(N_HEADS)]


def _flash_init(states):
    for m_ref, l_ref, acc_ref, _, _, _ in states:
        m_ref[...] = jnp.full_like(m_ref, NEG)
        l_ref[...] = jnp.zeros_like(l_ref)
        acc_ref[...] = jnp.zeros_like(acc_ref)


def _flash_finish(o_ref, states):
    o_t = jnp.concatenate([acc_ref[...] / l_ref[...] for _, l_ref, acc_ref, _, _, _ in states], axis=0)
    o_ref[...] = o_t.T


def _chunk_causal(c, tq):
    krow = c * K_CHUNK + lax.broadcasted_iota(I32, (K_CHUNK, tq), 0)
    qcol = lax.broadcasted_iota(I32, (K_CHUNK, tq), 1)
    return krow <= qcol


def _head_rows(x, h, rows_per_head):
    row = lax.broadcasted_iota(I32, x.shape, 0)
    keep = jnp.logical_and(row >= h * rows_per_head, row < (h + 1) * rows_per_head)
    return jnp.where(keep, x, jnp.zeros_like(x))


def _own_half(shape, h, axis):
    pos = lax.broadcasted_iota(I32, shape, axis)
    return (pos >= HEAD_DIM) if h % HEADS_PER_LANE_TILE else (pos < HEAD_DIM)


def _fox_prompt_kernel(qi_ref, kj_ref, q_ref, k_ref, vt_ref, fqt_ref, fk_ref, o_ref, *scratch):
    states = _flash_states(scratch)
    t = pl.program_id(1)
    i = qi_ref[t]
    j = kj_ref[t]
    tq, tk = q_ref.shape[2], k_ref.shape[0]

    @pl.when(j == 0)
    def _():
        _flash_init(states)

    def lhs_of(h, c):
        rows = slice(c * K_CHUNK, (c + 1) * K_CHUNK)
        k = k_ref[rows, _pair_of(h)]
        return jnp.where(_own_half(k.shape, h, 1), k, fk_ref[rows, h * LANES:(h + 1) * LANES])

    def rhs_of(h):
        qt = q_ref[0, _pair_of(h), :]
        return jnp.where(_own_half(qt.shape, h, 0), qt, fqt_ref[0, h * LANES:(h + 1) * LANES, :])

    def step(diagonal):
        adjust = (lambda h, c, s: jnp.where(_chunk_causal(c, tq), s, NEG)) if diagonal else None
        _flash_step(lhs_of, rhs_of, vt_ref, adjust, states, tk, tq)

    @pl.when(j < i)
    def _():
        step(False)

    @pl.when(j == i)
    def _():
        step(True)
        _flash_finish(o_ref, states)


def _tri_tables(nq, own_first):
    qi, kj = [], []
    for i in range(nq):
        order = ([i] + list(range(i))) if own_first else list(range(i + 1))
        for j in order:
            qi.append(i)
            kj.append(j)
    return jnp.asarray(qi, I32), jnp.asarray(kj, I32)


def _fox_prompt(qt, k, vt, fk_aug, fqt_aug, batch, seq):
    tq = tk = ATTN_TILE
    nq = seq // tq
    qi, kj = _tri_tables(nq, own_first=False)
    qmap = lambda b, t, qi, kj: (b * nq + qi[t], 0)
    kmap = lambda b, t, qi, kj: (b * nq + kj[t], 0)
    qtmap = lambda b, t, qi, kj: (b, 0, qi[t])
    ktmap = lambda b, t, qi, kj: (b, 0, kj[t])
    return pl.pallas_call(
        _fox_prompt_kernel,
        grid_spec=pltpu.PrefetchScalarGridSpec(
            num_scalar_prefetch=2,
            grid=(batch, qi.shape[0]),
            in_specs=[pl.BlockSpec((1, GROUP_W, tq), qtmap),
                      pl.BlockSpec((tk, GROUP_W), kmap),
                      pl.BlockSpec((1, GROUP_W, tk), ktmap),
                      pl.BlockSpec((1, N_HEADS * LANES, tq), qtmap),
                      pl.BlockSpec((tk, N_HEADS * LANES), kmap)],
            out_specs=pl.BlockSpec((tq, GROUP_W), qmap),
            scratch_shapes=_flash_scratch(tq, tk)),
        out_shape=jax.ShapeDtypeStruct((batch * seq, GROUP_W), F32),
        compiler_params=_cparams("arbitrary", "arbitrary"),
        name="fox_prompt",
    )(qi, kj, qt, k, vt, fqt_aug, fk_aug)


def _moba_prompt_kernel(qi_ref, kj_ref, q_ref, k_ref, vt_ref, kmean_ref, bias_ref, o_ref, *scratch):
    states = _flash_states(scratch[:-1])
    qb_ref = scratch[-1]
    t = pl.program_id(1)
    i = qi_ref[t]
    j = kj_ref[t]
    tq, tk = q_ref.shape[2], k_ref.shape[0]
    nb = kmean_ref.shape[1]
    nbp = -(-nb // 8) * 8
    assert 3 + nb <= AUG
    far_kind = bias_ref.shape[1] - 1

    def k_chunk(h, c, spare):
        k = k_ref[c * K_CHUNK:(c + 1) * K_CHUNK, _pair_of(h)]
        return jnp.where(_own_half(k.shape, h, 1), k, spare)

    def off_diagonal_step(far):
        lane = lax.broadcasted_iota(I32, (K_CHUNK, LANES), 1)
        key_side = []
        for base in (_spare_half(0), _spare_half(1)):
            slot = lane - base
            ones_at = (slot == 3 + j)
            if far:
                ones_at = jnp.logical_or(ones_at, jnp.logical_and(slot >= 0, slot < 3))
            key_side.append(jnp.where(ones_at, 1.0, 0.0).astype(BF16))
        adjust = None if far else (lambda h, c, s: s + bias_ref[h, 1, c * K_CHUNK:(c + 1) * K_CHUNK, :])
        _flash_step(lambda h, c: k_chunk(h, c, key_side[h % HEADS_PER_LANE_TILE]),
                    lambda h: qb_ref[h], vt_ref, adjust, states, tk, tq)

    @pl.when(j == i)
    def _():
        _flash_init(states)
        blk_row = lax.broadcasted_iota(I32, (nbp, tq), 0)
        aug_row = lax.broadcasted_iota(I32, (AUG, tq), 0)
        past = blk_row < i
        for h in range(N_HEADS):
            km = _pad_rows(kmean_ref[0, :, _pair_of(h)], nbp).astype(BF16)
            g = jnp.dot(km, _masked_qt(q_ref, h), preferred_element_type=F32)
            c_hi, c_mid, c_lo = _split3_bf16(bias_ref[h, far_kind, 0:1, 0:1])
            aug = jnp.where(aug_row == 0, c_hi.astype(F32),
                            jnp.where(aug_row == 1, c_mid.astype(F32),
                                      jnp.where(aug_row == 2, c_lo.astype(F32), 0.0)))
            for jb in range(nb):
                gj = g[jb:jb + 1, :]
                beats = jnp.logical_and(past, jnp.logical_or(g > gj, jnp.logical_and(g == gj, blk_row < jb)))
                rank = jnp.sum(jnp.where(beats, 1.0, 0.0), axis=0, keepdims=True)
                chosen = jnp.logical_and(rank < MOBA_TOPK, jb < i)
                aug = jnp.where(aug_row == 3 + jb, jnp.where(chosen, 0.0, NEG), aug)
            own = q_ref[0, h * HEAD_DIM:(h + 1) * HEAD_DIM, :]
            spare = jnp.concatenate([aug.astype(BF16), jnp.zeros((HEAD_DIM - AUG, tq), BF16)], axis=0)
            qb_ref[h] = jnp.concatenate([spare, own] if h % HEADS_PER_LANE_TILE else [own, spare], axis=0)

        def adjust(h, c, s):
            s = s + bias_ref[h, 0, c * K_CHUNK:(c + 1) * K_CHUNK, :]
            return jnp.where(_chunk_causal(c, tq), s, NEG)

        zero = jnp.zeros((K_CHUNK, LANES), BF16)
        _flash_step(lambda h, c: k_chunk(h, c, zero), lambda h: qb_ref[h], vt_ref, adjust, states, tk, tq)

        @pl.when(i == 0)
        def _():
            _flash_finish(o_ref, states)

    @pl.when(j == i - 1)
    def _():
        off_diagonal_step(far=False)
        _flash_finish(o_ref, states)

    @pl.when(j < i - 1)
    def _():
        off_diagonal_step(far=True)


def _moba_prompt(qt, k, vt, kmean, bias, batch, seq):
    tq = tk = MOBA_BLOCK
    nq = seq // tq
    nbp = -(-nq // 8) * 8
    qi, kj = _tri_tables(nq, own_first=True)
    qmap = lambda b, t, qi, kj: (b * nq + qi[t], 0)
    kmap = lambda b, t, qi, kj: (b * nq + kj[t], 0)
    return pl.pallas_call(
        _moba_prompt_kernel,
        grid_spec=pltpu.PrefetchScalarGridSpec(
            num_scalar_prefetch=2,
            grid=(batch, qi.shape[0]),
            in_specs=[pl.BlockSpec((1, GROUP_W, tq), lambda b, t, qi, kj: (b, 0, qi[t])),
                      pl.BlockSpec((tk, GROUP_W), kmap),
                      pl.BlockSpec((1, GROUP_W, tk), lambda b, t, qi, kj: (b, 0, kj[t])),
                      pl.BlockSpec((1, nq, GROUP_W), lambda b, t, qi, kj: (b, 0, 0)),
                      pl.BlockSpec(bias.shape, lambda b, t, qi, kj: (0, 0, 0, 0))],
            out_specs=pl.BlockSpec((tq, GROUP_W), qmap),
            scratch_shapes=_flash_scratch(tq, tk) + [pltpu.VMEM((N_HEADS, LANES, tq), BF16)]),
        out_shape=jax.ShapeDtypeStruct((batch * seq, GROUP_W), F32),
        compiler_params=_cparams("arbitrary", "arbitrary"),
        name="moba_prompt",
    )(qi, kj, qt, k, vt, kmean, bias)


def _block_diag_q(q):
    ds = q.shape[0]
    lane_head = lax.broadcasted_iota(I32, (N_HEADS, ds, GROUP_W), 2) // HEAD_DIM
    head = lax.broadcasted_iota(I32, (N_HEADS, ds, GROUP_W), 0)
    q3 = jnp.where(lane_head == head, q.astype(F32)[None, :, :], 0.0)
    return q3.reshape(N_HEADS * ds, GROUP_W).astype(BF16)


def _head_diag_out(acc, ds):
    acc3 = acc.reshape(N_HEADS, ds, GROUP_W)
    lane_head = lax.broadcasted_iota(I32, (N_HEADS, ds, GROUP_W), 2) // HEAD_DIM
    head = lax.broadcasted_iota(I32, (N_HEADS, ds, GROUP_W), 0)
    return jnp.sum(jnp.where(lane_head == head, acc3, 0.0), axis=0)


def _pad_rows(x, rows):
    if x.shape[0] == rows:
        return x
    return jnp.concatenate([x, jnp.zeros((rows - x.shape[0], x.shape[1]), x.dtype)], axis=0)


def _ring_copy(pt_ref, page_index, cache_ref, buf_ref, sem_ref, slot, rr):
    return pltpu.make_async_copy(cache_ref.at[pt_ref[page_index]], buf_ref.at[slot, rr], sem_ref.at[slot])


def _ring_fetch(pt_ref, page_index_of, u, cache_ref, buf_ref, sem_ref):
    slot = u % PAGE_RING
    for rr in range(buf_ref.shape[1]):
        _ring_copy(pt_ref, page_index_of(u, rr), cache_ref, buf_ref, sem_ref, slot, rr).start(priority=rr % 2)


def _ring_wait(pt_ref, page_index_of, u, cache_ref, buf_ref, sem_ref):
    slot = u % PAGE_RING
    for rr in range(buf_ref.shape[1]):
        _ring_copy(pt_ref, page_index_of(u, rr), cache_ref, buf_ref, sem_ref, slot, rr).wait()


def _ring_advance(pt_ref, page_index_of, u, n_fetches, cache_ref, buf_ref, sem_ref):
    @pl.when(u == 0)
    def _():
        for ahead in range(min(PAGE_RING - 1, n_fetches)):
            _ring_fetch(pt_ref, page_index_of, ahead, cache_ref, buf_ref, sem_ref)

    @pl.when(u + (PAGE_RING - 1) < n_fetches)
    def _():
        _ring_fetch(pt_ref, page_index_of, u + (PAGE_RING - 1), cache_ref, buf_ref, sem_ref)

    _ring_wait(pt_ref, page_index_of, u, cache_ref, buf_ref, sem_ref)
    return u % PAGE_RING


def _fox_sample_kernel(pt_ref, q_ref, kn_ref, vn_ref, lfn_ref, *refs, pages, n_pages, n_batch):
    lf_refs = refs[0:pages]
    (k_hbm, v_hbm, o_ref, qbd_ref, m_ref, l_ref, acc_ref, carry_ref, rq_ref,
     kbuf, vbuf, ksem, vsem) = refs[pages:]
    g = pl.program_id(1)
    ng = n_pages // pages
    ds = q_ref.shape[1]
    page = kbuf.shape[3]

    def page_index_of(u, rr):
        return (u // ng) * n_pages + (ng - 1 - u % ng) * pages + (pages - 1 - rr)

    u = pl.program_id(0) * ng + g
    slot = _ring_advance(pt_ref, page_index_of, u, n_batch * ng, k_hbm, kbuf, ksem)
    _ring_advance(pt_ref, page_index_of, u, n_batch * ng, v_hbm, vbuf, vsem)
    lane3 = lax.broadcasted_iota(I32, (N_HEADS, ds, page), 2)
    qidx3 = lax.broadcasted_iota(I32, (N_HEADS, ds, page), 1)
    r = lax.broadcasted_iota(I32, (page, page), 0)
    c = lax.broadcasted_iota(I32, (page, page), 1)
    after = jnp.where(r > c, 1.0, 0.0).astype(BF16)

    def suffix_sum(x):
        hi, mid, lo = _split3_bf16(x)
        return (jnp.dot(hi, after, preferred_element_type=F32) + jnp.dot(mid, after, preferred_element_type=F32)
                + jnp.dot(lo, after, preferred_element_type=F32))

    def biased(s, later):
        s3 = s.reshape(N_HEADS, ds, page) + later[:, None, :] - rq_ref[...].reshape(N_HEADS, ds, 1)
        return s3

    def softmax_update(s, pv_of):
        m_prev = m_ref[...]
        m_new = jnp.maximum(m_prev, jnp.max(s, axis=1, keepdims=True))
        alpha = jnp.exp(m_prev - m_new)
        p = jnp.exp(s - m_new)
        l_ref[...] = alpha * l_ref[...] + jnp.sum(p, axis=1, keepdims=True)
        m_ref[...] = m_new
        acc_ref[...] = alpha * acc_ref[...] + pv_of(p.astype(BF16))

    @pl.when(g == 0)
    def _():
        qbd_ref[...] = _block_diag_q(q_ref[0])
        m_ref[...] = jnp.full_like(m_ref, NEG)
        l_ref[...] = jnp.zeros_like(l_ref)
        acc_ref[...] = jnp.zeros_like(acc_ref)
        x = lfn_ref[0]
        later_new = suffix_sum(x)
        rq3 = jnp.sum(jnp.where(lane3 == qidx3, later_new[:, None, :], 0.0), axis=2, keepdims=True)
        rq_ref[...] = rq3.reshape(N_HEADS * ds, 1)
        kn = _pad_rows(kn_ref[0], page).astype(BF16)
        vn = _pad_rows(vn_ref[0], page).astype(BF16)
        s3 = biased(lax.dot_general(qbd_ref[...], kn, NT, preferred_element_type=F32), later_new)
        s3 = jnp.where(lane3 <= qidx3, s3, NEG)
        softmax_update(s3.reshape(N_HEADS * ds, page), lambda p: jnp.dot(p, vn, preferred_element_type=F32))
        carry_ref[...] = jnp.sum(x, axis=1, keepdims=True)

    xs = [lf_refs[rr][0] for rr in range(pages)]
    within = suffix_sum(jnp.concatenate(xs, axis=0))
    run = carry_ref[...]
    parts = []
    for rr in range(pages):
        later = within[rr * N_HEADS:(rr + 1) * N_HEADS, :] + run
        s = jnp.dot(qbd_ref[...], kbuf[slot, rr].astype(BF16), preferred_element_type=F32)
        parts.append(biased(s, later).reshape(N_HEADS * ds, page))
        run = run + jnp.sum(xs[rr], axis=1, keepdims=True)
    carry_ref[...] = run

    def pv_of(p):
        pv = None
        for rr in range(pages):
            term = lax.dot_general(p[:, rr * page:(rr + 1) * page], vbuf[slot, rr].astype(BF16), NT,
                                   preferred_element_type=F32)
            pv = term if pv is None else pv + term
        return pv

    softmax_update(jnp.concatenate(parts, axis=1), pv_of)

    @pl.when(g == ng - 1)
    def _():
        o_ref[0] = _head_diag_out(acc_ref[...] / l_ref[...], ds)


def _page_ring_scratch(pages, page):
    return [pltpu.VMEM((PAGE_RING, pages, GROUP_W, page), F32), pltpu.SemaphoreType.DMA((PAGE_RING,))]


def _fox_sample(page_table, q3, kn3, vn3, lfn_t, cache_k, cache_v, cache_lf_t):
    db, n_pages = page_table.shape
    ds = q3.shape[1]
    pages = PAGES_PER_STEP
    assert n_pages % pages == 0
    ng = n_pages // pages
    page = cache_k.shape[2]

    def page_map(rr):
        return lambda b, g, pt: (pt[b * n_pages + (ng - 1 - g) * pages + (pages - 1 - rr)], 0, 0)

    bmap = lambda b, g, pt: (b, 0, 0)
    in_specs = [pl.BlockSpec((1, ds, GROUP_W), bmap), pl.BlockSpec((1, ds, GROUP_W), bmap),
                pl.BlockSpec((1, ds, GROUP_W), bmap), pl.BlockSpec((1, N_HEADS, page), bmap)]
    in_specs += [pl.BlockSpec((1, N_HEADS, page), page_map(rr)) for rr in range(pages)]
    in_specs += [pl.BlockSpec(memory_space=pl.ANY), pl.BlockSpec(memory_space=pl.ANY)]
    rows = N_HEADS * ds
    k_ring, k_sem = _page_ring_scratch(pages, page)
    v_ring, v_sem = _page_ring_scratch(pages, page)
    return pl.pallas_call(
        functools.partial(_fox_sample_kernel, pages=pages, n_pages=n_pages, n_batch=db),
        grid_spec=pltpu.PrefetchScalarGridSpec(
            num_scalar_prefetch=1,
            grid=(db, ng),
            in_specs=in_specs,
            out_specs=pl.BlockSpec((1, ds, GROUP_W), bmap),
            scratch_shapes=[pltpu.VMEM((rows, GROUP_W), BF16), pltpu.VMEM((rows, 1), F32),
                            pltpu.VMEM((rows, 1), F32), pltpu.VMEM((rows, GROUP_W), F32),
                            pltpu.VMEM((N_HEADS, 1), F32), pltpu.VMEM((rows, 1), F32),
                            k_ring, v_ring, k_sem, v_sem]),
        out_shape=jax.ShapeDtypeStruct((db, ds, GROUP_W), F32),
        compiler_params=_cparams("arbitrary", "arbitrary"),
        name="fox_sample",
    )(page_table.reshape(-1), q3, kn3, vn3, lfn_t, *([cache_lf_t] * pages), cache_k, cache_v)


def _moba_sample_kernel(pt_ref, q_ref, kn_ref, vn_ref, bias_ref, k_hbm, v_hbm, o_ref,
                        qbd_ref, s_ref, p_ref, pnew_ref, linv_ref, acc_ref, kbuf, vbuf, ksem, vsem,
                        *, pages, n_batch):
    ph = pl.program_id(1)
    g = pl.program_id(2)
    ds = q_ref.shape[1]
    page = kbuf.shape[3]
    n_pages = s_ref.shape[0]
    ng = n_pages // pages
    rows = N_HEADS * ds
    pages_per_block = MOBA_BLOCK // page
    n_blocks = n_pages // pages_per_block
    u = pl.program_id(0) * ng + g

    def page_index_of(u, rr):
        return (u // ng) * n_pages + (u % ng) * pages + rr

    @pl.when(jnp.logical_and(ph == 0, g == 0))
    def _():
        qbd_ref[...] = _block_diag_q(q_ref[0])

    @pl.when(ph == 0)
    def _():
        slot = _ring_advance(pt_ref, page_index_of, u, n_batch * ng, k_hbm, kbuf, ksem)
        for rr in range(pages):
            s_ref[g * pages + rr] = jnp.dot(qbd_ref[...], kbuf[slot, rr].astype(BF16), preferred_element_type=F32)

    @pl.when(jnp.logical_and(ph == 0, g == ng - 1))
    def _():
        lane = lax.broadcasted_iota(I32, (rows, LANES), 1)
        gate = jnp.full((rows, LANES), -jnp.inf, F32)
        for b in range(n_blocks):
            tot = s_ref[b * pages_per_block]
            for u in range(1, pages_per_block):
                tot = tot + s_ref[b * pages_per_block + u]
            gate = jnp.where(lane == b, jnp.sum(tot, axis=1, keepdims=True) * (1.0 / MOBA_BLOCK), gate)
        chosen = []
        for _ in range(min(MOBA_TOPK, n_blocks)):
            mx = jnp.max(gate, axis=1, keepdims=True)
            idx = jnp.min(jnp.where(gate == mx, lane.astype(F32), float(LANES)), axis=1, keepdims=True)
            chosen.append(idx)
            gate = jnp.where(lane.astype(F32) == idx, -jnp.inf, gate)
        lane_n = lax.broadcasted_iota(I32, (N_HEADS, ds, page), 2)
        qidx_n = lax.broadcasted_iota(I32, (N_HEADS, ds, page), 1)
        s_new = lax.dot_general(qbd_ref[...], _pad_rows(kn_ref[0], page).astype(BF16), NT,
                                preferred_element_type=F32) + bias_ref[1]
        s_new = jnp.where((lane_n <= qidx_n).reshape(rows, page), s_new, NEG)
        m_tile = s_new
        for pg in range(n_pages):
            b = float(pg // pages_per_block)
            picked = chosen[0] == b
            for idx in chosen[1:]:
                picked = jnp.logical_or(picked, idx == b)
            bias = bias_ref[0] if pg == n_pages - 1 else bias_ref[2]
            s = jnp.where(picked, s_ref[pg] + bias, NEG)
            s_ref[pg] = s
            m_tile = jnp.maximum(m_tile, s)
        m = jnp.max(m_tile, axis=1, keepdims=True)
        p_new = jnp.exp(s_new - m)
        l_tile = p_new
        pnew_ref[...] = p_new.astype(BF16)
        for pg in range(n_pages):
            p = jnp.exp(s_ref[pg] - m)
            l_tile = l_tile + p
            p_ref[pg] = p.astype(BF16)
        linv_ref[...] = 1.0 / jnp.sum(l_tile, axis=1, keepdims=True)

    @pl.when(jnp.logical_and(ph == 1, g == 0))
    def _():
        acc_ref[...] = jnp.dot(pnew_ref[...], _pad_rows(vn_ref[0], page).astype(BF16),
                               preferred_element_type=F32)

    @pl.when(ph == 1)
    def _():
        slot = _ring_advance(pt_ref, page_index_of, u, n_batch * ng, v_hbm, vbuf, vsem)
        acc = acc_ref[...]
        for rr in range(pages):
            acc = acc + lax.dot_general(p_ref[g * pages + rr], vbuf[slot, rr].astype(BF16), NT,
                                        preferred_element_type=F32)
        acc_ref[...] = acc

    @pl.when(jnp.logical_and(ph == 1, g == ng - 1))
    def _():
        o_ref[0] = _head_diag_out(acc_ref[...] * linv_ref[...], ds)


def _moba_sample(page_table, q3, kn3, vn3, bias3, cache_k, cache_v):
    db, n_pages = page_table.shape
    ds = q3.shape[1]
    pages = PAGES_PER_STEP
    page = cache_k.shape[2]
    assert n_pages % pages == 0 and MOBA_BLOCK % page == 0 and (n_pages * page) % MOBA_BLOCK == 0
    assert T5_MAX_DIST <= page and ds <= page
    ng = n_pages // pages
    rows = N_HEADS * ds
    bmap = lambda b, ph, g, pt: (b, 0, 0)
    in_specs = [pl.BlockSpec((1, ds, GROUP_W), bmap), pl.BlockSpec((1, ds, GROUP_W), bmap),
                pl.BlockSpec((1, ds, GROUP_W), bmap),
                pl.BlockSpec(bias3.shape, lambda b, ph, g, pt: (0, 0, 0)),
                pl.BlockSpec(memory_space=pl.ANY), pl.BlockSpec(memory_space=pl.ANY)]
    k_ring, k_sem = _page_ring_scratch(pages, page)
    v_ring, v_sem = _page_ring_scratch(pages, page)
    return pl.pallas_call(
        functools.partial(_moba_sample_kernel, pages=pages, n_batch=db),
        grid_spec=pltpu.PrefetchScalarGridSpec(
            num_scalar_prefetch=1,
            grid=(db, 2, ng),
            in_specs=in_specs,
            out_specs=pl.BlockSpec((1, ds, GROUP_W), bmap),
            scratch_shapes=[pltpu.VMEM((rows, GROUP_W), BF16),
                            pltpu.VMEM((n_pages, rows, page), F32),
                            pltpu.VMEM((n_pages, rows, page), BF16),
                            pltpu.VMEM((rows, page), BF16),
                            pltpu.VMEM((rows, 1), F32),
                            pltpu.VMEM((rows, GROUP_W), F32),
                            k_ring, v_ring, k_sem, v_sem]),
        out_shape=jax.ShapeDtypeStruct((db, ds, GROUP_W), F32),
        compiler_params=_cparams("arbitrary", "arbitrary", "arbitrary"),
        name="moba_sample",
    )(page_table.reshape(-1), q3, kn3, vn3, bias3, cache_k, cache_v)


def _outproj_kernel(of_ref, om_ref, x_ref, gate_ref, shift_ref, scale_ref, gf_ref, gm_ref, wo_ref, gffn_ref,
                    wr_ref, br_ref, cnt0_ref, x1_ref, h2_ref, ri_ref, rw_ref, cnt_ref, carry_ref):
    t = pl.program_id(0)

    @pl.when(t == 0)
    def _():
        carry_ref[...] = cnt0_ref[...]

    tm = x_ref.shape[0]
    nf = (_rms(of_ref[...]) * gf_ref[...]).astype(BF16)
    nm = (_rms(om_ref[...]) * gm_ref[...]).astype(BF16)
    o = (jnp.dot(nf, wo_ref[0:GROUP_W, :], preferred_element_type=F32)
         + jnp.dot(nm, wo_ref[GROUP_W:2 * GROUP_W, :], preferred_element_type=F32))
    x1 = x_ref[...] + gate_ref[...] * o
    x1_ref[...] = x1
    h2 = _rms(x1) * gffn_ref[...]
    h2 = h2 * (1.0 + scale_ref[...]) + shift_ref[...]
    hi = h2.astype(BF16)
    half = h2.shape[1] // 2
    hi32 = pltpu.bitcast(hi.astype(F32), jnp.uint32)
    h2_ref[...] = jnp.bitwise_or(jnp.right_shift(hi32[:, :half], jnp.uint32(16)),
                                 jnp.bitwise_and(hi32[:, half:], jnp.uint32(0xFFFF0000)))

    lo = (h2 - hi.astype(F32)).astype(BF16)
    wr = wr_ref[...]
    whi = wr.astype(BF16)
    wlo = (wr - whi.astype(F32)).astype(BF16)
    lg = (jnp.dot(hi, whi, preferred_element_type=F32) + jnp.dot(lo, whi, preferred_element_type=F32)
          + jnp.dot(hi, wlo, preferred_element_type=F32)) + br_ref[...]
    lane = lax.broadcasted_iota(I32, (tm, LANES), 1)
    lane_f = lane.astype(F32)
    ninf = -jnp.inf
    is_g = jnp.logical_and(lane >= N_EXPERTS, lane < N_EXPERTS + N_GROUPS)
    glog = jnp.where(is_g, lg, ninf)
    gmax = jnp.max(glog, axis=1, keepdims=True)
    gidx = jnp.min(jnp.where(glog == gmax, lane_f, 2.0 * LANES), axis=1, keepdims=True).astype(I32) - N_EXPERTS
    g_w = 1.0 / jnp.sum(jnp.exp(glog - gmax), axis=1, keepdims=True)
    in_grp = jnp.logical_and(lane >= gidx * EXPERTS_PER_GROUP, lane < (gidx + 1) * EXPERTS_PER_GROUP)
    elog = jnp.where(in_grp, lg, ninf)
    e1 = jnp.max(elog, axis=1, keepdims=True)
    i1 = jnp.min(jnp.where(elog == e1, lane_f, 2.0 * LANES), axis=1, keepdims=True).astype(I32)
    z = jnp.sum(jnp.exp(elog - e1), axis=1, keepdims=True)
    elog2 = jnp.where(lane == i1, ninf, elog)
    e2 = jnp.max(elog2, axis=1, keepdims=True)
    i2 = jnp.min(jnp.where(elog2 == e2, lane_f, 2.0 * LANES), axis=1, keepdims=True).astype(I32)
    p1 = 1.0 / z
    p2 = jnp.exp(e2 - e1) / z
    w1 = g_w * (p1 / (p1 + p2))
    w2 = g_w * (p2 / (p1 + p2))

    a = jnp.where(jnp.logical_or(lane == i1, lane == i2), 1.0, 0.0)
    r = lax.broadcasted_iota(I32, (tm, tm), 0)
    c = lax.broadcasted_iota(I32, (tm, tm), 1)
    before = jnp.where(c < r, 1.0, 0.0).astype(BF16)
    pos = jnp.dot(before, a.astype(BF16), preferred_element_type=F32) + carry_ref[...]
    r1 = jnp.sum(jnp.where(lane == i1, pos, 0.0), axis=1, keepdims=True)
    r2 = jnp.sum(jnp.where(lane == i2, pos, 0.0), axis=1, keepdims=True)
    carry_ref[...] = carry_ref[...] + jnp.sum(a, axis=0, keepdims=True)
    cnt_ref[...] = carry_ref[...]

    ri = jnp.where(lane == 0, i1, 0) + jnp.where(lane == 1, i2, 0)
    ri = ri + jnp.where(lane == 2, r1.astype(I32), 0) + jnp.where(lane == 3, r2.astype(I32), 0)
    ri_ref[...] = ri
    rw_ref[...] = jnp.where(lane == 0, w1, 0.0) + jnp.where(lane == 1, w2, 0.0)


def _outproj(of, om, x2, gate3, shift3, scale3, gf, gm, wo, gffn, wr, br, cnt0, *, rows_per_mod, name):
    n, d = x2.shape
    tm = ROW_TILE
    nt = n // tm
    mod_rows = gate3.shape[1]
    mod_map = lambda t: ((t * tm) // rows_per_mod, 0, 0)
    row_map = lambda t: (t, 0)
    const2 = lambda t: (0, 0)
    return pl.pallas_call(
        _outproj_kernel,
        grid=(nt,),
        in_specs=[pl.BlockSpec((tm, GROUP_W), row_map), pl.BlockSpec((tm, GROUP_W), row_map),
                  pl.BlockSpec((tm, d), row_map),
                  pl.BlockSpec((None, mod_rows, d), mod_map), pl.BlockSpec((None, mod_rows, d), mod_map),
                  pl.BlockSpec((None, mod_rows, d), mod_map),
                  pl.BlockSpec((1, GROUP_W), const2), pl.BlockSpec((1, GROUP_W), const2),
                  pl.BlockSpec((d, d), const2), pl.BlockSpec((1, d), const2),
                  pl.BlockSpec((d, LANES), const2), pl.BlockSpec((1, LANES), const2),
                  pl.BlockSpec((1, LANES), const2)],
        out_specs=[pl.BlockSpec((tm, d), row_map), pl.BlockSpec((tm, d // 2), row_map),
                   pl.BlockSpec((tm, LANES), row_map), pl.BlockSpec((tm, LANES), row_map),
                   pl.BlockSpec((1, LANES), const2)],
        out_shape=[jax.ShapeDtypeStruct((n, d), F32), jax.ShapeDtypeStruct((n, d // 2), jnp.uint32),
                   jax.ShapeDtypeStruct((n, LANES), I32), jax.ShapeDtypeStruct((n, LANES), F32),
                   jax.ShapeDtypeStruct((1, LANES), F32)],
        scratch_shapes=[pltpu.VMEM((1, LANES), F32)],
        compiler_params=_cparams("arbitrary"),
        name=name,
    )(of, om, x2, gate3, shift3, scale3, gf, gm, wo, gffn, wr, br, cnt0)


def _scatter_kernel(dest_ref, h_ref, rows_in_ref, rows_ref, sem):
    del rows_in_ref
    tm = h_ref.shape[0]

    def row_copy(r, k):
        return pltpu.make_async_copy(h_ref.at[pl.ds(r, 1)], rows_ref.at[pl.ds(dest_ref[k, r], 1)], sem)

    for r in range(tm):
        row_copy(r, 0).start(priority=0)
        row_copy(r, 1).start(priority=1)
    for _ in range(2):
        pltpu.make_async_copy(h_ref, rows_ref.at[pl.ds(0, tm)], sem).wait()


def _scatter_rows(dest2, h2, rows):
    n, d = h2.shape
    tm = ROW_TILE
    return pl.pallas_call(
        _scatter_kernel,
        grid=(n // tm,),
        in_specs=[pl.BlockSpec((2, tm), lambda t: (0, t), memory_space=pltpu.SMEM),
                  pl.BlockSpec((tm, d), lambda t: (t, 0)),
                  pl.BlockSpec(memory_space=pl.ANY)],
        out_specs=pl.BlockSpec(memory_space=pl.ANY),
        out_shape=jax.ShapeDtypeStruct(rows.shape, rows.dtype),
        scratch_shapes=[pltpu.SemaphoreType.DMA(())],
        input_output_aliases={2: 0},
        compiler_params=_cparams("arbitrary"),
        name="moe_scatter",
    )(dest2, h2, rows)


def _moe_kernel(be_ref, nu_ref, rows_ref, w1_ref, w3_ref, w2_ref, y_ref, w1b_ref, w3b_ref, w2b_ref,
                rbuf_ref, rsem):
    blk = pl.program_id(0)
    n_used = nu_ref[0]

    def fetch(u):
        start = pl.multiple_of(u * MOE_BLOCK, MOE_BLOCK)
        return pltpu.make_async_copy(rows_ref.at[pl.ds(start, MOE_BLOCK)], rbuf_ref.at[u % ROW_RING],
                                     rsem.at[u % ROW_RING])

    @pl.when(blk == 0)
    def _():
        for ahead in range(ROW_RING - 1):
            @pl.when(ahead < n_used)
            def _():
                fetch(ahead).start()

    @pl.when(blk + (ROW_RING - 1) < n_used)
    def _():
        fetch(blk + (ROW_RING - 1)).start()

    @pl.when(blk < n_used)
    def _():
        prev = be_ref[jnp.maximum(blk - 1, 0)]

        @pl.when(jnp.logical_or(blk == 0, be_ref[blk] != prev))
        def _():
            w1b_ref[...] = w1_ref[0].astype(BF16)
            w3b_ref[...] = w3_ref[0].astype(BF16)
            w2b_ref[...] = w2_ref[0].astype(BF16)

        fetch(blk).wait()
        packed = rbuf_ref[blk % ROW_RING]
        x_lo = pltpu.bitcast(jnp.left_shift(packed, jnp.uint32(16)), F32)
        x_hi = pltpu.bitcast(jnp.bitwise_and(packed, jnp.uint32(0xFFFF0000)), F32)
        x = jnp.concatenate([x_lo, x_hi], axis=1).astype(BF16)
        a = jnp.dot(x, w1b_ref[...], preferred_element_type=F32)
        b = jnp.dot(x, w3b_ref[...], preferred_element_type=F32)
        hm = (a * jax.nn.sigmoid(a)) * b
        y_ref[...] = jnp.dot(hm.astype(BF16), w2b_ref[...], preferred_element_type=F32)

    @pl.when(blk >= nu_ref[0])
    def _():
        y_ref[...] = jnp.zeros_like(y_ref)


def _moe_experts(block_e, n_used, rows, w1, w3, w2):
    p = rows.shape[0]
    d, de = w1.shape[1], w1.shape[2]
    assert rows.shape[1] * 2 == d
    nblk = p // MOE_BLOCK
    return pl.pallas_call(
        _moe_kernel,
        grid_spec=pltpu.PrefetchScalarGridSpec(
            num_scalar_prefetch=2,
            grid=(nblk,),
            in_specs=[pl.BlockSpec(memory_space=pl.ANY),
                      pl.BlockSpec((1, d, de), lambda b, be, nu: (be[b], 0, 0)),
                      pl.BlockSpec((1, d, de), lambda b, be, nu: (be[b], 0, 0)),
                      pl.BlockSpec((1, de, d), lambda b, be, nu: (be[b], 0, 0))],
            out_specs=pl.BlockSpec((MOE_BLOCK, d), lambda b, be, nu: (b, 0)),
            scratch_shapes=[pltpu.VMEM((d, de), BF16), pltpu.VMEM((d, de), BF16), pltpu.VMEM((de, d), BF16),
                            pltpu.VMEM((ROW_RING, MOE_BLOCK, d // 2), jnp.uint32),
                            pltpu.SemaphoreType.DMA((ROW_RING,))]),
        out_shape=jax.ShapeDtypeStruct((p, d), F32),
        compiler_params=_cparams("arbitrary"),
        name="moe_experts",
    )(block_e, n_used, rows, w1, w3, w2)


def _combine_kernel(dest_ref, dest_next_ref, x1_ref, gate_ref, rw_ref, gfin_ref, y_ref, o_ref, ybuf_ref, sem):
    tm = x1_ref.shape[0]
    t = pl.program_id(0)

    def gather(idx_ref, slot):
        for r in range(tm):
            for k in range(2):
                pltpu.make_async_copy(y_ref.at[pl.ds(idx_ref[k, r], 1)], ybuf_ref.at[slot, k, pl.ds(r, 1)],
                                      sem.at[slot]).start(priority=k)

    @pl.when(t == 0)
    def _():
        gather(dest_ref, 0)

    for slot in range(2):
        @pl.when(jnp.logical_and(t + 1 < pl.num_programs(0), (t + 1) % 2 == slot))
        def _():
            gather(dest_next_ref, slot)

    cur = t % 2
    for k in range(2):
        pltpu.make_async_copy(y_ref.at[pl.ds(0, tm)], ybuf_ref.at[cur, k], sem.at[cur]).wait()
    rw = rw_ref[...]
    moe = rw[:, 0:1] * ybuf_ref[cur, 0] + rw[:, 1:2] * ybuf_ref[cur, 1]
    xo = x1_ref[...] + gate_ref[...] * moe
    o_ref[...] = _rms(xo) * gfin_ref[...]


def _combine(dest2, x1, gate3, rw, gfin, y, *, rows_per_mod, name):
    n, d = x1.shape
    tm = ROW_TILE
    nt = n // tm
    mod_rows = gate3.shape[1]
    return pl.pallas_call(
        _combine_kernel,
        grid=(nt,),
        in_specs=[pl.BlockSpec((2, tm), lambda t: (0, t), memory_space=pltpu.SMEM),
                  pl.BlockSpec((2, tm), lambda t: (0, jnp.minimum(t + 1, nt - 1)), memory_space=pltpu.SMEM),
                  pl.BlockSpec((tm, d), lambda t: (t, 0)),
                  pl.BlockSpec((None, mod_rows, d), lambda t: ((t * tm) // rows_per_mod, 0, 0)),
                  pl.BlockSpec((tm, LANES), lambda t: (t, 0)),
                  pl.BlockSpec((1, d), lambda t: (0, 0)),
                  pl.BlockSpec(memory_space=pl.ANY)],
        out_specs=pl.BlockSpec((tm, d), lambda t: (t, 0)),
        out_shape=jax.ShapeDtypeStruct((n, d), F32),
        scratch_shapes=[pltpu.VMEM((2, 2, tm, d), F32), pltpu.SemaphoreType.DMA((2,))],
        compiler_params=_cparams("arbitrary"),
        name=name,
    )(dest2, dest2, x1, gate3, rw, gfin, y)


def kernel(x_prompt, x_sample, cache_fox_k, cache_fox_v, cache_fox_logf, cache_moba_k, cache_moba_v, page_table, c_prompt, c_sample, w_ada, b_ada, g_attn, w_in, b_forget, g_out_fox, g_out_moba, t5_bias, w_out, g_ffn, w_router_group, b_router_group, w_router_expert, b_router_expert, w1, w3, w2, g_final):
    bsz, seq, d = x_prompt.shape
    db, ds, _ = x_sample.shape
    depth = w_ada.shape[0]
    n_phys, page = cache_fox_k.shape[1], cache_fox_k.shape[2]
    assert depth == 1, "one trunk layer"
    assert seq % ROW_TILE == 0 and (db * ds) % ROW_TILE == 0 and ROW_TILE % ds == 0
    assert N_EXPERTS + N_GROUPS <= LANES and N_HEADS <= LANES
    l = 0
    n_p, n_s = bsz * seq, db * ds
    xp2 = x_prompt.reshape(n_p, d)
    xs2 = x_sample.reshape(n_s, d)

    mod = _ada(jnp.concatenate([c_prompt, c_sample], axis=0), w_ada[l], b_ada[l])
    mod_p = [mod[:bsz, i * d:(i + 1) * d].reshape(bsz, 1, d) for i in range(6)]
    mod_s = [jnp.repeat(mod[bsz:, i * d:(i + 1) * d], ds, axis=0).reshape(n_s // ROW_TILE, ROW_TILE, d)
             for i in range(6)]

    w = GROUP_W
    wl = w_in[l]
    wcat = jnp.concatenate([wl[:, :3 * w], wl[:, 3 * w + N_HEADS:],
                            wl[:, 3 * w:3 * w + N_HEADS], jnp.zeros((d, LANES - N_HEADS), F32)],
                           axis=1).astype(BF16)
    bf_pad = jnp.pad(b_forget[l], (0, LANES - N_HEADS)).reshape(1, LANES)
    g_attn2 = g_attn[l].reshape(1, d)

    fqt, fk_t, fv_t, lf_t, mqt, mk_t, mv_t, fk_aug, fqt_aug, kmean, fkb, fvtb, mkb, mvtb = _inproj(
        xp2, mod_p[0], mod_p[1], g_attn2, wcat, bf_pad, rows_per_mod=seq, prompt_extras=True, seq_len=seq)
    sfq, sfk, sfv, slf, smq, smk, smv = _inproj(
        xs2, mod_s[0], mod_s[1], g_attn2, wcat, bf_pad, rows_per_mod=ROW_TILE, prompt_extras=False, seq_len=ds)

    o_fox_p = _fox_prompt(fqt, fkb, fvtb, fk_aug, fqt_aug, bsz, seq)
    bias_p = _t5_tiles(t5_bias, (0, MOBA_BLOCK, 2 * MOBA_BLOCK), MOBA_BLOCK, MOBA_BLOCK, sign=-1, scale=LOG2E)
    o_moba_p = _moba_prompt(mqt, mkb, mvtb, kmean.reshape(bsz, seq // MOBA_BLOCK, w), bias_p, bsz, seq)

    to3 = lambda a: a.reshape(db, ds, w)
    page_t = lambda c: c[l].transpose(0, 2, 3, 1).reshape(n_phys, w, page)
    cfk, cfv, cmk, cmv = page_t(cache_fox_k), page_t(cache_fox_v), page_t(cache_moba_k), page_t(cache_moba_v)
    clf_t = cache_fox_logf[l].transpose(0, 2, 1)
    slf_t = jnp.pad(slf.reshape(db, ds, N_HEADS).transpose(0, 2, 1), ((0, 0), (0, 0), (0, page - ds)))
    o_fox_s = _fox_sample(page_table, to3(sfq), to3(sfk), to3(sfv), slf_t, cfk, cfv, clf_t)
    bias_s = _t5_tiles(t5_bias, (page, 0, 2 * T5_MAX_DIST + page), ds, page, sign=1)
    bias_s = bias_s.transpose(1, 0, 2, 3).reshape(3, N_HEADS * ds, page)
    o_moba_s = _moba_sample(page_table, to3(smq), to3(smk), to3(smv), bias_s, cmk, cmv)

    wr = jnp.concatenate([w_router_expert[l], w_router_group[l],
                          jnp.zeros((d, LANES - N_EXPERTS - N_GROUPS), F32)], axis=1)
    br = jnp.concatenate([b_router_expert[l], b_router_group[l],
                          jnp.zeros((LANES - N_EXPERTS - N_GROUPS,), F32)]).reshape(1, LANES)
    wo = w_out[l].astype(BF16)
    gf, gm, gffn = g_out_fox[l].reshape(1, w), g_out_moba[l].reshape(1, w), g_ffn[l].reshape(1, d)
    x1_p, h2_p, ri_p, rw_p, cnt_p = _outproj(
        o_fox_p, o_moba_p, xp2, mod_p[2], mod_p[3], mod_p[4], gf, gm, wo, gffn, wr, br,
        jnp.zeros((1, LANES), F32), rows_per_mod=seq, name="outproj_prompt")
    x1_s, h2_s, ri_s, rw_s, cnt = _outproj(
        o_fox_s.reshape(n_s, w), o_moba_s.reshape(n_s, w), xs2, mod_s[2], mod_s[3], mod_s[4], gf, gm, wo, gffn,
        wr, br, cnt_p, rows_per_mod=ROW_TILE, name="outproj_sample")

    counts = cnt[0, :N_EXPERTS].astype(I32)
    padded = (counts + MOE_BLOCK - 1) // MOE_BLOCK * MOE_BLOCK
    pend = jnp.cumsum(padded)
    pstart = pend - padded
    n_asg = 2 * (n_p + n_s)
    n_blocks = -(-(n_asg + N_EXPERTS * (MOE_BLOCK - 1)) // MOE_BLOCK)
    blk_start = jnp.arange(n_blocks, dtype=I32) * MOE_BLOCK
    block_e = jnp.clip(jnp.sum((pend[None, :] <= blk_start[:, None]).astype(I32), axis=1), 0, N_EXPERTS - 1)
    n_used = (pend[-1:] // MOE_BLOCK).astype(I32)

    def dest_of(ri):
        hit = ri[:, 0:2, None] == jnp.arange(N_EXPERTS, dtype=I32)
        return (jnp.sum(jnp.where(hit, pstart, 0), axis=-1) + ri[:, 2:4]).T.astype(I32)

    dest_p, dest_s = dest_of(ri_p), dest_of(ri_s)
    rows = jnp.zeros((n_blocks * MOE_BLOCK, d // 2), jnp.uint32)
    rows = _scatter_rows(dest_p, h2_p, rows)
    rows = _scatter_rows(dest_s, h2_s, rows)
    y = _moe_experts(block_e, n_used, rows, w1[l], w3[l], w2[l])
    gfin = g_final.reshape(1, d)
    y_prompt = _combine(dest_p, x1_p, mod_p[5], rw_p, gfin, y, rows_per_mod=seq, name="combine_prompt")
    y_sample = _combine(dest_s, x1_s, mod_s[5], rw_s, gfin, y, rows_per_mod=ROW_TILE, name="combine_sample")

    hp = lambda a: a.reshape(depth, bsz, N_HEADS, HEAD_DIM, seq).transpose(0, 1, 4, 2, 3)
    hs = lambda a: a.reshape(depth, db, ds, N_HEADS, HEAD_DIM)
    return (y_prompt.reshape(bsz, seq, d), y_sample.reshape(db, ds, d),
            hp(fk_t), hp(fv_t), lf_t.reshape(depth, bsz, N_HEADS, seq).transpose(0, 1, 3, 2), hp(mk_t), hp(mv_t),
            hs(sfk), hs(sfv), slf.reshape(depth, db, ds, N_HEADS), hs(smk), hs(smv))
```

```python
import functools
import math

import jax
import jax.numpy as jnp
from jax import lax
from jax.experimental import pallas as pl
from jax.experimental.pallas import tpu as pltpu

F32 = jnp.float32
BF16 = jnp.bfloat16
I32 = jnp.int32

HEAD_DIM = 64
N_HEADS = 8
GROUP_W = N_HEADS * HEAD_DIM
LANES = 128
HEADS_PER_LANE_TILE = LANES // HEAD_DIM
N_PAIRS = N_HEADS // HEADS_PER_LANE_TILE
SM_SCALE = HEAD_DIM ** -0.5
LOG2E = math.log2(math.e)
MOBA_BLOCK = 256
MOBA_TOPK = 3
T5_BUCKETS = 32
T5_MAX_DIST = 128
N_GROUPS = 4
EXPERTS_PER_GROUP = 8
N_EXPERTS = N_GROUPS * EXPERTS_PER_GROUP
RMS_EPS = 1e-6
ROW_TILE = 256
ATTN_TILE = 256
K_CHUNK = 64
AUG = LANES // N_HEADS
SUM_ROWS = 16
MOE_BLOCK = 256
PAGES_PER_STEP = 16
PAGE_RING = 3
ROW_RING = 3
NEG = -1e30
VMEM_LIMIT = 48 * 1024 * 1024
HIGHEST = lax.Precision.HIGHEST
NT = (((1,), (1,)), ((), ()))


def _cparams(*sem):
    return pltpu.CompilerParams(dimension_semantics=sem, vmem_limit_bytes=VMEM_LIMIT)


def _rms(x):
    return x * lax.rsqrt(jnp.mean(x * x, axis=-1, keepdims=True) + RMS_EPS)


def _spare_half(h):
    return 0 if h % HEADS_PER_LANE_TILE else HEAD_DIM


def _split3_bf16(x):
    hi = x.astype(BF16)
    r1 = x - hi.astype(F32)
    mid = r1.astype(BF16)
    lo = (r1 - mid.astype(F32)).astype(BF16)
    return hi, mid, lo


def _ada_kernel(c_ref, w_ref, b_ref, o_ref):
    c = c_ref[...]
    a = c * jax.nn.sigmoid(c)
    o_ref[...] = jnp.dot(a, w_ref[...], preferred_element_type=F32, precision=HIGHEST) + b_ref[...]


def _ada(c, w, b):
    n, d = c.shape
    e = w.shape[1]
    tn = 1024
    return pl.pallas_call(
        _ada_kernel,
        grid=(e // tn,),
        in_specs=[pl.BlockSpec((n, d), lambda j: (0, 0)),
                  pl.BlockSpec((d, tn), lambda j: (0, j)),
                  pl.BlockSpec((1, tn), lambda j: (0, j))],
        out_specs=pl.BlockSpec((n, tn), lambda j: (0, j)),
        out_shape=jax.ShapeDtypeStruct((n, e), F32),
        compiler_params=_cparams("arbitrary"),
        name="ada",
    )(c, w, b.reshape(1, e))


def _inproj_kernel(x_ref, shift_ref, scale_ref, g_ref, w_ref, bf_ref,
                   fq_ref, fk_ref, fv_ref, lf_ref, mq_ref, mk_ref, mv_ref, *rest,
                   prompt_extras, tiles_per_seq):
    x = x_ref[...]
    h = _rms(x) * g_ref[...]
    h = h * (1.0 + scale_ref[...]) + shift_ref[...]
    z = jnp.dot(h.astype(BF16), w_ref[...], preferred_element_type=F32)
    w = GROUP_W
    q_scale = SM_SCALE * LOG2E if prompt_extras else SM_SCALE
    fq = z[:, 0:w] * q_scale
    mq = z[:, 3 * w:4 * w] * q_scale
    if prompt_extras:
        fq_ref[0] = fq.T.astype(BF16)
        mq_ref[0] = mq.T.astype(BF16)
    else:
        fq_ref[...] = fq.astype(BF16)
        mq_ref[...] = mq.astype(BF16)
    fk, fv = z[:, w:2 * w], z[:, 2 * w:3 * w]
    mk, mv = z[:, 4 * w:5 * w], z[:, 5 * w:6 * w]
    fg = z[:, 6 * w:6 * w + LANES] + bf_ref[...]
    lf = jnp.minimum(fg, 0.0) - jnp.log1p(jnp.exp(-jnp.abs(fg)))
    if not prompt_extras:
        fk_ref[...] = fk
        fv_ref[...] = fv
        mk_ref[...] = mk
        mv_ref[...] = mv
        lf_ref[...] = lf[:, :N_HEADS]
        return
    fkaug_ref, fqtaug_ref, kmean_ref, fkb_ref, fvtb_ref, mkb_ref, mvtb_ref, carry_ref = rest
    tm = x.shape[0]
    fk_ref[0] = fk.T
    fv_t = fv.T
    fv_ref[0] = fv_t
    mk_ref[0] = mk.T
    mv_t = mv.T
    mv_ref[0] = mv_t
    fkb_ref[...] = fk.astype(BF16)
    fvtb_ref[0] = fv_t.astype(BF16)
    mkb_ref[...] = mk.astype(BF16)
    mvtb_ref[0] = mv_t.astype(BF16)
    lf_ref[0] = lf.T[:N_HEADS, :]
    kmean_ref[0] = jnp.mean(mk, axis=0, keepdims=True)

    @pl.when(pl.program_id(0) % tiles_per_seq == 0)
    def _():
        carry_ref[...] = jnp.zeros_like(carry_ref)

    r = lax.broadcasted_iota(I32, (tm, tm), 0)
    c = lax.broadcasted_iota(I32, (tm, tm), 1)
    tri = jnp.where(c <= r, 1.0, 0.0).astype(BF16)
    hi, mid, lo = _split3_bf16(lf)
    cs = (jnp.dot(tri, hi, preferred_element_type=F32)
          + jnp.dot(tri, mid, preferred_element_type=F32)
          + jnp.dot(tri, lo, preferred_element_type=F32)) + carry_ref[...]
    carry_ref[...] = cs[tm - 1:tm, :]
    hi, mid, lo = _split3_bf16(cs * LOG2E)
    er = lax.broadcasted_iota(I32, (LANES, LANES), 0)
    ec = lax.broadcasted_iota(I32, (LANES, LANES), 1)

    def spread(x, off):
        e = jnp.where(jnp.logical_and(er < N_HEADS, ec == AUG * er + off), 1.0, 0.0).astype(BF16)
        return jnp.dot(x, e, preferred_element_type=F32)

    lane = lax.broadcasted_iota(I32, (tm, LANES), 1)
    slot = lane % AUG
    fq_aug = (spread(hi, 0) + spread(mid, 1) + spread(lo, 2)
              + jnp.where(jnp.logical_and(slot >= 3, slot < 6), 1.0, 0.0))
    fk_aug = jnp.where(slot < 3, 1.0, 0.0) - (spread(hi, 3) + spread(mid, 4) + spread(lo, 5))
    fq_aug_t = fq_aug.T
    k_tiles, q_tiles = [], []
    for hd in range(N_HEADS):
        base = _spare_half(hd)
        shift = (base - AUG * hd) % LANES
        moved = pltpu.roll(fk_aug, shift, 1) if shift else fk_aug
        k_tiles.append(jnp.where(jnp.logical_and(lane >= base, lane < base + AUG), moved, 0.0))
        q_rows = [fq_aug_t[AUG * hd:AUG * (hd + 1), :]]
        if base:
            q_rows.insert(0, jnp.zeros((base, tm), F32))
        if LANES - base - AUG:
            q_rows.append(jnp.zeros((LANES - base - AUG, tm), F32))
        q_tiles.append(jnp.concatenate(q_rows, axis=0))
    fkaug_ref[...] = jnp.concatenate(k_tiles, axis=1).astype(BF16)
    fqtaug_ref[0] = jnp.concatenate(q_tiles, axis=0).astype(BF16)


def _inproj(x2, shift3, scale3, g_attn, wcat, bf_pad, *, rows_per_mod, prompt_extras, seq_len):
    n, d = x2.shape
    tm = ROW_TILE
    assert n % tm == 0
    nt = n // tm
    mod_rows = shift3.shape[1]
    mod_map = lambda t: ((t * tm) // rows_per_mod, 0, 0)
    row_map = lambda t: (t, 0)
    const2 = lambda t: (0, 0)
    ecols = wcat.shape[1]
    row_bf16 = (jax.ShapeDtypeStruct((n, GROUP_W), BF16), pl.BlockSpec((tm, GROUP_W), row_map))
    scratch = []
    tiles_per_seq = 1
    if prompt_extras:
        assert tm == MOBA_BLOCK and seq_len % tm == 0
        tiles_per_seq = seq_len // tm
        nb = n // seq_len
        t_map = lambda t: (t // tiles_per_seq, 0, t % tiles_per_seq)
        kv = (jax.ShapeDtypeStruct((nb, GROUP_W, seq_len), F32), pl.BlockSpec((1, GROUP_W, tm), t_map))
        lfo = (jax.ShapeDtypeStruct((nb, N_HEADS, seq_len), F32), pl.BlockSpec((1, N_HEADS, tm), t_map))
        kvt_bf16 = (jax.ShapeDtypeStruct((nb, GROUP_W, seq_len), BF16), pl.BlockSpec((1, GROUP_W, tm), t_map))
        outs = [kvt_bf16, kv, kv, lfo, kvt_bf16, kv, kv,
                (jax.ShapeDtypeStruct((n, N_HEADS * LANES), BF16), pl.BlockSpec((tm, N_HEADS * LANES), row_map)),
                (jax.ShapeDtypeStruct((nb, N_HEADS * LANES, seq_len), BF16),
                 pl.BlockSpec((1, N_HEADS * LANES, tm), t_map)),
                (jax.ShapeDtypeStruct((nt, 1, GROUP_W), F32), pl.BlockSpec((1, 1, GROUP_W), lambda t: (t, 0, 0)))]
        outs += [row_bf16, kvt_bf16, row_bf16, kvt_bf16]
        scratch = [pltpu.VMEM((1, LANES), F32)]
    else:
        kv = (jax.ShapeDtypeStruct((n, GROUP_W), F32), pl.BlockSpec((tm, GROUP_W), row_map))
        lfo = (jax.ShapeDtypeStruct((n, N_HEADS), F32), pl.BlockSpec((tm, N_HEADS), row_map))
        outs = [row_bf16, kv, kv, lfo, row_bf16, kv, kv]
    return pl.pallas_call(
        functools.partial(_inproj_kernel, prompt_extras=prompt_extras, tiles_per_seq=tiles_per_seq),
        grid=(nt,),
        in_specs=[pl.BlockSpec((tm, d), row_map),
                  pl.BlockSpec((None, mod_rows, d), mod_map),
                  pl.BlockSpec((None, mod_rows, d), mod_map),
                  pl.BlockSpec((1, d), const2),
                  pl.BlockSpec((d, ecols), const2),
                  pl.BlockSpec((1, LANES), const2)],
        out_specs=[o[1] for o in outs],
        out_shape=[o[0] for o in outs],
        scratch_shapes=scratch,
        compiler_params=_cparams("arbitrary"),
        name="inproj_prompt" if prompt_extras else "inproj_sample",
    )(x2, shift3, scale3, g_attn, wcat, bf_pad)


def _t5_kernel(t5_ref, o_ref, *, offs, sign, scale):
    h = pl.program_id(0)
    rows, cols = o_ref.shape[2], o_ref.shape[3]
    r = lax.broadcasted_iota(I32, (rows, cols), 0)
    c = lax.broadcasted_iota(I32, (rows, cols), 1)
    max_exact = T5_BUCKETS // 2
    for k, off in enumerate(offs):
        rel = jnp.maximum(off + sign * (r - c), 0)
        relf = jnp.maximum(rel, 1).astype(F32)
        large = max_exact + (jnp.log(relf / max_exact) / math.log(T5_MAX_DIST / max_exact)
                             * (T5_BUCKETS - max_exact)).astype(I32)
        large = jnp.minimum(large, T5_BUCKETS - 1)
        bucket = jnp.where(rel < max_exact, rel, large)
        acc = jnp.zeros((rows, cols), F32)
        for b in range(T5_BUCKETS):
            acc = jnp.where(bucket == b, t5_ref[b, h], acc)
        o_ref[0, k] = acc if scale == 1.0 else acc * scale


def _t5_tiles(t5_bias, offs, rows, cols, sign, scale=1.0):
    nh = t5_bias.shape[1]
    return pl.pallas_call(
        functools.partial(_t5_kernel, offs=tuple(offs), sign=sign, scale=scale),
        grid=(nh,),
        in_specs=[pl.BlockSpec(memory_space=pltpu.SMEM)],
        out_specs=pl.BlockSpec((1, len(offs), rows, cols), lambda h: (h, 0, 0, 0)),
        out_shape=jax.ShapeDtypeStruct((nh, len(offs), rows, cols), F32),
        compiler_params=_cparams("arbitrary"),
        name="t5_tiles",
    )(t5_bias)


def _pair_masks(rows):
    lane = lax.broadcasted_iota(I32, (rows, LANES), 1)
    lo = lane < HEAD_DIM
    return lo, jnp.logical_not(lo)


def _masked_qt(qt_ref, h):
    p, e = divmod(h, HEADS_PER_LANE_TILE)
    qt = qt_ref[0, p * LANES:(p + 1) * LANES, :]
    row = lax.broadcasted_iota(I32, qt.shape, 0)
    keep = (row >= HEAD_DIM) if e else (row < HEAD_DIM)
    return jnp.where(keep, qt, jnp.zeros_like(qt))


def _pair_of(h):
    return slice((h // HEADS_PER_LANE_TILE) * LANES, (h // HEADS_PER_LANE_TILE + 1) * LANES)


def _flash_step(lhs_of, rhs_of, vt_ref, adjust, states, tk, tq):
    chunks = [slice(c * K_CHUNK, (c + 1) * K_CHUNK) for c in range(tk // K_CHUNK)]
    for h, (m_ref, _, _, s_scr, _, a_ref) in enumerate(states):
        rhs = rhs_of(h)
        m8 = jnp.full((8, tq), NEG, F32)
        for c, rows in enumerate(chunks):
            s = jnp.dot(lhs_of(h, c), rhs, preferred_element_type=F32)
            if adjust is not None:
                s = adjust(h, c, s)
            s_scr[rows, :] = s
            m8 = jnp.maximum(m8, jnp.max(s.reshape(K_CHUNK // 8, 8, tq), axis=0))
        m_prev = m_ref[...]
        m_new = jnp.maximum(m_prev, jnp.max(m8, axis=0, keepdims=True))
        a_ref[...] = jnp.exp2(m_prev - m_new)
        m_ref[...] = m_new
    for m_ref, _, _, s_scr, p_scr, _ in states:
        m_new = m_ref[...]
        for rows in chunks:
            p_scr[rows, :] = jnp.exp2(s_scr[rows, :] - m_new).astype(BF16)
    ones = jnp.ones((SUM_ROWS, tk), BF16)
    for h, (_, l_ref, acc_ref, _, p_scr, a_ref) in enumerate(states):
        feat = slice(h * HEAD_DIM, (h + 1) * HEAD_DIM)
        vt_sum = jnp.concatenate([vt_ref[0, feat, :], ones], axis=0)
        pv = jnp.dot(vt_sum, p_scr[...], preferred_element_type=F32)
        acc_ref[...] = a_ref[...] * acc_ref[...] + pv[:HEAD_DIM]
        l_ref[...] = a_ref[...] * l_ref[...] + pv[HEAD_DIM:HEAD_DIM + 1]


FLASH_BUFS_PER_HEAD = 6


def _flash_scratch(tq, tk):
    per_head = [pltpu.VMEM((1, tq), F32), pltpu.VMEM((1, tq), F32), pltpu.VMEM((HEAD_DIM, tq), F32),
                pltpu.VMEM((tk, tq), F32), pltpu.VMEM((tk, tq), BF16), pltpu.VMEM((1, tq), F32)]
    return per_head * N_HEADS


def _flash_states(scratch):
    n = FLASH_BUFS_PER_HEAD
    return [scratch[n * h:n * (h + 1)] for h in range(N_HEADS)]


def _flash_init(states):
    for m_ref, l_ref, acc_ref, _, _, _ in states:
        m_ref[...] = jnp.full_like(m_ref, NEG)
        l_ref[...] = jnp.zeros_like(l_ref)
        acc_ref[...] = jnp.zeros_like(acc_ref)


def _flash_finish(o_ref, states):
    o_t = jnp.concatenate([acc_ref[...] / l_ref[...] for _, l_ref, acc_ref, _, _, _ in states], axis=0)
    o_ref[...] = o_t.T


def _chunk_causal(c, tq):
    krow = c * K_CHUNK + lax.broadcasted_iota(I32, (K_CHUNK, tq), 0)
    qcol = lax.broadcasted_iota(I32, (K_CHUNK, tq), 1)
    return krow <= qcol


def _head_rows(x, h, rows_per_head):
    row = lax.broadcasted_iota(I32, x.shape, 0)
    keep = jnp.logical_and(row >= h * rows_per_head, row < (h + 1) * rows_per_head)
    return jnp.where(keep, x, jnp.zeros_like(x))


def _own_half(shape, h, axis):
    pos = lax.broadcasted_iota(I32, shape, axis)
    return (pos >= HEAD_DIM) if h % HEADS_PER_LANE_TILE else (pos < HEAD_DIM)


def _fox_prompt_body(i, j, q_ref, k_ref, vt_ref, fqt_ref, fk_ref, o_ref, states):
    tq, tk = q_ref.shape[2], k_ref.shape[0]

    def lhs_of(h, c):
        rows = slice(c * K_CHUNK, (c + 1) * K_CHUNK)
        k = k_ref[rows, _pair_of(h)]
        return jnp.where(_own_half(k.shape, h, 1), k, fk_ref[rows, h * LANES:(h + 1) * LANES])

    def rhs_of(h):
        qt = q_ref[0, _pair_of(h), :]
        return jnp.where(_own_half(qt.shape, h, 0), qt, fqt_ref[0, h * LANES:(h + 1) * LANES, :])

    def step(diagonal):
        adjust = (lambda h, c, s: jnp.where(_chunk_causal(c, tq), s, NEG)) if diagonal else None
        _flash_step(lhs_of, rhs_of, vt_ref, adjust, states, tk, tq)

    @pl.when(j == i)
    def _():
        _flash_init(states)
        step(True)

    @pl.when(j < i)
    def _():
        step(False)

    @pl.when(jnp.logical_or(j == i - 1, i == 0))
    def _():
        _flash_finish(o_ref, states)


def _tri_tables(nq, own_first):
    qi, kj = [], []
    for i in range(nq):
        order = ([i] + list(range(i))) if own_first else list(range(i + 1))
        for j in order:
            qi.append(i)
            kj.append(j)
    return jnp.asarray(qi, I32), jnp.asarray(kj, I32)


def _moba_prompt_body(i, j, q_ref, k_ref, vt_ref, kmean_ref, bias_ref, o_ref, states, qb_ref):
    tq, tk = q_ref.shape[2], k_ref.shape[0]
    nb = kmean_ref.shape[1]
    nbp = -(-nb // 8) * 8
    assert 3 + nb <= AUG
    far_kind = bias_ref.shape[1] - 1

    def k_chunk(h, c, spare):
        k = k_ref[c * K_CHUNK:(c + 1) * K_CHUNK, _pair_of(h)]
        return jnp.where(_own_half(k.shape, h, 1), k, spare)

    def off_diagonal_step(far):
        lane = lax.broadcasted_iota(I32, (K_CHUNK, LANES), 1)
        key_side = []
        for base in (_spare_half(0), _spare_half(1)):
            slot = lane - base
            ones_at = (slot == 3 + j)
            if far:
                ones_at = jnp.logical_or(ones_at, jnp.logical_and(slot >= 0, slot < 3))
            key_side.append(jnp.where(ones_at, 1.0, 0.0).astype(BF16))
        adjust = None if far else (lambda h, c, s: s + bias_ref[h, 1, c * K_CHUNK:(c + 1) * K_CHUNK, :])
        _flash_step(lambda h, c: k_chunk(h, c, key_side[h % HEADS_PER_LANE_TILE]),
                    lambda h: qb_ref[h], vt_ref, adjust, states, tk, tq)

    @pl.when(j == i)
    def _():
        _flash_init(states)
        blk_row = lax.broadcasted_iota(I32, (nbp, tq), 0)
        aug_row = lax.broadcasted_iota(I32, (AUG, tq), 0)
        past = blk_row < i
        for h in range(N_HEADS):
            km = _pad_rows(kmean_ref[0, :, _pair_of(h)], nbp).astype(BF16)
            g = jnp.dot(km, _masked_qt(q_ref, h), preferred_element_type=F32)
            c_hi, c_mid, c_lo = _split3_bf16(bias_ref[h, far_kind, 0:1, 0:1])
            aug = jnp.where(aug_row == 0, c_hi.astype(F32),
                            jnp.where(aug_row == 1, c_mid.astype(F32),
                                      jnp.where(aug_row == 2, c_lo.astype(F32), 0.0)))
            for jb in range(nb):
                gj = g[jb:jb + 1, :]
                beats = jnp.logical_and(past, jnp.logical_or(g > gj, jnp.logical_and(g == gj, blk_row < jb)))
                rank = jnp.sum(jnp.where(beats, 1.0, 0.0), axis=0, keepdims=True)
                chosen = jnp.logical_and(rank < MOBA_TOPK, jb < i)
                aug = jnp.where(aug_row == 3 + jb, jnp.where(chosen, 0.0, NEG), aug)
            own = q_ref[0, h * HEAD_DIM:(h + 1) * HEAD_DIM, :]
            spare = jnp.concatenate([aug.astype(BF16), jnp.zeros((HEAD_DIM - AUG, tq), BF16)], axis=0)
            qb_ref[h] = jnp.concatenate([spare, own] if h % HEADS_PER_LANE_TILE else [own, spare], axis=0)

        def adjust(h, c, s):
            s = s + bias_ref[h, 0, c * K_CHUNK:(c + 1) * K_CHUNK, :]
            return jnp.where(_chunk_causal(c, tq), s, NEG)

        zero = jnp.zeros((K_CHUNK, LANES), BF16)
        _flash_step(lambda h, c: k_chunk(h, c, zero), lambda h: qb_ref[h], vt_ref, adjust, states, tk, tq)

        @pl.when(i == 0)
        def _():
            _flash_finish(o_ref, states)

    @pl.when(j == i - 1)
    def _():
        off_diagonal_step(far=False)
        _flash_finish(o_ref, states)

    @pl.when(j < i - 1)
    def _():
        off_diagonal_step(far=True)


def _prompt_attn_kernel(qi_ref, kj_ref, fq_ref, fk_ref, fvt_ref, fqt_aug_ref, fk_aug_ref,
                        mq_ref, mk_ref, mvt_ref, kmean_ref, bias_ref, o_fox_ref, o_moba_ref, *scratch):
    n = FLASH_BUFS_PER_HEAD * N_HEADS
    t = pl.program_id(1)
    i = qi_ref[t]
    j = kj_ref[t]
    _fox_prompt_body(i, j, fq_ref, fk_ref, fvt_ref, fqt_aug_ref, fk_aug_ref, o_fox_ref,
                     _flash_states(scratch[:n]))
    _moba_prompt_body(i, j, mq_ref, mk_ref, mvt_ref, kmean_ref, bias_ref, o_moba_ref,
                      _flash_states(scratch[n:2 * n]), scratch[2 * n])


def _prompt_attention(fqt, fk, fvt, fk_aug, fqt_aug, mqt, mk, mvt, kmean, bias, batch, seq):
    assert ATTN_TILE == MOBA_BLOCK
    tq = tk = MOBA_BLOCK
    nq = seq // tq
    qi, kj = _tri_tables(nq, own_first=True)
    qmap = lambda b, t, qi, kj: (b * nq + qi[t], 0)
    kmap = lambda b, t, qi, kj: (b * nq + kj[t], 0)
    qtmap = lambda b, t, qi, kj: (b, 0, qi[t])
    ktmap = lambda b, t, qi, kj: (b, 0, kj[t])
    out = jax.ShapeDtypeStruct((batch * seq, GROUP_W), F32)
    return pl.pallas_call(
        _prompt_attn_kernel,
        grid_spec=pltpu.PrefetchScalarGridSpec(
            num_scalar_prefetch=2,
            grid=(batch, qi.shape[0]),
            in_specs=[pl.BlockSpec((1, GROUP_W, tq), qtmap),
                      pl.BlockSpec((tk, GROUP_W), kmap),
                      pl.BlockSpec((1, GROUP_W, tk), ktmap),
                      pl.BlockSpec((1, N_HEADS * LANES, tq), qtmap),
                      pl.BlockSpec((tk, N_HEADS * LANES), kmap),
                      pl.BlockSpec((1, GROUP_W, tq), qtmap),
                      pl.BlockSpec((tk, GROUP_W), kmap),
                      pl.BlockSpec((1, GROUP_W, tk), ktmap),
                      pl.BlockSpec((1, nq, GROUP_W), lambda b, t, qi, kj: (b, 0, 0)),
                      pl.BlockSpec(bias.shape, lambda b, t, qi, kj: (0, 0, 0, 0))],
            out_specs=[pl.BlockSpec((tq, GROUP_W), qmap), pl.BlockSpec((tq, GROUP_W), qmap)],
            scratch_shapes=(_flash_scratch(tq, tk) + _flash_scratch(tq, tk)
                            + [pltpu.VMEM((N_HEADS, LANES, tq), BF16)])),
        out_shape=[out, out],
        compiler_params=_cparams("arbitrary", "arbitrary"),
        name="prompt_attention",
    )(qi, kj, fqt, fk, fvt, fqt_aug, fk_aug, mqt, mk, mvt, kmean, bias)


def _block_diag_q(q):
    ds = q.shape[0]
    lane_head = lax.broadcasted_iota(I32, (N_HEADS, ds, GROUP_W), 2) // HEAD_DIM
    head = lax.broadcasted_iota(I32, (N_HEADS, ds, GROUP_W), 0)
    q3 = jnp.where(lane_head == head, q.astype(F32)[None, :, :], 0.0)
    return q3.reshape(N_HEADS * ds, GROUP_W).astype(BF16)


def _head_diag_out(acc, ds):
    acc3 = acc.reshape(N_HEADS, ds, GROUP_W)
    lane_head = lax.broadcasted_iota(I32, (N_HEADS, ds, GROUP_W), 2) // HEAD_DIM
    head = lax.broadcasted_iota(I32, (N_HEADS, ds, GROUP_W), 0)
    return jnp.sum(jnp.where(lane_head == head, acc3, 0.0), axis=0)


def _pad_rows(x, rows):
    if x.shape[0] == rows:
        return x
    return jnp.concatenate([x, jnp.zeros((rows - x.shape[0], x.shape[1]), x.dtype)], axis=0)


def _ring_copy(pt_ref, page_index, cache_ref, buf_ref, sem_ref, slot, rr):
    return pltpu.make_async_copy(cache_ref.at[pt_ref[page_index]], buf_ref.at[slot, rr], sem_ref.at[slot])


def _ring_fetch(pt_ref, page_index_of, u, cache_ref, buf_ref, sem_ref):
    slot = u % PAGE_RING
    for rr in range(buf_ref.shape[1]):
        _ring_copy(pt_ref, page_index_of(u, rr), cache_ref, buf_ref, sem_ref, slot, rr).start(priority=rr % 2)


def _ring_wait(pt_ref, page_index_of, u, cache_ref, buf_ref, sem_ref):
    slot = u % PAGE_RING
    for rr in range(buf_ref.shape[1]):
        _ring_copy(pt_ref, page_index_of(u, rr), cache_ref, buf_ref, sem_ref, slot, rr).wait()


def _ring_advance(pt_ref, page_index_of, u, n_fetches, cache_ref, buf_ref, sem_ref):
    @pl.when(u == 0)
    def _():
        for ahead in range(min(PAGE_RING - 1, n_fetches)):
            _ring_fetch(pt_ref, page_index_of, ahead, cache_ref, buf_ref, sem_ref)

    @pl.when(u + (PAGE_RING - 1) < n_fetches)
    def _():
        _ring_fetch(pt_ref, page_index_of, u + (PAGE_RING - 1), cache_ref, buf_ref, sem_ref)

    _ring_wait(pt_ref, page_index_of, u, cache_ref, buf_ref, sem_ref)
    return u % PAGE_RING


def _fox_sample_kernel(pt_ref, q_ref, kn_ref, vn_ref, lfn_ref, *refs, pages, n_pages, n_batch):
    lf_refs = refs[0:pages]
    (k_hbm, v_hbm, o_ref, qbd_ref, m_ref, l_ref, acc_ref, carry_ref, rq_ref,
     kbuf, vbuf, ksem, vsem) = refs[pages:]
    g = pl.program_id(1)
    ng = n_pages // pages
    ds = q_ref.shape[1]
    page = kbuf.shape[3]

    def page_index_of(u, rr):
        return (u // ng) * n_pages + (ng - 1 - u % ng) * pages + (pages - 1 - rr)

    u = pl.program_id(0) * ng + g
    slot = _ring_advance(pt_ref, page_index_of, u, n_batch * ng, k_hbm, kbuf, ksem)
    _ring_advance(pt_ref, page_index_of, u, n_batch * ng, v_hbm, vbuf, vsem)
    lane3 = lax.broadcasted_iota(I32, (N_HEADS, ds, page), 2)
    qidx3 = lax.broadcasted_iota(I32, (N_HEADS, ds, page), 1)
    r = lax.broadcasted_iota(I32, (page, page), 0)
    c = lax.broadcasted_iota(I32, (page, page), 1)
    after = jnp.where(r > c, 1.0, 0.0).astype(BF16)

    def suffix_sum(x):
        hi, mid, lo = _split3_bf16(x)
        return (jnp.dot(hi, after, preferred_element_type=F32) + jnp.dot(mid, after, preferred_element_type=F32)
                + jnp.dot(lo, after, preferred_element_type=F32))

    def biased(s, later):
        s3 = s.reshape(N_HEADS, ds, page) + later[:, None, :] - rq_ref[...].reshape(N_HEADS, ds, 1)
        return s3

    def softmax_update(s, pv_of):
        m_prev = m_ref[...]
        m_new = jnp.maximum(m_prev, jnp.max(s, axis=1, keepdims=True))
        alpha = jnp.exp(m_prev - m_new)
        p = jnp.exp(s - m_new)
        l_ref[...] = alpha * l_ref[...] + jnp.sum(p, axis=1, keepdims=True)
        m_ref[...] = m_new
        acc_ref[...] = alpha * acc_ref[...] + pv_of(p.astype(BF16))

    @pl.when(g == 0)
    def _():
        qbd_ref[...] = _block_diag_q(q_ref[0])
        m_ref[...] = jnp.full_like(m_ref, NEG)
        l_ref[...] = jnp.zeros_like(l_ref)
        acc_ref[...] = jnp.zeros_like(acc_ref)
        x = lfn_ref[0]
        later_new = suffix_sum(x)
        rq3 = jnp.sum(jnp.where(lane3 == qidx3, later_new[:, None, :], 0.0), axis=2, keepdims=True)
        rq_ref[...] = rq3.reshape(N_HEADS * ds, 1)
        kn = _pad_rows(kn_ref[0], page).astype(BF16)
        vn = _pad_rows(vn_ref[0], page).astype(BF16)
        s3 = biased(lax.dot_general(qbd_ref[...], kn, NT, preferred_element_type=F32), later_new)
        s3 = jnp.where(lane3 <= qidx3, s3, NEG)
        softmax_update(s3.reshape(N_HEADS * ds, page), lambda p: jnp.dot(p, vn, preferred_element_type=F32))
        carry_ref[...] = jnp.sum(x, axis=1, keepdims=True)

    xs = [lf_refs[rr][0] for rr in range(pages)]
    within = suffix_sum(jnp.concatenate(xs, axis=0))
    run = carry_ref[...]
    parts = []
    for rr in range(pages):
        later = within[rr * N_HEADS:(rr + 1) * N_HEADS, :] + run
        s = jnp.dot(qbd_ref[...], kbuf[slot, rr].astype(BF16), preferred_element_type=F32)
        parts.append(biased(s, later).reshape(N_HEADS * ds, page))
        run = run + jnp.sum(xs[rr], axis=1, keepdims=True)
    carry_ref[...] = run

    def pv_of(p):
        pv = None
        for rr in range(pages):
            term = lax.dot_general(p[:, rr * page:(rr + 1) * page], vbuf[slot, rr].astype(BF16), NT,
                                   preferred_element_type=F32)
            pv = term if pv is None else pv + term
        return pv

    softmax_update(jnp.concatenate(parts, axis=1), pv_of)

    @pl.when(g == ng - 1)
    def _():
        o_ref[0] = _head_diag_out(acc_ref[...] / l_ref[...], ds)


def _page_ring_scratch(pages, page):
    return [pltpu.VMEM((PAGE_RING, pages, GROUP_W, page), F32), pltpu.SemaphoreType.DMA((PAGE_RING,))]


def _fox_sample(page_table, q3, kn3, vn3, lfn_t, cache_k, cache_v, cache_lf_t):
    db, n_pages = page_table.shape
    ds = q3.shape[1]
    pages = PAGES_PER_STEP
    assert n_pages % pages == 0
    ng = n_pages // pages
    page = cache_k.shape[2]

    def page_map(rr):
        return lambda b, g, pt: (pt[b * n_pages + (ng - 1 - g) * pages + (pages - 1 - rr)], 0, 0)

    bmap = lambda b, g, pt: (b, 0, 0)
    in_specs = [pl.BlockSpec((1, ds, GROUP_W), bmap), pl.BlockSpec((1, ds, GROUP_W), bmap),
                pl.BlockSpec((1, ds, GROUP_W), bmap), pl.BlockSpec((1, N_HEADS, page), bmap)]
    in_specs += [pl.BlockSpec((1, N_HEADS, page), page_map(rr)) for rr in range(pages)]
    in_specs += [pl.BlockSpec(memory_space=pl.ANY), pl.BlockSpec(memory_space=pl.ANY)]
    rows = N_HEADS * ds
    k_ring, k_sem = _page_ring_scratch(pages, page)
    v_ring, v_sem = _page_ring_scratch(pages, page)
    return pl.pallas_call(
        functools.partial(_fox_sample_kernel, pages=pages, n_pages=n_pages, n_batch=db),
        grid_spec=pltpu.PrefetchScalarGridSpec(
            num_scalar_prefetch=1,
            grid=(db, ng),
            in_specs=in_specs,
            out_specs=pl.BlockSpec((1, ds, GROUP_W), bmap),
            scratch_shapes=[pltpu.VMEM((rows, GROUP_W), BF16), pltpu.VMEM((rows, 1), F32),
                            pltpu.VMEM((rows, 1), F32), pltpu.VMEM((rows, GROUP_W), F32),
                            pltpu.VMEM((N_HEADS, 1), F32), pltpu.VMEM((rows, 1), F32),
                            k_ring, v_ring, k_sem, v_sem]),
        out_shape=jax.ShapeDtypeStruct((db, ds, GROUP_W), F32),
        compiler_params=_cparams("arbitrary", "arbitrary"),
        name="fox_sample",
    )(page_table.reshape(-1), q3, kn3, vn3, lfn_t, *([cache_lf_t] * pages), cache_k, cache_v)


def _moba_sample_kernel(pt_ref, q_ref, kn_ref, vn_ref, bias_ref, k_hbm, v_hbm, o_ref,
                        qbd_ref, s_ref, p_ref, pnew_ref, linv_ref, acc_ref, kbuf, vbuf, ksem, vsem,
                        *, pages, n_batch):
    ph = pl.program_id(1)
    g = pl.program_id(2)
    ds = q_ref.shape[1]
    page = kbuf.shape[3]
    n_pages = s_ref.shape[0]
    ng = n_pages // pages
    rows = N_HEADS * ds
    pages_per_block = MOBA_BLOCK // page
    n_blocks = n_pages // pages_per_block
    u = pl.program_id(0) * ng + g

    def page_index_of(u, rr):
        return (u // ng) * n_pages + (u % ng) * pages + rr

    @pl.when(jnp.logical_and(ph == 0, g == 0))
    def _():
        qbd_ref[...] = _block_diag_q(q_ref[0])

    @pl.when(ph == 0)
    def _():
        slot = _ring_advance(pt_ref, page_index_of, u, n_batch * ng, k_hbm, kbuf, ksem)
        for rr in range(pages):
            s_ref[g * pages + rr] = jnp.dot(qbd_ref[...], kbuf[slot, rr].astype(BF16), preferred_element_type=F32)

    @pl.when(jnp.logical_and(ph == 0, g == ng - 1))
    def _():
        lane = lax.broadcasted_iota(I32, (rows, LANES), 1)
        gate = jnp.full((rows, LANES), -jnp.inf, F32)
        for b in range(n_blocks):
            tot = s_ref[b * pages_per_block]
            for u in range(1, pages_per_block):
                tot = tot + s_ref[b * pages_per_block + u]
            gate = jnp.where(lane == b, jnp.sum(tot, axis=1, keepdims=True) * (1.0 / MOBA_BLOCK), gate)
        chosen = []
        for _ in range(min(MOBA_TOPK, n_blocks)):
            mx = jnp.max(gate, axis=1, keepdims=True)
            idx = jnp.min(jnp.where(gate == mx, lane.astype(F32), float(LANES)), axis=1, keepdims=True)
            chosen.append(idx)
            gate = jnp.where(lane.astype(F32) == idx, -jnp.inf, gate)
        lane_n = lax.broadcasted_iota(I32, (N_HEADS, ds, page), 2)
        qidx_n = lax.broadcasted_iota(I32, (N_HEADS, ds, page), 1)
        s_new = lax.dot_general(qbd_ref[...], _pad_rows(kn_ref[0], page).astype(BF16), NT,
                                preferred_element_type=F32) + bias_ref[1]
        s_new = jnp.where((lane_n <= qidx_n).reshape(rows, page), s_new, NEG)
        m_tile = s_new
        for pg in range(n_pages):
            b = float(pg // pages_per_block)
            picked = chosen[0] == b
            for idx in chosen[1:]:
                picked = jnp.logical_or(picked, idx == b)
            bias = bias_ref[0] if pg == n_pages - 1 else bias_ref[2]
            s = jnp.where(picked, s_ref[pg] + bias, NEG)
            s_ref[pg] = s
            m_tile = jnp.maximum(m_tile, s)
        m = jnp.max(m_tile, axis=1, keepdims=True)
        p_new = jnp.exp(s_new - m)
        l_tile = p_new
        pnew_ref[...] = p_new.astype(BF16)
        for pg in range(n_pages):
            p = jnp.exp(s_ref[pg] - m)
            l_tile = l_tile + p
            p_ref[pg] = p.astype(BF16)
        linv_ref[...] = 1.0 / jnp.sum(l_tile, axis=1, keepdims=True)

    @pl.when(jnp.logical_and(ph == 1, g == 0))
    def _():
        acc_ref[...] = jnp.dot(pnew_ref[...], _pad_rows(vn_ref[0], page).astype(BF16),
                               preferred_element_type=F32)

    @pl.when(ph == 1)
    def _():
        slot = _ring_advance(pt_ref, page_index_of, u, n_batch * ng, v_hbm, vbuf, vsem)
        acc = acc_ref[...]
        for rr in range(pages):
            acc = acc + lax.dot_general(p_ref[g * pages + rr], vbuf[slot, rr].astype(BF16), NT,
                                        preferred_element_type=F32)
        acc_ref[...] = acc

    @pl.when(jnp.logical_and(ph == 1, g == ng - 1))
    def _():
        o_ref[0] = _head_diag_out(acc_ref[...] * linv_ref[...], ds)


def _moba_sample(page_table, q3, kn3, vn3, bias3, cache_k, cache_v):
    db, n_pages = page_table.shape
    ds = q3.shape[1]
    pages = PAGES_PER_STEP
    page = cache_k.shape[2]
    assert n_pages % pages == 0 and MOBA_BLOCK % page == 0 and (n_pages * page) % MOBA_BLOCK == 0
    assert T5_MAX_DIST <= page and ds <= page
    ng = n_pages // pages
    rows = N_HEADS * ds
    bmap = lambda b, ph, g, pt: (b, 0, 0)
    in_specs = [pl.BlockSpec((1, ds, GROUP_W), bmap), pl.BlockSpec((1, ds, GROUP_W), bmap),
                pl.BlockSpec((1, ds, GROUP_W), bmap),
                pl.BlockSpec(bias3.shape, lambda b, ph, g, pt: (0, 0, 0)),
                pl.BlockSpec(memory_space=pl.ANY), pl.BlockSpec(memory_space=pl.ANY)]
    k_ring, k_sem = _page_ring_scratch(pages, page)
    v_ring, v_sem = _page_ring_scratch(pages, page)
    return pl.pallas_call(
        functools.partial(_moba_sample_kernel, pages=pages, n_batch=db),
        grid_spec=pltpu.PrefetchScalarGridSpec(
            num_scalar_prefetch=1,
            grid=(db, 2, ng),
            in_specs=in_specs,
            out_specs=pl.BlockSpec((1, ds, GROUP_W), bmap),
            scratch_shapes=[pltpu.VMEM((rows, GROUP_W), BF16),
                            pltpu.VMEM((n_pages, rows, page), F32),
                            pltpu.VMEM((n_pages, rows, page), BF16),
                            pltpu.VMEM((rows, page), BF16),
                            pltpu.VMEM((rows, 1), F32),
                            pltpu.VMEM((rows, GROUP_W), F32),
                            k_ring, v_ring, k_sem, v_sem]),
        out_shape=jax.ShapeDtypeStruct((db, ds, GROUP_W), F32),
        compiler_params=_cparams("arbitrary", "arbitrary", "arbitrary"),
        name="moba_sample",
    )(page_table.reshape(-1), q3, kn3, vn3, bias3, cache_k, cache_v)


def _outproj_kernel(of_ref, om_ref, x_ref, gate_ref, shift_ref, scale_ref, gf_ref, gm_ref, wo_ref, gffn_ref,
                    wr_ref, br_ref, cnt0_ref, x1_ref, h2_ref, ri_ref, rw_ref, cnt_ref, carry_ref):
    t = pl.program_id(0)

    @pl.when(t == 0)
    def _():
        carry_ref[...] = cnt0_ref[...]

    tm = x_ref.shape[0]
    nf = (_rms(of_ref[...]) * gf_ref[...]).astype(BF16)
    nm = (_rms(om_ref[...]) * gm_ref[...]).astype(BF16)
    o = (jnp.dot(nf, wo_ref[0:GROUP_W, :], preferred_element_type=F32)
         + jnp.dot(nm, wo_ref[GROUP_W:2 * GROUP_W, :], preferred_element_type=F32))
    x1 = x_ref[...] + gate_ref[...] * o
    x1_ref[...] = x1
    h2 = _rms(x1) * gffn_ref[...]
    h2 = h2 * (1.0 + scale_ref[...]) + shift_ref[...]
    hi = h2.astype(BF16)
    half = h2.shape[1] // 2
    hi32 = pltpu.bitcast(hi.astype(F32), jnp.uint32)
    h2_ref[...] = jnp.bitwise_or(jnp.right_shift(hi32[:, :half], jnp.uint32(16)),
                                 jnp.bitwise_and(hi32[:, half:], jnp.uint32(0xFFFF0000)))

    lo = (h2 - hi.astype(F32)).astype(BF16)
    wr = wr_ref[...]
    whi = wr.astype(BF16)
    wlo = (wr - whi.astype(F32)).astype(BF16)
    lg = (jnp.dot(hi, whi, preferred_element_type=F32) + jnp.dot(lo, whi, preferred_element_type=F32)
          + jnp.dot(hi, wlo, preferred_element_type=F32)) + br_ref[...]
    lane = lax.broadcasted_iota(I32, (tm, LANES), 1)
    lane_f = lane.astype(F32)
    ninf = -jnp.inf
    is_g = jnp.logical_and(lane >= N_EXPERTS, lane < N_EXPERTS + N_GROUPS)
    glog = jnp.where(is_g, lg, ninf)
    gmax = jnp.max(glog, axis=1, keepdims=True)
    gidx = jnp.min(jnp.where(glog == gmax, lane_f, 2.0 * LANES), axis=1, keepdims=True).astype(I32) - N_EXPERTS
    g_w = 1.0 / jnp.sum(jnp.exp(glog - gmax), axis=1, keepdims=True)
    in_grp = jnp.logical_and(lane >= gidx * EXPERTS_PER_GROUP, lane < (gidx + 1) * EXPERTS_PER_GROUP)
    elog = jnp.where(in_grp, lg, ninf)
    e1 = jnp.max(elog, axis=1, keepdims=True)
    i1 = jnp.min(jnp.where(elog == e1, lane_f, 2.0 * LANES), axis=1, keepdims=True).astype(I32)
    z = jnp.sum(jnp.exp(elog - e1), axis=1, keepdims=True)
    elog2 = jnp.where(lane == i1, ninf, elog)
    e2 = jnp.max(elog2, axis=1, keepdims=True)
    i2 = jnp.min(jnp.where(elog2 == e2, lane_f, 2.0 * LANES), axis=1, keepdims=True).astype(I32)
    p1 = 1.0 / z
    p2 = jnp.exp(e2 - e1) / z
    w1 = g_w * (p1 / (p1 + p2))
    w2 = g_w * (p2 / (p1 + p2))

    a = jnp.where(jnp.logical_or(lane == i1, lane == i2), 1.0, 0.0)
    r = lax.broadcasted_iota(I32, (tm, tm), 0)
    c = lax.broadcasted_iota(I32, (tm, tm), 1)
    before = jnp.where(c < r, 1.0, 0.0).astype(BF16)
    pos = jnp.dot(before, a.astype(BF16), preferred_element_type=F32) + carry_ref[...]
    r1 = jnp.sum(jnp.where(lane == i1, pos, 0.0), axis=1, keepdims=True)
    r2 = jnp.sum(jnp.where(lane == i2, pos, 0.0), axis=1, keepdims=True)
    carry_ref[...] = carry_ref[...] + jnp.sum(a, axis=0, keepdims=True)
    cnt_ref[...] = carry_ref[...]

    ri = jnp.where(lane == 0, i1, 0) + jnp.where(lane == 1, i2, 0)
    ri = ri + jnp.where(lane == 2, r1.astype(I32), 0) + jnp.where(lane == 3, r2.astype(I32), 0)
    ri_ref[...] = ri
    rw_ref[...] = jnp.where(lane == 0, w1, 0.0) + jnp.where(lane == 1, w2, 0.0)


def _outproj(of, om, x2, gate3, shift3, scale3, gf, gm, wo, gffn, wr, br, cnt0, *, rows_per_mod, name):
    n, d = x2.shape
    tm = ROW_TILE
    nt = n // tm
    mod_rows = gate3.shape[1]
    mod_map = lambda t: ((t * tm) // rows_per_mod, 0, 0)
    row_map = lambda t: (t, 0)
    const2 = lambda t: (0, 0)
    return pl.pallas_call(
        _outproj_kernel,
        grid=(nt,),
        in_specs=[pl.BlockSpec((tm, GROUP_W), row_map), pl.BlockSpec((tm, GROUP_W), row_map),
                  pl.BlockSpec((tm, d), row_map),
                  pl.BlockSpec((None, mod_rows, d), mod_map), pl.BlockSpec((None, mod_rows, d), mod_map),
                  pl.BlockSpec((None, mod_rows, d), mod_map),
                  pl.BlockSpec((1, GROUP_W), const2), pl.BlockSpec((1, GROUP_W), const2),
                  pl.BlockSpec((d, d), const2), pl.BlockSpec((1, d), const2),
                  pl.BlockSpec((d, LANES), const2), pl.BlockSpec((1, LANES), const2),
                  pl.BlockSpec((1, LANES), const2)],
        out_specs=[pl.BlockSpec((tm, d), row_map), pl.BlockSpec((tm, d // 2), row_map),
                   pl.BlockSpec((tm, LANES), row_map), pl.BlockSpec((tm, LANES), row_map),
                   pl.BlockSpec((1, LANES), const2)],
        out_shape=[jax.ShapeDtypeStruct((n, d), F32), jax.ShapeDtypeStruct((n, d // 2), jnp.uint32),
                   jax.ShapeDtypeStruct((n, LANES), I32), jax.ShapeDtypeStruct((n, LANES), F32),
                   jax.ShapeDtypeStruct((1, LANES), F32)],
        scratch_shapes=[pltpu.VMEM((1, LANES), F32)],
        compiler_params=_cparams("arbitrary"),
        name=name,
    )(of, om, x2, gate3, shift3, scale3, gf, gm, wo, gffn, wr, br, cnt0)


def _scatter_kernel(dest_ref, h_ref, rows_in_ref, rows_ref, stage_ref, sem, *, n_tiles):
    del rows_in_ref
    tm = h_ref.shape[0]
    t = pl.program_id(0)

    def retire(slot):
        for _ in range(2):
            pltpu.make_async_copy(stage_ref.at[slot], rows_ref.at[pl.ds(0, tm)], sem.at[slot]).wait()

    for slot in range(2):
        @pl.when(t % 2 == slot)
        def _():
            stage_ref[slot] = h_ref[...]
            for r in range(tm):
                for k in range(2):
                    pltpu.make_async_copy(stage_ref.at[slot, pl.ds(r, 1)], rows_ref.at[pl.ds(dest_ref[k, r], 1)],
                                          sem.at[slot]).start(priority=k)

    for slot in range(2):
        @pl.when(jnp.logical_and(t > 0, (t - 1) % 2 == slot))
        def _():
            retire(slot)

        @pl.when(jnp.logical_and(t == n_tiles - 1, t % 2 == slot))
        def _():
            retire(slot)


def _scatter_rows(dest2, h2, rows):
    n, d = h2.shape
    tm = ROW_TILE
    return pl.pallas_call(
        functools.partial(_scatter_kernel, n_tiles=n // tm),
        grid=(n // tm,),
        in_specs=[pl.BlockSpec((2, tm), lambda t: (0, t), memory_space=pltpu.SMEM),
                  pl.BlockSpec((tm, d), lambda t: (t, 0)),
                  pl.BlockSpec(memory_space=pl.ANY)],
        out_specs=pl.BlockSpec(memory_space=pl.ANY),
        out_shape=jax.ShapeDtypeStruct(rows.shape, rows.dtype),
        scratch_shapes=[pltpu.VMEM((2, tm, d), h2.dtype), pltpu.SemaphoreType.DMA((2,))],
        input_output_aliases={2: 0},
        compiler_params=_cparams("arbitrary"),
        name="moe_scatter",
    )(dest2, h2, rows)


def _moe_kernel(be_ref, nu_ref, rows_ref, w1_ref, w3_ref, w2_ref, y_ref, w1b_ref, w3b_ref, w2b_ref,
                rbuf_ref, rsem):
    blk = pl.program_id(0)
    n_used = nu_ref[0]

    def fetch(u):
        start = pl.multiple_of(u * MOE_BLOCK, MOE_BLOCK)
        return pltpu.make_async_copy(rows_ref.at[pl.ds(start, MOE_BLOCK)], rbuf_ref.at[u % ROW_RING],
                                     rsem.at[u % ROW_RING])

    @pl.when(blk == 0)
    def _():
        for ahead in range(ROW_RING - 1):
            @pl.when(ahead < n_used)
            def _():
                fetch(ahead).start()

    @pl.when(blk + (ROW_RING - 1) < n_used)
    def _():
        fetch(blk + (ROW_RING - 1)).start()

    @pl.when(blk < n_used)
    def _():
        prev = be_ref[jnp.maximum(blk - 1, 0)]

        @pl.when(jnp.logical_or(blk == 0, be_ref[blk] != prev))
        def _():
            w1b_ref[...] = w1_ref[0].astype(BF16)
            w3b_ref[...] = w3_ref[0].astype(BF16)
            w2b_ref[...] = w2_ref[0].astype(BF16)

        fetch(blk).wait()
        packed = rbuf_ref[blk % ROW_RING]
        x_lo = pltpu.bitcast(jnp.left_shift(packed, jnp.uint32(16)), F32)
        x_hi = pltpu.bitcast(jnp.bitwise_and(packed, jnp.uint32(0xFFFF0000)), F32)
        x = jnp.concatenate([x_lo, x_hi], axis=1).astype(BF16)
        a = jnp.dot(x, w1b_ref[...], preferred_element_type=F32)
        b = jnp.dot(x, w3b_ref[...], preferred_element_type=F32)
        hm = (a * jax.nn.sigmoid(a)) * b
        y_ref[...] = jnp.dot(hm.astype(BF16), w2b_ref[...], preferred_element_type=F32)

    @pl.when(blk >= nu_ref[0])
    def _():
        y_ref[...] = jnp.zeros_like(y_ref)


def _moe_experts(block_e, n_used, rows, w1, w3, w2):
    p = rows.shape[0]
    d, de = w1.shape[1], w1.shape[2]
    assert rows.shape[1] * 2 == d
    nblk = p // MOE_BLOCK
    return pl.pallas_call(
        _moe_kernel,
        grid_spec=pltpu.PrefetchScalarGridSpec(
            num_scalar_prefetch=2,
            grid=(nblk,),
            in_specs=[pl.BlockSpec(memory_space=pl.ANY),
                      pl.BlockSpec((1, d, de), lambda b, be, nu: (be[b], 0, 0)),
                      pl.BlockSpec((1, d, de), lambda b, be, nu: (be[b], 0, 0)),
                      pl.BlockSpec((1, de, d), lambda b, be, nu: (be[b], 0, 0))],
            out_specs=pl.BlockSpec((MOE_BLOCK, d), lambda b, be, nu: (b, 0)),
            scratch_shapes=[pltpu.VMEM((d, de), BF16), pltpu.VMEM((d, de), BF16), pltpu.VMEM((de, d), BF16),
                            pltpu.VMEM((ROW_RING, MOE_BLOCK, d // 2), jnp.uint32),
                            pltpu.SemaphoreType.DMA((ROW_RING,))]),
        out_shape=jax.ShapeDtypeStruct((p, d), F32),
        compiler_params=_cparams("arbitrary"),
        name="moe_experts",
    )(block_e, n_used, rows, w1, w3, w2)


def _combine_kernel(dest_ref, dest_next_ref, x1_ref, gate_ref, rw_ref, gfin_ref, y_ref, o_ref, ybuf_ref, sem):
    tm = x1_ref.shape[0]
    t = pl.program_id(0)

    def gather(idx_ref, slot):
        for r in range(tm):
            for k in range(2):
                pltpu.make_async_copy(y_ref.at[pl.ds(idx_ref[k, r], 1)], ybuf_ref.at[slot, k, pl.ds(r, 1)],
                                      sem.at[slot]).start(priority=k)

    @pl.when(t == 0)
    def _():
        gather(dest_ref, 0)

    for slot in range(2):
        @pl.when(jnp.logical_and(t + 1 < pl.num_programs(0), (t + 1) % 2 == slot))
        def _():
            gather(dest_next_ref, slot)

    cur = t % 2
    for k in range(2):
        pltpu.make_async_copy(y_ref.at[pl.ds(0, tm)], ybuf_ref.at[cur, k], sem.at[cur]).wait()
    rw = rw_ref[...]
    moe = rw[:, 0:1] * ybuf_ref[cur, 0] + rw[:, 1:2] * ybuf_ref[cur, 1]
    xo = x1_ref[...] + gate_ref[...] * moe
    o_ref[...] = _rms(xo) * gfin_ref[...]


def _combine(dest2, x1, gate3, rw, gfin, y, *, rows_per_mod, name):
    n, d = x1.shape
    tm = ROW_TILE
    nt = n // tm
    mod_rows = gate3.shape[1]
    return pl.pallas_call(
        _combine_kernel,
        grid=(nt,),
        in_specs=[pl.BlockSpec((2, tm), lambda t: (0, t), memory_space=pltpu.SMEM),
                  pl.BlockSpec((2, tm), lambda t: (0, jnp.minimum(t + 1, nt - 1)), memory_space=pltpu.SMEM),
                  pl.BlockSpec((tm, d), lambda t: (t, 0)),
                  pl.BlockSpec((None, mod_rows, d), lambda t: ((t * tm) // rows_per_mod, 0, 0)),
                  pl.BlockSpec((tm, LANES), lambda t: (t, 0)),
                  pl.BlockSpec((1, d), lambda t: (0, 0)),
                  pl.BlockSpec(memory_space=pl.ANY)],
        out_specs=pl.BlockSpec((tm, d), lambda t: (t, 0)),
        out_shape=jax.ShapeDtypeStruct((n, d), F32),
        scratch_shapes=[pltpu.VMEM((2, 2, tm, d), F32), pltpu.SemaphoreType.DMA((2,))],
        compiler_params=_cparams("arbitrary"),
        name=name,
    )(dest2, dest2, x1, gate3, rw, gfin, y)


def kernel(x_prompt, x_sample, cache_fox_k, cache_fox_v, cache_fox_logf, cache_moba_k, cache_moba_v, page_table, c_prompt, c_sample, w_ada, b_ada, g_attn, w_in, b_forget, g_out_fox, g_out_moba, t5_bias, w_out, g_ffn, w_router_group, b_router_group, w_router_expert, b_router_expert, w1, w3, w2, g_final):
    bsz, seq, d = x_prompt.shape
    db, ds, _ = x_sample.shape
    depth = w_ada.shape[0]
    n_phys, page = cache_fox_k.shape[1], cache_fox_k.shape[2]
    assert depth == 1, "one trunk layer"
    assert seq % ROW_TILE == 0 and (db * ds) % ROW_TILE == 0 and ROW_TILE % ds == 0
    assert N_EXPERTS + N_GROUPS <= LANES and N_HEADS <= LANES
    l = 0
    n_p, n_s = bsz * seq, db * ds
    xp2 = x_prompt.reshape(n_p, d)
    xs2 = x_sample.reshape(n_s, d)

    mod = _ada(jnp.concatenate([c_prompt, c_sample], axis=0), w_ada[l], b_ada[l])
    mod_p = [mod[:bsz, i * d:(i + 1) * d].reshape(bsz, 1, d) for i in range(6)]
    mod_s = [jnp.repeat(mod[bsz:, i * d:(i + 1) * d], ds, axis=0).reshape(n_s // ROW_TILE, ROW_TILE, d)
             for i in range(6)]

    w = GROUP_W
    wl = w_in[l]
    wcat = jnp.concatenate([wl[:, :3 * w], wl[:, 3 * w + N_HEADS:],
                            wl[:, 3 * w:3 * w + N_HEADS], jnp.zeros((d, LANES - N_HEADS), F32)],
                           axis=1).astype(BF16)
    bf_pad = jnp.pad(b_forget[l], (0, LANES - N_HEADS)).reshape(1, LANES)
    g_attn2 = g_attn[l].reshape(1, d)

    fqt, fk_t, fv_t, lf_t, mqt, mk_t, mv_t, fk_aug, fqt_aug, kmean, fkb, fvtb, mkb, mvtb = _inproj(
        xp2, mod_p[0], mod_p[1], g_attn2, wcat, bf_pad, rows_per_mod=seq, prompt_extras=True, seq_len=seq)
    sfq, sfk, sfv, slf, smq, smk, smv = _inproj(
        xs2, mod_s[0], mod_s[1], g_attn2, wcat, bf_pad, rows_per_mod=ROW_TILE, prompt_extras=False, seq_len=ds)

    bias_p = _t5_tiles(t5_bias, (0, MOBA_BLOCK, 2 * MOBA_BLOCK), MOBA_BLOCK, MOBA_BLOCK, sign=-1, scale=LOG2E)
    o_fox_p, o_moba_p = _prompt_attention(fqt, fkb, fvtb, fk_aug, fqt_aug, mqt, mkb, mvtb,
                                          kmean.reshape(bsz, seq // MOBA_BLOCK, w), bias_p, bsz, seq)

    to3 = lambda a: a.reshape(db, ds, w)
    page_t = lambda c: c[l].transpose(0, 2, 3, 1).reshape(n_phys, w, page)
    cfk, cfv, cmk, cmv = page_t(cache_fox_k), page_t(cache_fox_v), page_t(cache_moba_k), page_t(cache_moba_v)
    clf_t = cache_fox_logf[l].transpose(0, 2, 1)
    slf_t = jnp.pad(slf.reshape(db, ds, N_HEADS).transpose(0, 2, 1), ((0, 0), (0, 0), (0, page - ds)))
    o_fox_s = _fox_sample(page_table, to3(sfq), to3(sfk), to3(sfv), slf_t, cfk, cfv, clf_t)
    bias_s = _t5_tiles(t5_bias, (page, 0, 2 * T5_MAX_DIST + page), ds, page, sign=1)
    bias_s = bias_s.transpose(1, 0, 2, 3).reshape(3, N_HEADS * ds, page)
    o_moba_s = _moba_sample(page_table, to3(smq), to3(smk), to3(smv), bias_s, cmk, cmv)

    wr = jnp.concatenate([w_router_expert[l], w_router_group[l],
                          jnp.zeros((d, LANES - N_EXPERTS - N_GROUPS), F32)], axis=1)
    br = jnp.concatenate([b_router_expert[l], b_router_group[l],
                          jnp.zeros((LANES - N_EXPERTS - N_GROUPS,), F32)]).reshape(1, LANES)
    wo = w_out[l].astype(BF16)
    gf, gm, gffn = g_out_fox[l].reshape(1, w), g_out_moba[l].reshape(1, w), g_ffn[l].reshape(1, d)
    x1_p, h2_p, ri_p, rw_p, cnt_p = _outproj(
        o_fox_p, o_moba_p, xp2, mod_p[2], mod_p[3], mod_p[4], gf, gm, wo, gffn, wr, br,
        jnp.zeros((1, LANES), F32), rows_per_mod=seq, name="outproj_prompt")
    x1_s, h2_s, ri_s, rw_s, cnt = _outproj(
        o_fox_s.reshape(n_s, w), o_moba_s.reshape(n_s, w), xs2, mod_s[2], mod_s[3], mod_s[4], gf, gm, wo, gffn,
        wr, br, cnt_p, rows_per_mod=ROW_TILE, name="outproj_sample")

    counts = cnt[0, :N_EXPERTS].astype(I32)
    padded = (counts + MOE_BLOCK - 1) // MOE_BLOCK * MOE_BLOCK
    pend = jnp.cumsum(padded)
    pstart = pend - padded
    n_asg = 2 * (n_p + n_s)
    n_blocks = -(-(n_asg + N_EXPERTS * (MOE_BLOCK - 1)) // MOE_BLOCK)
    blk_start = jnp.arange(n_blocks, dtype=I32) * MOE_BLOCK
    block_e = jnp.clip(jnp.sum((pend[None, :] <= blk_start[:, None]).astype(I32), axis=1), 0, N_EXPERTS - 1)
    n_used = (pend[-1:] // MOE_BLOCK).astype(I32)

    def dest_of(ri):
        hit = ri[:, 0:2, None] == jnp.arange(N_EXPERTS, dtype=I32)
        return (jnp.sum(jnp.where(hit, pstart, 0), axis=-1) + ri[:, 2:4]).T.astype(I32)

    dest_p, dest_s = dest_of(ri_p), dest_of(ri_s)
    rows = jnp.zeros((n_blocks * MOE_BLOCK, d // 2), jnp.uint32)
    rows = _scatter_rows(dest_p, h2_p, rows)
    rows = _scatter_rows(dest_s, h2_s, rows)
    y = _moe_experts(block_e, n_used, rows, w1[l], w3[l], w2[l])
    gfin = g_final.reshape(1, d)
    y_prompt = _combine(dest_p, x1_p, mod_p[5], rw_p, gfin, y, rows_per_mod=seq, name="combine_prompt")
    y_sample = _combine(dest_s, x1_s, mod_s[5], rw_s, gfin, y, rows_per_mod=ROW_TILE, name="combine_sample")

    hp = lambda a: a.reshape(depth, bsz, N_HEADS, HEAD_DIM, seq).transpose(0, 1, 4, 2, 3)
    hs = lambda a: a.reshape(depth, db, ds, N_HEADS, HEAD_DIM)
    return (y_prompt.reshape(bsz, seq, d), y_sample.reshape(db, ds, d),
            hp(fk_t), hp(fv_t), lf_t.reshape(depth, bsz, N_HEADS, seq).transpose(0, 1, 3, 2), hp(mk_t), hp(mv_t),
            hs(sfk), hs(sfv), slf.reshape(depth, db, ds, N_HEADS), hs(smk), hs(smv))
```

```python
import functools
import math

import jax
import jax.numpy as jnp
from jax import lax
from jax.experimental import pallas as pl
from jax.experimental.pallas import tpu as pltpu

F32 = jnp.float32
BF16 = jnp.bfloat16
I32 = jnp.int32

HEAD_DIM = 64
N_HEADS = 8
GROUP_W = N_HEADS * HEAD_DIM
LANES = 128
HEADS_PER_LANE_TILE = LANES // HEAD_DIM
N_PAIRS = N_HEADS // HEADS_PER_LANE_TILE
SM_SCALE = HEAD_DIM ** -0.5
LOG2E = math.log2(math.e)
MOBA_BLOCK = 256
MOBA_TOPK = 3
T5_BUCKETS = 32
T5_MAX_DIST = 128
N_GROUPS = 4
EXPERTS_PER_GROUP = 8
N_EXPERTS = N_GROUPS * EXPERTS_PER_GROUP
RMS_EPS = 1e-6
ROW_TILE = 256
ATTN_TILE = 256
K_CHUNK = 64
AUG = LANES // N_HEADS
SUM_ROWS = 16
MOE_BLOCK = 256
PAGES_PER_STEP = 16
PAGE_RING = 3
ROW_RING = 3
NEG = -1e30
VMEM_LIMIT = 48 * 1024 * 1024
PROMPT_ATTN_VMEM_LIMIT = 58 * 1024 * 1024
HIGHEST = lax.Precision.HIGHEST
NT = (((1,), (1,)), ((), ()))


def _cparams(*sem):
    return pltpu.CompilerParams(dimension_semantics=sem, vmem_limit_bytes=VMEM_LIMIT)


def _rms(x):
    return x * lax.rsqrt(jnp.mean(x * x, axis=-1, keepdims=True) + RMS_EPS)


def _spare_half(h):
    return 0 if h % HEADS_PER_LANE_TILE else HEAD_DIM


def _split3_bf16(x):
    hi = x.astype(BF16)
    r1 = x - hi.astype(F32)
    mid = r1.astype(BF16)
    lo = (r1 - mid.astype(F32)).astype(BF16)
    return hi, mid, lo


def _ada_kernel(c_ref, w_ref, b_ref, o_ref):
    c = c_ref[...]
    a = c * jax.nn.sigmoid(c)
    o_ref[...] = jnp.dot(a, w_ref[...], preferred_element_type=F32, precision=HIGHEST) + b_ref[...]


def _ada(c, w, b):
    n, d = c.shape
    e = w.shape[1]
    tn = 1024
    return pl.pallas_call(
        _ada_kernel,
        grid=(e // tn,),
        in_specs=[pl.BlockSpec((n, d), lambda j: (0, 0)),
                  pl.BlockSpec((d, tn), lambda j: (0, j)),
                  pl.BlockSpec((1, tn), lambda j: (0, j))],
        out_specs=pl.BlockSpec((n, tn), lambda j: (0, j)),
        out_shape=jax.ShapeDtypeStruct((n, e), F32),
        compiler_params=_cparams("arbitrary"),
        name="ada",
    )(c, w, b.reshape(1, e))


def _inproj_kernel(x_ref, shift_ref, scale_ref, g_ref, w_ref, bf_ref,
                   fq_ref, fk_ref, fv_ref, lf_ref, mq_ref, mk_ref, mv_ref, *rest,
                   prompt_extras, tiles_per_seq):
    x = x_ref[...]
    h = _rms(x) * g_ref[...]
    h = h * (1.0 + scale_ref[...]) + shift_ref[...]
    z = jnp.dot(h.astype(BF16), w_ref[...], preferred_element_type=F32)
    w = GROUP_W
    q_scale = SM_SCALE * LOG2E if prompt_extras else SM_SCALE
    fq = z[:, 0:w] * q_scale
    mq = z[:, 3 * w:4 * w] * q_scale
    if prompt_extras:
        fq_ref[0] = fq.T.astype(BF16)
        mq_ref[0] = mq.T.astype(BF16)
    else:
        fq_ref[...] = fq.astype(BF16)
        mq_ref[...] = mq.astype(BF16)
    fk, fv = z[:, w:2 * w], z[:, 2 * w:3 * w]
    mk, mv = z[:, 4 * w:5 * w], z[:, 5 * w:6 * w]
    fg = z[:, 6 * w:6 * w + LANES] + bf_ref[...]
    lf = jnp.minimum(fg, 0.0) - jnp.log1p(jnp.exp(-jnp.abs(fg)))
    if not prompt_extras:
        fk_ref[...] = fk
        fv_ref[...] = fv
        mk_ref[...] = mk
        mv_ref[...] = mv
        lf_ref[...] = lf[:, :N_HEADS]
        return
    fkaug_ref, fqtaug_ref, kmean_ref, fkb_ref, fvtb_ref, mkb_ref, mvtb_ref, carry_ref = rest
    tm = x.shape[0]
    fk_ref[0] = fk.T
    fv_t = fv.T
    fv_ref[0] = fv_t
    mk_ref[0] = mk.T
    mv_t = mv.T
    mv_ref[0] = mv_t
    fkb_ref[...] = fk.astype(BF16)
    fvtb_ref[0] = fv_t.astype(BF16)
    mkb_ref[...] = mk.astype(BF16)
    mvtb_ref[0] = mv_t.astype(BF16)
    lf_ref[0] = lf.T[:N_HEADS, :]
    kmean_ref[0] = jnp.mean(mk, axis=0, keepdims=True)

    @pl.when(pl.program_id(0) % tiles_per_seq == 0)
    def _():
        carry_ref[...] = jnp.zeros_like(carry_ref)

    r = lax.broadcasted_iota(I32, (tm, tm), 0)
    c = lax.broadcasted_iota(I32, (tm, tm), 1)
    tri = jnp.where(c <= r, 1.0, 0.0).astype(BF16)
    hi, mid, lo = _split3_bf16(lf)
    cs = (jnp.dot(tri, hi, preferred_element_type=F32)
          + jnp.dot(tri, mid, preferred_element_type=F32)
          + jnp.dot(tri, lo, preferred_element_type=F32)) + carry_ref[...]
    carry_ref[...] = cs[tm - 1:tm, :]
    hi, mid, lo = _split3_bf16(cs * LOG2E)
    er = lax.broadcasted_iota(I32, (LANES, LANES), 0)
    ec = lax.broadcasted_iota(I32, (LANES, LANES), 1)

    def spread(x, off):
        e = jnp.where(jnp.logical_and(er < N_HEADS, ec == AUG * er + off), 1.0, 0.0).astype(BF16)
        return jnp.dot(x, e, preferred_element_type=F32)

    lane = lax.broadcasted_iota(I32, (tm, LANES), 1)
    slot = lane % AUG
    fq_aug = (spread(hi, 0) + spread(mid, 1) + spread(lo, 2)
              + jnp.where(jnp.logical_and(slot >= 3, slot < 6), 1.0, 0.0))
    fk_aug = jnp.where(slot < 3, 1.0, 0.0) - (spread(hi, 3) + spread(mid, 4) + spread(lo, 5))
    fq_aug_t = fq_aug.T
    k_tiles, q_tiles = [], []
    for hd in range(N_HEADS):
        base = _spare_half(hd)
        shift = (base - AUG * hd) % LANES
        moved = pltpu.roll(fk_aug, shift, 1) if shift else fk_aug
        k_tiles.append(jnp.where(jnp.logical_and(lane >= base, lane < base + AUG), moved, 0.0))
        q_rows = [fq_aug_t[AUG * hd:AUG * (hd + 1), :]]
        if base:
            q_rows.insert(0, jnp.zeros((base, tm), F32))
        if LANES - base - AUG:
            q_rows.append(jnp.zeros((LANES - base - AUG, tm), F32))
        q_tiles.append(jnp.concatenate(q_rows, axis=0))
    fkaug_ref[...] = jnp.concatenate(k_tiles, axis=1).astype(BF16)
    fqtaug_ref[0] = jnp.concatenate(q_tiles, axis=0).astype(BF16)


def _inproj(x2, shift3, scale3, g_attn, wcat, bf_pad, *, rows_per_mod, prompt_extras, seq_len):
    n, d = x2.shape
    tm = ROW_TILE
    assert n % tm == 0
    nt = n // tm
    mod_rows = shift3.shape[1]
    mod_map = lambda t: ((t * tm) // rows_per_mod, 0, 0)
    row_map = lambda t: (t, 0)
    const2 = lambda t: (0, 0)
    ecols = wcat.shape[1]
    row_bf16 = (jax.ShapeDtypeStruct((n, GROUP_W), BF16), pl.BlockSpec((tm, GROUP_W), row_map))
    scratch = []
    tiles_per_seq = 1
    if prompt_extras:
        assert tm == MOBA_BLOCK and seq_len % tm == 0
        tiles_per_seq = seq_len // tm
        nb = n // seq_len
        t_map = lambda t: (t // tiles_per_seq, 0, t % tiles_per_seq)
        kv = (jax.ShapeDtypeStruct((nb, GROUP_W, seq_len), F32), pl.BlockSpec((1, GROUP_W, tm), t_map))
        lfo = (jax.ShapeDtypeStruct((nb, N_HEADS, seq_len), F32), pl.BlockSpec((1, N_HEADS, tm), t_map))
        kvt_bf16 = (jax.ShapeDtypeStruct((nb, GROUP_W, seq_len), BF16), pl.BlockSpec((1, GROUP_W, tm), t_map))
        outs = [kvt_bf16, kv, kv, lfo, kvt_bf16, kv, kv,
                (jax.ShapeDtypeStruct((n, N_HEADS * LANES), BF16), pl.BlockSpec((tm, N_HEADS * LANES), row_map)),
                (jax.ShapeDtypeStruct((nb, N_HEADS * LANES, seq_len), BF16),
                 pl.BlockSpec((1, N_HEADS * LANES, tm), t_map)),
                (jax.ShapeDtypeStruct((nt, 1, GROUP_W), F32), pl.BlockSpec((1, 1, GROUP_W), lambda t: (t, 0, 0)))]
        outs += [row_bf16, kvt_bf16, row_bf16, kvt_bf16]
        scratch = [pltpu.VMEM((1, LANES), F32)]
    else:
        kv = (jax.ShapeDtypeStruct((n, GROUP_W), F32), pl.BlockSpec((tm, GROUP_W), row_map))
        lfo = (jax.ShapeDtypeStruct((n, N_HEADS), F32), pl.BlockSpec((tm, N_HEADS), row_map))
        outs = [row_bf16, kv, kv, lfo, row_bf16, kv, kv]
    return pl.pallas_call(
        functools.partial(_inproj_kernel, prompt_extras=prompt_extras, tiles_per_seq=tiles_per_seq),
        grid=(nt,),
        in_specs=[pl.BlockSpec((tm, d), row_map),
                  pl.BlockSpec((None, mod_rows, d), mod_map),
                  pl.BlockSpec((None, mod_rows, d), mod_map),
                  pl.BlockSpec((1, d), const2),
                  pl.BlockSpec((d, ecols), const2),
                  pl.BlockSpec((1, LANES), const2)],
        out_specs=[o[1] for o in outs],
        out_shape=[o[0] for o in outs],
        scratch_shapes=scratch,
        compiler_params=_cparams("arbitrary"),
        name="inproj_prompt" if prompt_extras else "inproj_sample",
    )(x2, shift3, scale3, g_attn, wcat, bf_pad)


def _t5_kernel(t5_ref, o_ref, *, offs, sign, scale):
    h = pl.program_id(0)
    rows, cols = o_ref.shape[2], o_ref.shape[3]
    r = lax.broadcasted_iota(I32, (rows, cols), 0)
    c = lax.broadcasted_iota(I32, (rows, cols), 1)
    max_exact = T5_BUCKETS // 2
    for k, off in enumerate(offs):
        rel = jnp.maximum(off + sign * (r - c), 0)
        relf = jnp.maximum(rel, 1).astype(F32)
        large = max_exact + (jnp.log(relf / max_exact) / math.log(T5_MAX_DIST / max_exact)
                             * (T5_BUCKETS - max_exact)).astype(I32)
        large = jnp.minimum(large, T5_BUCKETS - 1)
        bucket = jnp.where(rel < max_exact, rel, large)
        acc = jnp.zeros((rows, cols), F32)
        for b in range(T5_BUCKETS):
            acc = jnp.where(bucket == b, t5_ref[b, h], acc)
        o_ref[0, k] = acc if scale == 1.0 else acc * scale


def _t5_tiles(t5_bias, offs, rows, cols, sign, scale=1.0):
    nh = t5_bias.shape[1]
    return pl.pallas_call(
        functools.partial(_t5_kernel, offs=tuple(offs), sign=sign, scale=scale),
        grid=(nh,),
        in_specs=[pl.BlockSpec(memory_space=pltpu.SMEM)],
        out_specs=pl.BlockSpec((1, len(offs), rows, cols), lambda h: (h, 0, 0, 0)),
        out_shape=jax.ShapeDtypeStruct((nh, len(offs), rows, cols), F32),
        compiler_params=_cparams("arbitrary"),
        name="t5_tiles",
    )(t5_bias)


def _pair_masks(rows):
    lane = lax.broadcasted_iota(I32, (rows, LANES), 1)
    lo = lane < HEAD_DIM
    return lo, jnp.logical_not(lo)


def _masked_qt(qt_ref, h):
    p, e = divmod(h, HEADS_PER_LANE_TILE)
    qt = qt_ref[0, p * LANES:(p + 1) * LANES, :]
    row = lax.broadcasted_iota(I32, qt.shape, 0)
    keep = (row >= HEAD_DIM) if e else (row < HEAD_DIM)
    return jnp.where(keep, qt, jnp.zeros_like(qt))


def _pair_of(h):
    return slice((h // HEADS_PER_LANE_TILE) * LANES, (h // HEADS_PER_LANE_TILE + 1) * LANES)


def _flash_step(lhs_of, rhs_of, vt_of, adjust, states, tk, tq):
    chunks = [slice(c * K_CHUNK, (c + 1) * K_CHUNK) for c in range(tk // K_CHUNK)]
    for h, (m_ref, _, _, s_scr, _, a_ref) in enumerate(states):
        rhs = rhs_of(h)
        m8 = jnp.full((8, tq), NEG, F32)
        for c, rows in enumerate(chunks):
            s = jnp.dot(lhs_of(h, c), rhs, preferred_element_type=F32)
            if adjust is not None:
                s = adjust(h, c, s)
            s_scr[rows, :] = s
            m8 = jnp.maximum(m8, jnp.max(s.reshape(K_CHUNK // 8, 8, tq), axis=0))
        m_prev = m_ref[...]
        m_new = jnp.maximum(m_prev, jnp.max(m8, axis=0, keepdims=True))
        a_ref[...] = jnp.exp2(m_prev - m_new)
        m_ref[...] = m_new
    for m_ref, _, _, s_scr, p_scr, _ in states:
        m_new = m_ref[...]
        for rows in chunks:
            p_scr[rows, :] = jnp.exp2(s_scr[rows, :] - m_new).astype(BF16)
    ones = jnp.ones((SUM_ROWS, tk), BF16)
    for h, (_, l_ref, acc_ref, _, p_scr, a_ref) in enumerate(states):
        vt_sum = jnp.concatenate([vt_of(h), ones], axis=0)
        pv = jnp.dot(vt_sum, p_scr[...], preferred_element_type=F32)
        acc_ref[...] = a_ref[...] * acc_ref[...] + pv[:HEAD_DIM]
        l_ref[...] = a_ref[...] * l_ref[...] + pv[HEAD_DIM:HEAD_DIM + 1]


FLASH_BUFS_PER_HEAD = 6


def _flash_scratch(tq, tk):
    per_head = [pltpu.VMEM((1, tq), F32), pltpu.VMEM((1, tq), F32), pltpu.VMEM((HEAD_DIM, tq), F32),
                pltpu.VMEM((tk, tq), F32), pltpu.VMEM((tk, tq), BF16), pltpu.VMEM((1, tq), F32)]
    return per_head * N_HEADS


def _flash_states(scratch):
    n = FLASH_BUFS_PER_HEAD
    return [scratch[n * h:n * (h + 1)] for h in range(N_HEADS)]


def _flash_init(states):
    for m_ref, l_ref, acc_ref, _, _, _ in states:
        m_ref[...] = jnp.full_like(m_ref, NEG)
        l_ref[...] = jnp.zeros_like(l_ref)
        acc_ref[...] = jnp.zeros_like(acc_ref)


def _flash_finish(o_ref, states):
    o_t = jnp.concatenate([acc_ref[...] / l_ref[...] for _, l_ref, acc_ref, _, _, _ in states], axis=0)
    o_ref[...] = o_t.T


def _chunk_causal(c, tq):
    krow = c * K_CHUNK + lax.broadcasted_iota(I32, (K_CHUNK, tq), 0)
    qcol = lax.broadcasted_iota(I32, (K_CHUNK, tq), 1)
    return krow <= qcol


def _head_rows(x, h, rows_per_head):
    row = lax.broadcasted_iota(I32, x.shape, 0)
    keep = jnp.logical_and(row >= h * rows_per_head, row < (h + 1) * rows_per_head)
    return jnp.where(keep, x, jnp.zeros_like(x))


def _own_half(shape, h, axis):
    pos = lax.broadcasted_iota(I32, shape, axis)
    return (pos >= HEAD_DIM) if h % HEADS_PER_LANE_TILE else (pos < HEAD_DIM)


def _fox_prompt_body(i, j, q_ref, k_ref, vt_ref, fqt_ref, fk_ref, o_ref, states):
    tq = tk = q_ref.shape[2]
    kbase = pl.multiple_of(j * tk, tk)

    def lhs_of(h, c):
        rows = pl.ds(kbase + c * K_CHUNK, K_CHUNK)
        k = k_ref[rows, _pair_of(h)]
        return jnp.where(_own_half(k.shape, h, 1), k, fk_ref[rows, h * LANES:(h + 1) * LANES])

    def rhs_of(h):
        qt = q_ref[0, _pair_of(h), :]
        return jnp.where(_own_half(qt.shape, h, 0), qt, fqt_ref[0, h * LANES:(h + 1) * LANES, :])

    def vt_of(h):
        return vt_ref[0, h * HEAD_DIM:(h + 1) * HEAD_DIM, pl.ds(kbase, tk)]

    def step(diagonal):
        adjust = (lambda h, c, s: jnp.where(_chunk_causal(c, tq), s, NEG)) if diagonal else None
        _flash_step(lhs_of, rhs_of, vt_of, adjust, states, tk, tq)

    @pl.when(j == i)
    def _():
        _flash_init(states)
        step(True)

    @pl.when(j < i)
    def _():
        step(False)

    @pl.when(jnp.logical_or(j == i - 1, i == 0))
    def _():
        _flash_finish(o_ref, states)


def _tri_tables(nq, own_first):
    qi, kj = [], []
    for i in range(nq):
        order = ([i] + list(range(i))) if own_first else list(range(i + 1))
        for j in order:
            qi.append(i)
            kj.append(j)
    return jnp.asarray(qi, I32), jnp.asarray(kj, I32)


def _moba_prompt_body(i, j, q_ref, k_ref, vt_ref, kmean_ref, bias_ref, o_ref, states, qb_ref):
    tq = tk = q_ref.shape[2]
    kbase = pl.multiple_of(j * tk, tk)
    nb = kmean_ref.shape[1]
    nbp = -(-nb // 8) * 8
    assert 3 + nb <= AUG
    far_kind = bias_ref.shape[1] - 1

    def k_chunk(h, c, spare):
        k = k_ref[pl.ds(kbase + c * K_CHUNK, K_CHUNK), _pair_of(h)]
        return jnp.where(_own_half(k.shape, h, 1), k, spare)

    def vt_of(h):
        return vt_ref[0, h * HEAD_DIM:(h + 1) * HEAD_DIM, pl.ds(kbase, tk)]

    def off_diagonal_step(far):
        lane = lax.broadcasted_iota(I32, (K_CHUNK, LANES), 1)
        key_side = []
        for base in (_spare_half(0), _spare_half(1)):
            slot = lane - base
            ones_at = (slot == 3 + j)
            if far:
                ones_at = jnp.logical_or(ones_at, jnp.logical_and(slot >= 0, slot < 3))
            key_side.append(jnp.where(ones_at, 1.0, 0.0).astype(BF16))
        adjust = None if far else (lambda h, c, s: s + bias_ref[h, 1, c * K_CHUNK:(c + 1) * K_CHUNK, :])
        _flash_step(lambda h, c: k_chunk(h, c, key_side[h % HEADS_PER_LANE_TILE]),
                    lambda h: qb_ref[h], vt_of, adjust, states, tk, tq)

    @pl.when(j == i)
    def _():
        _flash_init(states)
        blk_row = lax.broadcasted_iota(I32, (nbp, tq), 0)
        aug_row = lax.broadcasted_iota(I32, (AUG, tq), 0)
        past = blk_row < i
        for h in range(N_HEADS):
            km = _pad_rows(kmean_ref[0, :, _pair_of(h)], nbp).astype(BF16)
            g = jnp.dot(km, _masked_qt(q_ref, h), preferred_element_type=F32)
            c_hi, c_mid, c_lo = _split3_bf16(bias_ref[h, far_kind, 0:1, 0:1])
            aug = jnp.where(aug_row == 0, c_hi.astype(F32),
                            jnp.where(aug_row == 1, c_mid.astype(F32),
                                      jnp.where(aug_row == 2, c_lo.astype(F32), 0.0)))
            for jb in range(nb):
                gj = g[jb:jb + 1, :]
                beats = jnp.logical_and(past, jnp.logical_or(g > gj, jnp.logical_and(g == gj, blk_row < jb)))
                rank = jnp.sum(jnp.where(beats, 1.0, 0.0), axis=0, keepdims=True)
                chosen = jnp.logical_and(rank < MOBA_TOPK, jb < i)
                aug = jnp.where(aug_row == 3 + jb, jnp.where(chosen, 0.0, NEG), aug)
            own = q_ref[0, h * HEAD_DIM:(h + 1) * HEAD_DIM, :]
            spare = jnp.concatenate([aug.astype(BF16), jnp.zeros((HEAD_DIM - AUG, tq), BF16)], axis=0)
            qb_ref[h] = jnp.concatenate([spare, own] if h % HEADS_PER_LANE_TILE else [own, spare], axis=0)

        def adjust(h, c, s):
            s = s + bias_ref[h, 0, c * K_CHUNK:(c + 1) * K_CHUNK, :]
            return jnp.where(_chunk_causal(c, tq), s, NEG)

        zero = jnp.zeros((K_CHUNK, LANES), BF16)
        _flash_step(lambda h, c: k_chunk(h, c, zero), lambda h: qb_ref[h], vt_of, adjust, states, tk, tq)

        @pl.when(i == 0)
        def _():
            _flash_finish(o_ref, states)

    @pl.when(j == i - 1)
    def _():
        off_diagonal_step(far=False)
        _flash_finish(o_ref, states)

    @pl.when(j < i - 1)
    def _():
        off_diagonal_step(far=True)


def _prompt_attn_kernel(qi_ref, kj_ref, fq_ref, fk_ref, fvt_ref, fqt_aug_ref, fk_aug_ref,
                        mq_ref, mk_ref, mvt_ref, kmean_ref, bias_ref, o_fox_ref, o_moba_ref, *scratch):
    n = FLASH_BUFS_PER_HEAD * N_HEADS
    t = pl.program_id(1)
    i = qi_ref[t]
    j = kj_ref[t]
    _fox_prompt_body(i, j, fq_ref, fk_ref, fvt_ref, fqt_aug_ref, fk_aug_ref, o_fox_ref,
                     _flash_states(scratch[:n]))
    _moba_prompt_body(i, j, mq_ref, mk_ref, mvt_ref, kmean_ref, bias_ref, o_moba_ref,
                      _flash_states(scratch[n:2 * n]), scratch[2 * n])


def _prompt_attention(fqt, fk, fvt, fk_aug, fqt_aug, mqt, mk, mvt, kmean, bias, batch, seq):
    assert ATTN_TILE == MOBA_BLOCK
    tq = tk = MOBA_BLOCK
    nq = seq // tq
    qi, kj = _tri_tables(nq, own_first=True)
    qmap = lambda b, t, qi, kj: (b * nq + qi[t], 0)
    qtmap = lambda b, t, qi, kj: (b, 0, qi[t])
    kmap = lambda b, t, qi, kj: (b, 0)
    ktmap = lambda b, t, qi, kj: (b, 0, 0)
    out = jax.ShapeDtypeStruct((batch * seq, GROUP_W), F32)
    return pl.pallas_call(
        _prompt_attn_kernel,
        grid_spec=pltpu.PrefetchScalarGridSpec(
            num_scalar_prefetch=2,
            grid=(batch, qi.shape[0]),
            in_specs=[pl.BlockSpec((1, GROUP_W, tq), qtmap),
                      pl.BlockSpec((seq, GROUP_W), kmap),
                      pl.BlockSpec((1, GROUP_W, seq), ktmap),
                      pl.BlockSpec((1, N_HEADS * LANES, tq), qtmap),
                      pl.BlockSpec((seq, N_HEADS * LANES), kmap),
                      pl.BlockSpec((1, GROUP_W, tq), qtmap),
                      pl.BlockSpec((seq, GROUP_W), kmap),
                      pl.BlockSpec((1, GROUP_W, seq), ktmap),
                      pl.BlockSpec((1, nq, GROUP_W), ktmap),
                      pl.BlockSpec(bias.shape, lambda b, t, qi, kj: (0, 0, 0, 0))],
            out_specs=[pl.BlockSpec((tq, GROUP_W), qmap), pl.BlockSpec((tq, GROUP_W), qmap)],
            scratch_shapes=(_flash_scratch(tq, tk) + _flash_scratch(tq, tk)
                            + [pltpu.VMEM((N_HEADS, LANES, tq), BF16)])),
        out_shape=[out, out],
        compiler_params=pltpu.CompilerParams(dimension_semantics=("arbitrary", "arbitrary"),
                                             vmem_limit_bytes=PROMPT_ATTN_VMEM_LIMIT),
        name="prompt_attention",
    )(qi, kj, fqt, fk, fvt, fqt_aug, fk_aug, mqt, mk, mvt, kmean, bias)


def _block_diag_q(q):
    ds = q.shape[0]
    lane_head = lax.broadcasted_iota(I32, (N_HEADS, ds, GROUP_W), 2) // HEAD_DIM
    head = lax.broadcasted_iota(I32, (N_HEADS, ds, GROUP_W), 0)
    q3 = jnp.where(lane_head == head, q.astype(F32)[None, :, :], 0.0)
    return q3.reshape(N_HEADS * ds, GROUP_W).astype(BF16)


def _head_diag_out(acc, ds):
    acc3 = acc.reshape(N_HEADS, ds, GROUP_W)
    lane_head = lax.broadcasted_iota(I32, (N_HEADS, ds, GROUP_W), 2) // HEAD_DIM
    head = lax.broadcasted_iota(I32, (N_HEADS, ds, GROUP_W), 0)
    return jnp.sum(jnp.where(lane_head == head, acc3, 0.0), axis=0)


def _pad_rows(x, rows):
    if x.shape[0] == rows:
        return x
    return jnp.concatenate([x, jnp.zeros((rows - x.shape[0], x.shape[1]), x.dtype)], axis=0)


def _ring_copy(pt_ref, page_index, cache_ref, buf_ref, sem_ref, slot, rr):
    return pltpu.make_async_copy(cache_ref.at[pt_ref[page_index]], buf_ref.at[slot, rr], sem_ref.at[slot])


def _ring_fetch(pt_ref, page_index_of, u, cache_ref, buf_ref, sem_ref):
    slot = u % PAGE_RING
    for rr in range(buf_ref.shape[1]):
        _ring_copy(pt_ref, page_index_of(u, rr), cache_ref, buf_ref, sem_ref, slot, rr).start(priority=rr % 2)


def _ring_wait(pt_ref, page_index_of, u, cache_ref, buf_ref, sem_ref):
    slot = u % PAGE_RING
    for rr in range(buf_ref.shape[1]):
        _ring_copy(pt_ref, page_index_of(u, rr), cache_ref, buf_ref, sem_ref, slot, rr).wait()


def _ring_advance(pt_ref, page_index_of, u, n_fetches, cache_ref, buf_ref, sem_ref):
    @pl.when(u == 0)
    def _():
        for ahead in range(min(PAGE_RING - 1, n_fetches)):
            _ring_fetch(pt_ref, page_index_of, ahead, cache_ref, buf_ref, sem_ref)

    @pl.when(u + (PAGE_RING - 1) < n_fetches)
    def _():
        _ring_fetch(pt_ref, page_index_of, u + (PAGE_RING - 1), cache_ref, buf_ref, sem_ref)

    _ring_wait(pt_ref, page_index_of, u, cache_ref, buf_ref, sem_ref)
    return u % PAGE_RING


def _fox_sample_kernel(pt_ref, q_ref, kn_ref, vn_ref, lfn_ref, *refs, pages, n_pages, n_batch):
    lf_refs = refs[0:pages]
    (k_hbm, v_hbm, o_ref, qbd_ref, m_ref, l_ref, acc_ref, carry_ref, rq_ref,
     kbuf, vbuf, ksem, vsem) = refs[pages:]
    g = pl.program_id(1)
    ng = n_pages // pages
    ds = q_ref.shape[1]
    page = kbuf.shape[3]

    def page_index_of(u, rr):
        return (u // ng) * n_pages + (ng - 1 - u % ng) * pages + (pages - 1 - rr)

    u = pl.program_id(0) * ng + g
    slot = _ring_advance(pt_ref, page_index_of, u, n_batch * ng, k_hbm, kbuf, ksem)
    _ring_advance(pt_ref, page_index_of, u, n_batch * ng, v_hbm, vbuf, vsem)
    lane3 = lax.broadcasted_iota(I32, (N_HEADS, ds, page), 2)
    qidx3 = lax.broadcasted_iota(I32, (N_HEADS, ds, page), 1)
    r = lax.broadcasted_iota(I32, (page, page), 0)
    c = lax.broadcasted_iota(I32, (page, page), 1)
    after = jnp.where(r > c, 1.0, 0.0).astype(BF16)

    def suffix_sum(x):
        hi, mid, lo = _split3_bf16(x)
        return (jnp.dot(hi, after, preferred_element_type=F32) + jnp.dot(mid, after, preferred_element_type=F32)
                + jnp.dot(lo, after, preferred_element_type=F32))

    def biased(s, later):
        s3 = s.reshape(N_HEADS, ds, page) + later[:, None, :] - rq_ref[...].reshape(N_HEADS, ds, 1)
        return s3

    def softmax_update(s, pv_of):
        m_prev = m_ref[...]
        m_new = jnp.maximum(m_prev, jnp.max(s, axis=1, keepdims=True))
        alpha = jnp.exp(m_prev - m_new)
        p = jnp.exp(s - m_new)
        l_ref[...] = alpha * l_ref[...] + jnp.sum(p, axis=1, keepdims=True)
        m_ref[...] = m_new
        acc_ref[...] = alpha * acc_ref[...] + pv_of(p.astype(BF16))

    @pl.when(g == 0)
    def _():
        qbd_ref[...] = _block_diag_q(q_ref[0])
        m_ref[...] = jnp.full_like(m_ref, NEG)
        l_ref[...] = jnp.zeros_like(l_ref)
        acc_ref[...] = jnp.zeros_like(acc_ref)
        x = lfn_ref[0]
        later_new = suffix_sum(x)
        rq3 = jnp.sum(jnp.where(lane3 == qidx3, later_new[:, None, :], 0.0), axis=2, keepdims=True)
        rq_ref[...] = rq3.reshape(N_HEADS * ds, 1)
        kn = _pad_rows(kn_ref[0], page).astype(BF16)
        vn = _pad_rows(vn_ref[0], page).astype(BF16)
        s3 = biased(lax.dot_general(qbd_ref[...], kn, NT, preferred_element_type=F32), later_new)
        s3 = jnp.where(lane3 <= qidx3, s3, NEG)
        softmax_update(s3.reshape(N_HEADS * ds, page), lambda p: jnp.dot(p, vn, preferred_element_type=F32))
        carry_ref[...] = jnp.sum(x, axis=1, keepdims=True)

    xs = [lf_refs[rr][0] for rr in range(pages)]
    within = suffix_sum(jnp.concatenate(xs, axis=0))
    run = carry_ref[...]
    parts = []
    for rr in range(pages):
        later = within[rr * N_HEADS:(rr + 1) * N_HEADS, :] + run
        s = jnp.dot(qbd_ref[...], kbuf[slot, rr].astype(BF16), preferred_element_type=F32)
        parts.append(biased(s, later).reshape(N_HEADS * ds, page))
        run = run + jnp.sum(xs[rr], axis=1, keepdims=True)
    carry_ref[...] = run

    def pv_of(p):
        pv = None
        for rr in range(pages):
            term = lax.dot_general(p[:, rr * page:(rr + 1) * page], vbuf[slot, rr].astype(BF16), NT,
                                   preferred_element_type=F32)
            pv = term if pv is None else pv + term
        return pv

    softmax_update(jnp.concatenate(parts, axis=1), pv_of)

    @pl.when(g == ng - 1)
    def _():
        o_ref[0] = _head_diag_out(acc_ref[...] / l_ref[...], ds)


def _page_ring_scratch(pages, page):
    return [pltpu.VMEM((PAGE_RING, pages, GROUP_W, page), F32), pltpu.SemaphoreType.DMA((PAGE_RING,))]


def _fox_sample(page_table, q3, kn3, vn3, lfn_t, cache_k, cache_v, cache_lf_t):
    db, n_pages = page_table.shape
    ds = q3.shape[1]
    pages = PAGES_PER_STEP
    assert n_pages % pages == 0
    ng = n_pages // pages
    page = cache_k.shape[2]

    def page_map(rr):
        return lambda b, g, pt: (pt[b * n_pages + (ng - 1 - g) * pages + (pages - 1 - rr)], 0, 0)

    bmap = lambda b, g, pt: (b, 0, 0)
    in_specs = [pl.BlockSpec((1, ds, GROUP_W), bmap), pl.BlockSpec((1, ds, GROUP_W), bmap),
                pl.BlockSpec((1, ds, GROUP_W), bmap), pl.BlockSpec((1, N_HEADS, page), bmap)]
    in_specs += [pl.BlockSpec((1, N_HEADS, page), page_map(rr)) for rr in range(pages)]
    in_specs += [pl.BlockSpec(memory_space=pl.ANY), pl.BlockSpec(memory_space=pl.ANY)]
    rows = N_HEADS * ds
    k_ring, k_sem = _page_ring_scratch(pages, page)
    v_ring, v_sem = _page_ring_scratch(pages, page)
    return pl.pallas_call(
        functools.partial(_fox_sample_kernel, pages=pages, n_pages=n_pages, n_batch=db),
        grid_spec=pltpu.PrefetchScalarGridSpec(
            num_scalar_prefetch=1,
            grid=(db, ng),
            in_specs=in_specs,
            out_specs=pl.BlockSpec((1, ds, GROUP_W), bmap),
            scratch_shapes=[pltpu.VMEM((rows, GROUP_W), BF16), pltpu.VMEM((rows, 1), F32),
                            pltpu.VMEM((rows, 1), F32), pltpu.VMEM((rows, GROUP_W), F32),
                            pltpu.VMEM((N_HEADS, 1), F32), pltpu.VMEM((rows, 1), F32),
                            k_ring, v_ring, k_sem, v_sem]),
        out_shape=jax.ShapeDtypeStruct((db, ds, GROUP_W), F32),
        compiler_params=_cparams("arbitrary", "arbitrary"),
        name="fox_sample",
    )(page_table.reshape(-1), q3, kn3, vn3, lfn_t, *([cache_lf_t] * pages), cache_k, cache_v)


def _moba_sample_kernel(pt_ref, q_ref, kn_ref, vn_ref, bias_ref, k_hbm, v_hbm, o_ref,
                        qbd_ref, s_ref, p_ref, pnew_ref, linv_ref, acc_ref, kbuf, vbuf, ksem, vsem,
                        *, pages, n_batch):
    ph = pl.program_id(1)
    g = pl.program_id(2)
    ds = q_ref.shape[1]
    page = kbuf.shape[3]
    n_pages = s_ref.shape[0]
    ng = n_pages // pages
    rows = N_HEADS * ds
    pages_per_block = MOBA_BLOCK // page
    n_blocks = n_pages // pages_per_block
    u = pl.program_id(0) * ng + g

    def page_index_of(u, rr):
        return (u // ng) * n_pages + (u % ng) * pages + rr

    @pl.when(jnp.logical_and(ph == 0, g == 0))
    def _():
        qbd_ref[...] = _block_diag_q(q_ref[0])

    @pl.when(ph == 0)
    def _():
        slot = _ring_advance(pt_ref, page_index_of, u, n_batch * ng, k_hbm, kbuf, ksem)
        for rr in range(pages):
            s_ref[g * pages + rr] = jnp.dot(qbd_ref[...], kbuf[slot, rr].astype(BF16), preferred_element_type=F32)

    @pl.when(jnp.logical_and(ph == 0, g == ng - 1))
    def _():
        lane = lax.broadcasted_iota(I32, (rows, LANES), 1)
        gate = jnp.full((rows, LANES), -jnp.inf, F32)
        for b in range(n_blocks):
            tot = s_ref[b * pages_per_block]
            for u in range(1, pages_per_block):
                tot = tot + s_ref[b * pages_per_block + u]
            gate = jnp.where(lane == b, jnp.sum(tot, axis=1, keepdims=True) * (1.0 / MOBA_BLOCK), gate)
        chosen = []
        for _ in range(min(MOBA_TOPK, n_blocks)):
            mx = jnp.max(gate, axis=1, keepdims=True)
            idx = jnp.min(jnp.where(gate == mx, lane.astype(F32), float(LANES)), axis=1, keepdims=True)
            chosen.append(idx)
            gate = jnp.where(lane.astype(F32) == idx, -jnp.inf, gate)
        lane_n = lax.broadcasted_iota(I32, (N_HEADS, ds, page), 2)
        qidx_n = lax.broadcasted_iota(I32, (N_HEADS, ds, page), 1)
        s_new = lax.dot_general(qbd_ref[...], _pad_rows(kn_ref[0], page).astype(BF16), NT,
                                preferred_element_type=F32) + bias_ref[1]
        s_new = jnp.where((lane_n <= qidx_n).reshape(rows, page), s_new, NEG)
        m_tile = s_new
        for pg in range(n_pages):
            b = float(pg // pages_per_block)
            picked = chosen[0] == b
            for idx in chosen[1:]:
                picked = jnp.logical_or(picked, idx == b)
            bias = bias_ref[0] if pg == n_pages - 1 else bias_ref[2]
            s = jnp.where(picked, s_ref[pg] + bias, NEG)
            s_ref[pg] = s
            m_tile = jnp.maximum(m_tile, s)
        m = jnp.max(m_tile, axis=1, keepdims=True)
        p_new = jnp.exp(s_new - m)
        l_tile = p_new
        pnew_ref[...] = p_new.astype(BF16)
        for pg in range(n_pages):
            p = jnp.exp(s_ref[pg] - m)
            l_tile = l_tile + p
            p_ref[pg] = p.astype(BF16)
        linv_ref[...] = 1.0 / jnp.sum(l_tile, axis=1, keepdims=True)

    @pl.when(jnp.logical_and(ph == 1, g == 0))
    def _():
        acc_ref[...] = jnp.dot(pnew_ref[...], _pad_rows(vn_ref[0], page).astype(BF16),
                               preferred_element_type=F32)

    @pl.when(ph == 1)
    def _():
        slot = _ring_advance(pt_ref, page_index_of, u, n_batch * ng, v_hbm, vbuf, vsem)
        acc = acc_ref[...]
        for rr in range(pages):
            acc = acc + lax.dot_general(p_ref[g * pages + rr], vbuf[slot, rr].astype(BF16), NT,
                                        preferred_element_type=F32)
        acc_ref[...] = acc

    @pl.when(jnp.logical_and(ph == 1, g == ng - 1))
    def _():
        o_ref[0] = _head_diag_out(acc_ref[...] * linv_ref[...], ds)


def _moba_sample(page_table, q3, kn3, vn3, bias3, cache_k, cache_v):
    db, n_pages = page_table.shape
    ds = q3.shape[1]
    pages = PAGES_PER_STEP
    page = cache_k.shape[2]
    assert n_pages % pages == 0 and MOBA_BLOCK % page == 0 and (n_pages * page) % MOBA_BLOCK == 0
    assert T5_MAX_DIST <= page and ds <= page
    ng = n_pages // pages
    rows = N_HEADS * ds
    bmap = lambda b, ph, g, pt: (b, 0, 0)
    in_specs = [pl.BlockSpec((1, ds, GROUP_W), bmap), pl.BlockSpec((1, ds, GROUP_W), bmap),
                pl.BlockSpec((1, ds, GROUP_W), bmap),
                pl.BlockSpec(bias3.shape, lambda b, ph, g, pt: (0, 0, 0)),
                pl.BlockSpec(memory_space=pl.ANY), pl.BlockSpec(memory_space=pl.ANY)]
    k_ring, k_sem = _page_ring_scratch(pages, page)
    v_ring, v_sem = _page_ring_scratch(pages, page)
    return pl.pallas_call(
        functools.partial(_moba_sample_kernel, pages=pages, n_batch=db),
        grid_spec=pltpu.PrefetchScalarGridSpec(
            num_scalar_prefetch=1,
            grid=(db, 2, ng),
            in_specs=in_specs,
            out_specs=pl.BlockSpec((1, ds, GROUP_W), bmap),
            scratch_shapes=[pltpu.VMEM((rows, GROUP_W), BF16),
                            pltpu.VMEM((n_pages, rows, page), F32),
                            pltpu.VMEM((n_pages, rows, page), BF16),
                            pltpu.VMEM((rows, page), BF16),
                            pltpu.VMEM((rows, 1), F32),
                            pltpu.VMEM((rows, GROUP_W), F32),
                            k_ring, v_ring, k_sem, v_sem]),
        out_shape=jax.ShapeDtypeStruct((db, ds, GROUP_W), F32),
        compiler_params=_cparams("arbitrary", "arbitrary", "arbitrary"),
        name="moba_sample",
    )(page_table.reshape(-1), q3, kn3, vn3, bias3, cache_k, cache_v)


def _outproj_kernel(of_ref, om_ref, x_ref, gate_ref, shift_ref, scale_ref, gf_ref, gm_ref, wo_ref, gffn_ref,
                    wr_ref, br_ref, cnt0_ref, x1_ref, h2_ref, ri_ref, rw_ref, cnt_ref, carry_ref):
    t = pl.program_id(0)

    @pl.when(t == 0)
    def _():
        carry_ref[...] = cnt0_ref[...]

    tm = x_ref.shape[0]
    nf = (_rms(of_ref[...]) * gf_ref[...]).astype(BF16)
    nm = (_rms(om_ref[...]) * gm_ref[...]).astype(BF16)
    o = (jnp.dot(nf, wo_ref[0:GROUP_W, :], preferred_element_type=F32)
         + jnp.dot(nm, wo_ref[GROUP_W:2 * GROUP_W, :], preferred_element_type=F32))
    x1 = x_ref[...] + gate_ref[...] * o
    x1_ref[...] = x1
    h2 = _rms(x1) * gffn_ref[...]
    h2 = h2 * (1.0 + scale_ref[...]) + shift_ref[...]
    hi = h2.astype(BF16)
    half = h2.shape[1] // 2
    hi32 = pltpu.bitcast(hi.astype(F32), jnp.uint32)
    h2_ref[...] = jnp.bitwise_or(jnp.right_shift(hi32[:, :half], jnp.uint32(16)),
                                 jnp.bitwise_and(hi32[:, half:], jnp.uint32(0xFFFF0000)))

    lo = (h2 - hi.astype(F32)).astype(BF16)
    wr = wr_ref[...]
    whi = wr.astype(BF16)
    wlo = (wr - whi.astype(F32)).astype(BF16)
    lg = (jnp.dot(hi, whi, preferred_element_type=F32) + jnp.dot(lo, whi, preferred_element_type=F32)
          + jnp.dot(hi, wlo, preferred_element_type=F32)) + br_ref[...]
    lane = lax.broadcasted_iota(I32, (tm, LANES), 1)
    lane_f = lane.astype(F32)
    ninf = -jnp.inf
    is_g = jnp.logical_and(lane >= N_EXPERTS, lane < N_EXPERTS + N_GROUPS)
    glog = jnp.where(is_g, lg, ninf)
    gmax = jnp.max(glog, axis=1, keepdims=True)
    gidx = jnp.min(jnp.where(glog == gmax, lane_f, 2.0 * LANES), axis=1, keepdims=True).astype(I32) - N_EXPERTS
    g_w = 1.0 / jnp.sum(jnp.exp(glog - gmax), axis=1, keepdims=True)
    in_grp = jnp.logical_and(lane >= gidx * EXPERTS_PER_GROUP, lane < (gidx + 1) * EXPERTS_PER_GROUP)
    elog = jnp.where(in_grp, lg, ninf)
    e1 = jnp.max(elog, axis=1, keepdims=True)
    i1 = jnp.min(jnp.where(elog == e1, lane_f, 2.0 * LANES), axis=1, keepdims=True).astype(I32)
    z = jnp.sum(jnp.exp(elog - e1), axis=1, keepdims=True)
    elog2 = jnp.where(lane == i1, ninf, elog)
    e2 = jnp.max(elog2, axis=1, keepdims=True)
    i2 = jnp.min(jnp.where(elog2 == e2, lane_f, 2.0 * LANES), axis=1, keepdims=True).astype(I32)
    p1 = 1.0 / z
    p2 = jnp.exp(e2 - e1) / z
    w1 = g_w * (p1 / (p1 + p2))
    w2 = g_w * (p2 / (p1 + p2))

    a = jnp.where(jnp.logical_or(lane == i1, lane == i2), 1.0, 0.0)
    r = lax.broadcasted_iota(I32, (tm, tm), 0)
    c = lax.broadcasted_iota(I32, (tm, tm), 1)
    before = jnp.where(c < r, 1.0, 0.0).astype(BF16)
    pos = jnp.dot(before, a.astype(BF16), preferred_element_type=F32) + carry_ref[...]
    r1 = jnp.sum(jnp.where(lane == i1, pos, 0.0), axis=1, keepdims=True)
    r2 = jnp.sum(jnp.where(lane == i2, pos, 0.0), axis=1, keepdims=True)
    carry_ref[...] = carry_ref[...] + jnp.sum(a, axis=0, keepdims=True)
    cnt_ref[...] = carry_ref[...]

    ri = jnp.where(lane == 0, i1, 0) + jnp.where(lane == 1, i2, 0)
    ri = ri + jnp.where(lane == 2, r1.astype(I32), 0) + jnp.where(lane == 3, r2.astype(I32), 0)
    ri_ref[...] = ri
    rw_ref[...] = jnp.where(lane == 0, w1, 0.0) + jnp.where(lane == 1, w2, 0.0)


def _outproj(of, om, x2, gate3, shift3, scale3, gf, gm, wo, gffn, wr, br, cnt0, *, rows_per_mod, name):
    n, d = x2.shape
    tm = ROW_TILE
    nt = n // tm
    mod_rows = gate3.shape[1]
    mod_map = lambda t: ((t * tm) // rows_per_mod, 0, 0)
    row_map = lambda t: (t, 0)
    const2 = lambda t: (0, 0)
    return pl.pallas_call(
        _outproj_kernel,
        grid=(nt,),
        in_specs=[pl.BlockSpec((tm, GROUP_W), row_map), pl.BlockSpec((tm, GROUP_W), row_map),
                  pl.BlockSpec((tm, d), row_map),
                  pl.BlockSpec((None, mod_rows, d), mod_map), pl.BlockSpec((None, mod_rows, d), mod_map),
                  pl.BlockSpec((None, mod_rows, d), mod_map),
                  pl.BlockSpec((1, GROUP_W), const2), pl.BlockSpec((1, GROUP_W), const2),
                  pl.BlockSpec((d, d), const2), pl.BlockSpec((1, d), const2),
                  pl.BlockSpec((d, LANES), const2), pl.BlockSpec((1, LANES), const2),
                  pl.BlockSpec((1, LANES), const2)],
        out_specs=[pl.BlockSpec((tm, d), row_map), pl.BlockSpec((tm, d // 2), row_map),
                   pl.BlockSpec((tm, LANES), row_map), pl.BlockSpec((tm, LANES), row_map),
                   pl.BlockSpec((1, LANES), const2)],
        out_shape=[jax.ShapeDtypeStruct((n, d), F32), jax.ShapeDtypeStruct((n, d // 2), jnp.uint32),
                   jax.ShapeDtypeStruct((n, LANES), I32), jax.ShapeDtypeStruct((n, LANES), F32),
                   jax.ShapeDtypeStruct((1, LANES), F32)],
        scratch_shapes=[pltpu.VMEM((1, LANES), F32)],
        compiler_params=_cparams("arbitrary"),
        name=name,
    )(of, om, x2, gate3, shift3, scale3, gf, gm, wo, gffn, wr, br, cnt0)


def _scatter_kernel(dest_ref, h_ref, rows_in_ref, rows_ref, stage_ref, sem, *, n_tiles):
    del rows_in_ref
    tm = h_ref.shape[0]
    t = pl.program_id(0)

    def retire(slot):
        for _ in range(2):
            pltpu.make_async_copy(stage_ref.at[slot], rows_ref.at[pl.ds(0, tm)], sem.at[slot]).wait()

    for slot in range(2):
        @pl.when(t % 2 == slot)
        def _():
            stage_ref[slot] = h_ref[...]
            for r in range(tm):
                for k in range(2):
                    pltpu.make_async_copy(stage_ref.at[slot, pl.ds(r, 1)], rows_ref.at[pl.ds(dest_ref[k, r], 1)],
                                          sem.at[slot]).start(priority=k)

    for slot in range(2):
        @pl.when(jnp.logical_and(t > 0, (t - 1) % 2 == slot))
        def _():
            retire(slot)

        @pl.when(jnp.logical_and(t == n_tiles - 1, t % 2 == slot))
        def _():
            retire(slot)


def _scatter_rows(dest2, h2, rows):
    n, d = h2.shape
    tm = ROW_TILE
    return pl.pallas_call(
        functools.partial(_scatter_kernel, n_tiles=n // tm),
        grid=(n // tm,),
        in_specs=[pl.BlockSpec((2, tm), lambda t: (0, t), memory_space=pltpu.SMEM),
                  pl.BlockSpec((tm, d), lambda t: (t, 0)),
                  pl.BlockSpec(memory_space=pl.ANY)],
        out_specs=pl.BlockSpec(memory_space=pl.ANY),
        out_shape=jax.ShapeDtypeStruct(rows.shape, rows.dtype),
        scratch_shapes=[pltpu.VMEM((2, tm, d), h2.dtype), pltpu.SemaphoreType.DMA((2,))],
        input_output_aliases={2: 0},
        compiler_params=_cparams("arbitrary"),
        name="moe_scatter",
    )(dest2, h2, rows)


def _moe_kernel(be_ref, nu_ref, rows_ref, w1_ref, w3_ref, w2_ref, y_ref, w1b_ref, w3b_ref, w2b_ref,
                rbuf_ref, rsem):
    blk = pl.program_id(0)
    n_used = nu_ref[0]

    def fetch(u):
        start = pl.multiple_of(u * MOE_BLOCK, MOE_BLOCK)
        return pltpu.make_async_copy(rows_ref.at[pl.ds(start, MOE_BLOCK)], rbuf_ref.at[u % ROW_RING],
                                     rsem.at[u % ROW_RING])

    @pl.when(blk == 0)
    def _():
        for ahead in range(ROW_RING - 1):
            @pl.when(ahead < n_used)
            def _():
                fetch(ahead).start()

    @pl.when(blk + (ROW_RING - 1) < n_used)
    def _():
        fetch(blk + (ROW_RING - 1)).start()

    @pl.when(blk < n_used)
    def _():
        prev = be_ref[jnp.maximum(blk - 1, 0)]

        @pl.when(jnp.logical_or(blk == 0, be_ref[blk] != prev))
        def _():
            w1b_ref[...] = w1_ref[0].astype(BF16)
            w3b_ref[...] = w3_ref[0].astype(BF16)
            w2b_ref[...] = w2_ref[0].astype(BF16)

        fetch(blk).wait()
        packed = rbuf_ref[blk % ROW_RING]
        x_lo = pltpu.bitcast(jnp.left_shift(packed, jnp.uint32(16)), F32)
        x_hi = pltpu.bitcast(jnp.bitwise_and(packed, jnp.uint32(0xFFFF0000)), F32)
        x = jnp.concatenate([x_lo, x_hi], axis=1).astype(BF16)
        a = jnp.dot(x, w1b_ref[...], preferred_element_type=F32)
        b = jnp.dot(x, w3b_ref[...], preferred_element_type=F32)
        hm = (a * jax.nn.sigmoid(a)) * b
        y_ref[...] = jnp.dot(hm.astype(BF16), w2b_ref[...], preferred_element_type=F32)

    @pl.when(blk >= nu_ref[0])
    def _():
        y_ref[...] = jnp.zeros_like(y_ref)


def _moe_experts(block_e, n_used, rows, w1, w3, w2):
    p = rows.shape[0]
    d, de = w1.shape[1], w1.shape[2]
    assert rows.shape[1] * 2 == d
    nblk = p // MOE_BLOCK
    return pl.pallas_call(
        _moe_kernel,
        grid_spec=pltpu.PrefetchScalarGridSpec(
            num_scalar_prefetch=2,
            grid=(nblk,),
            in_specs=[pl.BlockSpec(memory_space=pl.ANY),
                      pl.BlockSpec((1, d, de), lambda b, be, nu: (be[b], 0, 0)),
                      pl.BlockSpec((1, d, de), lambda b, be, nu: (be[b], 0, 0)),
                      pl.BlockSpec((1, de, d), lambda b, be, nu: (be[b], 0, 0))],
            out_specs=pl.BlockSpec((MOE_BLOCK, d), lambda b, be, nu: (b, 0)),
            scratch_shapes=[pltpu.VMEM((d, de), BF16), pltpu.VMEM((d, de), BF16), pltpu.VMEM((de, d), BF16),
                            pltpu.VMEM((ROW_RING, MOE_BLOCK, d // 2), jnp.uint32),
                            pltpu.SemaphoreType.DMA((ROW_RING,))]),
        out_shape=jax.ShapeDtypeStruct((p, d), F32),
        compiler_params=_cparams("arbitrary"),
        name="moe_experts",
    )(block_e, n_used, rows, w1, w3, w2)


def _combine_kernel(dest_ref, dest_next_ref, x1_ref, gate_ref, rw_ref, gfin_ref, y_ref, o_ref, ybuf_ref, sem):
    tm = x1_ref.shape[0]
    t = pl.program_id(0)

    def gather(idx_ref, slot):
        for r in range(tm):
            for k in range(2):
                pltpu.make_async_copy(y_ref.at[pl.ds(idx_ref[k, r], 1)], ybuf_ref.at[slot, k, pl.ds(r, 1)],
                                      sem.at[slot]).start(priority=k)

    @pl.when(t == 0)
    def _():
        gather(dest_ref, 0)

    for slot in range(2):
        @pl.when(jnp.logical_and(t + 1 < pl.num_programs(0), (t + 1) % 2 == slot))
        def _():
            gather(dest_next_ref, slot)

    cur = t % 2
    for k in range(2):
        pltpu.make_async_copy(y_ref.at[pl.ds(0, tm)], ybuf_ref.at[cur, k], sem.at[cur]).wait()
    rw = rw_ref[...]
    moe = rw[:, 0:1] * ybuf_ref[cur, 0] + rw[:, 1:2] * ybuf_ref[cur, 1]
    xo = x1_ref[...] + gate_ref[...] * moe
    o_ref[...] = _rms(xo) * gfin_ref[...]


def _combine(dest2, x1, gate3, rw, gfin, y, *, rows_per_mod, name):
    n, d = x1.shape
    tm = ROW_TILE
    nt = n // tm
    mod_rows = gate3.shape[1]
    return pl.pallas_call(
        _combine_kernel,
        grid=(nt,),
        in_specs=[pl.BlockSpec((2, tm), lambda t: (0, t), memory_space=pltpu.SMEM),
                  pl.BlockSpec((2, tm), lambda t: (0, jnp.minimum(t + 1, nt - 1)), memory_space=pltpu.SMEM),
                  pl.BlockSpec((tm, d), lambda t: (t, 0)),
                  pl.BlockSpec((None, mod_rows, d), lambda t: ((t * tm) // rows_per_mod, 0, 0)),
                  pl.BlockSpec((tm, LANES), lambda t: (t, 0)),
                  pl.BlockSpec((1, d), lambda t: (0, 0)),
                  pl.BlockSpec(memory_space=pl.ANY)],
        out_specs=pl.BlockSpec((tm, d), lambda t: (t, 0)),
        out_shape=jax.ShapeDtypeStruct((n, d), F32),
        scratch_shapes=[pltpu.VMEM((2, 2, tm, d), F32), pltpu.SemaphoreType.DMA((2,))],
        compiler_params=_cparams("arbitrary"),
        name=name,
    )(dest2, dest2, x1, gate3, rw, gfin, y)


def kernel(x_prompt, x_sample, cache_fox_k, cache_fox_v, cache_fox_logf, cache_moba_k, cache_moba_v, page_table, c_prompt, c_sample, w_ada, b_ada, g_attn, w_in, b_forget, g_out_fox, g_out_moba, t5_bias, w_out, g_ffn, w_router_group, b_router_group, w_router_expert, b_router_expert, w1, w3, w2, g_final):
    bsz, seq, d = x_prompt.shape
    db, ds, _ = x_sample.shape
    depth = w_ada.shape[0]
    n_phys, page = cache_fox_k.shape[1], cache_fox_k.shape[2]
    assert depth == 1, "one trunk layer"
    assert seq % ROW_TILE == 0 and (db * ds) % ROW_TILE == 0 and ROW_TILE % ds == 0
    assert N_EXPERTS + N_GROUPS <= LANES and N_HEADS <= LANES
    l = 0
    n_p, n_s = bsz * seq, db * ds
    xp2 = x_prompt.reshape(n_p, d)
    xs2 = x_sample.reshape(n_s, d)

    mod = _ada(jnp.concatenate([c_prompt, c_sample], axis=0), w_ada[l], b_ada[l])
    mod_p = [mod[:bsz, i * d:(i + 1) * d].reshape(bsz, 1, d) for i in range(6)]
    mod_s = [jnp.repeat(mod[bsz:, i * d:(i + 1) * d], ds, axis=0).reshape(n_s // ROW_TILE, ROW_TILE, d)
             for i in range(6)]

    w = GROUP_W
    wl = w_in[l]
    wcat = jnp.concatenate([wl[:, :3 * w], wl[:, 3 * w + N_HEADS:],
                            wl[:, 3 * w:3 * w + N_HEADS], jnp.zeros((d, LANES - N_HEADS), F32)],
                           axis=1).astype(BF16)
    bf_pad = jnp.pad(b_forget[l], (0, LANES - N_HEADS)).reshape(1, LANES)
    g_attn2 = g_attn[l].reshape(1, d)

    fqt, fk_t, fv_t, lf_t, mqt, mk_t, mv_t, fk_aug, fqt_aug, kmean, fkb, fvtb, mkb, mvtb = _inproj(
        xp2, mod_p[0], mod_p[1], g_attn2, wcat, bf_pad, rows_per_mod=seq, prompt_extras=True, seq_len=seq)
    sfq, sfk, sfv, slf, smq, smk, smv = _inproj(
        xs2, mod_s[0], mod_s[1], g_attn2, wcat, bf_pad, rows_per_mod=ROW_TILE, prompt_extras=False, seq_len=ds)

    bias_p = _t5_tiles(t5_bias, (0, MOBA_BLOCK, 2 * MOBA_BLOCK), MOBA_BLOCK, MOBA_BLOCK, sign=-1, scale=LOG2E)
    o_fox_p, o_moba_p = _prompt_attention(fqt, fkb, fvtb, fk_aug, fqt_aug, mqt, mkb, mvtb,
                                          kmean.reshape(bsz, seq // MOBA_BLOCK, w), bias_p, bsz, seq)

    to3 = lambda a: a.reshape(db, ds, w)
    page_t = lambda c: c[l].transpose(0, 2, 3, 1).reshape(n_phys, w, page)
    cfk, cfv, cmk, cmv = page_t(cache_fox_k), page_t(cache_fox_v), page_t(cache_moba_k), page_t(cache_moba_v)
    clf_t = cache_fox_logf[l].transpose(0, 2, 1)
    slf_t = jnp.pad(slf.reshape(db, ds, N_HEADS).transpose(0, 2, 1), ((0, 0), (0, 0), (0, page - ds)))
    o_fox_s = _fox_sample(page_table, to3(sfq), to3(sfk), to3(sfv), slf_t, cfk, cfv, clf_t)
    bias_s = _t5_tiles(t5_bias, (page, 0, 2 * T5_MAX_DIST + page), ds, page, sign=1)
    bias_s = bias_s.transpose(1, 0, 2, 3).reshape(3, N_HEADS * ds, page)
    o_moba_s = _moba_sample(page_table, to3(smq), to3(smk), to3(smv), bias_s, cmk, cmv)

    wr = jnp.concatenate([w_router_expert[l], w_router_group[l],
                          jnp.zeros((d, LANES - N_EXPERTS - N_GROUPS), F32)], axis=1)
    br = jnp.concatenate([b_router_expert[l], b_router_group[l],
                          jnp.zeros((LANES - N_EXPERTS - N_GROUPS,), F32)]).reshape(1, LANES)
    wo = w_out[l].astype(BF16)
    gf, gm, gffn = g_out_fox[l].reshape(1, w), g_out_moba[l].reshape(1, w), g_ffn[l].reshape(1, d)
    x1_p, h2_p, ri_p, rw_p, cnt_p = _outproj(
        o_fox_p, o_moba_p, xp2, mod_p[2], mod_p[3], mod_p[4], gf, gm, wo, gffn, wr, br,
        jnp.zeros((1, LANES), F32), rows_per_mod=seq, name="outproj_prompt")
    x1_s, h2_s, ri_s, rw_s, cnt = _outproj(
        o_fox_s.reshape(n_s, w), o_moba_s.reshape(n_s, w), xs2, mod_s[2], mod_s[3], mod_s[4], gf, gm, wo, gffn,
        wr, br, cnt_p, rows_per_mod=ROW_TILE, name="outproj_sample")

    counts = cnt[0, :N_EXPERTS].astype(I32)
    padded = (counts + MOE_BLOCK - 1) // MOE_BLOCK * MOE_BLOCK
    pend = jnp.cumsum(padded)
    pstart = pend - padded
    n_asg = 2 * (n_p + n_s)
    n_blocks = -(-(n_asg + N_EXPERTS * (MOE_BLOCK - 1)) // MOE_BLOCK)
    blk_start = jnp.arange(n_blocks, dtype=I32) * MOE_BLOCK
    block_e = jnp.clip(jnp.sum((pend[None, :] <= blk_start[:, None]).astype(I32), axis=1), 0, N_EXPERTS - 1)
    n_used = (pend[-1:] // MOE_BLOCK).astype(I32)

    def dest_of(ri):
        hit = ri[:, 0:2, None] == jnp.arange(N_EXPERTS, dtype=I32)
        return (jnp.sum(jnp.where(hit, pstart, 0), axis=-1) + ri[:, 2:4]).T.astype(I32)

    dest_p, dest_s = dest_of(ri_p), dest_of(ri_s)
    rows = jnp.zeros((n_blocks * MOE_BLOCK, d // 2), jnp.uint32)
    rows = _scatter_rows(dest_p, h2_p, rows)
    rows = _scatter_rows(dest_s, h2_s, rows)
    y = _moe_experts(block_e, n_used, rows, w1[l], w3[l], w2[l])
    gfin = g_final.reshape(1, d)
    y_prompt = _combine(dest_p, x1_p, mod_p[5], rw_p, gfin, y, rows_per_mod=seq, name="combine_prompt")
    y_sample = _combine(dest_s, x1_s, mod_s[5], rw_s, gfin, y, rows_per_mod=ROW_TILE, name="combine_sample")

    hp = lambda a: a.reshape(depth, bsz, N_HEADS, HEAD_DIM, seq).transpose(0, 1, 4, 2, 3)
    hs = lambda a: a.reshape(depth, db, ds, N_HEADS, HEAD_DIM)
    return (y_prompt.reshape(bsz, seq, d), y_sample.reshape(db, ds, d),
            hp(fk_t), hp(fv_t), lf_t.reshape(depth, bsz, N_HEADS, seq).transpose(0, 1, 3, 2), hp(mk_t), hp(mv_t),
            hs(sfk), hs(sfv), slf.reshape(depth, db, ds, N_HEADS), hs(smk), hs(smv))
```

```python
import functools
import math

import jax
import jax.numpy as jnp
from jax import lax
from jax.experimental import pallas as pl
from jax.experimental.pallas import tpu as pltpu

F32 = jnp.float32
BF16 = jnp.bfloat16
I32 = jnp.int32

HEAD_DIM = 64
N_HEADS = 8
GROUP_W = N_HEADS * HEAD_DIM
LANES = 128
HEADS_PER_LANE_TILE = LANES // HEAD_DIM
N_PAIRS = N_HEADS // HEADS_PER_LANE_TILE
SM_SCALE = HEAD_DIM ** -0.5
LOG2E = math.log2(math.e)
MOBA_BLOCK = 256
MOBA_TOPK = 3
T5_BUCKETS = 32
T5_MAX_DIST = 128
N_GROUPS = 4
EXPERTS_PER_GROUP = 8
N_EXPERTS = N_GROUPS * EXPERTS_PER_GROUP
RMS_EPS = 1e-6
ROW_TILE = 256
ATTN_TILE = 256
K_CHUNK = 128
AUG = LANES // N_HEADS
SUM_ROWS = 16
MOE_BLOCK = 512
PAGES_PER_STEP = 16
PAGE_RING = 3
ROW_RING = 3
NEG = -1e30
VMEM_LIMIT = 48 * 1024 * 1024
HIGHEST = lax.Precision.HIGHEST
NT = (((1,), (1,)), ((), ()))


def _cparams(*sem):
    return pltpu.CompilerParams(dimension_semantics=sem, vmem_limit_bytes=VMEM_LIMIT)


def _rms(x):
    return x * lax.rsqrt(jnp.mean(x * x, axis=-1, keepdims=True) + RMS_EPS)


def _spare_half(h):
    return 0 if h % HEADS_PER_LANE_TILE else HEAD_DIM


def _split3_bf16(x):
    hi = x.astype(BF16)
    r1 = x - hi.astype(F32)
    mid = r1.astype(BF16)
    lo = (r1 - mid.astype(F32)).astype(BF16)
    return hi, mid, lo


def _ada_kernel(c_ref, w_ref, b_ref, o_ref):
    c = c_ref[...]
    a = c * jax.nn.sigmoid(c)
    o_ref[...] = jnp.dot(a, w_ref[...], preferred_element_type=F32, precision=HIGHEST) + b_ref[...]


def _ada(c, w, b):
    n, d = c.shape
    e = w.shape[1]
    tn = 1024
    return pl.pallas_call(
        _ada_kernel,
        grid=(e // tn,),
        in_specs=[pl.BlockSpec((n, d), lambda j: (0, 0)),
                  pl.BlockSpec((d, tn), lambda j: (0, j)),
                  pl.BlockSpec((1, tn), lambda j: (0, j))],
        out_specs=pl.BlockSpec((n, tn), lambda j: (0, j)),
        out_shape=jax.ShapeDtypeStruct((n, e), F32),
        compiler_params=_cparams("arbitrary"),
        name="ada",
    )(c, w, b.reshape(1, e))


def _inproj_kernel(x_ref, shift_ref, scale_ref, g_ref, w_ref, bf_ref,
                   fq_ref, fk_ref, fv_ref, lf_ref, mq_ref, mk_ref, mv_ref, *rest,
                   prompt_extras, tiles_per_seq):
    x = x_ref[...]
    h = _rms(x) * g_ref[...]
    h = h * (1.0 + scale_ref[...]) + shift_ref[...]
    z = jnp.dot(h.astype(BF16), w_ref[...], preferred_element_type=F32)
    w = GROUP_W
    q_scale = SM_SCALE * LOG2E if prompt_extras else SM_SCALE
    fq = z[:, 0:w] * q_scale
    mq = z[:, 3 * w:4 * w] * q_scale
    if prompt_extras:
        fq_ref[0] = fq.T.astype(BF16)
        mq_ref[0] = mq.T.astype(BF16)
    else:
        fq_ref[...] = fq.astype(BF16)
        mq_ref[...] = mq.astype(BF16)
    fk, fv = z[:, w:2 * w], z[:, 2 * w:3 * w]
    mk, mv = z[:, 4 * w:5 * w], z[:, 5 * w:6 * w]
    fg = z[:, 6 * w:6 * w + LANES] + bf_ref[...]
    lf = jnp.minimum(fg, 0.0) - jnp.log1p(jnp.exp(-jnp.abs(fg)))
    if not prompt_extras:
        fk_ref[...] = fk
        fv_ref[...] = fv
        mk_ref[...] = mk
        mv_ref[...] = mv
        lf_ref[...] = lf[:, :N_HEADS]
        return
    fkaug_ref, fqtaug_ref, kmean_ref, fkb_ref, fvtb_ref, mkb_ref, mvtb_ref, carry_ref = rest
    tm = x.shape[0]
    fk_ref[0] = fk.T
    fv_t = fv.T
    fv_ref[0] = fv_t
    mk_ref[0] = mk.T
    mv_t = mv.T
    mv_ref[0] = mv_t
    fkb_ref[...] = fk.astype(BF16)
    fvtb_ref[0] = fv_t.astype(BF16)
    mkb_ref[...] = mk.astype(BF16)
    mvtb_ref[0] = mv_t.astype(BF16)
    lf_ref[0] = lf.T[:N_HEADS, :]
    kmean_ref[0] = jnp.mean(mk, axis=0, keepdims=True)

    @pl.when(pl.program_id(0) % tiles_per_seq == 0)
    def _():
        carry_ref[...] = jnp.zeros_like(carry_ref)

    r = lax.broadcasted_iota(I32, (tm, tm), 0)
    c = lax.broadcasted_iota(I32, (tm, tm), 1)
    tri = jnp.where(c <= r, 1.0, 0.0).astype(BF16)
    hi, mid, lo = _split3_bf16(lf)
    cs = (jnp.dot(tri, hi, preferred_element_type=F32)
          + jnp.dot(tri, mid, preferred_element_type=F32)
          + jnp.dot(tri, lo, preferred_element_type=F32)) + carry_ref[...]
    carry_ref[...] = cs[tm - 1:tm, :]
    hi, mid, lo = _split3_bf16(cs * LOG2E)
    er = lax.broadcasted_iota(I32, (LANES, LANES), 0)
    ec = lax.broadcasted_iota(I32, (LANES, LANES), 1)

    def spread(x, off):
        e = jnp.where(jnp.logical_and(er < N_HEADS, ec == AUG * er + off), 1.0, 0.0).astype(BF16)
        return jnp.dot(x, e, preferred_element_type=F32)

    lane = lax.broadcasted_iota(I32, (tm, LANES), 1)
    slot = lane % AUG
    fq_aug = (spread(hi, 0) + spread(mid, 1) + spread(lo, 2)
              + jnp.where(jnp.logical_and(slot >= 3, slot < 6), 1.0, 0.0))
    fk_aug = jnp.where(slot < 3, 1.0, 0.0) - (spread(hi, 3) + spread(mid, 4) + spread(lo, 5))
    fq_aug_t = fq_aug.T
    k_tiles, q_tiles = [], []
    for hd in range(N_HEADS):
        base = _spare_half(hd)
        shift = (base - AUG * hd) % LANES
        moved = pltpu.roll(fk_aug, shift, 1) if shift else fk_aug
        k_tiles.append(jnp.where(jnp.logical_and(lane >= base, lane < base + AUG), moved, 0.0))
        q_rows = [fq_aug_t[AUG * hd:AUG * (hd + 1), :]]
        if base:
            q_rows.insert(0, jnp.zeros((base, tm), F32))
        if LANES - base - AUG:
            q_rows.append(jnp.zeros((LANES - base - AUG, tm), F32))
        q_tiles.append(jnp.concatenate(q_rows, axis=0))
    fkaug_ref[...] = jnp.concatenate(k_tiles, axis=1).astype(BF16)
    fqtaug_ref[0] = jnp.concatenate(q_tiles, axis=0).astype(BF16)


def _inproj(x2, shift3, scale3, g_attn, wcat, bf_pad, *, rows_per_mod, prompt_extras, seq_len):
    n, d = x2.shape
    tm = ROW_TILE
    assert n % tm == 0
    nt = n // tm
    mod_rows = shift3.shape[1]
    mod_map = lambda t: ((t * tm) // rows_per_mod, 0, 0)
    row_map = lambda t: (t, 0)
    const2 = lambda t: (0, 0)
    ecols = wcat.shape[1]
    row_bf16 = (jax.ShapeDtypeStruct((n, GROUP_W), BF16), pl.BlockSpec((tm, GROUP_W), row_map))
    scratch = []
    tiles_per_seq = 1
    if prompt_extras:
        assert tm == MOBA_BLOCK and seq_len % tm == 0
        tiles_per_seq = seq_len // tm
        nb = n // seq_len
        t_map = lambda t: (t // tiles_per_seq, 0, t % tiles_per_seq)
        kv = (jax.ShapeDtypeStruct((nb, GROUP_W, seq_len), F32), pl.BlockSpec((1, GROUP_W, tm), t_map))
        lfo = (jax.ShapeDtypeStruct((nb, N_HEADS, seq_len), F32), pl.BlockSpec((1, N_HEADS, tm), t_map))
        kvt_bf16 = (jax.ShapeDtypeStruct((nb, GROUP_W, seq_len), BF16), pl.BlockSpec((1, GROUP_W, tm), t_map))
        outs = [kvt_bf16, kv, kv, lfo, kvt_bf16, kv, kv,
                (jax.ShapeDtypeStruct((n, N_HEADS * LANES), BF16), pl.BlockSpec((tm, N_HEADS * LANES), row_map)),
                (jax.ShapeDtypeStruct((nb, N_HEADS * LANES, seq_len), BF16),
                 pl.BlockSpec((1, N_HEADS * LANES, tm), t_map)),
                (jax.ShapeDtypeStruct((nt, 1, GROUP_W), F32), pl.BlockSpec((1, 1, GROUP_W), lambda t: (t, 0, 0)))]
        outs += [row_bf16, kvt_bf16, row_bf16, kvt_bf16]
        scratch = [pltpu.VMEM((1, LANES), F32)]
    else:
        kv = (jax.ShapeDtypeStruct((n, GROUP_W), F32), pl.BlockSpec((tm, GROUP_W), row_map))
        lfo = (jax.ShapeDtypeStruct((n, N_HEADS), F32), pl.BlockSpec((tm, N_HEADS), row_map))
        outs = [row_bf16, kv, kv, lfo, row_bf16, kv, kv]
    return pl.pallas_call(
        functools.partial(_inproj_kernel, prompt_extras=prompt_extras, tiles_per_seq=tiles_per_seq),
        grid=(nt,),
        in_specs=[pl.BlockSpec((tm, d), row_map),
                  pl.BlockSpec((None, mod_rows, d), mod_map),
                  pl.BlockSpec((None, mod_rows, d), mod_map),
                  pl.BlockSpec((1, d), const2),
                  pl.BlockSpec((d, ecols), const2),
                  pl.BlockSpec((1, LANES), const2)],
        out_specs=[o[1] for o in outs],
        out_shape=[o[0] for o in outs],
        scratch_shapes=scratch,
        compiler_params=_cparams("arbitrary"),
        name="inproj_prompt" if prompt_extras else "inproj_sample",
    )(x2, shift3, scale3, g_attn, wcat, bf_pad)


def _t5_kernel(t5_ref, o_ref, *, offs, sign, scale):
    h = pl.program_id(0)
    rows, cols = o_ref.shape[2], o_ref.shape[3]
    r = lax.broadcasted_iota(I32, (rows, cols), 0)
    c = lax.broadcasted_iota(I32, (rows, cols), 1)
    max_exact = T5_BUCKETS // 2
    for k, off in enumerate(offs):
        if off - max(rows, cols) + 1 >= T5_MAX_DIST:
            o_ref[0, k] = jnp.full((rows, cols), t5_ref[T5_BUCKETS - 1, h] * scale, F32)
            continue
        rel = jnp.maximum(off + sign * (r - c), 0)
        relf = jnp.maximum(rel, 1).astype(F32)
        large = max_exact + (jnp.log(relf / max_exact) / math.log(T5_MAX_DIST / max_exact)
                             * (T5_BUCKETS - max_exact)).astype(I32)
        large = jnp.minimum(large, T5_BUCKETS - 1)
        bucket = jnp.where(rel < max_exact, rel, large)
        acc = jnp.zeros((rows, cols), F32)
        for b in range(T5_BUCKETS):
            acc = jnp.where(bucket == b, t5_ref[b, h], acc)
        o_ref[0, k] = acc if scale == 1.0 else acc * scale


def _t5_tiles(t5_bias, offs, rows, cols, sign, scale=1.0):
    nh = t5_bias.shape[1]
    return pl.pallas_call(
        functools.partial(_t5_kernel, offs=tuple(offs), sign=sign, scale=scale),
        grid=(nh,),
        in_specs=[pl.BlockSpec(memory_space=pltpu.SMEM)],
        out_specs=pl.BlockSpec((1, len(offs), rows, cols), lambda h: (h, 0, 0, 0)),
        out_shape=jax.ShapeDtypeStruct((nh, len(offs), rows, cols), F32),
        compiler_params=_cparams("arbitrary"),
        name="t5_tiles",
    )(t5_bias)


def _pair_masks(rows):
    lane = lax.broadcasted_iota(I32, (rows, LANES), 1)
    lo = lane < HEAD_DIM
    return lo, jnp.logical_not(lo)


def _masked_qt(qt_ref, h):
    p, e = divmod(h, HEADS_PER_LANE_TILE)
    qt = qt_ref[0, p * LANES:(p + 1) * LANES, :]
    row = lax.broadcasted_iota(I32, qt.shape, 0)
    keep = (row >= HEAD_DIM) if e else (row < HEAD_DIM)
    return jnp.where(keep, qt, jnp.zeros_like(qt))


def _pair_of(h):
    return slice((h // HEADS_PER_LANE_TILE) * LANES, (h // HEADS_PER_LANE_TILE + 1) * LANES)


def _flash_step(lhs_of, rhs_of, vt_ref, adjust, states, tk, tq):
    chunks = [slice(c * K_CHUNK, (c + 1) * K_CHUNK) for c in range(tk // K_CHUNK)]
    for h, (m_ref, _, _, s_scr, _, a_ref) in enumerate(states):
        rhs = rhs_of(h)
        m8 = jnp.full((8, tq), NEG, F32)
        for c, rows in enumerate(chunks):
            s = jnp.dot(lhs_of(h, c), rhs, preferred_element_type=F32)
            if adjust is not None:
                s = adjust(h, c, s)
            s_scr[rows, :] = s
            m8 = jnp.maximum(m8, jnp.max(s.reshape(K_CHUNK // 8, 8, tq), axis=0))
        m_prev = m_ref[...]
        m_new = jnp.maximum(m_prev, jnp.max(m8, axis=0, keepdims=True))
        a_ref[...] = jnp.exp2(m_prev - m_new)
        m_ref[...] = m_new
    for m_ref, _, _, s_scr, p_scr, _ in states:
        m_new = m_ref[...]
        for rows in chunks:
            p_scr[rows, :] = jnp.exp2(s_scr[rows, :] - m_new).astype(BF16)
    ones = jnp.ones((SUM_ROWS, tk), BF16)
    for h, (_, l_ref, acc_ref, _, p_scr, a_ref) in enumerate(states):
        feat = slice(h * HEAD_DIM, (h + 1) * HEAD_DIM)
        vt_sum = jnp.concatenate([vt_ref[0, feat, :], ones], axis=0)
        pv = jnp.dot(vt_sum, p_scr[...], preferred_element_type=F32)
        acc_ref[...] = a_ref[...] * acc_ref[...] + pv[:HEAD_DIM]
        l_ref[...] = a_ref[...] * l_ref[...] + pv[HEAD_DIM:HEAD_DIM + 1]


FLASH_BUFS_PER_HEAD = 6


def _flash_scratch(tq, tk):
    per_head = [pltpu.VMEM((1, tq), F32), pltpu.VMEM((1, tq), F32), pltpu.VMEM((HEAD_DIM, tq), F32),
                pltpu.VMEM((tk, tq), F32), pltpu.VMEM((tk, tq), BF16), pltpu.VMEM((1, tq), F32)]
    return per_head * N_HEADS


def _flash_states(scratch):
    n = FLASH_BUFS_PER_HEAD
    return [scratch[n * h:n * (h + 1)] for h in range(N_HEADS)]


def _flash_init(states):
    for m_ref, l_ref, acc_ref, _, _, _ in states:
        m_ref[...] = jnp.full_like(m_ref, NEG)
        l_ref[...] = jnp.zeros_like(l_ref)
        acc_ref[...] = jnp.zeros_like(acc_ref)


def _flash_finish(o_ref, states):
    o_t = jnp.concatenate([acc_ref[...] / l_ref[...] for _, l_ref, acc_ref, _, _, _ in states], axis=0)
    o_ref[...] = o_t.T


def _chunk_causal(c, tq):
    krow = c * K_CHUNK + lax.broadcasted_iota(I32, (K_CHUNK, tq), 0)
    qcol = lax.broadcasted_iota(I32, (K_CHUNK, tq), 1)
    return krow <= qcol


def _head_rows(x, h, rows_per_head):
    row = lax.broadcasted_iota(I32, x.shape, 0)
    keep = jnp.logical_and(row >= h * rows_per_head, row < (h + 1) * rows_per_head)
    return jnp.where(keep, x, jnp.zeros_like(x))


def _own_half(shape, h, axis):
    pos = lax.broadcasted_iota(I32, shape, axis)
    return (pos >= HEAD_DIM) if h % HEADS_PER_LANE_TILE else (pos < HEAD_DIM)


def _fox_prompt_body(i, j, q_ref, k_ref, vt_ref, fqt_ref, fk_ref, o_ref, states):
    tq, tk = q_ref.shape[2], k_ref.shape[0]

    def lhs_of(h, c):
        rows = slice(c * K_CHUNK, (c + 1) * K_CHUNK)
        k = k_ref[rows, _pair_of(h)]
        return jnp.where(_own_half(k.shape, h, 1), k, fk_ref[rows, h * LANES:(h + 1) * LANES])

    def rhs_of(h):
        qt = q_ref[0, _pair_of(h), :]
        return jnp.where(_own_half(qt.shape, h, 0), qt, fqt_ref[0, h * LANES:(h + 1) * LANES, :])

    def step(diagonal):
        adjust = (lambda h, c, s: jnp.where(_chunk_causal(c, tq), s, NEG)) if diagonal else None
        _flash_step(lhs_of, rhs_of, vt_ref, adjust, states, tk, tq)

    @pl.when(j == i)
    def _():
        _flash_init(states)
        step(True)

    @pl.when(j < i)
    def _():
        step(False)

    @pl.when(jnp.logical_or(j == i - 1, i == 0))
    def _():
        _flash_finish(o_ref, states)


def _tri_tables(nq, own_first):
    qi, kj = [], []
    for i in range(nq):
        order = ([i] + list(range(i))) if own_first else list(range(i + 1))
        for j in order:
            qi.append(i)
            kj.append(j)
    return jnp.asarray(qi, I32), jnp.asarray(kj, I32)


def _moba_prompt_body(i, j, q_ref, k_ref, vt_ref, kmean_ref, bias_ref, o_ref, states, qb_ref):
    tq, tk = q_ref.shape[2], k_ref.shape[0]
    nb = kmean_ref.shape[1]
    nbp = -(-nb // 8) * 8
    assert 3 + nb <= AUG
    far_kind = bias_ref.shape[1] - 1

    def k_chunk(h, c, spare):
        k = k_ref[c * K_CHUNK:(c + 1) * K_CHUNK, _pair_of(h)]
        return jnp.where(_own_half(k.shape, h, 1), k, spare)

    def off_diagonal_step(far):
        lane = lax.broadcasted_iota(I32, (K_CHUNK, LANES), 1)
        key_side = []
        for base in (_spare_half(0), _spare_half(1)):
            slot = lane - base
            ones_at = (slot == 3 + j)
            if far:
                ones_at = jnp.logical_or(ones_at, jnp.logical_and(slot >= 0, slot < 3))
            key_side.append(jnp.where(ones_at, 1.0, 0.0).astype(BF16))
        adjust = None if far else (lambda h, c, s: s + bias_ref[h, 1, c * K_CHUNK:(c + 1) * K_CHUNK, :])
        _flash_step(lambda h, c: k_chunk(h, c, key_side[h % HEADS_PER_LANE_TILE]),
                    lambda h: qb_ref[h], vt_ref, adjust, states, tk, tq)

    @pl.when(j == i)
    def _():
        _flash_init(states)
        blk_row = lax.broadcasted_iota(I32, (nbp, tq), 0)
        aug_row = lax.broadcasted_iota(I32, (AUG, tq), 0)
        past = blk_row < i
        for h in range(N_HEADS):
            km = _pad_rows(kmean_ref[0, :, _pair_of(h)], nbp).astype(BF16)
            g = jnp.dot(km, _masked_qt(q_ref, h), preferred_element_type=F32)
            c_hi, c_mid, c_lo = _split3_bf16(bias_ref[h, far_kind, 0:1, 0:1])
            aug = jnp.where(aug_row == 0, c_hi.astype(F32),
                            jnp.where(aug_row == 1, c_mid.astype(F32),
                                      jnp.where(aug_row == 2, c_lo.astype(F32), 0.0)))
            for jb in range(nb):
                gj = g[jb:jb + 1, :]
                beats = jnp.logical_and(past, jnp.logical_or(g > gj, jnp.logical_and(g == gj, blk_row < jb)))
                rank = jnp.sum(jnp.where(beats, 1.0, 0.0), axis=0, keepdims=True)
                chosen = jnp.logical_and(rank < MOBA_TOPK, jb < i)
                aug = jnp.where(aug_row == 3 + jb, jnp.where(chosen, 0.0, NEG), aug)
            own = q_ref[0, h * HEAD_DIM:(h + 1) * HEAD_DIM, :]
            spare = jnp.concatenate([aug.astype(BF16), jnp.zeros((HEAD_DIM - AUG, tq), BF16)], axis=0)
            qb_ref[h] = jnp.concatenate([spare, own] if h % HEADS_PER_LANE_TILE else [own, spare], axis=0)

        def adjust(h, c, s):
            s = s + bias_ref[h, 0, c * K_CHUNK:(c + 1) * K_CHUNK, :]
            return jnp.where(_chunk_causal(c, tq), s, NEG)

        zero = jnp.zeros((K_CHUNK, LANES), BF16)
        _flash_step(lambda h, c: k_chunk(h, c, zero), lambda h: qb_ref[h], vt_ref, adjust, states, tk, tq)

        @pl.when(i == 0)
        def _():
            _flash_finish(o_ref, states)

    @pl.when(j == i - 1)
    def _():
        off_diagonal_step(far=False)
        _flash_finish(o_ref, states)

    @pl.when(j < i - 1)
    def _():
        off_diagonal_step(far=True)


def _prompt_attn_kernel(qi_ref, kj_ref, fq_ref, fk_ref, fvt_ref, fqt_aug_ref, fk_aug_ref,
                        mq_ref, mk_ref, mvt_ref, kmean_ref, bias_ref, o_fox_ref, o_moba_ref, *scratch):
    n = FLASH_BUFS_PER_HEAD * N_HEADS
    t = pl.program_id(1)
    i = qi_ref[t]
    j = kj_ref[t]
    _fox_prompt_body(i, j, fq_ref, fk_ref, fvt_ref, fqt_aug_ref, fk_aug_ref, o_fox_ref,
                     _flash_states(scratch[:n]))
    _moba_prompt_body(i, j, mq_ref, mk_ref, mvt_ref, kmean_ref, bias_ref, o_moba_ref,
                      _flash_states(scratch[n:2 * n]), scratch[2 * n])


def _prompt_attention(fqt, fk, fvt, fk_aug, fqt_aug, mqt, mk, mvt, kmean, bias, batch, seq):
    assert ATTN_TILE == MOBA_BLOCK
    tq = tk = MOBA_BLOCK
    nq = seq // tq
    qi, kj = _tri_tables(nq, own_first=True)
    qmap = lambda b, t, qi, kj: (b * nq + qi[t], 0)
    kmap = lambda b, t, qi, kj: (b * nq + kj[t], 0)
    qtmap = lambda b, t, qi, kj: (b, 0, qi[t])
    ktmap = lambda b, t, qi, kj: (b, 0, kj[t])
    out = jax.ShapeDtypeStruct((batch * seq, GROUP_W), F32)
    return pl.pallas_call(
        _prompt_attn_kernel,
        grid_spec=pltpu.PrefetchScalarGridSpec(
            num_scalar_prefetch=2,
            grid=(batch, qi.shape[0]),
            in_specs=[pl.BlockSpec((1, GROUP_W, tq), qtmap),
                      pl.BlockSpec((tk, GROUP_W), kmap),
                      pl.BlockSpec((1, GROUP_W, tk), ktmap),
                      pl.BlockSpec((1, N_HEADS * LANES, tq), qtmap),
                      pl.BlockSpec((tk, N_HEADS * LANES), kmap),
                      pl.BlockSpec((1, GROUP_W, tq), qtmap),
                      pl.BlockSpec((tk, GROUP_W), kmap),
                      pl.BlockSpec((1, GROUP_W, tk), ktmap),
                      pl.BlockSpec((1, nq, GROUP_W), lambda b, t, qi, kj: (b, 0, 0)),
                      pl.BlockSpec(bias.shape, lambda b, t, qi, kj: (0, 0, 0, 0))],
            out_specs=[pl.BlockSpec((tq, GROUP_W), qmap), pl.BlockSpec((tq, GROUP_W), qmap)],
            scratch_shapes=(_flash_scratch(tq, tk) + _flash_scratch(tq, tk)
                            + [pltpu.VMEM((N_HEADS, LANES, tq), BF16)])),
        out_shape=[out, out],
        compiler_params=_cparams("arbitrary", "arbitrary"),
        name="prompt_attention",
    )(qi, kj, fqt, fk, fvt, fqt_aug, fk_aug, mqt, mk, mvt, kmean, bias)


def _block_diag_q(q):
    ds = q.shape[0]
    lane_head = lax.broadcasted_iota(I32, (N_HEADS, ds, GROUP_W), 2) // HEAD_DIM
    head = lax.broadcasted_iota(I32, (N_HEADS, ds, GROUP_W), 0)
    q3 = jnp.where(lane_head == head, q.astype(F32)[None, :, :], 0.0)
    return q3.reshape(N_HEADS * ds, GROUP_W).astype(BF16)


def _head_diag_out(acc, ds):
    acc3 = acc.reshape(N_HEADS, ds, GROUP_W)
    lane_head = lax.broadcasted_iota(I32, (N_HEADS, ds, GROUP_W), 2) // HEAD_DIM
    head = lax.broadcasted_iota(I32, (N_HEADS, ds, GROUP_W), 0)
    return jnp.sum(jnp.where(lane_head == head, acc3, 0.0), axis=0)


def _pad_rows(x, rows):
    if x.shape[0] == rows:
        return x
    return jnp.concatenate([x, jnp.zeros((rows - x.shape[0], x.shape[1]), x.dtype)], axis=0)


def _ring_copy(pt_ref, page_index, cache_ref, buf_ref, sem_ref, slot, rr):
    return pltpu.make_async_copy(cache_ref.at[pt_ref[page_index]], buf_ref.at[slot, rr], sem_ref.at[slot])


def _ring_fetch(pt_ref, page_index_of, u, cache_ref, buf_ref, sem_ref):
    slot = u % PAGE_RING
    for rr in range(buf_ref.shape[1]):
        _ring_copy(pt_ref, page_index_of(u, rr), cache_ref, buf_ref, sem_ref, slot, rr).start(priority=rr % 2)


def _ring_wait(pt_ref, page_index_of, u, cache_ref, buf_ref, sem_ref):
    slot = u % PAGE_RING
    for rr in range(buf_ref.shape[1]):
        _ring_copy(pt_ref, page_index_of(u, rr), cache_ref, buf_ref, sem_ref, slot, rr).wait()


def _ring_advance(pt_ref, page_index_of, u, n_fetches, cache_ref, buf_ref, sem_ref):
    @pl.when(u == 0)
    def _():
        for ahead in range(min(PAGE_RING - 1, n_fetches)):
            _ring_fetch(pt_ref, page_index_of, ahead, cache_ref, buf_ref, sem_ref)

    @pl.when(u + (PAGE_RING - 1) < n_fetches)
    def _():
        _ring_fetch(pt_ref, page_index_of, u + (PAGE_RING - 1), cache_ref, buf_ref, sem_ref)

    _ring_wait(pt_ref, page_index_of, u, cache_ref, buf_ref, sem_ref)
    return u % PAGE_RING


def _fox_sample_kernel(pt_ref, q_ref, kn_ref, vn_ref, lfn_ref, *refs, pages, n_pages, n_batch):
    lf_refs = refs[0:pages]
    (k_hbm, v_hbm, o_ref, qbd_ref, m_ref, l_ref, acc_ref, carry_ref, rq_ref,
     kbuf, vbuf, ksem, vsem) = refs[pages:]
    g = pl.program_id(1)
    ng = n_pages // pages
    ds = q_ref.shape[1]
    page = kbuf.shape[3]

    def page_index_of(u, rr):
        return (u // ng) * n_pages + (ng - 1 - u % ng) * pages + (pages - 1 - rr)

    u = pl.program_id(0) * ng + g
    slot = _ring_advance(pt_ref, page_index_of, u, n_batch * ng, k_hbm, kbuf, ksem)
    _ring_advance(pt_ref, page_index_of, u, n_batch * ng, v_hbm, vbuf, vsem)
    lane3 = lax.broadcasted_iota(I32, (N_HEADS, ds, page), 2)
    qidx3 = lax.broadcasted_iota(I32, (N_HEADS, ds, page), 1)
    r = lax.broadcasted_iota(I32, (page, page), 0)
    c = lax.broadcasted_iota(I32, (page, page), 1)
    after = jnp.where(r > c, 1.0, 0.0).astype(BF16)

    def suffix_sum(x):
        hi, mid, lo = _split3_bf16(x)
        return (jnp.dot(hi, after, preferred_element_type=F32) + jnp.dot(mid, after, preferred_element_type=F32)
                + jnp.dot(lo, after, preferred_element_type=F32))

    def biased(s, later):
        s3 = s.reshape(N_HEADS, ds, page) + later[:, None, :] - rq_ref[...].reshape(N_HEADS, ds, 1)
        return s3

    def softmax_update(s, pv_of):
        m_prev = m_ref[...]
        m_new = jnp.maximum(m_prev, jnp.max(s, axis=1, keepdims=True))
        alpha = jnp.exp(m_prev - m_new)
        p = jnp.exp(s - m_new)
        l_ref[...] = alpha * l_ref[...] + jnp.sum(p, axis=1, keepdims=True)
        m_ref[...] = m_new
        acc_ref[...] = alpha * acc_ref[...] + pv_of(p.astype(BF16))

    @pl.when(g == 0)
    def _():
        qbd_ref[...] = _block_diag_q(q_ref[0])
        m_ref[...] = jnp.full_like(m_ref, NEG)
        l_ref[...] = jnp.zeros_like(l_ref)
        acc_ref[...] = jnp.zeros_like(acc_ref)
        x = lfn_ref[0]
        later_new = suffix_sum(x)
        rq3 = jnp.sum(jnp.where(lane3 == qidx3, later_new[:, None, :], 0.0), axis=2, keepdims=True)
        rq_ref[...] = rq3.reshape(N_HEADS * ds, 1)
        kn = _pad_rows(kn_ref[0], page).astype(BF16)
        vn = _pad_rows(vn_ref[0], page).astype(BF16)
        s3 = biased(lax.dot_general(qbd_ref[...], kn, NT, preferred_element_type=F32), later_new)
        s3 = jnp.where(lane3 <= qidx3, s3, NEG)
        softmax_update(s3.reshape(N_HEADS * ds, page), lambda p: jnp.dot(p, vn, preferred_element_type=F32))
        carry_ref[...] = jnp.sum(x, axis=1, keepdims=True)

    xs = [lf_refs[rr][0] for rr in range(pages)]
    within = suffix_sum(jnp.concatenate(xs, axis=0))
    run = carry_ref[...]
    parts = []
    for rr in range(pages):
        later = within[rr * N_HEADS:(rr + 1) * N_HEADS, :] + run
        s = jnp.dot(qbd_ref[...], kbuf[slot, rr].astype(BF16), preferred_element_type=F32)
        parts.append(biased(s, later).reshape(N_HEADS * ds, page))
        run = run + jnp.sum(xs[rr], axis=1, keepdims=True)
    carry_ref[...] = run

    def pv_of(p):
        pv = None
        for rr in range(pages):
            term = lax.dot_general(p[:, rr * page:(rr + 1) * page], vbuf[slot, rr].astype(BF16), NT,
                                   preferred_element_type=F32)
            pv = term if pv is None else pv + term
        return pv

    softmax_update(jnp.concatenate(parts, axis=1), pv_of)

    @pl.when(g == ng - 1)
    def _():
        o_ref[0] = _head_diag_out(acc_ref[...] / l_ref[...], ds)


def _page_ring_scratch(pages, page):
    return [pltpu.VMEM((PAGE_RING, pages, GROUP_W, page), F32), pltpu.SemaphoreType.DMA((PAGE_RING,))]


def _fox_sample(page_table, q3, kn3, vn3, lfn_t, cache_k, cache_v, cache_lf_t):
    db, n_pages = page_table.shape
    ds = q3.shape[1]
    pages = PAGES_PER_STEP
    assert n_pages % pages == 0
    ng = n_pages // pages
    page = cache_k.shape[2]

    def page_map(rr):
        return lambda b, g, pt: (pt[b * n_pages + (ng - 1 - g) * pages + (pages - 1 - rr)], 0, 0)

    bmap = lambda b, g, pt: (b, 0, 0)
    in_specs = [pl.BlockSpec((1, ds, GROUP_W), bmap), pl.BlockSpec((1, ds, GROUP_W), bmap),
                pl.BlockSpec((1, ds, GROUP_W), bmap), pl.BlockSpec((1, N_HEADS, page), bmap)]
    in_specs += [pl.BlockSpec((1, N_HEADS, page), page_map(rr)) for rr in range(pages)]
    in_specs += [pl.BlockSpec(memory_space=pl.ANY), pl.BlockSpec(memory_space=pl.ANY)]
    rows = N_HEADS * ds
    k_ring, k_sem = _page_ring_scratch(pages, page)
    v_ring, v_sem = _page_ring_scratch(pages, page)
    return pl.pallas_call(
        functools.partial(_fox_sample_kernel, pages=pages, n_pages=n_pages, n_batch=db),
        grid_spec=pltpu.PrefetchScalarGridSpec(
            num_scalar_prefetch=1,
            grid=(db, ng),
            in_specs=in_specs,
            out_specs=pl.BlockSpec((1, ds, GROUP_W), bmap),
            scratch_shapes=[pltpu.VMEM((rows, GROUP_W), BF16), pltpu.VMEM((rows, 1), F32),
                            pltpu.VMEM((rows, 1), F32), pltpu.VMEM((rows, GROUP_W), F32),
                            pltpu.VMEM((N_HEADS, 1), F32), pltpu.VMEM((rows, 1), F32),
                            k_ring, v_ring, k_sem, v_sem]),
        out_shape=jax.ShapeDtypeStruct((db, ds, GROUP_W), F32),
        compiler_params=_cparams("arbitrary", "arbitrary"),
        name="fox_sample",
    )(page_table.reshape(-1), q3, kn3, vn3, lfn_t, *([cache_lf_t] * pages), cache_k, cache_v)


def _moba_sample_kernel(pt_ref, q_ref, kn_ref, vn_ref, bias_ref, k_hbm, v_hbm, o_ref,
                        qbd_ref, s_ref, p_ref, pnew_ref, linv_ref, acc_ref, kbuf, vbuf, ksem, vsem,
                        *, pages, n_batch):
    ph = pl.program_id(1)
    g = pl.program_id(2)
    ds = q_ref.shape[1]
    page = kbuf.shape[3]
    n_pages = s_ref.shape[0]
    ng = n_pages // pages
    rows = N_HEADS * ds
    pages_per_block = MOBA_BLOCK // page
    n_blocks = n_pages // pages_per_block
    u = pl.program_id(0) * ng + g

    def page_index_of(u, rr):
        return (u // ng) * n_pages + (u % ng) * pages + rr

    @pl.when(jnp.logical_and(ph == 0, g == 0))
    def _():
        qbd_ref[...] = _block_diag_q(q_ref[0])

    @pl.when(ph == 0)
    def _():
        slot = _ring_advance(pt_ref, page_index_of, u, n_batch * ng, k_hbm, kbuf, ksem)
        for rr in range(pages):
            s_ref[g * pages + rr] = jnp.dot(qbd_ref[...], kbuf[slot, rr].astype(BF16), preferred_element_type=F32)

    @pl.when(jnp.logical_and(ph == 0, g == ng - 1))
    def _():
        lane = lax.broadcasted_iota(I32, (rows, LANES), 1)
        gate = jnp.full((rows, LANES), -jnp.inf, F32)
        for b in range(n_blocks):
            tot = s_ref[b * pages_per_block]
            for u in range(1, pages_per_block):
                tot = tot + s_ref[b * pages_per_block + u]
            gate = jnp.where(lane == b, jnp.sum(tot, axis=1, keepdims=True) * (1.0 / MOBA_BLOCK), gate)
        chosen = []
        for _ in range(min(MOBA_TOPK, n_blocks)):
            mx = jnp.max(gate, axis=1, keepdims=True)
            idx = jnp.min(jnp.where(gate == mx, lane.astype(F32), float(LANES)), axis=1, keepdims=True)
            chosen.append(idx)
            gate = jnp.where(lane.astype(F32) == idx, -jnp.inf, gate)
        lane_n = lax.broadcasted_iota(I32, (N_HEADS, ds, page), 2)
        qidx_n = lax.broadcasted_iota(I32, (N_HEADS, ds, page), 1)
        s_new = lax.dot_general(qbd_ref[...], _pad_rows(kn_ref[0], page).astype(BF16), NT,
                                preferred_element_type=F32) + bias_ref[1]
        s_new = jnp.where((lane_n <= qidx_n).reshape(rows, page), s_new, NEG)
        m_tile = s_new
        for pg in range(n_pages):
            b = float(pg // pages_per_block)
            picked = chosen[0] == b
            for idx in chosen[1:]:
                picked = jnp.logical_or(picked, idx == b)
            bias = bias_ref[0] if pg == n_pages - 1 else bias_ref[2]
            s = jnp.where(picked, s_ref[pg] + bias, NEG)
            s_ref[pg] = s
            m_tile = jnp.maximum(m_tile, s)
        m = jnp.max(m_tile, axis=1, keepdims=True)
        p_new = jnp.exp(s_new - m)
        l_tile = p_new
        pnew_ref[...] = p_new.astype(BF16)
        for pg in range(n_pages):
            p = jnp.exp(s_ref[pg] - m)
            l_tile = l_tile + p
            p_ref[pg] = p.astype(BF16)
        linv_ref[...] = 1.0 / jnp.sum(l_tile, axis=1, keepdims=True)

    @pl.when(jnp.logical_and(ph == 1, g == 0))
    def _():
        acc_ref[...] = jnp.dot(pnew_ref[...], _pad_rows(vn_ref[0], page).astype(BF16),
                               preferred_element_type=F32)

    @pl.when(ph == 1)
    def _():
        slot = _ring_advance(pt_ref, page_index_of, u, n_batch * ng, v_hbm, vbuf, vsem)
        acc = acc_ref[...]
        for rr in range(pages):
            acc = acc + lax.dot_general(p_ref[g * pages + rr], vbuf[slot, rr].astype(BF16), NT,
                                        preferred_element_type=F32)
        acc_ref[...] = acc

    @pl.when(jnp.logical_and(ph == 1, g == ng - 1))
    def _():
        o_ref[0] = _head_diag_out(acc_ref[...] * linv_ref[...], ds)


def _moba_sample(page_table, q3, kn3, vn3, bias3, cache_k, cache_v):
    db, n_pages = page_table.shape
    ds = q3.shape[1]
    pages = PAGES_PER_STEP
    page = cache_k.shape[2]
    assert n_pages % pages == 0 and MOBA_BLOCK % page == 0 and (n_pages * page) % MOBA_BLOCK == 0
    assert T5_MAX_DIST <= page and ds <= page
    ng = n_pages // pages
    rows = N_HEADS * ds
    bmap = lambda b, ph, g, pt: (b, 0, 0)
    in_specs = [pl.BlockSpec((1, ds, GROUP_W), bmap), pl.BlockSpec((1, ds, GROUP_W), bmap),
                pl.BlockSpec((1, ds, GROUP_W), bmap),
                pl.BlockSpec(bias3.shape, lambda b, ph, g, pt: (0, 0, 0)),
                pl.BlockSpec(memory_space=pl.ANY), pl.BlockSpec(memory_space=pl.ANY)]
    k_ring, k_sem = _page_ring_scratch(pages, page)
    v_ring, v_sem = _page_ring_scratch(pages, page)
    return pl.pallas_call(
        functools.partial(_moba_sample_kernel, pages=pages, n_batch=db),
        grid_spec=pltpu.PrefetchScalarGridSpec(
            num_scalar_prefetch=1,
            grid=(db, 2, ng),
            in_specs=in_specs,
            out_specs=pl.BlockSpec((1, ds, GROUP_W), bmap),
            scratch_shapes=[pltpu.VMEM((rows, GROUP_W), BF16),
                            pltpu.VMEM((n_pages, rows, page), F32),
                            pltpu.VMEM((n_pages, rows, page), BF16),
                            pltpu.VMEM((rows, page), BF16),
                            pltpu.VMEM((rows, 1), F32),
                            pltpu.VMEM((rows, GROUP_W), F32),
                            k_ring, v_ring, k_sem, v_sem]),
        out_shape=jax.ShapeDtypeStruct((db, ds, GROUP_W), F32),
        compiler_params=_cparams("arbitrary", "arbitrary", "arbitrary"),
        name="moba_sample",
    )(page_table.reshape(-1), q3, kn3, vn3, bias3, cache_k, cache_v)


def _outproj_kernel(of_ref, om_ref, x_ref, gate_ref, shift_ref, scale_ref, gf_ref, gm_ref, wo_ref, gffn_ref,
                    wr_ref, br_ref, cnt0_ref, x1_ref, h2_ref, ri_ref, rw_ref, cnt_ref, carry_ref):
    t = pl.program_id(0)

    @pl.when(t == 0)
    def _():
        carry_ref[...] = cnt0_ref[...]

    tm = x_ref.shape[0]
    nf = (_rms(of_ref[...]) * gf_ref[...]).astype(BF16)
    nm = (_rms(om_ref[...]) * gm_ref[...]).astype(BF16)
    o = (jnp.dot(nf, wo_ref[0:GROUP_W, :], preferred_element_type=F32)
         + jnp.dot(nm, wo_ref[GROUP_W:2 * GROUP_W, :], preferred_element_type=F32))
    x1 = x_ref[...] + gate_ref[...] * o
    x1_ref[...] = x1
    h2 = _rms(x1) * gffn_ref[...]
    h2 = h2 * (1.0 + scale_ref[...]) + shift_ref[...]
    hi = h2.astype(BF16)
    half = h2.shape[1] // 2
    hi32 = pltpu.bitcast(hi.astype(F32), jnp.uint32)
    h2_ref[...] = jnp.bitwise_or(jnp.right_shift(hi32[:, :half], jnp.uint32(16)),
                                 jnp.bitwise_and(hi32[:, half:], jnp.uint32(0xFFFF0000)))

    lo = (h2 - hi.astype(F32)).astype(BF16)
    wr = wr_ref[...]
    whi = wr.astype(BF16)
    wlo = (wr - whi.astype(F32)).astype(BF16)
    lg = (jnp.dot(hi, whi, preferred_element_type=F32) + jnp.dot(lo, whi, preferred_element_type=F32)
          + jnp.dot(hi, wlo, preferred_element_type=F32)) + br_ref[...]
    lane = lax.broadcasted_iota(I32, (tm, LANES), 1)
    lane_f = lane.astype(F32)
    ninf = -jnp.inf
    is_g = jnp.logical_and(lane >= N_EXPERTS, lane < N_EXPERTS + N_GROUPS)
    glog = jnp.where(is_g, lg, ninf)
    gmax = jnp.max(glog, axis=1, keepdims=True)
    gidx = jnp.min(jnp.where(glog == gmax, lane_f, 2.0 * LANES), axis=1, keepdims=True).astype(I32) - N_EXPERTS
    g_w = 1.0 / jnp.sum(jnp.exp(glog - gmax), axis=1, keepdims=True)
    in_grp = jnp.logical_and(lane >= gidx * EXPERTS_PER_GROUP, lane < (gidx + 1) * EXPERTS_PER_GROUP)
    elog = jnp.where(in_grp, lg, ninf)
    e1 = jnp.max(elog, axis=1, keepdims=True)
    i1 = jnp.min(jnp.where(elog == e1, lane_f, 2.0 * LANES), axis=1, keepdims=True).astype(I32)
    z = jnp.sum(jnp.exp(elog - e1), axis=1, keepdims=True)
    elog2 = jnp.where(lane == i1, ninf, elog)
    e2 = jnp.max(elog2, axis=1, keepdims=True)
    i2 = jnp.min(jnp.where(elog2 == e2, lane_f, 2.0 * LANES), axis=1, keepdims=True).astype(I32)
    p1 = 1.0 / z
    p2 = jnp.exp(e2 - e1) / z
    w1 = g_w * (p1 / (p1 + p2))
    w2 = g_w * (p2 / (p1 + p2))

    a = jnp.where(jnp.logical_or(lane == i1, lane == i2), 1.0, 0.0)
    r = lax.broadcasted_iota(I32, (tm, tm), 0)
    c = lax.broadcasted_iota(I32, (tm, tm), 1)
    before = jnp.where(c < r, 1.0, 0.0).astype(BF16)
    pos = jnp.dot(before, a.astype(BF16), preferred_element_type=F32) + carry_ref[...]
    r1 = jnp.sum(jnp.where(lane == i1, pos, 0.0), axis=1, keepdims=True)
    r2 = jnp.sum(jnp.where(lane == i2, pos, 0.0), axis=1, keepdims=True)
    carry_ref[...] = carry_ref[...] + jnp.sum(a, axis=0, keepdims=True)
    cnt_ref[...] = carry_ref[...]

    ri = jnp.where(lane == 0, i1, 0) + jnp.where(lane == 1, i2, 0)
    ri = ri + jnp.where(lane == 2, r1.astype(I32), 0) + jnp.where(lane == 3, r2.astype(I32), 0)
    ri_ref[...] = ri
    rw_ref[...] = jnp.where(lane == 0, w1, 0.0) + jnp.where(lane == 1, w2, 0.0)


def _outproj(of, om, x2, gate3, shift3, scale3, gf, gm, wo, gffn, wr, br, cnt0, *, rows_per_mod, name):
    n, d = x2.shape
    tm = ROW_TILE
    nt = n // tm
    mod_rows = gate3.shape[1]
    mod_map = lambda t: ((t * tm) // rows_per_mod, 0, 0)
    row_map = lambda t: (t, 0)
    const2 = lambda t: (0, 0)
    return pl.pallas_call(
        _outproj_kernel,
        grid=(nt,),
        in_specs=[pl.BlockSpec((tm, GROUP_W), row_map), pl.BlockSpec((tm, GROUP_W), row_map),
                  pl.BlockSpec((tm, d), row_map),
                  pl.BlockSpec((None, mod_rows, d), mod_map), pl.BlockSpec((None, mod_rows, d), mod_map),
                  pl.BlockSpec((None, mod_rows, d), mod_map),
                  pl.BlockSpec((1, GROUP_W), const2), pl.BlockSpec((1, GROUP_W), const2),
                  pl.BlockSpec((d, d), const2), pl.BlockSpec((1, d), const2),
                  pl.BlockSpec((d, LANES), const2), pl.BlockSpec((1, LANES), const2),
                  pl.BlockSpec((1, LANES), const2)],
        out_specs=[pl.BlockSpec((tm, d), row_map), pl.BlockSpec((tm, d // 2), row_map),
                   pl.BlockSpec((tm, LANES), row_map), pl.BlockSpec((tm, LANES), row_map),
                   pl.BlockSpec((1, LANES), const2)],
        out_shape=[jax.ShapeDtypeStruct((n, d), F32), jax.ShapeDtypeStruct((n, d // 2), jnp.uint32),
                   jax.ShapeDtypeStruct((n, LANES), I32), jax.ShapeDtypeStruct((n, LANES), F32),
                   jax.ShapeDtypeStruct((1, LANES), F32)],
        scratch_shapes=[pltpu.VMEM((1, LANES), F32)],
        compiler_params=_cparams("arbitrary"),
        name=name,
    )(of, om, x2, gate3, shift3, scale3, gf, gm, wo, gffn, wr, br, cnt0)


def _scatter_kernel(dest_ref, h_ref, rows_in_ref, rows_ref, stage_ref, sem, *, n_tiles):
    del rows_in_ref
    tm = h_ref.shape[0]
    t = pl.program_id(0)

    def retire(slot):
        for _ in range(2):
            pltpu.make_async_copy(stage_ref.at[slot], rows_ref.at[pl.ds(0, tm)], sem.at[slot]).wait()

    for slot in range(2):
        @pl.when(t % 2 == slot)
        def _():
            stage_ref[slot] = h_ref[...]
            for r in range(tm):
                for k in range(2):
                    pltpu.make_async_copy(stage_ref.at[slot, pl.ds(r, 1)], rows_ref.at[pl.ds(dest_ref[k, r], 1)],
                                          sem.at[slot]).start(priority=k)

    for slot in range(2):
        @pl.when(jnp.logical_and(t > 0, (t - 1) % 2 == slot))
        def _():
            retire(slot)

        @pl.when(jnp.logical_and(t == n_tiles - 1, t % 2 == slot))
        def _():
            retire(slot)


def _scatter_rows(dest2, h2, rows):
    n, d = h2.shape
    tm = ROW_TILE
    return pl.pallas_call(
        functools.partial(_scatter_kernel, n_tiles=n // tm),
        grid=(n // tm,),
        in_specs=[pl.BlockSpec((2, tm), lambda t: (0, t), memory_space=pltpu.SMEM),
                  pl.BlockSpec((tm, d), lambda t: (t, 0)),
                  pl.BlockSpec(memory_space=pl.ANY)],
        out_specs=pl.BlockSpec(memory_space=pl.ANY),
        out_shape=jax.ShapeDtypeStruct(rows.shape, rows.dtype),
        scratch_shapes=[pltpu.VMEM((2, tm, d), h2.dtype), pltpu.SemaphoreType.DMA((2,))],
        input_output_aliases={2: 0},
        compiler_params=_cparams("arbitrary"),
        name="moe_scatter",
    )(dest2, h2, rows)


def _moe_kernel(be_ref, nu_ref, rows_ref, w1_ref, w3_ref, w2_ref, y_ref, w1b_ref, w3b_ref, w2b_ref,
                rbuf_ref, rsem):
    blk = pl.program_id(0)
    n_used = nu_ref[0]

    def fetch(u):
        start = pl.multiple_of(u * MOE_BLOCK, MOE_BLOCK)
        return pltpu.make_async_copy(rows_ref.at[pl.ds(start, MOE_BLOCK)], rbuf_ref.at[u % ROW_RING],
                                     rsem.at[u % ROW_RING])

    @pl.when(blk == 0)
    def _():
        for ahead in range(ROW_RING - 1):
            @pl.when(ahead < n_used)
            def _():
                fetch(ahead).start()

    @pl.when(blk + (ROW_RING - 1) < n_used)
    def _():
        fetch(blk + (ROW_RING - 1)).start()

    @pl.when(blk < n_used)
    def _():
        prev = be_ref[jnp.maximum(blk - 1, 0)]

        @pl.when(jnp.logical_or(blk == 0, be_ref[blk] != prev))
        def _():
            w1b_ref[...] = w1_ref[0].astype(BF16)
            w3b_ref[...] = w3_ref[0].astype(BF16)
            w2b_ref[...] = w2_ref[0].astype(BF16)

        fetch(blk).wait()
        packed = rbuf_ref[blk % ROW_RING]
        x_lo = pltpu.bitcast(jnp.left_shift(packed, jnp.uint32(16)), F32)
        x_hi = pltpu.bitcast(jnp.bitwise_and(packed, jnp.uint32(0xFFFF0000)), F32)
        x = jnp.concatenate([x_lo, x_hi], axis=1).astype(BF16)
        a = jnp.dot(x, w1b_ref[...], preferred_element_type=F32)
        b = jnp.dot(x, w3b_ref[...], preferred_element_type=F32)
        hm = (a * jax.nn.sigmoid(a)) * b
        y_ref[...] = jnp.dot(hm.astype(BF16), w2b_ref[...], preferred_element_type=F32)

    @pl.when(blk >= nu_ref[0])
    def _():
        y_ref[...] = jnp.zeros_like(y_ref)


def _moe_experts(block_e, n_used, rows, w1, w3, w2):
    p = rows.shape[0]
    d, de = w1.shape[1], w1.shape[2]
    assert rows.shape[1] * 2 == d
    nblk = p // MOE_BLOCK
    return pl.pallas_call(
        _moe_kernel,
        grid_spec=pltpu.PrefetchScalarGridSpec(
            num_scalar_prefetch=2,
            grid=(nblk,),
            in_specs=[pl.BlockSpec(memory_space=pl.ANY),
                      pl.BlockSpec((1, d, de), lambda b, be, nu: (be[b], 0, 0)),
                      pl.BlockSpec((1, d, de), lambda b, be, nu: (be[b], 0, 0)),
                      pl.BlockSpec((1, de, d), lambda b, be, nu: (be[b], 0, 0))],
            out_specs=pl.BlockSpec((MOE_BLOCK, d), lambda b, be, nu: (b, 0)),
            scratch_shapes=[pltpu.VMEM((d, de), BF16), pltpu.VMEM((d, de), BF16), pltpu.VMEM((de, d), BF16),
                            pltpu.VMEM((ROW_RING, MOE_BLOCK, d // 2), jnp.uint32),
                            pltpu.SemaphoreType.DMA((ROW_RING,))]),
        out_shape=jax.ShapeDtypeStruct((p, d), F32),
        compiler_params=_cparams("arbitrary"),
        name="moe_experts",
    )(block_e, n_used, rows, w1, w3, w2)


def _combine_kernel(dest_ref, dest_next_ref, x1_ref, gate_ref, rw_ref, gfin_ref, y_ref, o_ref, ybuf_ref, sem):
    tm = x1_ref.shape[0]
    t = pl.program_id(0)

    def gather(idx_ref, slot):
        for r in range(tm):
            for k in range(2):
                pltpu.make_async_copy(y_ref.at[pl.ds(idx_ref[k, r], 1)], ybuf_ref.at[slot, k, pl.ds(r, 1)],
                                      sem.at[slot]).start(priority=k)

    @pl.when(t == 0)
    def _():
        gather(dest_ref, 0)

    for slot in range(2):
        @pl.when(jnp.logical_and(t + 1 < pl.num_programs(0), (t + 1) % 2 == slot))
        def _():
            gather(dest_next_ref, slot)

    cur = t % 2
    for k in range(2):
        pltpu.make_async_copy(y_ref.at[pl.ds(0, tm)], ybuf_ref.at[cur, k], sem.at[cur]).wait()
    rw = rw_ref[...]
    moe = rw[:, 0:1] * ybuf_ref[cur, 0] + rw[:, 1:2] * ybuf_ref[cur, 1]
    xo = x1_ref[...] + gate_ref[...] * moe
    o_ref[...] = _rms(xo) * gfin_ref[...]


def _combine(dest2, x1, gate3, rw, gfin, y, *, rows_per_mod, name):
    n, d = x1.shape
    tm = ROW_TILE
    nt = n // tm
    mod_rows = gate3.shape[1]
    return pl.pallas_call(
        _combine_kernel,
        grid=(nt,),
        in_specs=[pl.BlockSpec((2, tm), lambda t: (0, t), memory_space=pltpu.SMEM),
                  pl.BlockSpec((2, tm), lambda t: (0, jnp.minimum(t + 1, nt - 1)), memory_space=pltpu.SMEM),
                  pl.BlockSpec((tm, d), lambda t: (t, 0)),
                  pl.BlockSpec((None, mod_rows, d), lambda t: ((t * tm) // rows_per_mod, 0, 0)),
                  pl.BlockSpec((tm, LANES), lambda t: (t, 0)),
                  pl.BlockSpec((1, d), lambda t: (0, 0)),
                  pl.BlockSpec(memory_space=pl.ANY)],
        out_specs=pl.BlockSpec((tm, d), lambda t: (t, 0)),
        out_shape=jax.ShapeDtypeStruct((n, d), F32),
        scratch_shapes=[pltpu.VMEM((2, 2, tm, d), F32), pltpu.SemaphoreType.DMA((2,))],
        compiler_params=_cparams("arbitrary"),
        name=name,
    )(dest2, dest2, x1, gate3, rw, gfin, y)


def kernel(x_prompt, x_sample, cache_fox_k, cache_fox_v, cache_fox_logf, cache_moba_k, cache_moba_v, page_table, c_prompt, c_sample, w_ada, b_ada, g_attn, w_in, b_forget, g_out_fox, g_out_moba, t5_bias, w_out, g_ffn, w_router_group, b_router_group, w_router_expert, b_router_expert, w1, w3, w2, g_final):
    bsz, seq, d = x_prompt.shape
    db, ds, _ = x_sample.shape
    depth = w_ada.shape[0]
    n_phys, page = cache_fox_k.shape[1], cache_fox_k.shape[2]
    assert depth == 1, "one trunk layer"
    assert seq % ROW_TILE == 0 and (db * ds) % ROW_TILE == 0 and ROW_TILE % ds == 0
    assert N_EXPERTS + N_GROUPS <= LANES and N_HEADS <= LANES
    l = 0
    n_p, n_s = bsz * seq, db * ds
    xp2 = x_prompt.reshape(n_p, d)
    xs2 = x_sample.reshape(n_s, d)

    mod = _ada(jnp.concatenate([c_prompt, c_sample], axis=0), w_ada[l], b_ada[l])
    mod_p = [mod[:bsz, i * d:(i + 1) * d].reshape(bsz, 1, d) for i in range(6)]
    mod_s = [jnp.repeat(mod[bsz:, i * d:(i + 1) * d], ds, axis=0).reshape(n_s // ROW_TILE, ROW_TILE, d)
             for i in range(6)]

    w = GROUP_W
    wl = w_in[l]
    wcat = jnp.concatenate([wl[:, :3 * w], wl[:, 3 * w + N_HEADS:],
                            wl[:, 3 * w:3 * w + N_HEADS], jnp.zeros((d, LANES - N_HEADS), F32)],
                           axis=1).astype(BF16)
    bf_pad = jnp.pad(b_forget[l], (0, LANES - N_HEADS)).reshape(1, LANES)
    g_attn2 = g_attn[l].reshape(1, d)

    fqt, fk_t, fv_t, lf_t, mqt, mk_t, mv_t, fk_aug, fqt_aug, kmean, fkb, fvtb, mkb, mvtb = _inproj(
        xp2, mod_p[0], mod_p[1], g_attn2, wcat, bf_pad, rows_per_mod=seq, prompt_extras=True, seq_len=seq)
    sfq, sfk, sfv, slf, smq, smk, smv = _inproj(
        xs2, mod_s[0], mod_s[1], g_attn2, wcat, bf_pad, rows_per_mod=ROW_TILE, prompt_extras=False, seq_len=ds)

    bias_p = _t5_tiles(t5_bias, (0, MOBA_BLOCK, 2 * MOBA_BLOCK), MOBA_BLOCK, MOBA_BLOCK, sign=-1, scale=LOG2E)
    o_fox_p, o_moba_p = _prompt_attention(fqt, fkb, fvtb, fk_aug, fqt_aug, mqt, mkb, mvtb,
                                          kmean.reshape(bsz, seq // MOBA_BLOCK, w), bias_p, bsz, seq)

    to3 = lambda a: a.reshape(db, ds, w)
    page_t = lambda c: c[l].transpose(0, 2, 3, 1).reshape(n_phys, w, page)
    cfk, cfv, cmk, cmv = page_t(cache_fox_k), page_t(cache_fox_v), page_t(cache_moba_k), page_t(cache_moba_v)
    clf_t = cache_fox_logf[l].transpose(0, 2, 1)
    slf_t = jnp.pad(slf.reshape(db, ds, N_HEADS).transpose(0, 2, 1), ((0, 0), (0, 0), (0, page - ds)))
    o_fox_s = _fox_sample(page_table, to3(sfq), to3(sfk), to3(sfv), slf_t, cfk, cfv, clf_t)
    bias_s = _t5_tiles(t5_bias, (page, 0, 2 * T5_MAX_DIST + page), ds, page, sign=1)
    bias_s = bias_s.transpose(1, 0, 2, 3).reshape(3, N_HEADS * ds, page)
    o_moba_s = _moba_sample(page_table, to3(smq), to3(smk), to3(smv), bias_s, cmk, cmv)

    wr = jnp.concatenate([w_router_expert[l], w_router_group[l],
                          jnp.zeros((d, LANES - N_EXPERTS - N_GROUPS), F32)], axis=1)
    br = jnp.concatenate([b_router_expert[l], b_router_group[l],
                          jnp.zeros((LANES - N_EXPERTS - N_GROUPS,), F32)]).reshape(1, LANES)
    wo = w_out[l].astype(BF16)
    gf, gm, gffn = g_out_fox[l].reshape(1, w), g_out_moba[l].reshape(1, w), g_ffn[l].reshape(1, d)
    x1_p, h2_p, ri_p, rw_p, cnt_p = _outproj(
        o_fox_p, o_moba_p, xp2, mod_p[2], mod_p[3], mod_p[4], gf, gm, wo, gffn, wr, br,
        jnp.zeros((1, LANES), F32), rows_per_mod=seq, name="outproj_prompt")
    x1_s, h2_s, ri_s, rw_s, cnt = _outproj(
        o_fox_s.reshape(n_s, w), o_moba_s.reshape(n_s, w), xs2, mod_s[2], mod_s[3], mod_s[4], gf, gm, wo, gffn,
        wr, br, cnt_p, rows_per_mod=ROW_TILE, name="outproj_sample")

    counts = cnt[0, :N_EXPERTS].astype(I32)
    padded = (counts + MOE_BLOCK - 1) // MOE_BLOCK * MOE_BLOCK
    pend = jnp.cumsum(padded)
    pstart = pend - padded
    n_asg = 2 * (n_p + n_s)
    n_blocks = -(-(n_asg + N_EXPERTS * (MOE_BLOCK - 1)) // MOE_BLOCK)
    blk_start = jnp.arange(n_blocks, dtype=I32) * MOE_BLOCK
    block_e = jnp.clip(jnp.sum((pend[None, :] <= blk_start[:, None]).astype(I32), axis=1), 0, N_EXPERTS - 1)
    n_used = (pend[-1:] // MOE_BLOCK).astype(I32)

    def dest_of(ri):
        hit = ri[:, 0:2, None] == jnp.arange(N_EXPERTS, dtype=I32)
        return (jnp.sum(jnp.where(hit, pstart, 0), axis=-1) + ri[:, 2:4]).T.astype(I32)

    dest_p, dest_s = dest_of(ri_p), dest_of(ri_s)
    rows = jnp.zeros((n_blocks * MOE_BLOCK, d // 2), jnp.uint32)
    rows = _scatter_rows(dest_p, h2_p, rows)
    rows = _scatter_rows(dest_s, h2_s, rows)
    y = _moe_experts(block_e, n_used, rows, w1[l], w3[l], w2[l])
    gfin = g_final.reshape(1, d)
    y_prompt = _combine(dest_p, x1_p, mod_p[5], rw_p, gfin, y, rows_per_mod=seq, name="combine_prompt")
    y_sample = _combine(dest_s, x1_s, mod_s[5], rw_s, gfin, y, rows_per_mod=ROW_TILE, name="combine_sample")

    hp = lambda a: a.reshape(depth, bsz, N_HEADS, HEAD_DIM, seq).transpose(0, 1, 4, 2, 3)
    hs = lambda a: a.reshape(depth, db, ds, N_HEADS, HEAD_DIM)
    return (y_prompt.reshape(bsz, seq, d), y_sample.reshape(db, ds, d),
            hp(fk_t), hp(fv_t), lf_t.reshape(depth, bsz, N_HEADS, seq).transpose(0, 1, 3, 2), hp(mk_t), hp(mv_t),
            hs(sfk), hs(sfv), slf.reshape(depth, db, ds, N_HEADS), hs(smk), hs(smv))
```

```python
import functools
import math

import jax
import jax.numpy as jnp
from jax import lax
from jax.experimental import pallas as pl
from jax.experimental.pallas import tpu as pltpu

F32 = jnp.float32
BF16 = jnp.bfloat16
I32 = jnp.int32

HEAD_DIM = 64
N_HEADS = 8
GROUP_W = N_HEADS * HEAD_DIM
LANES = 128
HEADS_PER_LANE_TILE = LANES // HEAD_DIM
N_PAIRS = N_HEADS // HEADS_PER_LANE_TILE
SM_SCALE = HEAD_DIM ** -0.5
LOG2E = math.log2(math.e)
MOBA_BLOCK = 256
MOBA_TOPK = 3
T5_BUCKETS = 32
T5_MAX_DIST = 128
N_GROUPS = 4
EXPERTS_PER_GROUP = 8
N_EXPERTS = N_GROUPS * EXPERTS_PER_GROUP
RMS_EPS = 1e-6
ROW_TILE = 256
DISPATCH_TILE = 512
ATTN_TILE = 256
K_CHUNK = 128
AUG = LANES // N_HEADS
SUM_ROWS = 16
MOE_BLOCK = 512
PAGES_PER_STEP = 16
PAGE_RING = 3
ROW_RING = 3
NEG = -1e30
VMEM_LIMIT = 48 * 1024 * 1024
HIGHEST = lax.Precision.HIGHEST
NT = (((1,), (1,)), ((), ()))


def _cparams(*sem):
    return pltpu.CompilerParams(dimension_semantics=sem, vmem_limit_bytes=VMEM_LIMIT)


def _rms(x):
    return x * lax.rsqrt(jnp.mean(x * x, axis=-1, keepdims=True) + RMS_EPS)


def _spare_half(h):
    return 0 if h % HEADS_PER_LANE_TILE else HEAD_DIM


def _split3_bf16(x):
    hi = x.astype(BF16)
    r1 = x - hi.astype(F32)
    mid = r1.astype(BF16)
    lo = (r1 - mid.astype(F32)).astype(BF16)
    return hi, mid, lo


def _ada_kernel(c_ref, w_ref, b_ref, o_ref):
    c = c_ref[...]
    a = c * jax.nn.sigmoid(c)
    o_ref[...] = jnp.dot(a, w_ref[...], preferred_element_type=F32, precision=HIGHEST) + b_ref[...]


def _ada(c, w, b):
    n, d = c.shape
    e = w.shape[1]
    tn = 1024
    return pl.pallas_call(
        _ada_kernel,
        grid=(e // tn,),
        in_specs=[pl.BlockSpec((n, d), lambda j: (0, 0)),
                  pl.BlockSpec((d, tn), lambda j: (0, j)),
                  pl.BlockSpec((1, tn), lambda j: (0, j))],
        out_specs=pl.BlockSpec((n, tn), lambda j: (0, j)),
        out_shape=jax.ShapeDtypeStruct((n, e), F32),
        compiler_params=_cparams("arbitrary"),
        name="ada",
    )(c, w, b.reshape(1, e))


def _inproj_kernel(x_ref, shift_ref, scale_ref, g_ref, w_ref, bf_ref,
                   fq_ref, fk_ref, fv_ref, lf_ref, mq_ref, mk_ref, mv_ref, *rest,
                   prompt_extras, tiles_per_seq):
    x = x_ref[...]
    h = _rms(x) * g_ref[...]
    h = h * (1.0 + scale_ref[...]) + shift_ref[...]
    z = jnp.dot(h.astype(BF16), w_ref[...], preferred_element_type=F32)
    w = GROUP_W
    q_scale = SM_SCALE * LOG2E if prompt_extras else SM_SCALE
    fq = z[:, 0:w] * q_scale
    mq = z[:, 3 * w:4 * w] * q_scale
    if prompt_extras:
        fq_ref[0] = fq.T.astype(BF16)
        mq_ref[0] = mq.T.astype(BF16)
    else:
        fq_ref[...] = fq.astype(BF16)
        mq_ref[...] = mq.astype(BF16)
    fk, fv = z[:, w:2 * w], z[:, 2 * w:3 * w]
    mk, mv = z[:, 4 * w:5 * w], z[:, 5 * w:6 * w]
    fg = z[:, 6 * w:6 * w + LANES] + bf_ref[...]
    lf = jnp.minimum(fg, 0.0) - jnp.log1p(jnp.exp(-jnp.abs(fg)))
    if not prompt_extras:
        fk_ref[...] = fk
        fv_ref[...] = fv
        mk_ref[...] = mk
        mv_ref[...] = mv
        lf_ref[...] = lf[:, :N_HEADS]
        return
    fkaug_ref, fqtaug_ref, kmean_ref, fkb_ref, fvtb_ref, mkb_ref, mvtb_ref, carry_ref = rest
    tm = x.shape[0]
    fk_ref[0] = fk.T
    fv_t = fv.T
    fv_ref[0] = fv_t
    mk_ref[0] = mk.T
    mv_t = mv.T
    mv_ref[0] = mv_t
    fkb_ref[...] = fk.astype(BF16)
    fvtb_ref[0] = fv_t.astype(BF16)
    mkb_ref[...] = mk.astype(BF16)
    mvtb_ref[0] = mv_t.astype(BF16)
    lf_ref[0] = lf.T[:N_HEADS, :]
    kmean_ref[0] = jnp.mean(mk, axis=0, keepdims=True)

    @pl.when(pl.program_id(0) % tiles_per_seq == 0)
    def _():
        carry_ref[...] = jnp.zeros_like(carry_ref)

    r = lax.broadcasted_iota(I32, (tm, tm), 0)
    c = lax.broadcasted_iota(I32, (tm, tm), 1)
    tri = jnp.where(c <= r, 1.0, 0.0).astype(BF16)
    hi, mid, lo = _split3_bf16(lf)
    cs = (jnp.dot(tri, hi, preferred_element_type=F32)
          + jnp.dot(tri, mid, preferred_element_type=F32)
          + jnp.dot(tri, lo, preferred_element_type=F32)) + carry_ref[...]
    carry_ref[...] = cs[tm - 1:tm, :]
    hi, mid, lo = _split3_bf16(cs * LOG2E)
    er = lax.broadcasted_iota(I32, (LANES, LANES), 0)
    ec = lax.broadcasted_iota(I32, (LANES, LANES), 1)

    def spread(x, off):
        e = jnp.where(jnp.logical_and(er < N_HEADS, ec == AUG * er + off), 1.0, 0.0).astype(BF16)
        return jnp.dot(x, e, preferred_element_type=F32)

    lane = lax.broadcasted_iota(I32, (tm, LANES), 1)
    slot = lane % AUG
    fq_aug = (spread(hi, 0) + spread(mid, 1) + spread(lo, 2)
              + jnp.where(jnp.logical_and(slot >= 3, slot < 6), 1.0, 0.0))
    fk_aug = jnp.where(slot < 3, 1.0, 0.0) - (spread(hi, 3) + spread(mid, 4) + spread(lo, 5))
    fq_aug_t = fq_aug.T
    k_tiles, q_tiles = [], []
    for hd in range(N_HEADS):
        base = _spare_half(hd)
        shift = (base - AUG * hd) % LANES
        moved = pltpu.roll(fk_aug, shift, 1) if shift else fk_aug
        k_tiles.append(jnp.where(jnp.logical_and(lane >= base, lane < base + AUG), moved, 0.0))
        q_rows = [fq_aug_t[AUG * hd:AUG * (hd + 1), :]]
        if base:
            q_rows.insert(0, jnp.zeros((base, tm), F32))
        if LANES - base - AUG:
            q_rows.append(jnp.zeros((LANES - base - AUG, tm), F32))
        q_tiles.append(jnp.concatenate(q_rows, axis=0))
    fkaug_ref[...] = jnp.concatenate(k_tiles, axis=1).astype(BF16)
    fqtaug_ref[0] = jnp.concatenate(q_tiles, axis=0).astype(BF16)


def _inproj(x2, shift3, scale3, g_attn, wcat, bf_pad, *, rows_per_mod, prompt_extras, seq_len):
    n, d = x2.shape
    tm = ROW_TILE
    assert n % tm == 0
    nt = n // tm
    mod_rows = shift3.shape[1]
    mod_map = lambda t: ((t * tm) // rows_per_mod, 0, 0)
    row_map = lambda t: (t, 0)
    const2 = lambda t: (0, 0)
    ecols = wcat.shape[1]
    row_bf16 = (jax.ShapeDtypeStruct((n, GROUP_W), BF16), pl.BlockSpec((tm, GROUP_W), row_map))
    scratch = []
    tiles_per_seq = 1
    if prompt_extras:
        assert tm == MOBA_BLOCK and seq_len % tm == 0
        tiles_per_seq = seq_len // tm
        nb = n // seq_len
        t_map = lambda t: (t // tiles_per_seq, 0, t % tiles_per_seq)
        kv = (jax.ShapeDtypeStruct((nb, GROUP_W, seq_len), F32), pl.BlockSpec((1, GROUP_W, tm), t_map))
        lfo = (jax.ShapeDtypeStruct((nb, N_HEADS, seq_len), F32), pl.BlockSpec((1, N_HEADS, tm), t_map))
        kvt_bf16 = (jax.ShapeDtypeStruct((nb, GROUP_W, seq_len), BF16), pl.BlockSpec((1, GROUP_W, tm), t_map))
        outs = [kvt_bf16, kv, kv, lfo, kvt_bf16, kv, kv,
                (jax.ShapeDtypeStruct((n, N_HEADS * LANES), BF16), pl.BlockSpec((tm, N_HEADS * LANES), row_map)),
                (jax.ShapeDtypeStruct((nb, N_HEADS * LANES, seq_len), BF16),
                 pl.BlockSpec((1, N_HEADS * LANES, tm), t_map)),
                (jax.ShapeDtypeStruct((nt, 1, GROUP_W), F32), pl.BlockSpec((1, 1, GROUP_W), lambda t: (t, 0, 0)))]
        outs += [row_bf16, kvt_bf16, row_bf16, kvt_bf16]
        scratch = [pltpu.VMEM((1, LANES), F32)]
    else:
        kv = (jax.ShapeDtypeStruct((n, GROUP_W), F32), pl.BlockSpec((tm, GROUP_W), row_map))
        lfo = (jax.ShapeDtypeStruct((n, N_HEADS), F32), pl.BlockSpec((tm, N_HEADS), row_map))
        outs = [row_bf16, kv, kv, lfo, row_bf16, kv, kv]
    return pl.pallas_call(
        functools.partial(_inproj_kernel, prompt_extras=prompt_extras, tiles_per_seq=tiles_per_seq),
        grid=(nt,),
        in_specs=[pl.BlockSpec((tm, d), row_map),
                  pl.BlockSpec((None, mod_rows, d), mod_map),
                  pl.BlockSpec((None, mod_rows, d), mod_map),
                  pl.BlockSpec((1, d), const2),
                  pl.BlockSpec((d, ecols), const2),
                  pl.BlockSpec((1, LANES), const2)],
        out_specs=[o[1] for o in outs],
        out_shape=[o[0] for o in outs],
        scratch_shapes=scratch,
        compiler_params=_cparams("arbitrary"),
        name="inproj_prompt" if prompt_extras else "inproj_sample",
    )(x2, shift3, scale3, g_attn, wcat, bf_pad)


def _t5_kernel(t5_ref, o_ref, *, offs, sign, scale):
    h = pl.program_id(0)
    rows, cols = o_ref.shape[2], o_ref.shape[3]
    r = lax.broadcasted_iota(I32, (rows, cols), 0)
    c = lax.broadcasted_iota(I32, (rows, cols), 1)
    max_exact = T5_BUCKETS // 2
    for k, off in enumerate(offs):
        if off - max(rows, cols) + 1 >= T5_MAX_DIST:
            o_ref[0, k] = jnp.full((rows, cols), t5_ref[T5_BUCKETS - 1, h] * scale, F32)
            continue
        rel = jnp.maximum(off + sign * (r - c), 0)
        relf = jnp.maximum(rel, 1).astype(F32)
        large = max_exact + (jnp.log(relf / max_exact) / math.log(T5_MAX_DIST / max_exact)
                             * (T5_BUCKETS - max_exact)).astype(I32)
        large = jnp.minimum(large, T5_BUCKETS - 1)
        bucket = jnp.where(rel < max_exact, rel, large)
        acc = jnp.zeros((rows, cols), F32)
        for b in range(T5_BUCKETS):
            acc = jnp.where(bucket == b, t5_ref[b, h], acc)
        o_ref[0, k] = acc if scale == 1.0 else acc * scale


def _t5_tiles(t5_bias, offs, rows, cols, sign, scale=1.0):
    nh = t5_bias.shape[1]
    return pl.pallas_call(
        functools.partial(_t5_kernel, offs=tuple(offs), sign=sign, scale=scale),
        grid=(nh,),
        in_specs=[pl.BlockSpec(memory_space=pltpu.SMEM)],
        out_specs=pl.BlockSpec((1, len(offs), rows, cols), lambda h: (h, 0, 0, 0)),
        out_shape=jax.ShapeDtypeStruct((nh, len(offs), rows, cols), F32),
        compiler_params=_cparams("arbitrary"),
        name="t5_tiles",
    )(t5_bias)


def _pair_masks(rows):
    lane = lax.broadcasted_iota(I32, (rows, LANES), 1)
    lo = lane < HEAD_DIM
    return lo, jnp.logical_not(lo)


def _masked_qt(qt_ref, h):
    p, e = divmod(h, HEADS_PER_LANE_TILE)
    qt = qt_ref[0, p * LANES:(p + 1) * LANES, :]
    row = lax.broadcasted_iota(I32, qt.shape, 0)
    keep = (row >= HEAD_DIM) if e else (row < HEAD_DIM)
    return jnp.where(keep, qt, jnp.zeros_like(qt))


def _pair_of(h):
    return slice((h // HEADS_PER_LANE_TILE) * LANES, (h // HEADS_PER_LANE_TILE + 1) * LANES)


def _flash_step(lhs_of, rhs_of, vt_ref, adjust, states, tk, tq):
    chunks = [slice(c * K_CHUNK, (c + 1) * K_CHUNK) for c in range(tk // K_CHUNK)]
    for h, (m_ref, _, _, s_scr, _, a_ref) in enumerate(states):
        rhs = rhs_of(h)
        m8 = jnp.full((8, tq), NEG, F32)
        for c, rows in enumerate(chunks):
            s = jnp.dot(lhs_of(h, c), rhs, preferred_element_type=F32)
            if adjust is not None:
                s = adjust(h, c, s)
            s_scr[rows, :] = s
            m8 = jnp.maximum(m8, jnp.max(s.reshape(K_CHUNK // 8, 8, tq), axis=0))
        m_prev = m_ref[...]
        m_new = jnp.maximum(m_prev, jnp.max(m8, axis=0, keepdims=True))
        a_ref[...] = jnp.exp2(m_prev - m_new)
        m_ref[...] = m_new
    for m_ref, _, _, s_scr, p_scr, _ in states:
        m_new = m_ref[...]
        for rows in chunks:
            p_scr[rows, :] = jnp.exp2(s_scr[rows, :] - m_new).astype(BF16)
    ones = jnp.ones((SUM_ROWS, tk), BF16)
    for h, (_, l_ref, acc_ref, _, p_scr, a_ref) in enumerate(states):
        feat = slice(h * HEAD_DIM, (h + 1) * HEAD_DIM)
        vt_sum = jnp.concatenate([vt_ref[0, feat, :], ones], axis=0)
        pv = jnp.dot(vt_sum, p_scr[...], preferred_element_type=F32)
        acc_ref[...] = a_ref[...] * acc_ref[...] + pv[:HEAD_DIM]
        l_ref[...] = a_ref[...] * l_ref[...] + pv[HEAD_DIM:HEAD_DIM + 1]


FLASH_BUFS_PER_HEAD = 6


def _flash_scratch(tq, tk):
    per_head = [pltpu.VMEM((1, tq), F32), pltpu.VMEM((1, tq), F32), pltpu.VMEM((HEAD_DIM, tq), F32),
                pltpu.VMEM((tk, tq), F32), pltpu.VMEM((tk, tq), BF16), pltpu.VMEM((1, tq), F32)]
    return per_head * N_HEADS


def _flash_states(scratch):
    n = FLASH_BUFS_PER_HEAD
    return [scratch[n * h:n * (h + 1)] for h in range(N_HEADS)]


def _flash_init(states):
    for m_ref, l_ref, acc_ref, _, _, _ in states:
        m_ref[...] = jnp.full_like(m_ref, NEG)
        l_ref[...] = jnp.zeros_like(l_ref)
        acc_ref[...] = jnp.zeros_like(acc_ref)


def _flash_finish(o_ref, states):
    o_t = jnp.concatenate([acc_ref[...] / l_ref[...] for _, l_ref, acc_ref, _, _, _ in states], axis=0)
    o_ref[...] = o_t.T


def _chunk_causal(c, tq):
    krow = c * K_CHUNK + lax.broadcasted_iota(I32, (K_CHUNK, tq), 0)
    qcol = lax.broadcasted_iota(I32, (K_CHUNK, tq), 1)
    return krow <= qcol


def _head_rows(x, h, rows_per_head):
    row = lax.broadcasted_iota(I32, x.shape, 0)
    keep = jnp.logical_and(row >= h * rows_per_head, row < (h + 1) * rows_per_head)
    return jnp.where(keep, x, jnp.zeros_like(x))


def _own_half(shape, h, axis):
    pos = lax.broadcasted_iota(I32, shape, axis)
    return (pos >= HEAD_DIM) if h % HEADS_PER_LANE_TILE else (pos < HEAD_DIM)


def _fox_prompt_body(i, j, q_ref, k_ref, vt_ref, fqt_ref, fk_ref, o_ref, states):
    tq, tk = q_ref.shape[2], k_ref.shape[0]

    def lhs_of(h, c):
        rows = slice(c * K_CHUNK, (c + 1) * K_CHUNK)
        k = k_ref[rows, _pair_of(h)]
        return jnp.where(_own_half(k.shape, h, 1), k, fk_ref[rows, h * LANES:(h + 1) * LANES])

    def rhs_of(h):
        qt = q_ref[0, _pair_of(h), :]
        return jnp.where(_own_half(qt.shape, h, 0), qt, fqt_ref[0, h * LANES:(h + 1) * LANES, :])

    def step(diagonal):
        adjust = (lambda h, c, s: jnp.where(_chunk_causal(c, tq), s, NEG)) if diagonal else None
        _flash_step(lhs_of, rhs_of, vt_ref, adjust, states, tk, tq)

    @pl.when(j == i)
    def _():
        _flash_init(states)
        step(True)

    @pl.when(j < i)
    def _():
        step(False)

    @pl.when(jnp.logical_or(j == i - 1, i == 0))
    def _():
        _flash_finish(o_ref, states)


def _tri_tables(nq, own_first):
    qi, kj = [], []
    for i in range(nq):
        order = ([i] + list(range(i))) if own_first else list(range(i + 1))
        for j in order:
            qi.append(i)
            kj.append(j)
    return jnp.asarray(qi, I32), jnp.asarray(kj, I32)


def _moba_prompt_body(i, j, q_ref, k_ref, vt_ref, kmean_ref, bias_ref, o_ref, states, qb_ref):
    tq, tk = q_ref.shape[2], k_ref.shape[0]
    nb = kmean_ref.shape[1]
    nbp = -(-nb // 8) * 8
    assert 3 + nb <= AUG
    far_kind = bias_ref.shape[1] - 1

    def k_chunk(h, c, spare):
        k = k_ref[c * K_CHUNK:(c + 1) * K_CHUNK, _pair_of(h)]
        return jnp.where(_own_half(k.shape, h, 1), k, spare)

    def off_diagonal_step(far):
        lane = lax.broadcasted_iota(I32, (K_CHUNK, LANES), 1)
        key_side = []
        for base in (_spare_half(0), _spare_half(1)):
            slot = lane - base
            ones_at = (slot == 3 + j)
            if far:
                ones_at = jnp.logical_or(ones_at, jnp.logical_and(slot >= 0, slot < 3))
            key_side.append(jnp.where(ones_at, 1.0, 0.0).astype(BF16))
        adjust = None if far else (lambda h, c, s: s + bias_ref[h, 1, c * K_CHUNK:(c + 1) * K_CHUNK, :])
        _flash_step(lambda h, c: k_chunk(h, c, key_side[h % HEADS_PER_LANE_TILE]),
                    lambda h: qb_ref[h], vt_ref, adjust, states, tk, tq)

    @pl.when(j == i)
    def _():
        _flash_init(states)
        blk_row = lax.broadcasted_iota(I32, (nbp, tq), 0)
        aug_row = lax.broadcasted_iota(I32, (AUG, tq), 0)
        past = blk_row < i
        for h in range(N_HEADS):
            km = _pad_rows(kmean_ref[0, :, _pair_of(h)], nbp).astype(BF16)
            g = jnp.dot(km, _masked_qt(q_ref, h), preferred_element_type=F32)
            c_hi, c_mid, c_lo = _split3_bf16(bias_ref[h, far_kind, 0:1, 0:1])
            aug = jnp.where(aug_row == 0, c_hi.astype(F32),
                            jnp.where(aug_row == 1, c_mid.astype(F32),
                                      jnp.where(aug_row == 2, c_lo.astype(F32), 0.0)))
            for jb in range(nb):
                gj = g[jb:jb + 1, :]
                beats = jnp.logical_and(past, jnp.logical_or(g > gj, jnp.logical_and(g == gj, blk_row < jb)))
                rank = jnp.sum(jnp.where(beats, 1.0, 0.0), axis=0, keepdims=True)
                chosen = jnp.logical_and(rank < MOBA_TOPK, jb < i)
                aug = jnp.where(aug_row == 3 + jb, jnp.where(chosen, 0.0, NEG), aug)
            own = q_ref[0, h * HEAD_DIM:(h + 1) * HEAD_DIM, :]
            spare = jnp.concatenate([aug.astype(BF16), jnp.zeros((HEAD_DIM - AUG, tq), BF16)], axis=0)
            qb_ref[h] = jnp.concatenate([spare, own] if h % HEADS_PER_LANE_TILE else [own, spare], axis=0)

        def adjust(h, c, s):
            s = s + bias_ref[h, 0, c * K_CHUNK:(c + 1) * K_CHUNK, :]
            return jnp.where(_chunk_causal(c, tq), s, NEG)

        zero = jnp.zeros((K_CHUNK, LANES), BF16)
        _flash_step(lambda h, c: k_chunk(h, c, zero), lambda h: qb_ref[h], vt_ref, adjust, states, tk, tq)

        @pl.when(i == 0)
        def _():
            _flash_finish(o_ref, states)

    @pl.when(j == i - 1)
    def _():
        off_diagonal_step(far=False)
        _flash_finish(o_ref, states)

    @pl.when(j < i - 1)
    def _():
        off_diagonal_step(far=True)


def _prompt_attn_kernel(qi_ref, kj_ref, fq_ref, fk_ref, fvt_ref, fqt_aug_ref, fk_aug_ref,
                        mq_ref, mk_ref, mvt_ref, kmean_ref, bias_ref, o_fox_ref, o_moba_ref, *scratch):
    n = FLASH_BUFS_PER_HEAD * N_HEADS
    t = pl.program_id(1)
    i = qi_ref[t]
    j = kj_ref[t]
    _fox_prompt_body(i, j, fq_ref, fk_ref, fvt_ref, fqt_aug_ref, fk_aug_ref, o_fox_ref,
                     _flash_states(scratch[:n]))
    _moba_prompt_body(i, j, mq_ref, mk_ref, mvt_ref, kmean_ref, bias_ref, o_moba_ref,
                      _flash_states(scratch[n:2 * n]), scratch[2 * n])


def _prompt_attention(fqt, fk, fvt, fk_aug, fqt_aug, mqt, mk, mvt, kmean, bias, batch, seq):
    assert ATTN_TILE == MOBA_BLOCK
    tq = tk = MOBA_BLOCK
    nq = seq // tq
    qi, kj = _tri_tables(nq, own_first=True)
    qmap = lambda b, t, qi, kj: (b * nq + qi[t], 0)
    kmap = lambda b, t, qi, kj: (b * nq + kj[t], 0)
    qtmap = lambda b, t, qi, kj: (b, 0, qi[t])
    ktmap = lambda b, t, qi, kj: (b, 0, kj[t])
    out = jax.ShapeDtypeStruct((batch * seq, GROUP_W), F32)
    return pl.pallas_call(
        _prompt_attn_kernel,
        grid_spec=pltpu.PrefetchScalarGridSpec(
            num_scalar_prefetch=2,
            grid=(batch, qi.shape[0]),
            in_specs=[pl.BlockSpec((1, GROUP_W, tq), qtmap),
                      pl.BlockSpec((tk, GROUP_W), kmap),
                      pl.BlockSpec((1, GROUP_W, tk), ktmap),
                      pl.BlockSpec((1, N_HEADS * LANES, tq), qtmap),
                      pl.BlockSpec((tk, N_HEADS * LANES), kmap),
                      pl.BlockSpec((1, GROUP_W, tq), qtmap),
                      pl.BlockSpec((tk, GROUP_W), kmap),
                      pl.BlockSpec((1, GROUP_W, tk), ktmap),
                      pl.BlockSpec((1, nq, GROUP_W), lambda b, t, qi, kj: (b, 0, 0)),
                      pl.BlockSpec(bias.shape, lambda b, t, qi, kj: (0, 0, 0, 0))],
            out_specs=[pl.BlockSpec((tq, GROUP_W), qmap), pl.BlockSpec((tq, GROUP_W), qmap)],
            scratch_shapes=(_flash_scratch(tq, tk) + _flash_scratch(tq, tk)
                            + [pltpu.VMEM((N_HEADS, LANES, tq), BF16)])),
        out_shape=[out, out],
        compiler_params=_cparams("arbitrary", "arbitrary"),
        name="prompt_attention",
    )(qi, kj, fqt, fk, fvt, fqt_aug, fk_aug, mqt, mk, mvt, kmean, bias)


def _block_diag_q(q):
    ds = q.shape[0]
    lane_head = lax.broadcasted_iota(I32, (N_HEADS, ds, GROUP_W), 2) // HEAD_DIM
    head = lax.broadcasted_iota(I32, (N_HEADS, ds, GROUP_W), 0)
    q3 = jnp.where(lane_head == head, q.astype(F32)[None, :, :], 0.0)
    return q3.reshape(N_HEADS * ds, GROUP_W).astype(BF16)


def _head_diag_out(acc, ds):
    acc3 = acc.reshape(N_HEADS, ds, GROUP_W)
    lane_head = lax.broadcasted_iota(I32, (N_HEADS, ds, GROUP_W), 2) // HEAD_DIM
    head = lax.broadcasted_iota(I32, (N_HEADS, ds, GROUP_W), 0)
    return jnp.sum(jnp.where(lane_head == head, acc3, 0.0), axis=0)


def _pad_rows(x, rows):
    if x.shape[0] == rows:
        return x
    return jnp.concatenate([x, jnp.zeros((rows - x.shape[0], x.shape[1]), x.dtype)], axis=0)


def _ring_copy(pt_ref, page_index, cache_ref, buf_ref, sem_ref, slot, rr):
    return pltpu.make_async_copy(cache_ref.at[pt_ref[page_index]], buf_ref.at[slot, rr], sem_ref.at[slot])


def _ring_fetch(pt_ref, page_index_of, u, cache_ref, buf_ref, sem_ref):
    slot = u % PAGE_RING
    for rr in range(buf_ref.shape[1]):
        _ring_copy(pt_ref, page_index_of(u, rr), cache_ref, buf_ref, sem_ref, slot, rr).start(priority=rr % 2)


def _ring_wait(pt_ref, page_index_of, u, cache_ref, buf_ref, sem_ref):
    slot = u % PAGE_RING
    for rr in range(buf_ref.shape[1]):
        _ring_copy(pt_ref, page_index_of(u, rr), cache_ref, buf_ref, sem_ref, slot, rr).wait()


def _ring_advance(pt_ref, page_index_of, u, n_fetches, cache_ref, buf_ref, sem_ref):
    @pl.when(u == 0)
    def _():
        for ahead in range(min(PAGE_RING - 1, n_fetches)):
            _ring_fetch(pt_ref, page_index_of, ahead, cache_ref, buf_ref, sem_ref)

    @pl.when(u + (PAGE_RING - 1) < n_fetches)
    def _():
        _ring_fetch(pt_ref, page_index_of, u + (PAGE_RING - 1), cache_ref, buf_ref, sem_ref)

    _ring_wait(pt_ref, page_index_of, u, cache_ref, buf_ref, sem_ref)
    return u % PAGE_RING


def _fox_sample_kernel(pt_ref, q_ref, kn_ref, vn_ref, lfn_ref, *refs, pages, n_pages, n_batch):
    lf_refs = refs[0:pages]
    (k_hbm, v_hbm, o_ref, qbd_ref, m_ref, l_ref, acc_ref, carry_ref, rq_ref,
     kbuf, vbuf, ksem, vsem) = refs[pages:]
    g = pl.program_id(1)
    ng = n_pages // pages
    ds = q_ref.shape[1]
    page = kbuf.shape[3]

    def page_index_of(u, rr):
        return (u // ng) * n_pages + (ng - 1 - u % ng) * pages + (pages - 1 - rr)

    u = pl.program_id(0) * ng + g
    slot = _ring_advance(pt_ref, page_index_of, u, n_batch * ng, k_hbm, kbuf, ksem)
    _ring_advance(pt_ref, page_index_of, u, n_batch * ng, v_hbm, vbuf, vsem)
    lane3 = lax.broadcasted_iota(I32, (N_HEADS, ds, page), 2)
    qidx3 = lax.broadcasted_iota(I32, (N_HEADS, ds, page), 1)
    r = lax.broadcasted_iota(I32, (page, page), 0)
    c = lax.broadcasted_iota(I32, (page, page), 1)
    after = jnp.where(r > c, 1.0, 0.0).astype(BF16)

    def suffix_sum(x):
        hi, mid, lo = _split3_bf16(x)
        return (jnp.dot(hi, after, preferred_element_type=F32) + jnp.dot(mid, after, preferred_element_type=F32)
                + jnp.dot(lo, after, preferred_element_type=F32))

    def biased(s, later):
        s3 = s.reshape(N_HEADS, ds, page) + later[:, None, :] - rq_ref[...].reshape(N_HEADS, ds, 1)
        return s3

    def softmax_update(s, pv_of):
        m_prev = m_ref[...]
        m_new = jnp.maximum(m_prev, jnp.max(s, axis=1, keepdims=True))
        alpha = jnp.exp(m_prev - m_new)
        p = jnp.exp(s - m_new)
        l_ref[...] = alpha * l_ref[...] + jnp.sum(p, axis=1, keepdims=True)
        m_ref[...] = m_new
        acc_ref[...] = alpha * acc_ref[...] + pv_of(p.astype(BF16))

    @pl.when(g == 0)
    def _():
        qbd_ref[...] = _block_diag_q(q_ref[0])
        m_ref[...] = jnp.full_like(m_ref, NEG)
        l_ref[...] = jnp.zeros_like(l_ref)
        acc_ref[...] = jnp.zeros_like(acc_ref)
        x = lfn_ref[0]
        later_new = suffix_sum(x)
        rq3 = jnp.sum(jnp.where(lane3 == qidx3, later_new[:, None, :], 0.0), axis=2, keepdims=True)
        rq_ref[...] = rq3.reshape(N_HEADS * ds, 1)
        kn = _pad_rows(kn_ref[0], page).astype(BF16)
        vn = _pad_rows(vn_ref[0], page).astype(BF16)
        s3 = biased(lax.dot_general(qbd_ref[...], kn, NT, preferred_element_type=F32), later_new)
        s3 = jnp.where(lane3 <= qidx3, s3, NEG)
        softmax_update(s3.reshape(N_HEADS * ds, page), lambda p: jnp.dot(p, vn, preferred_element_type=F32))
        carry_ref[...] = jnp.sum(x, axis=1, keepdims=True)

    xs = [lf_refs[rr][0] for rr in range(pages)]
    within = suffix_sum(jnp.concatenate(xs, axis=0))
    run = carry_ref[...]
    parts = []
    for rr in range(pages):
        later = within[rr * N_HEADS:(rr + 1) * N_HEADS, :] + run
        s = jnp.dot(qbd_ref[...], kbuf[slot, rr].astype(BF16), preferred_element_type=F32)
        parts.append(biased(s, later).reshape(N_HEADS * ds, page))
        run = run + jnp.sum(xs[rr], axis=1, keepdims=True)
    carry_ref[...] = run

    def pv_of(p):
        pv = None
        for rr in range(pages):
            term = lax.dot_general(p[:, rr * page:(rr + 1) * page], vbuf[slot, rr].astype(BF16), NT,
                                   preferred_element_type=F32)
            pv = term if pv is None else pv + term
        return pv

    softmax_update(jnp.concatenate(parts, axis=1), pv_of)

    @pl.when(g == ng - 1)
    def _():
        o_ref[0] = _head_diag_out(acc_ref[...] / l_ref[...], ds)


def _page_ring_scratch(pages, page):
    return [pltpu.VMEM((PAGE_RING, pages, GROUP_W, page), F32), pltpu.SemaphoreType.DMA((PAGE_RING,))]


def _fox_sample(page_table, q3, kn3, vn3, lfn_t, cache_k, cache_v, cache_lf_t):
    db, n_pages = page_table.shape
    ds = q3.shape[1]
    pages = PAGES_PER_STEP
    assert n_pages % pages == 0
    ng = n_pages // pages
    page = cache_k.shape[2]

    def page_map(rr):
        return lambda b, g, pt: (pt[b * n_pages + (ng - 1 - g) * pages + (pages - 1 - rr)], 0, 0)

    bmap = lambda b, g, pt: (b, 0, 0)
    in_specs = [pl.BlockSpec((1, ds, GROUP_W), bmap), pl.BlockSpec((1, ds, GROUP_W), bmap),
                pl.BlockSpec((1, ds, GROUP_W), bmap), pl.BlockSpec((1, N_HEADS, page), bmap)]
    in_specs += [pl.BlockSpec((1, N_HEADS, page), page_map(rr)) for rr in range(pages)]
    in_specs += [pl.BlockSpec(memory_space=pl.ANY), pl.BlockSpec(memory_space=pl.ANY)]
    rows = N_HEADS * ds
    k_ring, k_sem = _page_ring_scratch(pages, page)
    v_ring, v_sem = _page_ring_scratch(pages, page)
    return pl.pallas_call(
        functools.partial(_fox_sample_kernel, pages=pages, n_pages=n_pages, n_batch=db),
        grid_spec=pltpu.PrefetchScalarGridSpec(
            num_scalar_prefetch=1,
            grid=(db, ng),
            in_specs=in_specs,
            out_specs=pl.BlockSpec((1, ds, GROUP_W), bmap),
            scratch_shapes=[pltpu.VMEM((rows, GROUP_W), BF16), pltpu.VMEM((rows, 1), F32),
                            pltpu.VMEM((rows, 1), F32), pltpu.VMEM((rows, GROUP_W), F32),
                            pltpu.VMEM((N_HEADS, 1), F32), pltpu.VMEM((rows, 1), F32),
                            k_ring, v_ring, k_sem, v_sem]),
        out_shape=jax.ShapeDtypeStruct((db, ds, GROUP_W), F32),
        compiler_params=_cparams("arbitrary", "arbitrary"),
        name="fox_sample",
    )(page_table.reshape(-1), q3, kn3, vn3, lfn_t, *([cache_lf_t] * pages), cache_k, cache_v)


def _moba_sample_kernel(pt_ref, q_ref, kn_ref, vn_ref, bias_ref, k_hbm, v_hbm, o_ref,
                        qbd_ref, s_ref, p_ref, pnew_ref, linv_ref, acc_ref, kbuf, vbuf, ksem, vsem,
                        *, pages, n_batch):
    ph = pl.program_id(1)
    g = pl.program_id(2)
    ds = q_ref.shape[1]
    page = kbuf.shape[3]
    n_pages = s_ref.shape[0]
    ng = n_pages // pages
    rows = N_HEADS * ds
    pages_per_block = MOBA_BLOCK // page
    n_blocks = n_pages // pages_per_block
    u = pl.program_id(0) * ng + g

    def page_index_of(u, rr):
        return (u // ng) * n_pages + (u % ng) * pages + rr

    @pl.when(jnp.logical_and(ph == 0, g == 0))
    def _():
        qbd_ref[...] = _block_diag_q(q_ref[0])

    @pl.when(ph == 0)
    def _():
        slot = _ring_advance(pt_ref, page_index_of, u, n_batch * ng, k_hbm, kbuf, ksem)
        for rr in range(pages):
            s_ref[g * pages + rr] = jnp.dot(qbd_ref[...], kbuf[slot, rr].astype(BF16), preferred_element_type=F32)

    @pl.when(jnp.logical_and(ph == 0, g == ng - 1))
    def _():
        lane = lax.broadcasted_iota(I32, (rows, LANES), 1)
        gate = jnp.full((rows, LANES), -jnp.inf, F32)
        for b in range(n_blocks):
            tot = s_ref[b * pages_per_block]
            for u in range(1, pages_per_block):
                tot = tot + s_ref[b * pages_per_block + u]
            gate = jnp.where(lane == b, jnp.sum(tot, axis=1, keepdims=True) * (1.0 / MOBA_BLOCK), gate)
        chosen = []
        for _ in range(min(MOBA_TOPK, n_blocks)):
            mx = jnp.max(gate, axis=1, keepdims=True)
            idx = jnp.min(jnp.where(gate == mx, lane.astype(F32), float(LANES)), axis=1, keepdims=True)
            chosen.append(idx)
            gate = jnp.where(lane.astype(F32) == idx, -jnp.inf, gate)
        lane_n = lax.broadcasted_iota(I32, (N_HEADS, ds, page), 2)
        qidx_n = lax.broadcasted_iota(I32, (N_HEADS, ds, page), 1)
        s_new = lax.dot_general(qbd_ref[...], _pad_rows(kn_ref[0], page).astype(BF16), NT,
                                preferred_element_type=F32) + bias_ref[1]
        s_new = jnp.where((lane_n <= qidx_n).reshape(rows, page), s_new, NEG)
        m_tile = s_new
        for pg in range(n_pages):
            b = float(pg // pages_per_block)
            picked = chosen[0] == b
            for idx in chosen[1:]:
                picked = jnp.logical_or(picked, idx == b)
            bias = bias_ref[0] if pg == n_pages - 1 else bias_ref[2]
            s = jnp.where(picked, s_ref[pg] + bias, NEG)
            s_ref[pg] = s
            m_tile = jnp.maximum(m_tile, s)
        m = jnp.max(m_tile, axis=1, keepdims=True)
        p_new = jnp.exp(s_new - m)
        l_tile = p_new
        pnew_ref[...] = p_new.astype(BF16)
        for pg in range(n_pages):
            p = jnp.exp(s_ref[pg] - m)
            l_tile = l_tile + p
            p_ref[pg] = p.astype(BF16)
        linv_ref[...] = 1.0 / jnp.sum(l_tile, axis=1, keepdims=True)

    @pl.when(jnp.logical_and(ph == 1, g == 0))
    def _():
        acc_ref[...] = jnp.dot(pnew_ref[...], _pad_rows(vn_ref[0], page).astype(BF16),
                               preferred_element_type=F32)

    @pl.when(ph == 1)
    def _():
        slot = _ring_advance(pt_ref, page_index_of, u, n_batch * ng, v_hbm, vbuf, vsem)
        acc = acc_ref[...]
        for rr in range(pages):
            acc = acc + lax.dot_general(p_ref[g * pages + rr], vbuf[slot, rr].astype(BF16), NT,
                                        preferred_element_type=F32)
        acc_ref[...] = acc

    @pl.when(jnp.logical_and(ph == 1, g == ng - 1))
    def _():
        o_ref[0] = _head_diag_out(acc_ref[...] * linv_ref[...], ds)


def _moba_sample(page_table, q3, kn3, vn3, bias3, cache_k, cache_v):
    db, n_pages = page_table.shape
    ds = q3.shape[1]
    pages = PAGES_PER_STEP
    page = cache_k.shape[2]
    assert n_pages % pages == 0 and MOBA_BLOCK % page == 0 and (n_pages * page) % MOBA_BLOCK == 0
    assert T5_MAX_DIST <= page and ds <= page
    ng = n_pages // pages
    rows = N_HEADS * ds
    bmap = lambda b, ph, g, pt: (b, 0, 0)
    in_specs = [pl.BlockSpec((1, ds, GROUP_W), bmap), pl.BlockSpec((1, ds, GROUP_W), bmap),
                pl.BlockSpec((1, ds, GROUP_W), bmap),
                pl.BlockSpec(bias3.shape, lambda b, ph, g, pt: (0, 0, 0)),
                pl.BlockSpec(memory_space=pl.ANY), pl.BlockSpec(memory_space=pl.ANY)]
    k_ring, k_sem = _page_ring_scratch(pages, page)
    v_ring, v_sem = _page_ring_scratch(pages, page)
    return pl.pallas_call(
        functools.partial(_moba_sample_kernel, pages=pages, n_batch=db),
        grid_spec=pltpu.PrefetchScalarGridSpec(
            num_scalar_prefetch=1,
            grid=(db, 2, ng),
            in_specs=in_specs,
            out_specs=pl.BlockSpec((1, ds, GROUP_W), bmap),
            scratch_shapes=[pltpu.VMEM((rows, GROUP_W), BF16),
                            pltpu.VMEM((n_pages, rows, page), F32),
                            pltpu.VMEM((n_pages, rows, page), BF16),
                            pltpu.VMEM((rows, page), BF16),
                            pltpu.VMEM((rows, 1), F32),
                            pltpu.VMEM((rows, GROUP_W), F32),
                            k_ring, v_ring, k_sem, v_sem]),
        out_shape=jax.ShapeDtypeStruct((db, ds, GROUP_W), F32),
        compiler_params=_cparams("arbitrary", "arbitrary", "arbitrary"),
        name="moba_sample",
    )(page_table.reshape(-1), q3, kn3, vn3, bias3, cache_k, cache_v)


def _outproj_kernel(of_ref, om_ref, x_ref, gate_ref, shift_ref, scale_ref, gf_ref, gm_ref, wo_ref, gffn_ref,
                    wr_ref, br_ref, cnt0_ref, x1_ref, h2_ref, ri_ref, rw_ref, cnt_ref, carry_ref):
    t = pl.program_id(0)

    @pl.when(t == 0)
    def _():
        carry_ref[...] = cnt0_ref[...]

    tm = x_ref.shape[0]
    nf = (_rms(of_ref[...]) * gf_ref[...]).astype(BF16)
    nm = (_rms(om_ref[...]) * gm_ref[...]).astype(BF16)
    o = (jnp.dot(nf, wo_ref[0:GROUP_W, :], preferred_element_type=F32)
         + jnp.dot(nm, wo_ref[GROUP_W:2 * GROUP_W, :], preferred_element_type=F32))
    x1 = x_ref[...] + gate_ref[...] * o
    x1_ref[...] = x1
    h2 = _rms(x1) * gffn_ref[...]
    h2 = h2 * (1.0 + scale_ref[...]) + shift_ref[...]
    hi = h2.astype(BF16)
    half = h2.shape[1] // 2
    hi32 = pltpu.bitcast(hi.astype(F32), jnp.uint32)
    h2_ref[...] = jnp.bitwise_or(jnp.right_shift(hi32[:, :half], jnp.uint32(16)),
                                 jnp.bitwise_and(hi32[:, half:], jnp.uint32(0xFFFF0000)))

    lo = (h2 - hi.astype(F32)).astype(BF16)
    wr = wr_ref[...]
    whi = wr.astype(BF16)
    wlo = (wr - whi.astype(F32)).astype(BF16)
    lg = (jnp.dot(hi, whi, preferred_element_type=F32) + jnp.dot(lo, whi, preferred_element_type=F32)
          + jnp.dot(hi, wlo, preferred_element_type=F32)) + br_ref[...]
    lane = lax.broadcasted_iota(I32, (tm, LANES), 1)
    lane_f = lane.astype(F32)
    ninf = -jnp.inf
    is_g = jnp.logical_and(lane >= N_EXPERTS, lane < N_EXPERTS + N_GROUPS)
    glog = jnp.where(is_g, lg, ninf)
    gmax = jnp.max(glog, axis=1, keepdims=True)
    gidx = jnp.min(jnp.where(glog == gmax, lane_f, 2.0 * LANES), axis=1, keepdims=True).astype(I32) - N_EXPERTS
    g_w = 1.0 / jnp.sum(jnp.exp(glog - gmax), axis=1, keepdims=True)
    in_grp = jnp.logical_and(lane >= gidx * EXPERTS_PER_GROUP, lane < (gidx + 1) * EXPERTS_PER_GROUP)
    elog = jnp.where(in_grp, lg, ninf)
    e1 = jnp.max(elog, axis=1, keepdims=True)
    i1 = jnp.min(jnp.where(elog == e1, lane_f, 2.0 * LANES), axis=1, keepdims=True).astype(I32)
    z = jnp.sum(jnp.exp(elog - e1), axis=1, keepdims=True)
    elog2 = jnp.where(lane == i1, ninf, elog)
    e2 = jnp.max(elog2, axis=1, keepdims=True)
    i2 = jnp.min(jnp.where(elog2 == e2, lane_f, 2.0 * LANES), axis=1, keepdims=True).astype(I32)
    p1 = 1.0 / z
    p2 = jnp.exp(e2 - e1) / z
    w1 = g_w * (p1 / (p1 + p2))
    w2 = g_w * (p2 / (p1 + p2))

    a = jnp.where(jnp.logical_or(lane == i1, lane == i2), 1.0, 0.0)
    r = lax.broadcasted_iota(I32, (tm, tm), 0)
    c = lax.broadcasted_iota(I32, (tm, tm), 1)
    before = jnp.where(c < r, 1.0, 0.0).astype(BF16)
    pos = jnp.dot(before, a.astype(BF16), preferred_element_type=F32) + carry_ref[...]
    r1 = jnp.sum(jnp.where(lane == i1, pos, 0.0), axis=1, keepdims=True)
    r2 = jnp.sum(jnp.where(lane == i2, pos, 0.0), axis=1, keepdims=True)
    carry_ref[...] = carry_ref[...] + jnp.sum(a, axis=0, keepdims=True)
    cnt_ref[...] = carry_ref[...]

    ri = jnp.where(lane == 0, i1, 0) + jnp.where(lane == 1, i2, 0)
    ri = ri + jnp.where(lane == 2, r1.astype(I32), 0) + jnp.where(lane == 3, r2.astype(I32), 0)
    ri_ref[...] = ri
    rw_ref[...] = jnp.where(lane == 0, w1, 0.0) + jnp.where(lane == 1, w2, 0.0)


def _outproj(of, om, x2, gate3, shift3, scale3, gf, gm, wo, gffn, wr, br, cnt0, *, rows_per_mod, name):
    n, d = x2.shape
    tm = ROW_TILE
    nt = n // tm
    mod_rows = gate3.shape[1]
    mod_map = lambda t: ((t * tm) // rows_per_mod, 0, 0)
    row_map = lambda t: (t, 0)
    const2 = lambda t: (0, 0)
    return pl.pallas_call(
        _outproj_kernel,
        grid=(nt,),
        in_specs=[pl.BlockSpec((tm, GROUP_W), row_map), pl.BlockSpec((tm, GROUP_W), row_map),
                  pl.BlockSpec((tm, d), row_map),
                  pl.BlockSpec((None, mod_rows, d), mod_map), pl.BlockSpec((None, mod_rows, d), mod_map),
                  pl.BlockSpec((None, mod_rows, d), mod_map),
                  pl.BlockSpec((1, GROUP_W), const2), pl.BlockSpec((1, GROUP_W), const2),
                  pl.BlockSpec((d, d), const2), pl.BlockSpec((1, d), const2),
                  pl.BlockSpec((d, LANES), const2), pl.BlockSpec((1, LANES), const2),
                  pl.BlockSpec((1, LANES), const2)],
        out_specs=[pl.BlockSpec((tm, d), row_map), pl.BlockSpec((tm, d // 2), row_map),
                   pl.BlockSpec((tm, LANES), row_map), pl.BlockSpec((tm, LANES), row_map),
                   pl.BlockSpec((1, LANES), const2)],
        out_shape=[jax.ShapeDtypeStruct((n, d), F32), jax.ShapeDtypeStruct((n, d // 2), jnp.uint32),
                   jax.ShapeDtypeStruct((n, LANES), I32), jax.ShapeDtypeStruct((n, LANES), F32),
                   jax.ShapeDtypeStruct((1, LANES), F32)],
        scratch_shapes=[pltpu.VMEM((1, LANES), F32)],
        compiler_params=_cparams("arbitrary"),
        name=name,
    )(of, om, x2, gate3, shift3, scale3, gf, gm, wo, gffn, wr, br, cnt0)


def _scatter_kernel(dest_ref, h_ref, rows_in_ref, rows_ref, stage_ref, sem, *, n_tiles):
    del rows_in_ref
    tm = h_ref.shape[0]
    t = pl.program_id(0)

    def retire(slot):
        for _ in range(2):
            pltpu.make_async_copy(stage_ref.at[slot], rows_ref.at[pl.ds(0, tm)], sem.at[slot]).wait()

    for slot in range(2):
        @pl.when(t % 2 == slot)
        def _():
            stage_ref[slot] = h_ref[...]
            for r in range(tm):
                for k in range(2):
                    pltpu.make_async_copy(stage_ref.at[slot, pl.ds(r, 1)], rows_ref.at[pl.ds(dest_ref[k, r], 1)],
                                          sem.at[slot]).start(priority=k)

    for slot in range(2):
        @pl.when(jnp.logical_and(t > 0, (t - 1) % 2 == slot))
        def _():
            retire(slot)

        @pl.when(jnp.logical_and(t == n_tiles - 1, t % 2 == slot))
        def _():
            retire(slot)


def _scatter_rows(dest2, h2, rows):
    n, d = h2.shape
    tm = min(DISPATCH_TILE, n)
    return pl.pallas_call(
        functools.partial(_scatter_kernel, n_tiles=n // tm),
        grid=(n // tm,),
        in_specs=[pl.BlockSpec((2, tm), lambda t: (0, t), memory_space=pltpu.SMEM),
                  pl.BlockSpec((tm, d), lambda t: (t, 0)),
                  pl.BlockSpec(memory_space=pl.ANY)],
        out_specs=pl.BlockSpec(memory_space=pl.ANY),
        out_shape=jax.ShapeDtypeStruct(rows.shape, rows.dtype),
        scratch_shapes=[pltpu.VMEM((2, tm, d), h2.dtype), pltpu.SemaphoreType.DMA((2,))],
        input_output_aliases={2: 0},
        compiler_params=_cparams("arbitrary"),
        name="moe_scatter",
    )(dest2, h2, rows)


def _moe_kernel(be_ref, nu_ref, rows_ref, w1_ref, w3_ref, w2_ref, y_ref, w1b_ref, w3b_ref, w2b_ref,
                rbuf_ref, rsem):
    blk = pl.program_id(0)
    n_used = nu_ref[0]

    def fetch(u):
        start = pl.multiple_of(u * MOE_BLOCK, MOE_BLOCK)
        return pltpu.make_async_copy(rows_ref.at[pl.ds(start, MOE_BLOCK)], rbuf_ref.at[u % ROW_RING],
                                     rsem.at[u % ROW_RING])

    @pl.when(blk == 0)
    def _():
        for ahead in range(ROW_RING - 1):
            @pl.when(ahead < n_used)
            def _():
                fetch(ahead).start()

    @pl.when(blk + (ROW_RING - 1) < n_used)
    def _():
        fetch(blk + (ROW_RING - 1)).start()

    @pl.when(blk < n_used)
    def _():
        prev = be_ref[jnp.maximum(blk - 1, 0)]

        @pl.when(jnp.logical_or(blk == 0, be_ref[blk] != prev))
        def _():
            w1b_ref[...] = w1_ref[0].astype(BF16)
            w3b_ref[...] = w3_ref[0].astype(BF16)
            w2b_ref[...] = w2_ref[0].astype(BF16)

        fetch(blk).wait()
        packed = rbuf_ref[blk % ROW_RING]
        x_lo = pltpu.bitcast(jnp.left_shift(packed, jnp.uint32(16)), F32)
        x_hi = pltpu.bitcast(jnp.bitwise_and(packed, jnp.uint32(0xFFFF0000)), F32)
        x = jnp.concatenate([x_lo, x_hi], axis=1).astype(BF16)
        a = jnp.dot(x, w1b_ref[...], preferred_element_type=F32)
        b = jnp.dot(x, w3b_ref[...], preferred_element_type=F32)
        hm = (a * jax.nn.sigmoid(a)) * b
        y_ref[...] = jnp.dot(hm.astype(BF16), w2b_ref[...], preferred_element_type=F32)

    @pl.when(blk >= nu_ref[0])
    def _():
        y_ref[...] = jnp.zeros_like(y_ref)


def _moe_experts(block_e, n_used, rows, w1, w3, w2):
    p = rows.shape[0]
    d, de = w1.shape[1], w1.shape[2]
    assert rows.shape[1] * 2 == d
    nblk = p // MOE_BLOCK
    return pl.pallas_call(
        _moe_kernel,
        grid_spec=pltpu.PrefetchScalarGridSpec(
            num_scalar_prefetch=2,
            grid=(nblk,),
            in_specs=[pl.BlockSpec(memory_space=pl.ANY),
                      pl.BlockSpec((1, d, de), lambda b, be, nu: (be[b], 0, 0)),
                      pl.BlockSpec((1, d, de), lambda b, be, nu: (be[b], 0, 0)),
                      pl.BlockSpec((1, de, d), lambda b, be, nu: (be[b], 0, 0))],
            out_specs=pl.BlockSpec((MOE_BLOCK, d), lambda b, be, nu: (b, 0)),
            scratch_shapes=[pltpu.VMEM((d, de), BF16), pltpu.VMEM((d, de), BF16), pltpu.VMEM((de, d), BF16),
                            pltpu.VMEM((ROW_RING, MOE_BLOCK, d // 2), jnp.uint32),
                            pltpu.SemaphoreType.DMA((ROW_RING,))]),
        out_shape=jax.ShapeDtypeStruct((p, d), F32),
        compiler_params=_cparams("arbitrary"),
        name="moe_experts",
    )(block_e, n_used, rows, w1, w3, w2)


def _combine_kernel(dest_ref, dest_next_ref, x1_ref, gate_ref, rw_ref, gfin_ref, y_ref, o_ref, ybuf_ref, sem):
    tm = x1_ref.shape[0]
    t = pl.program_id(0)

    def gather(idx_ref, slot):
        for r in range(tm):
            for k in range(2):
                pltpu.make_async_copy(y_ref.at[pl.ds(idx_ref[k, r], 1)], ybuf_ref.at[slot, k, pl.ds(r, 1)],
                                      sem.at[slot]).start(priority=k)

    @pl.when(t == 0)
    def _():
        gather(dest_ref, 0)

    for slot in range(2):
        @pl.when(jnp.logical_and(t + 1 < pl.num_programs(0), (t + 1) % 2 == slot))
        def _():
            gather(dest_next_ref, slot)

    cur = t % 2
    for k in range(2):
        pltpu.make_async_copy(y_ref.at[pl.ds(0, tm)], ybuf_ref.at[cur, k], sem.at[cur]).wait()
    rw = rw_ref[...]
    moe = rw[:, 0:1] * ybuf_ref[cur, 0] + rw[:, 1:2] * ybuf_ref[cur, 1]
    xo = x1_ref[...] + gate_ref[...] * moe
    o_ref[...] = _rms(xo) * gfin_ref[...]


def _combine(dest2, x1, gate3, rw, gfin, y, *, rows_per_mod, name):
    n, d = x1.shape
    tm = min(DISPATCH_TILE, n)
    nt = n // tm
    mod_rows = gate3.shape[1]
    return pl.pallas_call(
        _combine_kernel,
        grid=(nt,),
        in_specs=[pl.BlockSpec((2, tm), lambda t: (0, t), memory_space=pltpu.SMEM),
                  pl.BlockSpec((2, tm), lambda t: (0, jnp.minimum(t + 1, nt - 1)), memory_space=pltpu.SMEM),
                  pl.BlockSpec((tm, d), lambda t: (t, 0)),
                  pl.BlockSpec((None, mod_rows, d), lambda t: ((t * tm) // rows_per_mod, 0, 0)),
                  pl.BlockSpec((tm, LANES), lambda t: (t, 0)),
                  pl.BlockSpec((1, d), lambda t: (0, 0)),
                  pl.BlockSpec(memory_space=pl.ANY)],
        out_specs=pl.BlockSpec((tm, d), lambda t: (t, 0)),
        out_shape=jax.ShapeDtypeStruct((n, d), F32),
        scratch_shapes=[pltpu.VMEM((2, 2, tm, d), F32), pltpu.SemaphoreType.DMA((2,))],
        compiler_params=_cparams("arbitrary"),
        name=name,
    )(dest2, dest2, x1, gate3, rw, gfin, y)


def kernel(x_prompt, x_sample, cache_fox_k, cache_fox_v, cache_fox_logf, cache_moba_k, cache_moba_v, page_table, c_prompt, c_sample, w_ada, b_ada, g_attn, w_in, b_forget, g_out_fox, g_out_moba, t5_bias, w_out, g_ffn, w_router_group, b_router_group, w_router_expert, b_router_expert, w1, w3, w2, g_final):
    bsz, seq, d = x_prompt.shape
    db, ds, _ = x_sample.shape
    depth = w_ada.shape[0]
    n_phys, page = cache_fox_k.shape[1], cache_fox_k.shape[2]
    assert depth == 1, "one trunk layer"
    assert seq % ROW_TILE == 0 and (db * ds) % ROW_TILE == 0 and ROW_TILE % ds == 0
    assert N_EXPERTS + N_GROUPS <= LANES and N_HEADS <= LANES
    l = 0
    n_p, n_s = bsz * seq, db * ds
    xp2 = x_prompt.reshape(n_p, d)
    xs2 = x_sample.reshape(n_s, d)

    mod = _ada(jnp.concatenate([c_prompt, c_sample], axis=0), w_ada[l], b_ada[l])
    mod_p = [mod[:bsz, i * d:(i + 1) * d].reshape(bsz, 1, d) for i in range(6)]
    mod_s = [jnp.repeat(mod[bsz:, i * d:(i + 1) * d], ds, axis=0).reshape(n_s // ROW_TILE, ROW_TILE, d)
             for i in range(6)]

    w = GROUP_W
    wl = w_in[l]
    wcat = jnp.concatenate([wl[:, :3 * w], wl[:, 3 * w + N_HEADS:],
                            wl[:, 3 * w:3 * w + N_HEADS], jnp.zeros((d, LANES - N_HEADS), F32)],
                           axis=1).astype(BF16)
    bf_pad = jnp.pad(b_forget[l], (0, LANES - N_HEADS)).reshape(1, LANES)
    g_attn2 = g_attn[l].reshape(1, d)

    fqt, fk_t, fv_t, lf_t, mqt, mk_t, mv_t, fk_aug, fqt_aug, kmean, fkb, fvtb, mkb, mvtb = _inproj(
        xp2, mod_p[0], mod_p[1], g_attn2, wcat, bf_pad, rows_per_mod=seq, prompt_extras=True, seq_len=seq)
    sfq, sfk, sfv, slf, smq, smk, smv = _inproj(
        xs2, mod_s[0], mod_s[1], g_attn2, wcat, bf_pad, rows_per_mod=ROW_TILE, prompt_extras=False, seq_len=ds)

    bias_p = _t5_tiles(t5_bias, (0, MOBA_BLOCK, 2 * MOBA_BLOCK), MOBA_BLOCK, MOBA_BLOCK, sign=-1, scale=LOG2E)
    o_fox_p, o_moba_p = _prompt_attention(fqt, fkb, fvtb, fk_aug, fqt_aug, mqt, mkb, mvtb,
                                          kmean.reshape(bsz, seq // MOBA_BLOCK, w), bias_p, bsz, seq)

    to3 = lambda a: a.reshape(db, ds, w)
    page_t = lambda c: c[l].transpose(0, 2, 3, 1).reshape(n_phys, w, page)
    cfk, cfv, cmk, cmv = page_t(cache_fox_k), page_t(cache_fox_v), page_t(cache_moba_k), page_t(cache_moba_v)
    clf_t = cache_fox_logf[l].transpose(0, 2, 1)
    slf_t = jnp.pad(slf.reshape(db, ds, N_HEADS).transpose(0, 2, 1), ((0, 0), (0, 0), (0, page - ds)))
    o_fox_s = _fox_sample(page_table, to3(sfq), to3(sfk), to3(sfv), slf_t, cfk, cfv, clf_t)
    bias_s = _t5_tiles(t5_bias, (page, 0, 2 * T5_MAX_DIST + page), ds, page, sign=1)
    bias_s = bias_s.transpose(1, 0, 2, 3).reshape(3, N_HEADS * ds, page)
    o_moba_s = _moba_sample(page_table, to3(smq), to3(smk), to3(smv), bias_s, cmk, cmv)

    wr = jnp.concatenate([w_router_expert[l], w_router_group[l],
                          jnp.zeros((d, LANES - N_EXPERTS - N_GROUPS), F32)], axis=1)
    br = jnp.concatenate([b_router_expert[l], b_router_group[l],
                          jnp.zeros((LANES - N_EXPERTS - N_GROUPS,), F32)]).reshape(1, LANES)
    wo = w_out[l].astype(BF16)
    gf, gm, gffn = g_out_fox[l].reshape(1, w), g_out_moba[l].reshape(1, w), g_ffn[l].reshape(1, d)
    x1_p, h2_p, ri_p, rw_p, cnt_p = _outproj(
        o_fox_p, o_moba_p, xp2, mod_p[2], mod_p[3], mod_p[4], gf, gm, wo, gffn, wr, br,
        jnp.zeros((1, LANES), F32), rows_per_mod=seq, name="outproj_prompt")
    x1_s, h2_s, ri_s, rw_s, cnt = _outproj(
        o_fox_s.reshape(n_s, w), o_moba_s.reshape(n_s, w), xs2, mod_s[2], mod_s[3], mod_s[4], gf, gm, wo, gffn,
        wr, br, cnt_p, rows_per_mod=ROW_TILE, name="outproj_sample")

    counts = cnt[0, :N_EXPERTS].astype(I32)
    padded = (counts + MOE_BLOCK - 1) // MOE_BLOCK * MOE_BLOCK
    pend = jnp.cumsum(padded)
    pstart = pend - padded
    n_asg = 2 * (n_p + n_s)
    n_blocks = -(-(n_asg + N_EXPERTS * (MOE_BLOCK - 1)) // MOE_BLOCK)
    blk_start = jnp.arange(n_blocks, dtype=I32) * MOE_BLOCK
    block_e = jnp.clip(jnp.sum((pend[None, :] <= blk_start[:, None]).astype(I32), axis=1), 0, N_EXPERTS - 1)
    n_used = (pend[-1:] // MOE_BLOCK).astype(I32)

    def dest_of(ri):
        hit = ri[:, 0:2, None] == jnp.arange(N_EXPERTS, dtype=I32)
        return (jnp.sum(jnp.where(hit, pstart, 0), axis=-1) + ri[:, 2:4]).T.astype(I32)

    dest_p, dest_s = dest_of(ri_p), dest_of(ri_s)
    rows = jnp.zeros((n_blocks * MOE_BLOCK, d // 2), jnp.uint32)
    rows = _scatter_rows(dest_p, h2_p, rows)
    rows = _scatter_rows(dest_s, h2_s, rows)
    y = _moe_experts(block_e, n_used, rows, w1[l], w3[l], w2[l])
    gfin = g_final.reshape(1, d)
    y_prompt = _combine(dest_p, x1_p, mod_p[5], rw_p, gfin, y, rows_per_mod=seq, name="combine_prompt")
    y_sample = _combine(dest_s, x1_s, mod_s[5], rw_s, gfin, y, rows_per_mod=ROW_TILE, name="combine_sample")

    hp = lambda a: a.reshape(depth, bsz, N_HEADS, HEAD_DIM, seq).transpose(0, 1, 4, 2, 3)
    hs = lambda a: a.reshape(depth, db, ds, N_HEADS, HEAD_DIM)
    return (y_prompt.reshape(bsz, seq, d), y_sample.reshape(db, ds, d),
            hp(fk_t), hp(fv_t), lf_t.reshape(depth, bsz, N_HEADS, seq).transpose(0, 1, 3, 2), hp(mk_t), hp(mv_t),
            hs(sfk), hs(sfv), slf.reshape(depth, db, ds, N_HEADS), hs(smk), hs(smv))
```
